```python
import jax, jax.numpy as jnp
from jax import lax
import numpy as np

D_MODEL = 1024
BATCH = 8
SEQ = 2048
DEPTH = 1

GRID_W = 64
CTX_LEN = 256
D_MIX = D_MODEL
D_ATTN = D_MIX // 2
D_CONV = D_MIX - D_ATTN
HEAD_DIM = 64
N_HEADS = D_ATTN // HEAD_DIM
WIN_H = 8
WIN_W = 16
CONV_K = 3
ROPE_THETA = 10000.0
RMS_EPS = 1e-6
SPLIT_POINTS = (D_ATTN, 2 * D_ATTN, 3 * D_ATTN, 4 * D_ATTN,
                4 * D_ATTN + D_CONV, 4 * D_ATTN + 2 * D_CONV, 4 * D_ATTN + 3 * D_CONV)
D_IN = 4 * D_ATTN + 4 * D_CONV

kernel_name = "hybrid_natten_shortconv_dit_block"


def rmsnorm(x, g):
    xf = x.astype(jnp.float32)
    y = xf * lax.rsqrt(jnp.mean(xf * xf, axis=-1, keepdims=True) + RMS_EPS)
    return (y * g.astype(jnp.float32)).astype(x.dtype)


def adaln(cond, w_ada, b_ada):
    m = jax.nn.silu(cond) @ w_ada + b_ada
    return jnp.split(m, 3, axis=-1)


def heads(t):
    return t.reshape(*t.shape[:-1], N_HEADS, HEAD_DIM)


def axial_rope(t, n_cols):
    S = t.shape[1]
    nf = HEAD_DIM // 4
    inv = ROPE_THETA ** (-jnp.arange(nf, dtype=jnp.float32) / nf)
    pos = jnp.arange(S, dtype=jnp.int32)
    row = (pos // n_cols).astype(jnp.float32)
    col = (pos % n_cols).astype(jnp.float32)
    ang = jnp.stack([row[:, None] * inv, col[:, None] * inv], axis=1)
    cos = jnp.cos(ang)[:, None, :, None, :]
    sin = jnp.sin(ang)[:, None, :, None, :]
    tf = t.astype(jnp.float32).reshape(*t.shape[:-1], 2, 2, nf)
    rot = jnp.stack([-tf[..., 1, :], tf[..., 0, :]], axis=-2)
    return (tf * cos + rot * sin).reshape(t.shape).astype(t.dtype)


def neighbourhood_attention(q, k, v, kc, vc, rpb):
    Bn, S, H, Dh = q.shape
    rows = S // GRID_W
    wh = min(WIN_H, rows)
    scale = Dh ** -0.5
    q_plain = q.reshape(Bn, rows, GRID_W, H, Dh)
    q_rot = axial_rope(q, GRID_W).reshape(Bn, rows, GRID_W, H, Dh)
    k_rot = axial_rope(k, GRID_W).reshape(Bn, rows, GRID_W, H, Dh)
    v_g = v.reshape(Bn, rows, GRID_W, H, Dh)
    qi = jnp.arange(rows)
    row_start = jnp.clip(qi - wh // 2, 0, rows - wh)
    row_idx = row_start[:, None] + jnp.arange(wh)[None, :]
    k_rows = k_rot[:, row_idx]
    v_rows = v_g[:, row_idx]
    cols = jnp.arange(GRID_W)
    col_start = jnp.clip(cols - WIN_W // 2, 0, GRID_W - WIN_W)
    col_valid = (cols[None, :] >= col_start[:, None]) & (cols[None, :] < col_start[:, None] + WIN_W)
    dr = row_idx - qi[:, None] + (WIN_H - 1)
    dc = jnp.clip(cols[None, :] - cols[:, None] + (WIN_W - 1), 0, 2 * WIN_W - 2)
    bias = rpb[:, dr[:, None, :, None], dc[None, :, None, :]].astype(jnp.float32)
    s_lat = jnp.einsum('biqhd,birkhd->bhiqrk', q_rot, k_rows,
                       preferred_element_type=jnp.float32) * scale + bias
    s_lat = jnp.where(col_valid[:, None, :], s_lat, -jnp.inf)
    s_ctx = jnp.einsum('biqhd,blhd->bhiql', q_plain, kc,
                       preferred_element_type=jnp.float32) * scale
    n_lat = wh * GRID_W
    s = jnp.concatenate([s_lat.reshape(*s_lat.shape[:4], n_lat), s_ctx], axis=-1)
    p = jax.nn.softmax(s, axis=-1).astype(v.dtype)
    p_lat = p[..., :n_lat].reshape(s_lat.shape)
    p_ctx = p[..., n_lat:]
    o = (jnp.einsum('bhiqrk,birkhd->biqhd', p_lat, v_rows)
         + jnp.einsum('bhiql,blhd->biqhd', p_ctx, vc))
    return o.reshape(Bn, S, H * Dh)


def context_attention(qc, kc, vc):
    Bn, L, H, Dh = qc.shape
    s = jnp.einsum('blhd,bmhd->bhlm', qc, kc, preferred_element_type=jnp.float32) * (Dh ** -0.5)
    p = jax.nn.softmax(s, axis=-1).astype(vc.dtype)
    return jnp.einsum('bhlm,bmhd->blhd', p, vc).reshape(Bn, L, H * Dh)


def centred_short_conv(u, w, b):
    L = u.shape[1]
    pad = CONV_K // 2
    up = jnp.pad(u, ((0, 0), (pad, pad), (0, 0)))
    y = b
    for i in range(CONV_K):
        y = y + up[:, i:i + L] * w[i]
    return y


def gated_short_conv(u, bg, cg, zc, conv_w, conv_b):
    return bg * centred_short_conv(cg * u, conv_w, conv_b) * jax.nn.silu(zc)


def hybrid_layer(x, ctx, c, c_ctx, w_ada, b_ada, norm_g, w_in, q_norm_g, k_norm_g,
                 rpb, conv_w, conv_b, w_out, update_ctx):
    shift, scale, gate = adaln(c, w_ada, b_ada)
    shift_c, scale_c, gate_c = adaln(c_ctx, w_ada, b_ada)
    h = rmsnorm(x, norm_g) * (1 + scale[:, None]) + shift[:, None]
    hc = rmsnorm(ctx, norm_g) * (1 + scale_c) + shift_c
    q, k, v, za, u, bg, cg, zc = jnp.split(h @ w_in, SPLIT_POINTS, axis=-1)
    q = rmsnorm(heads(q), q_norm_g)
    k = rmsnorm(heads(k), k_norm_g)
    if update_ctx:
        qc, kc, vc, zac, uc, bgc, cgc, zcc = jnp.split(hc @ w_in, SPLIT_POINTS, axis=-1)
    else:
        kc, vc = jnp.split(hc @ w_in[:, D_ATTN:3 * D_ATTN], 2, axis=-1)
    kc = rmsnorm(heads(kc), k_norm_g)
    vc = heads(vc)
    attn = neighbourhood_attention(q, k, heads(v), kc, vc, rpb) * jax.nn.silu(za)
    conv = gated_short_conv(u, bg, cg, zc, conv_w, conv_b)
    x_new = x + gate[:, None] * (jnp.concatenate([attn, conv], axis=-1) @ w_out)
    if update_ctx:
        qc = rmsnorm(heads(qc), q_norm_g)
        attn_c = context_attention(qc, kc, vc) * jax.nn.silu(zac)
        conv_c = gated_short_conv(uc, bgc, cgc, zcc, conv_w, conv_b)
        ctx_new = ctx + gate_c * (jnp.concatenate([attn_c, conv_c], axis=-1) @ w_out)
    else:
        ctx_new = ctx
    return x_new, ctx_new


def setup_inputs(seed: int = 0) -> dict:
    key = jax.random.key(seed)
    ks = jax.random.split(key, 14)
    f32 = jnp.float32
    x = jax.random.normal(ks[0], (BATCH, SEQ, D_MODEL), f32)
    c = jax.random.normal(ks[1], (BATCH, D_MODEL), f32)
    ctx = jax.random.normal(ks[2], (BATCH, CTX_LEN, D_MODEL), f32)
    c_ctx = jax.random.normal(ks[3], (D_MODEL,), f32)
    w_ada = jax.random.normal(ks[4], (DEPTH, D_MODEL, 3 * D_MODEL), f32) * (0.5 * D_MODEL ** -0.5)
    b_ada = jax.random.normal(ks[5], (DEPTH, 3 * D_MODEL), f32) * 0.01
    norm_g = 1.0 + 0.1 * jax.random.normal(ks[6], (DEPTH, D_MODEL), f32)
    w_in = jax.random.normal(ks[7], (DEPTH, D_MODEL, D_IN), f32) * D_MODEL ** -0.5
    q_norm_g = 1.0 + 0.1 * jax.random.normal(ks[8], (DEPTH, HEAD_DIM), f32)
    k_norm_g = 1.0 + 0.1 * jax.random.normal(ks[9], (DEPTH, HEAD_DIM), f32)
    rpb = 0.1 * jax.random.normal(ks[10], (DEPTH, N_HEADS, 2 * WIN_H - 1, 2 * WIN_W - 1), f32)
    conv_w = jax.random.normal(ks[11], (DEPTH, CONV_K, D_CONV), f32) * CONV_K ** -0.5
    conv_b = 0.01 * jax.random.normal(ks[12], (DEPTH, D_CONV), f32)
    w_out = jax.random.normal(ks[13], (DEPTH, D_MIX, D_MODEL), f32) * D_MIX ** -0.5
    return {"x": x, "c": c, "ctx": ctx, "c_ctx": c_ctx, "w_ada": w_ada, "b_ada": b_ada,
            "norm_g": norm_g, "w_in": w_in, "q_norm_g": q_norm_g, "k_norm_g": k_norm_g,
            "rpb": rpb, "conv_w": conv_w, "conv_b": conv_b, "w_out": w_out}


def reference(x, c, ctx, c_ctx, w_ada, b_ada, norm_g, w_in, q_norm_g, k_norm_g,
              rpb, conv_w, conv_b, w_out):
    for l in range(DEPTH):
        x, ctx = hybrid_layer(x, ctx, c, c_ctx, w_ada[l], b_ada[l], norm_g[l], w_in[l],
                              q_norm_g[l], k_norm_g[l], rpb[l], conv_w[l], conv_b[l], w_out[l],
                              update_ctx=(l < DEPTH - 1))
    return x
```

```python
import functools

import numpy as np
import jax
import jax.numpy as jnp
from jax import lax
from jax.experimental import pallas as pl
from jax.experimental.pallas import tpu as pltpu

F32 = jnp.float32
BF16 = jnp.bfloat16

HEAD_DIM = 64
GRID_W = 64
WIN_H = 8
WIN_W = 16
CONV_K = 3
ROPE_THETA = 10000.0
RMS_EPS = 1e-6
MASK_VALUE = -1e30

LANES = 128
BF16_ROWS = 16
HALO = BF16_ROWS
VMEM_LIMIT = 56 * 1024 * 1024


def _silu(z):
    return z * jax.nn.sigmoid(z)


def _adaln_kernel(cond_ref, w_ref, b_ref, o_ref):
    a = _silu(cond_ref[...]).astype(BF16)
    o_ref[...] = jnp.dot(a, w_ref[...].astype(BF16), preferred_element_type=F32) + b_ref[...]


def _adaln(cond, w_ada, b_ada):
    rows, d = cond.shape
    n = w_ada.shape[1]
    tn = 512
    return pl.pallas_call(
        _adaln_kernel,
        grid=(n // tn,),
        in_specs=[pl.BlockSpec((rows, d), lambda i: (0, 0)),
                  pl.BlockSpec((d, tn), lambda i: (0, i)),
                  pl.BlockSpec((1, tn), lambda i: (0, i))],
        out_specs=pl.BlockSpec((rows, tn), lambda i: (0, i)),
        out_shape=jax.ShapeDtypeStruct((rows, n), F32),
        compiler_params=pltpu.CompilerParams(vmem_limit_bytes=VMEM_LIMIT),
        name="adaln",
    )(cond, w_ada, b_ada.reshape(1, n))


def _modulated_norm(xt, mult, shift):
    ms = jnp.mean(xt * xt, axis=-1, keepdims=True)
    return (xt * lax.rsqrt(ms + RMS_EPS) * mult + shift).astype(BF16)


def _head_norm(t, gain, bd):
    t2 = t * t
    hi = t2.astype(BF16)
    lo = (t2 - hi.astype(F32)).astype(BF16)
    ss = (jnp.dot(hi, bd, preferred_element_type=F32) + jnp.dot(lo, bd, preferred_element_type=F32))
    return t * lax.rsqrt(ss * (1.0 / HEAD_DIM) + RMS_EPS) * gain


def _rope(t, cos, sin_lo, sin_hi):
    outs = []
    for c in range(t.shape[1] // LANES):
        tc = t[:, c * LANES:(c + 1) * LANES]
        up = pltpu.roll(tc, LANES - HEAD_DIM // 4, 1)
        dn = pltpu.roll(tc, HEAD_DIM // 4, 1)
        outs.append(tc * cos + up * sin_lo + dn * sin_hi)
    return jnp.concatenate(outs, axis=1)


def _ctx_kernel(ctx_ref, ada_ref, ng_ref, w_ref, kg_ref, bd_ref, kcT_ref, vc_ref, *, d_attn):
    shift = ada_ref[0, 0:1, :]
    scale = ada_ref[0, 1:2, :]
    mult = ng_ref[...] * (1.0 + scale)
    h = _modulated_norm(ctx_ref[0], mult, shift)
    kf = jnp.dot(h, w_ref[:, 0:d_attn], preferred_element_type=F32)
    kn = _head_norm(kf, kg_ref[...], bd_ref[...])
    kcT_ref[0] = kn.T.astype(BF16)
    vc_ref[0] = jnp.dot(h, w_ref[:, d_attn:2 * d_attn], preferred_element_type=F32).astype(BF16)


def _ctx_kv(ctx, ada3, ctx_row, norm_g, w_kv, kg, bd):
    b, l, d = ctx.shape
    d_attn = w_kv.shape[1] // 2
    return pl.pallas_call(
        functools.partial(_ctx_kernel, d_attn=d_attn),
        grid=(b,),
        in_specs=[pl.BlockSpec((1, l, d), lambda i: (i, 0, 0)),
                  pl.BlockSpec((1, 3, d), lambda i: (ctx_row, 0, 0)),
                  pl.BlockSpec((1, d), lambda i: (0, 0)),
                  pl.BlockSpec((d, 2 * d_attn), lambda i: (0, 0)),
                  pl.BlockSpec((1, d_attn), lambda i: (0, 0)),
                  pl.BlockSpec((d_attn, d_attn), lambda i: (0, 0))],
        out_specs=[pl.BlockSpec((1, d_attn, l), lambda i: (i, 0, 0)),
                   pl.BlockSpec((1, l, d_attn), lambda i: (i, 0, 0))],
        out_shape=[jax.ShapeDtypeStruct((b, d_attn, l), BF16),
                   jax.ShapeDtypeStruct((b, l, d_attn), BF16)],
        compiler_params=pltpu.CompilerParams(vmem_limit_bytes=VMEM_LIMIT),
        name="ctx_kv",
    )(ctx, ada3, norm_g, w_kv, kg, bd)


def _in_proj_kernel(x_ref, xp_ref, xn_ref, ada_ref, ng_ref, w_ref, qg_ref, kg_ref,
                    cos_ref, slo_ref, shi_ref, bd_ref, cw_ref, cb_ref,
                    qrot_ref, qpl_ref, kT_ref, v_ref, ga_ref, conv_ref,
                    hext_ref, cu_ref, *, tm, da, dc):
    j = pl.program_id(1)
    nj = pl.num_programs(1)
    shift = ada_ref[0, 0:1, :]
    scale = ada_ref[0, 1:2, :]
    mult = ng_ref[...] * (1.0 + scale)

    hext_ref[0:HALO, :] = _modulated_norm(xp_ref[0], mult, shift)
    hext_ref[HALO:HALO + tm, :] = _modulated_norm(x_ref[0], mult, shift)
    hext_ref[HALO + tm:, :] = _modulated_norm(xn_ref[0], mult, shift)
    h = hext_ref[HALO:HALO + tm, :]

    def proj(src, lo, width):
        return jnp.dot(src, w_ref[:, lo:lo + width], preferred_element_type=F32)

    bd = bd_ref[...]
    cos, slo, shi = cos_ref[...], slo_ref[...], shi_ref[...]

    qn = _head_norm(proj(h, 0, da), qg_ref[...], bd)
    qpl_ref[0] = qn.astype(BF16)
    qrot_ref[0] = _rope(qn, cos, slo, shi).astype(BF16)

    kn = _head_norm(proj(h, da, da), kg_ref[...], bd)
    kT_ref[0] = _rope(kn, cos, slo, shi).T.astype(BF16)

    v_ref[0] = proj(h, 2 * da, da).astype(BF16)
    ga_ref[0] = _silu(proj(h, 3 * da, da)).astype(BF16)

    hx = hext_ref[...]
    cu = proj(hx, 4 * da, dc) * proj(hx, 4 * da + 2 * dc, dc)
    row = lax.broadcasted_iota(jnp.int32, (tm + 2 * HALO, 1), 0)
    inside = ((row >= HALO) | (j > 0)) & ((row < HALO + tm) | (j < nj - 1))
    cu_ref[...] = jnp.where(inside, cu, 0.0)
    y = (cb_ref[...]
         + cw_ref[0:1, :] * cu_ref[HALO - 1:HALO - 1 + tm, :]
         + cw_ref[1:2, :] * cu_ref[HALO:HALO + tm, :]
         + cw_ref[2:3, :] * cu_ref[HALO + 1:HALO + 1 + tm, :])
    bg = proj(h, 4 * da + dc, dc)
    zc = proj(h, 4 * da + 3 * dc, dc)
    conv_ref[0] = (bg * y * _silu(zc)).astype(BF16)


def _in_proj(x, ada3, norm_g, w_in, qg, kg, cos, slo, shi, bd, conv_w, conv_b, *, tm):
    b, s, d = x.shape
    dc = conv_w.shape[1]
    da = (w_in.shape[1] - 4 * dc) // 4
    nh = tm // HALO
    last_halo = s // HALO - 1
    tok = lambda width: pl.BlockSpec((1, tm, width), lambda i, j: (i, j, 0))
    const = lambda shape: pl.BlockSpec(shape, lambda i, j: (0,) * len(shape))
    tab = pl.BlockSpec((tm, LANES), lambda i, j: (j, 0))
    out_tok = jax.ShapeDtypeStruct((b, s, da), BF16)
    return pl.pallas_call(
        functools.partial(_in_proj_kernel, tm=tm, da=da, dc=dc),
        grid=(b, s // tm),
        in_specs=[tok(d),
                  pl.BlockSpec((1, HALO, d), lambda i, j: (i, jnp.maximum(j * nh - 1, 0), 0)),
                  pl.BlockSpec((1, HALO, d), lambda i, j: (i, jnp.minimum((j + 1) * nh, last_halo), 0)),
                  pl.BlockSpec((1, 3, d), lambda i, j: (i, 0, 0)),
                  const((1, d)), const(w_in.shape), const((1, da)), const((1, da)),
                  tab, tab, tab, const((da, da)), const((CONV_K, dc)), const((1, dc))],
        out_specs=[tok(da), tok(da),
                   pl.BlockSpec((1, da, tm), lambda i, j: (i, 0, j)),
                   tok(da), tok(da), tok(dc)],
        out_shape=[out_tok, out_tok, jax.ShapeDtypeStruct((b, da, s), BF16), out_tok, out_tok,
                   jax.ShapeDtypeStruct((b, s, dc), BF16)],
        scratch_shapes=[pltpu.VMEM((tm + 2 * HALO, d), BF16),
                        pltpu.VMEM((tm + 2 * HALO, dc), F32)],
        compiler_params=pltpu.CompilerParams(vmem_limit_bytes=VMEM_LIMIT),
        name="in_proj",
    )(x, x, x, ada3, norm_g, w_in, qg, kg, cos, slo, shi, bd, conv_w, conv_b)


def _attn_kernel(qrot_ref, qpl_ref, kT_ref, v_ref, kcT_ref, vc_ref, ga_ref, bias_ref, o_ref,
                 kpair_ref, *, rows):
    n_pairs = rows // 2
    for p in range(n_pairs):
        kpair_ref[p] = kT_ref[0, :, p * LANES:(p + 1) * LANES]
    for p in range(n_pairs - 1):
        kpair_ref[n_pairs + p] = jnp.concatenate(
            [kT_ref[0, :, p * LANES + GRID_W:(p + 1) * LANES],
             kT_ref[0, :, (p + 1) * LANES:(p + 1) * LANES + GRID_W]], axis=1)
    kpair_ref[2 * n_pairs - 1] = jnp.zeros((LANES, LANES), BF16)

    lane = lax.broadcasted_iota(jnp.int32, (GRID_W, LANES), 1)
    first_head = lane < HEAD_DIM
    kcT = kcT_ref[0]
    vc = vc_ref[0]

    def stack_heads(q2):
        zero = jnp.zeros_like(q2)
        return jnp.concatenate([jnp.where(first_head, q2, zero), jnp.where(first_head, zero, q2)], axis=0)

    def row_body(i, carry):
        rs = jnp.clip(i - WIN_H // 2, 0, rows - WIN_H)
        off = i - rs
        par = rs & 1
        p0 = (rs - par) // 2 + par * n_pairs
        tok0 = pl.multiple_of(i * GRID_W, GRID_W)
        key0 = pl.multiple_of(rs * GRID_W, GRID_W)

        qs = stack_heads(qrot_ref[0, pl.ds(tok0, GRID_W), :])
        band = jnp.concatenate([kpair_ref[p0 + t] for t in range(WIN_H // 2)], axis=1)
        s_lat = jnp.dot(qs, band, preferred_element_type=F32) + bias_ref[0, off]
        qp = stack_heads(qpl_ref[0, pl.ds(tok0, GRID_W), :])
        s_ctx = jnp.dot(qp, kcT, preferred_element_type=F32)

        m = jnp.maximum(jnp.max(s_lat, axis=-1, keepdims=True), jnp.max(s_ctx, axis=-1, keepdims=True))
        e_lat = jnp.exp(s_lat - m)
        e_ctx = jnp.exp(s_ctx - m)
        denom = jnp.sum(e_lat, axis=-1, keepdims=True) + jnp.sum(e_ctx, axis=-1, keepdims=True)

        vband = v_ref[0, pl.ds(key0, WIN_H * GRID_W), :]
        o = (jnp.dot(e_lat.astype(BF16), vband, preferred_element_type=F32)
             + jnp.dot(e_ctx.astype(BF16), vc, preferred_element_type=F32))
        o = o / denom
        o2 = jnp.where(first_head, o[0:GRID_W], o[GRID_W:2 * GRID_W])
        gate = ga_ref[0, pl.ds(tok0, GRID_W), :].astype(F32)
        o_ref[0, pl.ds(tok0, GRID_W), :] = (o2 * gate).astype(BF16)
        return carry

    lax.fori_loop(0, rows, row_body, 0)


def _attention(qrot, qpl, kT, v, kcT, vc, ga, bias):
    b, s, da = qrot.shape
    l = vc.shape[1]
    rows = s // GRID_W
    n_hp = da // LANES
    tok = pl.BlockSpec((1, s, LANES), lambda hp, i: (i, 0, hp))
    return pl.pallas_call(
        functools.partial(_attn_kernel, rows=rows),
        grid=(n_hp, b),
        in_specs=[tok, tok,
                  pl.BlockSpec((1, LANES, s), lambda hp, i: (i, hp, 0)),
                  tok,
                  pl.BlockSpec((1, LANES, l), lambda hp, i: (i, hp, 0)),
                  pl.BlockSpec((1, l, LANES), lambda hp, i: (i, 0, hp)),
                  tok,
                  pl.BlockSpec((1,) + bias.shape[1:], lambda hp, i: (hp, 0, 0, 0))],
        out_specs=tok,
        out_shape=jax.ShapeDtypeStruct((b, s, da), BF16),
        scratch_shapes=[pltpu.VMEM((rows, LANES, LANES), BF16)],
        compiler_params=pltpu.CompilerParams(vmem_limit_bytes=VMEM_LIMIT),
        name="attn",
    )(qrot, qpl, kT, v, kcT, vc, ga, bias)


def _out_proj_kernel(x_ref, a_ref, c_ref, ada_ref, w_ref, o_ref, *, da):
    gate = ada_ref[0, 2:3, :]
    upd = (jnp.dot(a_ref[0], w_ref[0:da, :], preferred_element_type=F32)
           + jnp.dot(c_ref[0], w_ref[da:, :], preferred_element_type=F32))
    o_ref[0] = x_ref[0] + gate * upd


def _out_proj(x, attn, conv, ada3, w_out, *, tm):
    b, s, d = x.shape
    da = attn.shape[2]
    dc = conv.shape[2]
    return pl.pallas_call(
        functools.partial(_out_proj_kernel, da=da),
        grid=(b, s // tm),
        in_specs=[pl.BlockSpec((1, tm, d), lambda i, j: (i, j, 0)),
                  pl.BlockSpec((1, tm, da), lambda i, j: (i, j, 0)),
                  pl.BlockSpec((1, tm, dc), lambda i, j: (i, j, 0)),
                  pl.BlockSpec((1, 3, d), lambda i, j: (i, 0, 0)),
                  pl.BlockSpec(w_out.shape, lambda i, j: (0, 0))],
        out_specs=pl.BlockSpec((1, tm, d), lambda i, j: (i, j, 0)),
        out_shape=jax.ShapeDtypeStruct((b, s, d), F32),
        compiler_params=pltpu.CompilerParams(vmem_limit_bytes=VMEM_LIMIT),
        name="out_proj",
    )(x, attn, conv, ada3, w_out)


def _rope_tables(s):
    nf = HEAD_DIM // 4
    inv = (ROPE_THETA ** (-np.arange(nf, dtype=np.float32) / nf)).astype(np.float32)
    pos = np.arange(s)
    lane = np.arange(LANES)
    d = lane % HEAD_DIM
    axis = d // (2 * nf)
    half = (d % (2 * nf)) // nf
    coord = np.where(axis[None, :] == 0, (pos // GRID_W)[:, None], (pos % GRID_W)[:, None]).astype(np.float32)
    ang = (coord * inv[d % nf][None, :]).astype(np.float32)
    cos = np.cos(ang).astype(np.float32)
    sin = np.sin(ang).astype(np.float32)
    sin_lo = np.where(half[None, :] == 0, -sin, 0.0).astype(np.float32)
    sin_hi = np.where(half[None, :] == 1, sin, 0.0).astype(np.float32)
    return jnp.asarray(cos), jnp.asarray(sin_lo), jnp.asarray(sin_hi)


def _bias_table(rpb, rows):
    nh = rpb.shape[0]
    wh = min(WIN_H, rows)
    off = np.arange(WIN_H)[:, None]
    r = np.arange(wh)[None, :]
    dr = np.clip(r - off + (WIN_H - 1), 0, 2 * WIN_H - 2)
    cols = np.arange(GRID_W)
    col_start = np.clip(cols - WIN_W // 2, 0, GRID_W - WIN_W)
    valid = (cols[None, :] >= col_start[:, None]) & (cols[None, :] < col_start[:, None] + WIN_W)
    dcol = np.clip(cols[None, :] - cols[:, None] + (WIN_W - 1), 0, 2 * WIN_W - 2)
    t = rpb[:, dr[:, None, :, None], dcol[None, :, None, :]]
    t = jnp.where(valid[None, None, :, None, :], t, MASK_VALUE)
    t = t.reshape(nh // 2, 2, WIN_H, GRID_W, wh * GRID_W)
    return t.transpose(0, 2, 1, 3, 4).reshape(nh // 2, WIN_H, 2 * GRID_W, wh * GRID_W)


def kernel(x, c, ctx, c_ctx, w_ada, b_ada, norm_g, w_in, q_norm_g, k_norm_g, rpb, conv_w, conv_b, w_out):
    depth = w_ada.shape[0]
    b, s, d = x.shape
    dc = conv_w.shape[2]
    da = (w_in.shape[2] - 4 * dc) // 4
    n_heads = da // HEAD_DIM
    rows = s // GRID_W
    assert depth == 1 and s % GRID_W == 0 and rows >= WIN_H and da % LANES == 0

    cos, slo, shi = _rope_tables(s)
    seg = np.arange(da) // HEAD_DIM
    bd = jnp.asarray((seg[:, None] == seg[None, :]).astype(np.float32), dtype=BF16)

    cond_rows = -(-(b + 1) // 8) * 8
    cond = jnp.zeros((cond_rows, d), F32).at[:b].set(c).at[b].set(c_ctx)
    ada3 = _adaln(cond, w_ada[0], b_ada[0]).reshape(cond_rows, 3, d)

    w_in_b = w_in[0].astype(BF16)
    ng = norm_g[0].reshape(1, d)
    qg = jnp.tile(q_norm_g[0] * (HEAD_DIM ** -0.5), n_heads).reshape(1, da)
    kg = jnp.tile(k_norm_g[0], n_heads).reshape(1, da)

    kcT, vc = _ctx_kv(ctx, ada3, b, ng, w_in_b[:, da:3 * da], kg, bd)
    qrot, qpl, kT, v, ga, conv = _in_proj(x, ada3, ng, w_in_b, qg, kg, cos, slo, shi, bd,
                                          conv_w[0], conv_b[0].reshape(1, dc), tm=512)
    attn = _attention(qrot, qpl, kT, v, kcT, vc, ga, _bias_table(rpb[0], rows))
    return _out_proj(x, attn, conv, ada3, w_out[0].astype(BF16), tm=512)
```

```python
import functools

import numpy as np
import jax
import jax.numpy as jnp
from jax import lax
from jax.experimental import pallas as pl
from jax.experimental.pallas import tpu as pltpu

F32 = jnp.float32
BF16 = jnp.bfloat16

HEAD_DIM = 64
GRID_W = 64
WIN_H = 8
WIN_W = 16
CONV_K = 3
ROPE_THETA = 10000.0
RMS_EPS = 1e-6
MASK_VALUE = -1e30

LANES = 128
BF16_ROWS = 16
HALO = BF16_ROWS
VMEM_LIMIT = 56 * 1024 * 1024


def _silu(z):
    return z * jax.nn.sigmoid(z)


def _adaln_kernel(cond_ref, w_ref, b_ref, o_ref):
    a = _silu(cond_ref[...]).astype(BF16)
    o_ref[...] = jnp.dot(a, w_ref[...].astype(BF16), preferred_element_type=F32) + b_ref[...]


def _adaln(cond, w_ada, b_ada):
    rows, d = cond.shape
    n = w_ada.shape[1]
    tn = 512
    return pl.pallas_call(
        _adaln_kernel,
        grid=(n // tn,),
        in_specs=[pl.BlockSpec((rows, d), lambda i: (0, 0)),
                  pl.BlockSpec((d, tn), lambda i: (0, i)),
                  pl.BlockSpec((1, tn), lambda i: (0, i))],
        out_specs=pl.BlockSpec((rows, tn), lambda i: (0, i)),
        out_shape=jax.ShapeDtypeStruct((rows, n), F32),
        compiler_params=pltpu.CompilerParams(vmem_limit_bytes=VMEM_LIMIT),
        name="adaln",
    )(cond, w_ada, b_ada.reshape(1, n))


def _modulated_norm(xt, mult, shift):
    ms = jnp.mean(xt * xt, axis=-1, keepdims=True)
    return (xt * lax.rsqrt(ms + RMS_EPS) * mult + shift).astype(BF16)


def _head_norm(t, gain, bd):
    t2 = t * t
    hi = t2.astype(BF16)
    lo = (t2 - hi.astype(F32)).astype(BF16)
    ss = (jnp.dot(hi, bd, preferred_element_type=F32) + jnp.dot(lo, bd, preferred_element_type=F32))
    return t * lax.rsqrt(ss * (1.0 / HEAD_DIM) + RMS_EPS) * gain


def _rope(t, cos, sin_lo, sin_hi):
    outs = []
    for c in range(t.shape[1] // LANES):
        tc = t[:, c * LANES:(c + 1) * LANES]
        up = pltpu.roll(tc, LANES - HEAD_DIM // 4, 1)
        dn = pltpu.roll(tc, HEAD_DIM // 4, 1)
        outs.append(tc * cos + up * sin_lo + dn * sin_hi)
    return jnp.concatenate(outs, axis=1)


def _ctx_kernel(ctx_ref, ada_ref, ng_ref, w_ref, kg_ref, bd_ref, kcT_ref, vc_ref, *, d_attn):
    shift = ada_ref[0, 0:1, :]
    scale = ada_ref[0, 1:2, :]
    mult = ng_ref[...] * (1.0 + scale)
    h = _modulated_norm(ctx_ref[0], mult, shift)
    kf = jnp.dot(h, w_ref[:, 0:d_attn], preferred_element_type=F32)
    kn = _head_norm(kf, kg_ref[...], bd_ref[...])
    kcT_ref[0] = kn.T.astype(BF16)
    vc_ref[0] = jnp.dot(h, w_ref[:, d_attn:2 * d_attn], preferred_element_type=F32).astype(BF16)


def _ctx_kv(ctx, ada3, ctx_row, norm_g, w_kv, kg, bd):
    b, l, d = ctx.shape
    d_attn = w_kv.shape[1] // 2
    return pl.pallas_call(
        functools.partial(_ctx_kernel, d_attn=d_attn),
        grid=(b,),
        in_specs=[pl.BlockSpec((1, l, d), lambda i: (i, 0, 0)),
                  pl.BlockSpec((1, 3, d), lambda i: (ctx_row, 0, 0)),
                  pl.BlockSpec((1, d), lambda i: (0, 0)),
                  pl.BlockSpec((d, 2 * d_attn), lambda i: (0, 0)),
                  pl.BlockSpec((1, d_attn), lambda i: (0, 0)),
                  pl.BlockSpec((d_attn, d_attn), lambda i: (0, 0))],
        out_specs=[pl.BlockSpec((1, d_attn, l), lambda i: (i, 0, 0)),
                   pl.BlockSpec((1, l, d_attn), lambda i: (i, 0, 0))],
        out_shape=[jax.ShapeDtypeStruct((b, d_attn, l), BF16),
                   jax.ShapeDtypeStruct((b, l, d_attn), BF16)],
        compiler_params=pltpu.CompilerParams(vmem_limit_bytes=VMEM_LIMIT),
        name="ctx_kv",
    )(ctx, ada3, norm_g, w_kv, kg, bd)


def _in_proj_kernel(x_ref, xp_ref, xn_ref, ada_ref, ng_ref, w_ref, qg_ref, kg_ref,
                    cos_ref, slo_ref, shi_ref, bd_ref, cw_ref, cb_ref,
                    qrot_ref, qpl_ref, kT_ref, v_ref, ga_ref, conv_ref,
                    hext_ref, cu_ref, *, tm, da, dc):
    j = pl.program_id(1)
    nj = pl.num_programs(1)
    shift = ada_ref[0, 0:1, :]
    scale = ada_ref[0, 1:2, :]
    mult = ng_ref[...] * (1.0 + scale)

    hext_ref[0:HALO, :] = _modulated_norm(xp_ref[0], mult, shift)
    hext_ref[HALO:HALO + tm, :] = _modulated_norm(x_ref[0], mult, shift)
    hext_ref[HALO + tm:, :] = _modulated_norm(xn_ref[0], mult, shift)
    h = hext_ref[HALO:HALO + tm, :]

    def proj(src, lo, width):
        return jnp.dot(src, w_ref[:, lo:lo + width], preferred_element_type=F32)

    bd = bd_ref[...]
    cos, slo, shi = cos_ref[...], slo_ref[...], shi_ref[...]

    qn = _head_norm(proj(h, 0, da), qg_ref[...], bd)
    qpl_ref[0] = qn.astype(BF16)
    qrot_ref[0] = _rope(qn, cos, slo, shi).astype(BF16)

    kn = _head_norm(proj(h, da, da), kg_ref[...], bd)
    kT_ref[0] = _rope(kn, cos, slo, shi).T.astype(BF16)

    v_ref[0] = proj(h, 2 * da, da).astype(BF16)
    ga_ref[0] = _silu(proj(h, 3 * da, da)).astype(BF16)

    hx = hext_ref[...]
    cu = proj(hx, 4 * da, dc) * proj(hx, 4 * da + 2 * dc, dc)
    row = lax.broadcasted_iota(jnp.int32, (tm + 2 * HALO, 1), 0)
    inside = ((row >= HALO) | (j > 0)) & ((row < HALO + tm) | (j < nj - 1))
    cu_ref[...] = jnp.where(inside, cu, 0.0)
    y = (cb_ref[...]
         + cw_ref[0:1, :] * cu_ref[HALO - 1:HALO - 1 + tm, :]
         + cw_ref[1:2, :] * cu_ref[HALO:HALO + tm, :]
         + cw_ref[2:3, :] * cu_ref[HALO + 1:HALO + 1 + tm, :])
    bg = proj(h, 4 * da + dc, dc)
    zc = proj(h, 4 * da + 3 * dc, dc)
    conv_ref[0] = (bg * y * _silu(zc)).astype(BF16)


def _in_proj(x, ada3, norm_g, w_in, qg, kg, cos, slo, shi, bd, conv_w, conv_b, *, tm):
    b, s, d = x.shape
    dc = conv_w.shape[1]
    da = (w_in.shape[1] - 4 * dc) // 4
    nh = tm // HALO
    last_halo = s // HALO - 1
    tok = lambda width: pl.BlockSpec((1, tm, width), lambda i, j: (i, j, 0))
    const = lambda shape: pl.BlockSpec(shape, lambda i, j: (0,) * len(shape))
    tab = pl.BlockSpec((tm, LANES), lambda i, j: (j, 0))
    out_tok = jax.ShapeDtypeStruct((b, s, da), BF16)
    return pl.pallas_call(
        functools.partial(_in_proj_kernel, tm=tm, da=da, dc=dc),
        grid=(b, s // tm),
        in_specs=[tok(d),
                  pl.BlockSpec((1, HALO, d), lambda i, j: (i, jnp.maximum(j * nh - 1, 0), 0)),
                  pl.BlockSpec((1, HALO, d), lambda i, j: (i, jnp.minimum((j + 1) * nh, last_halo), 0)),
                  pl.BlockSpec((1, 3, d), lambda i, j: (i, 0, 0)),
                  const((1, d)), const(w_in.shape), const((1, da)), const((1, da)),
                  tab, tab, tab, const((da, da)), const((CONV_K, dc)), const((1, dc))],
        out_specs=[tok(da), tok(da),
                   pl.BlockSpec((1, da, tm), lambda i, j: (i, 0, j)),
                   tok(da), tok(da), tok(dc)],
        out_shape=[out_tok, out_tok, jax.ShapeDtypeStruct((b, da, s), BF16), out_tok, out_tok,
                   jax.ShapeDtypeStruct((b, s, dc), BF16)],
        scratch_shapes=[pltpu.VMEM((tm + 2 * HALO, d), BF16),
                        pltpu.VMEM((tm + 2 * HALO, dc), F32)],
        compiler_params=pltpu.CompilerParams(vmem_limit_bytes=VMEM_LIMIT),
        name="in_proj",
    )(x, x, x, ada3, norm_g, w_in, qg, kg, cos, slo, shi, bd, conv_w, conv_b)


def _attn_kernel(qrot_ref, qpl_ref, kT_ref, v_ref, kcT_ref, vc_ref, ga_ref, bias_ref, o_ref,
                 kpair_ref, *, rows):
    n_pairs = rows // 2
    for p in range(n_pairs):
        kpair_ref[p] = kT_ref[0, :, p * LANES:(p + 1) * LANES]
    for p in range(n_pairs - 1):
        kpair_ref[n_pairs + p] = jnp.concatenate(
            [kT_ref[0, :, p * LANES + GRID_W:(p + 1) * LANES],
             kT_ref[0, :, (p + 1) * LANES:(p + 1) * LANES + GRID_W]], axis=1)
    kpair_ref[2 * n_pairs - 1] = jnp.zeros((LANES, LANES), BF16)

    lane = lax.broadcasted_iota(jnp.int32, (GRID_W, LANES), 1)
    first_head = lane < HEAD_DIM
    kcT = kcT_ref[0]
    vc = vc_ref[0]

    def stack_heads(q2):
        zero = jnp.zeros_like(q2)
        return jnp.concatenate([jnp.where(first_head, q2, zero), jnp.where(first_head, zero, q2)], axis=0)

    def row_body(i, carry):
        rs = jnp.clip(i - WIN_H // 2, 0, rows - WIN_H)
        off = i - rs
        par = rs & 1
        p0 = (rs - par) // 2 + par * n_pairs
        tok0 = pl.multiple_of(i * GRID_W, GRID_W)
        key0 = pl.multiple_of(rs * GRID_W, GRID_W)

        qs = stack_heads(qrot_ref[0, pl.ds(tok0, GRID_W), :])
        band = jnp.concatenate([kpair_ref[p0 + t] for t in range(WIN_H // 2)], axis=1)
        s_lat = jnp.dot(qs, band, preferred_element_type=F32) + bias_ref[0, off]
        qp = stack_heads(qpl_ref[0, pl.ds(tok0, GRID_W), :])
        s_ctx = jnp.dot(qp, kcT, preferred_element_type=F32)

        m = jnp.maximum(jnp.max(s_lat, axis=-1, keepdims=True), jnp.max(s_ctx, axis=-1, keepdims=True))
        e_lat = jnp.exp(s_lat - m)
        e_ctx = jnp.exp(s_ctx - m)
        denom = jnp.sum(e_lat, axis=-1, keepdims=True) + jnp.sum(e_ctx, axis=-1, keepdims=True)

        vband = v_ref[0, pl.ds(key0, WIN_H * GRID_W), :]
        o = (jnp.dot(e_lat.astype(BF16), vband, preferred_element_type=F32)
             + jnp.dot(e_ctx.astype(BF16), vc, preferred_element_type=F32))
        o = o / denom
        o2 = jnp.where(first_head, o[0:GRID_W], o[GRID_W:2 * GRID_W])
        gate = ga_ref[0, pl.ds(tok0, GRID_W), :].astype(F32)
        o_ref[0, pl.ds(tok0, GRID_W), :] = (o2 * gate).astype(BF16)
        return carry

    lax.fori_loop(0, rows, row_body, 0)


def _attention(qrot, qpl, kT, v, kcT, vc, ga, bias):
    b, s, da = qrot.shape
    l = vc.shape[1]
    rows = s // GRID_W
    n_hp = da // LANES
    tok = pl.BlockSpec((1, s, LANES), lambda hp, i: (i, 0, hp))
    return pl.pallas_call(
        functools.partial(_attn_kernel, rows=rows),
        grid=(n_hp, b),
        in_specs=[tok, tok,
                  pl.BlockSpec((1, LANES, s), lambda hp, i: (i, hp, 0)),
                  tok,
                  pl.BlockSpec((1, LANES, l), lambda hp, i: (i, hp, 0)),
                  pl.BlockSpec((1, l, LANES), lambda hp, i: (i, 0, hp)),
                  tok,
                  pl.BlockSpec((1,) + bias.shape[1:], lambda hp, i: (hp, 0, 0, 0))],
        out_specs=tok,
        out_shape=jax.ShapeDtypeStruct((b, s, da), BF16),
        scratch_shapes=[pltpu.VMEM((rows, LANES, LANES), BF16)],
        compiler_params=pltpu.CompilerParams(vmem_limit_bytes=VMEM_LIMIT),
        name="attn",
    )(qrot, qpl, kT, v, kcT, vc, ga, bias)


def _out_proj_kernel(x_ref, a_ref, c_ref, ada_ref, w_ref, o_ref, *, da):
    gate = ada_ref[0, 2:3, :]
    upd = (jnp.dot(a_ref[0], w_ref[0:da, :], preferred_element_type=F32)
           + jnp.dot(c_ref[0], w_ref[da:, :], preferred_element_type=F32))
    o_ref[0] = x_ref[0] + gate * upd


def _out_proj(x, attn, conv, ada3, w_out, *, tm):
    b, s, d = x.shape
    da = attn.shape[2]
    dc = conv.shape[2]
    return pl.pallas_call(
        functools.partial(_out_proj_kernel, da=da),
        grid=(b, s // tm),
        in_specs=[pl.BlockSpec((1, tm, d), lambda i, j: (i, j, 0)),
                  pl.BlockSpec((1, tm, da), lambda i, j: (i, j, 0)),
                  pl.BlockSpec((1, tm, dc), lambda i, j: (i, j, 0)),
                  pl.BlockSpec((1, 3, d), lambda i, j: (i, 0, 0)),
                  pl.BlockSpec(w_out.shape, lambda i, j: (0, 0))],
        out_specs=pl.BlockSpec((1, tm, d), lambda i, j: (i, j, 0)),
        out_shape=jax.ShapeDtypeStruct((b, s, d), F32),
        compiler_params=pltpu.CompilerParams(vmem_limit_bytes=VMEM_LIMIT),
        name="out_proj",
    )(x, attn, conv, ada3, w_out)


def _rope_tables(s):
    nf = HEAD_DIM // 4
    inv = (ROPE_THETA ** (-np.arange(nf, dtype=np.float32) / nf)).astype(np.float32)
    pos = np.arange(s)
    lane = np.arange(LANES)
    d = lane % HEAD_DIM
    axis = d // (2 * nf)
    half = (d % (2 * nf)) // nf
    coord = np.where(axis[None, :] == 0, (pos // GRID_W)[:, None], (pos % GRID_W)[:, None]).astype(np.float32)
    ang = (coord * inv[d % nf][None, :]).astype(np.float32)
    cos = np.cos(ang).astype(np.float32)
    sin = np.sin(ang).astype(np.float32)
    sin_lo = np.where(half[None, :] == 0, -sin, 0.0).astype(np.float32)
    sin_hi = np.where(half[None, :] == 1, sin, 0.0).astype(np.float32)
    return jnp.asarray(cos), jnp.asarray(sin_lo), jnp.asarray(sin_hi)


def _bias_kernel(rpb_ref, o_ref):
    off = pl.program_id(1)
    cq = lax.broadcasted_iota(jnp.int32, (GRID_W, LANES), 0)
    ck = lax.broadcasted_iota(jnp.int32, (GRID_W, LANES), 1) % GRID_W
    col_start = jnp.clip(cq - WIN_W // 2, 0, GRID_W - WIN_W)
    valid = (ck >= col_start) & (ck < col_start + WIN_W)
    for a in range(2):
        for t in range(WIN_H // 2):
            dr0 = 2 * t - off + (WIN_H - 1)
            two = rpb_ref[a, pl.ds(dr0, 2), :]
            lanes = jnp.broadcast_to(jnp.concatenate([two[0:1], two[1:2]], axis=1), (GRID_W, LANES))
            toeplitz = pltpu.roll(lanes, LANES - (WIN_W - 1), 1, stride=1, stride_axis=0)
            o_ref[0, 0, a * GRID_W:(a + 1) * GRID_W, t * LANES:(t + 1) * LANES] = jnp.where(
                valid, toeplitz, MASK_VALUE)


def _bias_table(rpb):
    nh, ndr, ndc = rpb.shape
    assert ndr == 2 * WIN_H - 1 and ndc == 2 * WIN_W - 1 and ndc <= GRID_W
    rpb_pad = jnp.pad(rpb, ((0, 0), (0, 0), (0, GRID_W - ndc)))
    return pl.pallas_call(
        _bias_kernel,
        grid=(nh // 2, WIN_H),
        in_specs=[pl.BlockSpec((2, ndr, GRID_W), lambda hp, off: (hp, 0, 0))],
        out_specs=pl.BlockSpec((1, 1, 2 * GRID_W, WIN_H * GRID_W), lambda hp, off: (hp, off, 0, 0)),
        out_shape=jax.ShapeDtypeStruct((nh // 2, WIN_H, 2 * GRID_W, WIN_H * GRID_W), F32),
        name="bias_table",
    )(rpb_pad)


def kernel(x, c, ctx, c_ctx, w_ada, b_ada, norm_g, w_in, q_norm_g, k_norm_g, rpb, conv_w, conv_b, w_out):
    depth = w_ada.shape[0]
    b, s, d = x.shape
    dc = conv_w.shape[2]
    da = (w_in.shape[2] - 4 * dc) // 4
    n_heads = da // HEAD_DIM
    rows = s // GRID_W
    assert depth == 1 and s % GRID_W == 0 and rows >= WIN_H and da % LANES == 0

    cos, slo, shi = _rope_tables(s)
    seg = np.arange(da) // HEAD_DIM
    bd = jnp.asarray((seg[:, None] == seg[None, :]).astype(np.float32), dtype=BF16)

    cond_rows = -(-(b + 1) // 8) * 8
    cond = jnp.zeros((cond_rows, d), F32).at[:b].set(c).at[b].set(c_ctx)
    ada3 = _adaln(cond, w_ada[0], b_ada[0]).reshape(cond_rows, 3, d)

    w_in_b = w_in[0].astype(BF16)
    ng = norm_g[0].reshape(1, d)
    qg = jnp.tile(q_norm_g[0] * (HEAD_DIM ** -0.5), n_heads).reshape(1, da)
    kg = jnp.tile(k_norm_g[0], n_heads).reshape(1, da)

    kcT, vc = _ctx_kv(ctx, ada3, b, ng, w_in_b[:, da:3 * da], kg, bd)
    qrot, qpl, kT, v, ga, conv = _in_proj(x, ada3, ng, w_in_b, qg, kg, cos, slo, shi, bd,
                                          conv_w[0], conv_b[0].reshape(1, dc), tm=512)
    attn = _attention(qrot, qpl, kT, v, kcT, vc, ga, _bias_table(rpb[0]))
    return _out_proj(x, attn, conv, ada3, w_out[0].astype(BF16), tm=512)
```

```python
import functools

import numpy as np
import jax
import jax.numpy as jnp
from jax import lax
from jax.experimental import pallas as pl
from jax.experimental.pallas import tpu as pltpu

F32 = jnp.float32
BF16 = jnp.bfloat16

HEAD_DIM = 64
GRID_W = 64
WIN_H = 8
WIN_W = 16
CONV_K = 3
ROPE_THETA = 10000.0
RMS_EPS = 1e-6
MASK_VALUE = -1e30
LOG2_E = 1.4426950408889634

LANES = 128
BF16_ROWS = 16
HALO = BF16_ROWS
VMEM_LIMIT = 56 * 1024 * 1024


def _silu(z):
    return z * jax.nn.sigmoid(z)


def _adaln_kernel(cond_ref, w_ref, b_ref, o_ref):
    a = _silu(cond_ref[...]).astype(BF16)
    o_ref[...] = jnp.dot(a, w_ref[...].astype(BF16), preferred_element_type=F32) + b_ref[...]


def _adaln(cond, w_ada, b_ada):
    rows, d = cond.shape
    n = w_ada.shape[1]
    tn = 512
    return pl.pallas_call(
        _adaln_kernel,
        grid=(n // tn,),
        in_specs=[pl.BlockSpec((rows, d), lambda i: (0, 0)),
                  pl.BlockSpec((d, tn), lambda i: (0, i)),
                  pl.BlockSpec((1, tn), lambda i: (0, i))],
        out_specs=pl.BlockSpec((rows, tn), lambda i: (0, i)),
        out_shape=jax.ShapeDtypeStruct((rows, n), F32),
        compiler_params=pltpu.CompilerParams(vmem_limit_bytes=VMEM_LIMIT),
        name="adaln",
    )(cond, w_ada, b_ada.reshape(1, n))


def _modulated_norm(xt, mult, shift):
    ms = jnp.mean(xt * xt, axis=-1, keepdims=True)
    return (xt * lax.rsqrt(ms + RMS_EPS) * mult + shift).astype(BF16)


def _head_norm(t, gain, bd):
    t2 = t * t
    hi = t2.astype(BF16)
    lo = (t2 - hi.astype(F32)).astype(BF16)
    ss = (jnp.dot(hi, bd, preferred_element_type=F32) + jnp.dot(lo, bd, preferred_element_type=F32))
    return t * lax.rsqrt(ss * (1.0 / HEAD_DIM) + RMS_EPS) * gain


def _rope(t, cos, sin_lo, sin_hi):
    outs = []
    for c in range(t.shape[1] // LANES):
        tc = t[:, c * LANES:(c + 1) * LANES]
        up = pltpu.roll(tc, LANES - HEAD_DIM // 4, 1)
        dn = pltpu.roll(tc, HEAD_DIM // 4, 1)
        outs.append(tc * cos + up * sin_lo + dn * sin_hi)
    return jnp.concatenate(outs, axis=1)


def _ctx_kernel(ctx_ref, ada_ref, ng_ref, w_ref, kg_ref, bd_ref, kcT_ref, vc_ref, *, d_attn):
    shift = ada_ref[0, 0:1, :]
    scale = ada_ref[0, 1:2, :]
    mult = ng_ref[...] * (1.0 + scale)
    h = _modulated_norm(ctx_ref[0], mult, shift)
    kf = jnp.dot(h, w_ref[:, 0:d_attn], preferred_element_type=F32)
    kn = _head_norm(kf, kg_ref[...], bd_ref[...])
    kcT_ref[0] = kn.T.astype(BF16)
    vc_ref[0] = jnp.dot(h, w_ref[:, d_attn:2 * d_attn], preferred_element_type=F32).astype(BF16)


def _ctx_kv(ctx, ada3, ctx_row, norm_g, w_kv, kg, bd):
    b, l, d = ctx.shape
    d_attn = w_kv.shape[1] // 2
    return pl.pallas_call(
        functools.partial(_ctx_kernel, d_attn=d_attn),
        grid=(b,),
        in_specs=[pl.BlockSpec((1, l, d), lambda i: (i, 0, 0)),
                  pl.BlockSpec((1, 3, d), lambda i: (ctx_row, 0, 0)),
                  pl.BlockSpec((1, d), lambda i: (0, 0)),
                  pl.BlockSpec((d, 2 * d_attn), lambda i: (0, 0)),
                  pl.BlockSpec((1, d_attn), lambda i: (0, 0)),
                  pl.BlockSpec((d_attn, d_attn), lambda i: (0, 0))],
        out_specs=[pl.BlockSpec((1, d_attn, l), lambda i: (i, 0, 0)),
                   pl.BlockSpec((1, l, d_attn), lambda i: (i, 0, 0))],
        out_shape=[jax.ShapeDtypeStruct((b, d_attn, l), BF16),
                   jax.ShapeDtypeStruct((b, l, d_attn), BF16)],
        compiler_params=pltpu.CompilerParams(vmem_limit_bytes=VMEM_LIMIT),
        name="ctx_kv",
    )(ctx, ada3, norm_g, w_kv, kg, bd)


def _in_proj_kernel(x_ref, xp_ref, xn_ref, ada_ref, ng_ref, w_ref, qg_ref, kg_ref,
                    cos_ref, slo_ref, shi_ref, bd_ref, cw_ref, cb_ref,
                    qrot_ref, qpl_ref, kT_ref, v_ref, ga_ref, conv_ref,
                    hext_ref, cu_ref, *, tm, da, dc):
    j = pl.program_id(1)
    nj = pl.num_programs(1)
    shift = ada_ref[0, 0:1, :]
    scale = ada_ref[0, 1:2, :]
    mult = ng_ref[...] * (1.0 + scale)

    hext_ref[0:HALO, :] = _modulated_norm(xp_ref[0], mult, shift)
    hext_ref[HALO:HALO + tm, :] = _modulated_norm(x_ref[0], mult, shift)
    hext_ref[HALO + tm:, :] = _modulated_norm(xn_ref[0], mult, shift)
    h = hext_ref[HALO:HALO + tm, :]

    def proj(src, lo, width):
        return jnp.dot(src, w_ref[:, lo:lo + width], preferred_element_type=F32)

    bd = bd_ref[...]
    cos, slo, shi = cos_ref[...], slo_ref[...], shi_ref[...]

    qn = _head_norm(proj(h, 0, da), qg_ref[...], bd)
    qpl_ref[0] = qn.astype(BF16)
    qrot_ref[0] = _rope(qn, cos, slo, shi).astype(BF16)

    kn = _head_norm(proj(h, da, da), kg_ref[...], bd)
    kT_ref[0] = _rope(kn, cos, slo, shi).T.astype(BF16)

    v_ref[0] = proj(h, 2 * da, da).astype(BF16)
    ga_ref[0] = _silu(proj(h, 3 * da, da)).astype(BF16)

    hx = hext_ref[...]
    cu = proj(hx, 4 * da, dc) * proj(hx, 4 * da + 2 * dc, dc)
    row = lax.broadcasted_iota(jnp.int32, (tm + 2 * HALO, 1), 0)
    inside = ((row >= HALO) | (j > 0)) & ((row < HALO + tm) | (j < nj - 1))
    cu_ref[...] = jnp.where(inside, cu, 0.0)
    y = (cb_ref[...]
         + cw_ref[0:1, :] * cu_ref[HALO - 1:HALO - 1 + tm, :]
         + cw_ref[1:2, :] * cu_ref[HALO:HALO + tm, :]
         + cw_ref[2:3, :] * cu_ref[HALO + 1:HALO + 1 + tm, :])
    bg = proj(h, 4 * da + dc, dc)
    zc = proj(h, 4 * da + 3 * dc, dc)
    conv_ref[0] = (bg * y * _silu(zc)).astype(BF16)


def _in_proj(x, ada3, norm_g, w_in, qg, kg, cos, slo, shi, bd, conv_w, conv_b, *, tm):
    b, s, d = x.shape
    dc = conv_w.shape[1]
    da = (w_in.shape[1] - 4 * dc) // 4
    nh = tm // HALO
    last_halo = s // HALO - 1
    tok = lambda width: pl.BlockSpec((1, tm, width), lambda i, j: (i, j, 0))
    const = lambda shape: pl.BlockSpec(shape, lambda i, j: (0,) * len(shape))
    tab = pl.BlockSpec((tm, LANES), lambda i, j: (j, 0))
    out_tok = jax.ShapeDtypeStruct((b, s, da), BF16)
    return pl.pallas_call(
        functools.partial(_in_proj_kernel, tm=tm, da=da, dc=dc),
        grid=(b, s // tm),
        in_specs=[tok(d),
                  pl.BlockSpec((1, HALO, d), lambda i, j: (i, jnp.maximum(j * nh - 1, 0), 0)),
                  pl.BlockSpec((1, HALO, d), lambda i, j: (i, jnp.minimum((j + 1) * nh, last_halo), 0)),
                  pl.BlockSpec((1, 3, d), lambda i, j: (i, 0, 0)),
                  const((1, d)), const(w_in.shape), const((1, da)), const((1, da)),
                  tab, tab, tab, const((da, da)), const((CONV_K, dc)), const((1, dc))],
        out_specs=[tok(da), tok(da),
                   pl.BlockSpec((1, da, tm), lambda i, j: (i, 0, j)),
                   tok(da), tok(da), tok(dc)],
        out_shape=[out_tok, out_tok, jax.ShapeDtypeStruct((b, da, s), BF16), out_tok, out_tok,
                   jax.ShapeDtypeStruct((b, s, dc), BF16)],
        scratch_shapes=[pltpu.VMEM((tm + 2 * HALO, d), BF16),
                        pltpu.VMEM((tm + 2 * HALO, dc), F32)],
        compiler_params=pltpu.CompilerParams(vmem_limit_bytes=VMEM_LIMIT),
        name="in_proj",
    )(x, x, x, ada3, norm_g, w_in, qg, kg, cos, slo, shi, bd, conv_w, conv_b)


def _attn_kernel(qrot_ref, qpl_ref, kT_ref, v_ref, kcT_ref, vc_ref, ga_ref, bias_ref, o_ref,
                 kpair_ref, s_lat_ref, s_ctx_ref, p_lat_ref, p_ctx_ref, rl_ref, *, rows, group):
    n_pairs = rows // 2
    n_units = rows // group
    for p in range(n_pairs):
        kpair_ref[p] = kT_ref[0, :, p * LANES:(p + 1) * LANES]
    for p in range(n_pairs - 1):
        kpair_ref[n_pairs + p] = jnp.concatenate(
            [kT_ref[0, :, p * LANES + GRID_W:(p + 1) * LANES],
             kT_ref[0, :, (p + 1) * LANES:(p + 1) * LANES + GRID_W]], axis=1)
    kpair_ref[2 * n_pairs - 1] = jnp.zeros((LANES, LANES), BF16)

    lane = lax.broadcasted_iota(jnp.int32, (GRID_W, LANES), 1)
    first_head = lane < HEAD_DIM

    def stack_heads(q2):
        zero = jnp.zeros_like(q2)
        return jnp.concatenate([jnp.where(first_head, q2, zero), jnp.where(first_head, zero, q2)], axis=0)

    def geometry(u, g):
        i = u * group + g
        if isinstance(i, int):
            rs = min(max(i - WIN_H // 2, 0), rows - WIN_H)
            return i, rs, i * GRID_W, rs * GRID_W
        rs = jnp.clip(i - WIN_H // 2, 0, rows - WIN_H)
        return i, rs, pl.multiple_of(i * GRID_W, GRID_W), pl.multiple_of(rs * GRID_W, GRID_W)


    def scores_matmul(u):
        out = []
        for g in range(group):
            i, rs, tok0, _ = geometry(u, g)
            par = rs & 1
            p0 = (rs - par) // 2 + par * n_pairs
            qs = stack_heads(qrot_ref[0, pl.ds(tok0, GRID_W), :])
            band = jnp.concatenate([kpair_ref[p0 + t] for t in range(WIN_H // 2)], axis=1)
            qp = stack_heads(qpl_ref[0, pl.ds(tok0, GRID_W), :])
            out.append((jnp.dot(qs, band, preferred_element_type=F32), i - rs,
                        jnp.dot(qp, kcT_ref[0], preferred_element_type=F32)))
        return out

    def scores_store(u, vals):
        slot = u & 1
        for g, (s_lat, off, s_ctx) in enumerate(vals):
            s_lat_ref[slot, g] = s_lat + bias_ref[0, off]
            s_ctx_ref[slot, g] = s_ctx

    def softmax(u):
        slot = u & 1
        for g in range(group):
            s_lat = s_lat_ref[slot, g]
            s_ctx = s_ctx_ref[slot, g]
            m = jnp.maximum(jnp.max(s_lat, axis=-1, keepdims=True), jnp.max(s_ctx, axis=-1, keepdims=True))
            e_lat = jnp.exp2(s_lat - m)
            e_ctx = jnp.exp2(s_ctx - m)
            denom = jnp.sum(e_lat, axis=-1, keepdims=True) + jnp.sum(e_ctx, axis=-1, keepdims=True)
            p_lat_ref[slot, g] = e_lat.astype(BF16)
            p_ctx_ref[slot, g] = e_ctx.astype(BF16)
            rl_ref[slot, g] = jnp.broadcast_to(1.0 / denom, (2 * GRID_W, LANES))

    def pv_matmul(u):
        slot = u & 1
        out = []
        for g in range(group):
            _, _, _, key0 = geometry(u, g)
            vband = v_ref[0, pl.ds(key0, WIN_H * GRID_W), :]
            o = (jnp.dot(p_lat_ref[slot, g], vband, preferred_element_type=F32)
                 + jnp.dot(p_ctx_ref[slot, g], vc_ref[0], preferred_element_type=F32))
            out.append((o, rl_ref[slot, g]))
        return out

    def pv_store(u, vals):
        for g, (o, rl) in enumerate(vals):
            _, _, tok0, _ = geometry(u, g)
            o = o * rl
            o2 = jnp.where(first_head, o[0:GRID_W], o[GRID_W:2 * GRID_W])
            gate = ga_ref[0, pl.ds(tok0, GRID_W), :].astype(F32)
            o_ref[0, pl.ds(tok0, GRID_W), :] = (o2 * gate).astype(BF16)

    scores_store(0, scores_matmul(0))
    sc = scores_matmul(1)
    softmax(0)
    scores_store(1, sc)

    def steady(t, carry):
        pv = pv_matmul(t - 2)
        sc = scores_matmul(t)
        softmax(t - 1)
        scores_store(t, sc)
        pv_store(t - 2, pv)
        return carry

    lax.fori_loop(2, n_units, steady, 0)

    pv = pv_matmul(n_units - 2)
    softmax(n_units - 1)
    pv_store(n_units - 2, pv)
    pv_store(n_units - 1, pv_matmul(n_units - 1))


def _attention(qrot, qpl, kT, v, kcT, vc, ga, bias, *, group):
    b, s, da = qrot.shape
    l = vc.shape[1]
    rows = s // GRID_W
    n_hp = da // LANES
    nq = 2 * GRID_W
    nk = WIN_H * GRID_W
    assert rows % group == 0 and rows // group >= 2
    tok = pl.BlockSpec((1, s, LANES), lambda hp, i: (i, 0, hp))
    return pl.pallas_call(
        functools.partial(_attn_kernel, rows=rows, group=group),
        grid=(n_hp, b),
        in_specs=[tok, tok,
                  pl.BlockSpec((1, LANES, s), lambda hp, i: (i, hp, 0)),
                  tok,
                  pl.BlockSpec((1, LANES, l), lambda hp, i: (i, hp, 0)),
                  pl.BlockSpec((1, l, LANES), lambda hp, i: (i, 0, hp)),
                  tok,
                  pl.BlockSpec((1,) + bias.shape[1:], lambda hp, i: (hp, 0, 0, 0))],
        out_specs=tok,
        out_shape=jax.ShapeDtypeStruct((b, s, da), BF16),
        scratch_shapes=[pltpu.VMEM((rows, LANES, LANES), BF16),
                        pltpu.VMEM((2, group, nq, nk), F32),
                        pltpu.VMEM((2, group, nq, l), F32),
                        pltpu.VMEM((2, group, nq, nk), BF16),
                        pltpu.VMEM((2, group, nq, l), BF16),
                        pltpu.VMEM((2, group, nq, LANES), F32)],
        compiler_params=pltpu.CompilerParams(vmem_limit_bytes=VMEM_LIMIT),
        name="attn",
    )(qrot, qpl, kT, v, kcT, vc, ga, bias)


def _out_proj_kernel(x_ref, a_ref, c_ref, ada_ref, w_ref, o_ref, *, da):
    gate = ada_ref[0, 2:3, :]
    upd = (jnp.dot(a_ref[0], w_ref[0:da, :], preferred_element_type=F32)
           + jnp.dot(c_ref[0], w_ref[da:, :], preferred_element_type=F32))
    o_ref[0] = x_ref[0] + gate * upd


def _out_proj(x, attn, conv, ada3, w_out, *, tm):
    b, s, d = x.shape
    da = attn.shape[2]
    dc = conv.shape[2]
    return pl.pallas_call(
        functools.partial(_out_proj_kernel, da=da),
        grid=(b, s // tm),
        in_specs=[pl.BlockSpec((1, tm, d), lambda i, j: (i, j, 0)),
                  pl.BlockSpec((1, tm, da), lambda i, j: (i, j, 0)),
                  pl.BlockSpec((1, tm, dc), lambda i, j: (i, j, 0)),
                  pl.BlockSpec((1, 3, d), lambda i, j: (i, 0, 0)),
                  pl.BlockSpec(w_out.shape, lambda i, j: (0, 0))],
        out_specs=pl.BlockSpec((1, tm, d), lambda i, j: (i, j, 0)),
        out_shape=jax.ShapeDtypeStruct((b, s, d), F32),
        compiler_params=pltpu.CompilerParams(vmem_limit_bytes=VMEM_LIMIT),
        name="out_proj",
    )(x, attn, conv, ada3, w_out)


def _rope_tables(s):
    nf = HEAD_DIM // 4
    inv = (ROPE_THETA ** (-np.arange(nf, dtype=np.float32) / nf)).astype(np.float32)
    pos = np.arange(s)
    lane = np.arange(LANES)
    d = lane % HEAD_DIM
    axis = d // (2 * nf)
    half = (d % (2 * nf)) // nf
    coord = np.where(axis[None, :] == 0, (pos // GRID_W)[:, None], (pos % GRID_W)[:, None]).astype(np.float32)
    ang = (coord * inv[d % nf][None, :]).astype(np.float32)
    cos = np.cos(ang).astype(np.float32)
    sin = np.sin(ang).astype(np.float32)
    sin_lo = np.where(half[None, :] == 0, -sin, 0.0).astype(np.float32)
    sin_hi = np.where(half[None, :] == 1, sin, 0.0).astype(np.float32)
    return jnp.asarray(cos), jnp.asarray(sin_lo), jnp.asarray(sin_hi)


def _bias_kernel(rpb_ref, o_ref):
    off = pl.program_id(1)
    cq = lax.broadcasted_iota(jnp.int32, (GRID_W, LANES), 0)
    ck = lax.broadcasted_iota(jnp.int32, (GRID_W, LANES), 1) % GRID_W
    col_start = jnp.clip(cq - WIN_W // 2, 0, GRID_W - WIN_W)
    valid = (ck >= col_start) & (ck < col_start + WIN_W)
    for a in range(2):
        for t in range(WIN_H // 2):
            dr0 = 2 * t - off + (WIN_H - 1)
            two = rpb_ref[a, pl.ds(dr0, 2), :]
            lanes = jnp.broadcast_to(jnp.concatenate([two[0:1], two[1:2]], axis=1), (GRID_W, LANES))
            toeplitz = pltpu.roll(lanes, LANES - (WIN_W - 1), 1, stride=1, stride_axis=0)
            o_ref[0, 0, a * GRID_W:(a + 1) * GRID_W, t * LANES:(t + 1) * LANES] = jnp.where(
                valid, toeplitz * LOG2_E, MASK_VALUE)


def _bias_table(rpb):
    nh, ndr, ndc = rpb.shape
    assert ndr == 2 * WIN_H - 1 and ndc == 2 * WIN_W - 1 and ndc <= GRID_W
    rpb_pad = jnp.pad(rpb, ((0, 0), (0, 0), (0, GRID_W - ndc)))
    return pl.pallas_call(
        _bias_kernel,
        grid=(nh // 2, WIN_H),
        in_specs=[pl.BlockSpec((2, ndr, GRID_W), lambda hp, off: (hp, 0, 0))],
        out_specs=pl.BlockSpec((1, 1, 2 * GRID_W, WIN_H * GRID_W), lambda hp, off: (hp, off, 0, 0)),
        out_shape=jax.ShapeDtypeStruct((nh // 2, WIN_H, 2 * GRID_W, WIN_H * GRID_W), F32),
        name="bias_table",
    )(rpb_pad)


def kernel(x, c, ctx, c_ctx, w_ada, b_ada, norm_g, w_in, q_norm_g, k_norm_g, rpb, conv_w, conv_b, w_out):
    depth = w_ada.shape[0]
    b, s, d = x.shape
    dc = conv_w.shape[2]
    da = (w_in.shape[2] - 4 * dc) // 4
    n_heads = da // HEAD_DIM
    rows = s // GRID_W
    assert depth == 1 and s % GRID_W == 0 and rows >= WIN_H and da % LANES == 0

    cos, slo, shi = _rope_tables(s)
    seg = np.arange(da) // HEAD_DIM
    bd = jnp.asarray((seg[:, None] == seg[None, :]).astype(np.float32), dtype=BF16)

    cond_rows = -(-(b + 1) // 8) * 8
    cond = jnp.zeros((cond_rows, d), F32).at[:b].set(c).at[b].set(c_ctx)
    ada3 = _adaln(cond, w_ada[0], b_ada[0]).reshape(cond_rows, 3, d)

    w_in_b = w_in[0].astype(BF16)
    ng = norm_g[0].reshape(1, d)
    qg = jnp.tile(q_norm_g[0] * (HEAD_DIM ** -0.5 * LOG2_E), n_heads).reshape(1, da)
    kg = jnp.tile(k_norm_g[0], n_heads).reshape(1, da)

    kcT, vc = _ctx_kv(ctx, ada3, b, ng, w_in_b[:, da:3 * da], kg, bd)
    qrot, qpl, kT, v, ga, conv = _in_proj(x, ada3, ng, w_in_b, qg, kg, cos, slo, shi, bd,
                                          conv_w[0], conv_b[0].reshape(1, dc), tm=512)
    attn = _attention(qrot, qpl, kT, v, kcT, vc, ga, _bias_table(rpb[0]), group=2)
    return _out_proj(x, attn, conv, ada3, w_out[0].astype(BF16), tm=512)
```

```python
import functools

import numpy as np
import jax
import jax.numpy as jnp
from jax import lax
from jax.experimental import pallas as pl
from jax.experimental.pallas import tpu as pltpu

F32 = jnp.float32
BF16 = jnp.bfloat16

HEAD_DIM = 64
GRID_W = 64
WIN_H = 8
WIN_W = 16
CONV_K = 3
ROPE_THETA = 10000.0
RMS_EPS = 1e-6
MASK_VALUE = -1e30
LOG2_E = 1.4426950408889634

LANES = 128
BF16_ROWS = 16
HALO = BF16_ROWS
VMEM_LIMIT = 56 * 1024 * 1024


def _silu(z):
    return z * jax.nn.sigmoid(z)


def _adaln_kernel(cond_ref, w_ref, b_ref, o_ref):
    a = _silu(cond_ref[...]).astype(BF16)
    o_ref[...] = jnp.dot(a, w_ref[...].astype(BF16), preferred_element_type=F32) + b_ref[...]


def _adaln(cond, w_ada, b_ada):
    rows, d = cond.shape
    n = w_ada.shape[1]
    tn = 512
    return pl.pallas_call(
        _adaln_kernel,
        grid=(n // tn,),
        in_specs=[pl.BlockSpec((rows, d), lambda i: (0, 0)),
                  pl.BlockSpec((d, tn), lambda i: (0, i)),
                  pl.BlockSpec((1, tn), lambda i: (0, i))],
        out_specs=pl.BlockSpec((rows, tn), lambda i: (0, i)),
        out_shape=jax.ShapeDtypeStruct((rows, n), F32),
        compiler_params=pltpu.CompilerParams(vmem_limit_bytes=VMEM_LIMIT),
        name="adaln",
    )(cond, w_ada, b_ada.reshape(1, n))


def _modulated_norm(xt, mult, shift):
    ms = jnp.mean(xt * xt, axis=-1, keepdims=True)
    return (xt * lax.rsqrt(ms + RMS_EPS) * mult + shift).astype(BF16)


def _head_norm(t, gain, bd):
    ss = jnp.dot((t * t).astype(BF16), bd, preferred_element_type=F32)
    return t * lax.rsqrt(ss * (1.0 / HEAD_DIM) + RMS_EPS) * gain


def _rope(t, cos, sin_lo, sin_hi):
    outs = []
    for c in range(t.shape[1] // LANES):
        tc = t[:, c * LANES:(c + 1) * LANES]
        up = pltpu.roll(tc, LANES - HEAD_DIM // 4, 1)
        dn = pltpu.roll(tc, HEAD_DIM // 4, 1)
        outs.append(tc * cos + up * sin_lo + dn * sin_hi)
    return jnp.concatenate(outs, axis=1)


def _ctx_kernel(ctx_ref, ada_ref, ng_ref, w_ref, kg_ref, bd_ref, kc_ref, vc_ref, *, d_attn):
    shift = ada_ref[0, 0:1, :]
    scale = ada_ref[0, 1:2, :]
    mult = ng_ref[...] * (1.0 + scale)
    h = _modulated_norm(ctx_ref[0], mult, shift)
    kf = jnp.dot(h, w_ref[:, 0:d_attn], preferred_element_type=F32)
    kn = _head_norm(kf, kg_ref[...], bd_ref[...])
    kc_ref[0] = kn.astype(BF16)
    vc_ref[0] = jnp.dot(h, w_ref[:, d_attn:2 * d_attn], preferred_element_type=F32).astype(BF16)


def _ctx_kv(ctx, ada3, ctx_row, norm_g, w_kv, kg, bd):
    b, l, d = ctx.shape
    d_attn = w_kv.shape[1] // 2
    return pl.pallas_call(
        functools.partial(_ctx_kernel, d_attn=d_attn),
        grid=(b,),
        in_specs=[pl.BlockSpec((1, l, d), lambda i: (i, 0, 0)),
                  pl.BlockSpec((1, 3, d), lambda i: (ctx_row, 0, 0)),
                  pl.BlockSpec((1, d), lambda i: (0, 0)),
                  pl.BlockSpec((d, 2 * d_attn), lambda i: (0, 0)),
                  pl.BlockSpec((1, d_attn), lambda i: (0, 0)),
                  pl.BlockSpec((d_attn, d_attn), lambda i: (0, 0))],
        out_specs=[pl.BlockSpec((1, l, d_attn), lambda i: (i, 0, 0)),
                   pl.BlockSpec((1, l, d_attn), lambda i: (i, 0, 0))],
        out_shape=[jax.ShapeDtypeStruct((b, l, d_attn), BF16),
                   jax.ShapeDtypeStruct((b, l, d_attn), BF16)],
        compiler_params=pltpu.CompilerParams(vmem_limit_bytes=VMEM_LIMIT),
        name="ctx_kv",
    )(ctx, ada3, norm_g, w_kv, kg, bd)


def _in_proj_kernel(x_ref, xp_ref, xn_ref, ada_ref, ng_ref, w_ref, qg_ref, kg_ref,
                    cos_ref, slo_ref, shi_ref, bd_ref, cw_ref, cb_ref,
                    qrot_ref, qpl_ref, k_ref, v_ref, ga_ref, conv_ref,
                    hext_ref, cu_ref, *, tm, da, dc):
    j = pl.program_id(1)
    nj = pl.num_programs(1)
    shift = ada_ref[0, 0:1, :]
    scale = ada_ref[0, 1:2, :]
    mult = ng_ref[...] * (1.0 + scale)

    hext_ref[0:HALO, :] = _modulated_norm(xp_ref[0], mult, shift)
    hext_ref[HALO:HALO + tm, :] = _modulated_norm(x_ref[0], mult, shift)
    hext_ref[HALO + tm:, :] = _modulated_norm(xn_ref[0], mult, shift)
    h = hext_ref[HALO:HALO + tm, :]

    def proj(src, lo, width):
        return jnp.dot(src, w_ref[:, lo:lo + width], preferred_element_type=F32)

    bd = bd_ref[...]
    cos, slo, shi = cos_ref[...], slo_ref[...], shi_ref[...]

    qn = _head_norm(proj(h, 0, da), qg_ref[...], bd)
    qpl_ref[0] = qn.astype(BF16)
    qrot_ref[0] = _rope(qn, cos, slo, shi).astype(BF16)

    kn = _head_norm(proj(h, da, da), kg_ref[...], bd)
    k_ref[0] = _rope(kn, cos, slo, shi).astype(BF16)

    v_ref[0] = proj(h, 2 * da, da).astype(BF16)
    ga_ref[0] = _silu(proj(h, 3 * da, da)).astype(BF16)

    hx = hext_ref[...]
    cu = proj(hx, 4 * da, dc) * proj(hx, 4 * da + 2 * dc, dc)
    row = lax.broadcasted_iota(jnp.int32, (tm + 2 * HALO, 1), 0)
    inside = ((row >= HALO) | (j > 0)) & ((row < HALO + tm) | (j < nj - 1))
    cu_ref[...] = jnp.where(inside, cu, 0.0)
    y = (cb_ref[...]
         + cw_ref[0:1, :] * cu_ref[HALO - 1:HALO - 1 + tm, :]
         + cw_ref[1:2, :] * cu_ref[HALO:HALO + tm, :]
         + cw_ref[2:3, :] * cu_ref[HALO + 1:HALO + 1 + tm, :])
    bg = proj(h, 4 * da + dc, dc)
    zc = proj(h, 4 * da + 3 * dc, dc)
    conv_ref[0] = (bg * y * _silu(zc)).astype(BF16)


def _in_proj(x, ada3, norm_g, w_in, qg, kg, cos, slo, shi, bd, conv_w, conv_b, *, tm):
    b, s, d = x.shape
    dc = conv_w.shape[1]
    da = (w_in.shape[1] - 4 * dc) // 4
    nh = tm // HALO
    last_halo = s // HALO - 1
    tok = lambda width: pl.BlockSpec((1, tm, width), lambda i, j: (i, j, 0))
    const = lambda shape: pl.BlockSpec(shape, lambda i, j: (0,) * len(shape))
    tab = pl.BlockSpec((tm, LANES), lambda i, j: (j, 0))
    out_tok = jax.ShapeDtypeStruct((b, s, da), BF16)
    return pl.pallas_call(
        functools.partial(_in_proj_kernel, tm=tm, da=da, dc=dc),
        grid=(b, s // tm),
        in_specs=[tok(d),
                  pl.BlockSpec((1, HALO, d), lambda i, j: (i, jnp.maximum(j * nh - 1, 0), 0)),
                  pl.BlockSpec((1, HALO, d), lambda i, j: (i, jnp.minimum((j + 1) * nh, last_halo), 0)),
                  pl.BlockSpec((1, 3, d), lambda i, j: (i, 0, 0)),
                  const((1, d)), const(w_in.shape), const((1, da)), const((1, da)),
                  tab, tab, tab, const((da, da)), const((CONV_K, dc)), const((1, dc))],
        out_specs=[tok(da), tok(da), tok(da), tok(da), tok(da), tok(dc)],
        out_shape=[out_tok, out_tok, out_tok, out_tok, out_tok,
                   jax.ShapeDtypeStruct((b, s, dc), BF16)],
        scratch_shapes=[pltpu.VMEM((tm + 2 * HALO, d), BF16),
                        pltpu.VMEM((tm + 2 * HALO, dc), F32)],
        compiler_params=pltpu.CompilerParams(vmem_limit_bytes=VMEM_LIMIT),
        name="in_proj",
    )(x, x, x, ada3, norm_g, w_in, qg, kg, cos, slo, shi, bd, conv_w, conv_b)


def _dot_nt(a, b):
    return lax.dot_general(a, b, (((1,), (1,)), ((), ())), preferred_element_type=F32)


def _attn_kernel(qrot_ref, qpl_ref, k_ref, v_ref, kc_ref, vc_ref, ga_ref, bias_ref, o_ref,
                 s_lat_ref, s_ctx_ref, p_lat_ref, p_ctx_ref, rl_ref, *, rows, group):
    n_units = rows // group

    lane = lax.broadcasted_iota(jnp.int32, (GRID_W, LANES), 1)
    first_head = lane < HEAD_DIM

    def stack_heads(q2):
        zero = jnp.zeros_like(q2)
        return jnp.concatenate([jnp.where(first_head, q2, zero), jnp.where(first_head, zero, q2)], axis=0)

    def geometry(u, g):
        i = u * group + g
        if isinstance(i, int):
            rs = min(max(i - WIN_H // 2, 0), rows - WIN_H)
            return i, rs, i * GRID_W, rs * GRID_W
        rs = jnp.clip(i - WIN_H // 2, 0, rows - WIN_H)
        return i, rs, pl.multiple_of(i * GRID_W, GRID_W), pl.multiple_of(rs * GRID_W, GRID_W)


    def scores_matmul(u):
        out = []
        for g in range(group):
            i, rs, tok0, key0 = geometry(u, g)
            qs = stack_heads(qrot_ref[0, pl.ds(tok0, GRID_W), :])
            kband = k_ref[0, pl.ds(key0, WIN_H * GRID_W), :]
            qp = stack_heads(qpl_ref[0, pl.ds(tok0, GRID_W), :])
            out.append((_dot_nt(qs, kband), i - rs, _dot_nt(qp, kc_ref[0])))
        return out

    def scores_store(u, vals):
        slot = u & 1
        for g, (s_lat, off, s_ctx) in enumerate(vals):
            s_lat_ref[slot, g] = s_lat + bias_ref[0, off]
            s_ctx_ref[slot, g] = s_ctx

    def softmax(u):
        slot = u & 1
        for g in range(group):
            s_lat = s_lat_ref[slot, g]
            s_ctx = s_ctx_ref[slot, g]
            m = jnp.maximum(jnp.max(s_lat, axis=-1, keepdims=True), jnp.max(s_ctx, axis=-1, keepdims=True))
            e_lat = jnp.exp2(s_lat - m)
            e_ctx = jnp.exp2(s_ctx - m)
            denom = jnp.sum(e_lat, axis=-1, keepdims=True) + jnp.sum(e_ctx, axis=-1, keepdims=True)
            p_lat_ref[slot, g] = e_lat.astype(BF16)
            p_ctx_ref[slot, g] = e_ctx.astype(BF16)
            rl_ref[slot, g] = jnp.broadcast_to(1.0 / denom, (2 * GRID_W, LANES))

    def pv_matmul(u):
        slot = u & 1
        out = []
        for g in range(group):
            _, _, _, key0 = geometry(u, g)
            vband = v_ref[0, pl.ds(key0, WIN_H * GRID_W), :]
            o = (jnp.dot(p_lat_ref[slot, g], vband, preferred_element_type=F32)
                 + jnp.dot(p_ctx_ref[slot, g], vc_ref[0], preferred_element_type=F32))
            out.append((o, rl_ref[slot, g]))
        return out

    def pv_store(u, vals):
        for g, (o, rl) in enumerate(vals):
            _, _, tok0, _ = geometry(u, g)
            o = o * rl
            o2 = jnp.where(first_head, o[0:GRID_W], o[GRID_W:2 * GRID_W])
            gate = ga_ref[0, pl.ds(tok0, GRID_W), :].astype(F32)
            o_ref[0, pl.ds(tok0, GRID_W), :] = (o2 * gate).astype(BF16)

    scores_store(0, scores_matmul(0))
    sc = scores_matmul(1)
    softmax(0)
    scores_store(1, sc)

    def steady(t, carry):
        pv = pv_matmul(t - 2)
        sc = scores_matmul(t)
        softmax(t - 1)
        scores_store(t, sc)
        pv_store(t - 2, pv)
        return carry

    lax.fori_loop(2, n_units, steady, 0)

    pv = pv_matmul(n_units - 2)
    softmax(n_units - 1)
    pv_store(n_units - 2, pv)
    pv_store(n_units - 1, pv_matmul(n_units - 1))


def _attention(qrot, qpl, k, v, kc, vc, ga, bias, *, group):
    b, s, da = qrot.shape
    l = vc.shape[1]
    rows = s // GRID_W
    n_hp = da // LANES
    nq = 2 * GRID_W
    nk = WIN_H * GRID_W
    assert rows % group == 0 and rows // group >= 2
    tok = pl.BlockSpec((1, s, LANES), lambda hp, i: (i, 0, hp))
    ctx_tok = pl.BlockSpec((1, l, LANES), lambda hp, i: (i, 0, hp))
    return pl.pallas_call(
        functools.partial(_attn_kernel, rows=rows, group=group),
        grid=(n_hp, b),
        in_specs=[tok, tok, tok, tok, ctx_tok, ctx_tok, tok,
                  pl.BlockSpec((1,) + bias.shape[1:], lambda hp, i: (hp, 0, 0, 0))],
        out_specs=tok,
        out_shape=jax.ShapeDtypeStruct((b, s, da), BF16),
        scratch_shapes=[pltpu.VMEM((2, group, nq, nk), F32),
                        pltpu.VMEM((2, group, nq, l), F32),
                        pltpu.VMEM((2, group, nq, nk), BF16),
                        pltpu.VMEM((2, group, nq, l), BF16),
                        pltpu.VMEM((2, group, nq, LANES), F32)],
        compiler_params=pltpu.CompilerParams(vmem_limit_bytes=VMEM_LIMIT),
        name="attn",
    )(qrot, qpl, k, v, kc, vc, ga, bias)


def _out_proj_kernel(x_ref, a_ref, c_ref, ada_ref, w_ref, o_ref, *, da):
    gate = ada_ref[0, 2:3, :]
    upd = (jnp.dot(a_ref[0], w_ref[0:da, :], preferred_element_type=F32)
           + jnp.dot(c_ref[0], w_ref[da:, :], preferred_element_type=F32))
    o_ref[0] = x_ref[0] + gate * upd


def _out_proj(x, attn, conv, ada3, w_out, *, tm):
    b, s, d = x.shape
    da = attn.shape[2]
    dc = conv.shape[2]
    return pl.pallas_call(
        functools.partial(_out_proj_kernel, da=da),
        grid=(b, s // tm),
        in_specs=[pl.BlockSpec((1, tm, d), lambda i, j: (i, j, 0)),
                  pl.BlockSpec((1, tm, da), lambda i, j: (i, j, 0)),
                  pl.BlockSpec((1, tm, dc), lambda i, j: (i, j, 0)),
                  pl.BlockSpec((1, 3, d), lambda i, j: (i, 0, 0)),
                  pl.BlockSpec(w_out.shape, lambda i, j: (0, 0))],
        out_specs=pl.BlockSpec((1, tm, d), lambda i, j: (i, j, 0)),
        out_shape=jax.ShapeDtypeStruct((b, s, d), F32),
        compiler_params=pltpu.CompilerParams(vmem_limit_bytes=VMEM_LIMIT),
        name="out_proj",
    )(x, attn, conv, ada3, w_out)


def _rope_tables(s):
    nf = HEAD_DIM // 4
    inv = (ROPE_THETA ** (-np.arange(nf, dtype=np.float32) / nf)).astype(np.float32)
    pos = np.arange(s)
    lane = np.arange(LANES)
    d = lane % HEAD_DIM
    axis = d // (2 * nf)
    half = (d % (2 * nf)) // nf
    coord = np.where(axis[None, :] == 0, (pos // GRID_W)[:, None], (pos % GRID_W)[:, None]).astype(np.float32)
    ang = (coord * inv[d % nf][None, :]).astype(np.float32)
    cos = np.cos(ang).astype(np.float32)
    sin = np.sin(ang).astype(np.float32)
    sin_lo = np.where(half[None, :] == 0, -sin, 0.0).astype(np.float32)
    sin_hi = np.where(half[None, :] == 1, sin, 0.0).astype(np.float32)
    return jnp.asarray(cos), jnp.asarray(sin_lo), jnp.asarray(sin_hi)


def _bias_kernel(rpb_ref, o_ref):
    off = pl.program_id(1)
    cq = lax.broadcasted_iota(jnp.int32, (GRID_W, LANES), 0)
    ck = lax.broadcasted_iota(jnp.int32, (GRID_W, LANES), 1) % GRID_W
    col_start = jnp.clip(cq - WIN_W // 2, 0, GRID_W - WIN_W)
    valid = (ck >= col_start) & (ck < col_start + WIN_W)
    for a in range(2):
        for t in range(WIN_H // 2):
            dr0 = 2 * t - off + (WIN_H - 1)
            two = rpb_ref[a, pl.ds(dr0, 2), :]
            lanes = jnp.broadcast_to(jnp.concatenate([two[0:1], two[1:2]], axis=1), (GRID_W, LANES))
            toeplitz = pltpu.roll(lanes, LANES - (WIN_W - 1), 1, stride=1, stride_axis=0)
            o_ref[0, 0, a * GRID_W:(a + 1) * GRID_W, t * LANES:(t + 1) * LANES] = jnp.where(
                valid, toeplitz * LOG2_E, MASK_VALUE)


def _bias_table(rpb):
    nh, ndr, ndc = rpb.shape
    assert ndr == 2 * WIN_H - 1 and ndc == 2 * WIN_W - 1 and ndc <= GRID_W
    rpb_pad = jnp.pad(rpb, ((0, 0), (0, 0), (0, GRID_W - ndc)))
    return pl.pallas_call(
        _bias_kernel,
        grid=(nh // 2, WIN_H),
        in_specs=[pl.BlockSpec((2, ndr, GRID_W), lambda hp, off: (hp, 0, 0))],
        out_specs=pl.BlockSpec((1, 1, 2 * GRID_W, WIN_H * GRID_W), lambda hp, off: (hp, off, 0, 0)),
        out_shape=jax.ShapeDtypeStruct((nh // 2, WIN_H, 2 * GRID_W, WIN_H * GRID_W), F32),
        name="bias_table",
    )(rpb_pad)


def kernel(x, c, ctx, c_ctx, w_ada, b_ada, norm_g, w_in, q_norm_g, k_norm_g, rpb, conv_w, conv_b, w_out):
    depth = w_ada.shape[0]
    b, s, d = x.shape
    dc = conv_w.shape[2]
    da = (w_in.shape[2] - 4 * dc) // 4
    n_heads = da // HEAD_DIM
    rows = s // GRID_W
    assert depth == 1 and s % GRID_W == 0 and rows >= WIN_H and da % LANES == 0

    cos, slo, shi = _rope_tables(s)
    seg = np.arange(da) // HEAD_DIM
    bd = jnp.asarray((seg[:, None] == seg[None, :]).astype(np.float32), dtype=BF16)

    cond_rows = -(-(b + 1) // 8) * 8
    cond = jnp.zeros((cond_rows, d), F32).at[:b].set(c).at[b].set(c_ctx)
    ada3 = _adaln(cond, w_ada[0], b_ada[0]).reshape(cond_rows, 3, d)

    w_in_b = w_in[0].astype(BF16)
    ng = norm_g[0].reshape(1, d)
    qg = jnp.tile(q_norm_g[0] * (HEAD_DIM ** -0.5 * LOG2_E), n_heads).reshape(1, da)
    kg = jnp.tile(k_norm_g[0], n_heads).reshape(1, da)

    kc, vc = _ctx_kv(ctx, ada3, b, ng, w_in_b[:, da:3 * da], kg, bd)
    qrot, qpl, k, v, ga, conv = _in_proj(x, ada3, ng, w_in_b, qg, kg, cos, slo, shi, bd,
                                         conv_w[0], conv_b[0].reshape(1, dc), tm=512)
    attn = _attention(qrot, qpl, k, v, kc, vc, ga, _bias_table(rpb[0]), group=4)
    return _out_proj(x, attn, conv, ada3, w_out[0].astype(BF16), tm=512)
```

```python
import functools

import numpy as np
import jax
import jax.numpy as jnp
from jax import lax
from jax.experimental import pallas as pl
from jax.experimental.pallas import tpu as pltpu

F32 = jnp.float32
BF16 = jnp.bfloat16

HEAD_DIM = 64
GRID_W = 64
WIN_H = 8
WIN_W = 16
CONV_K = 3
ROPE_THETA = 10000.0
RMS_EPS = 1e-6
MASK_VALUE = -1e30
LOG2_E = 1.4426950408889634

LANES = 128
BF16_ROWS = 16
HALO = BF16_ROWS
VMEM_LIMIT = 56 * 1024 * 1024


def _silu(z):
    return z * jax.nn.sigmoid(z)


def _adaln_kernel(cond_ref, w_ref, b_ref, o_ref):
    a = _silu(cond_ref[...]).astype(BF16)
    o_ref[...] = jnp.dot(a, w_ref[...].astype(BF16), preferred_element_type=F32) + b_ref[...]


def _adaln(cond, w_ada, b_ada):
    rows, d = cond.shape
    n = w_ada.shape[1]
    tn = 512
    return pl.pallas_call(
        _adaln_kernel,
        grid=(n // tn,),
        in_specs=[pl.BlockSpec((rows, d), lambda i: (0, 0)),
                  pl.BlockSpec((d, tn), lambda i: (0, i)),
                  pl.BlockSpec((1, tn), lambda i: (0, i))],
        out_specs=pl.BlockSpec((rows, tn), lambda i: (0, i)),
        out_shape=jax.ShapeDtypeStruct((rows, n), F32),
        compiler_params=pltpu.CompilerParams(vmem_limit_bytes=VMEM_LIMIT),
        name="adaln",
    )(cond, w_ada, b_ada.reshape(1, n))


def _modulated_norm(xt, mult, shift):
    ms = jnp.mean(xt * xt, axis=-1, keepdims=True)
    return (xt * lax.rsqrt(ms + RMS_EPS) * mult + shift).astype(BF16)


def _head_norm(t, gain, bd):
    ss = jnp.dot((t * t).astype(BF16), bd, preferred_element_type=F32)
    return t * lax.rsqrt(ss * (1.0 / HEAD_DIM) + RMS_EPS) * gain


def _rope(t, cos, sin_lo, sin_hi):
    outs = []
    for c in range(t.shape[1] // LANES):
        tc = t[:, c * LANES:(c + 1) * LANES]
        up = pltpu.roll(tc, LANES - HEAD_DIM // 4, 1)
        dn = pltpu.roll(tc, HEAD_DIM // 4, 1)
        outs.append(tc * cos + up * sin_lo + dn * sin_hi)
    return jnp.concatenate(outs, axis=1)


def _ctx_kernel(ctx_ref, ada_ref, ng_ref, w_ref, kg_ref, bd_ref, kc_ref, vc_ref, *, d_attn):
    shift = ada_ref[0, 0:1, :]
    scale = ada_ref[0, 1:2, :]
    mult = ng_ref[...] * (1.0 + scale)
    h = _modulated_norm(ctx_ref[0], mult, shift)
    kf = jnp.dot(h, w_ref[:, 0:d_attn], preferred_element_type=F32)
    kn = _head_norm(kf, kg_ref[...], bd_ref[...])
    kc_ref[0] = kn.astype(BF16)
    vc_ref[0] = jnp.dot(h, w_ref[:, d_attn:2 * d_attn], preferred_element_type=F32).astype(BF16)


def _ctx_kv(ctx, ada3, ctx_row, norm_g, w_kv, kg, bd):
    b, l, d = ctx.shape
    d_attn = w_kv.shape[1] // 2
    return pl.pallas_call(
        functools.partial(_ctx_kernel, d_attn=d_attn),
        grid=(b,),
        in_specs=[pl.BlockSpec((1, l, d), lambda i: (i, 0, 0)),
                  pl.BlockSpec((1, 3, d), lambda i: (ctx_row, 0, 0)),
                  pl.BlockSpec((1, d), lambda i: (0, 0)),
                  pl.BlockSpec((d, 2 * d_attn), lambda i: (0, 0)),
                  pl.BlockSpec((1, d_attn), lambda i: (0, 0)),
                  pl.BlockSpec((d_attn, d_attn), lambda i: (0, 0))],
        out_specs=[pl.BlockSpec((1, l, d_attn), lambda i: (i, 0, 0)),
                   pl.BlockSpec((1, l, d_attn), lambda i: (i, 0, 0))],
        out_shape=[jax.ShapeDtypeStruct((b, l, d_attn), BF16),
                   jax.ShapeDtypeStruct((b, l, d_attn), BF16)],
        compiler_params=pltpu.CompilerParams(vmem_limit_bytes=VMEM_LIMIT),
        name="ctx_kv",
    )(ctx, ada3, norm_g, w_kv, kg, bd)


def _in_proj_kernel(x_ref, xp_ref, xn_ref, ada_ref, ng_ref, w_ref, qg_ref, kg_ref,
                    cos_ref, slo_ref, shi_ref, bd_ref, cw_ref, cb_ref,
                    qrot_ref, qpl_ref, k_ref, v_ref, ga_ref, conv_ref,
                    hext_ref, cu_ref, *, tm, da, dc):
    j = pl.program_id(1)
    nj = pl.num_programs(1)
    shift = ada_ref[0, 0:1, :]
    scale = ada_ref[0, 1:2, :]
    mult = ng_ref[...] * (1.0 + scale)

    hext_ref[0:HALO, :] = _modulated_norm(xp_ref[0], mult, shift)
    hext_ref[HALO:HALO + tm, :] = _modulated_norm(x_ref[0], mult, shift)
    hext_ref[HALO + tm:, :] = _modulated_norm(xn_ref[0], mult, shift)
    h = hext_ref[HALO:HALO + tm, :]

    def proj(src, lo, width):
        return jnp.dot(src, w_ref[:, lo:lo + width], preferred_element_type=F32)

    bd = bd_ref[...]
    cos, slo, shi = cos_ref[...], slo_ref[...], shi_ref[...]


    hx = hext_ref[...]
    cu = proj(hx, 4 * da, dc) * proj(hx, 4 * da + 2 * dc, dc)
    row = lax.broadcasted_iota(jnp.int32, (tm + 2 * HALO, 1), 0)
    inside = ((row >= HALO) | (j > 0)) & ((row < HALO + tm) | (j < nj - 1))
    cu_ref[...] = jnp.where(inside, cu, 0.0)
    y = (cb_ref[...]
         + cw_ref[0:1, :] * cu_ref[HALO - 1:HALO - 1 + tm, :]
         + cw_ref[1:2, :] * cu_ref[HALO:HALO + tm, :]
         + cw_ref[2:3, :] * cu_ref[HALO + 1:HALO + 1 + tm, :])
    bg = proj(h, 4 * da + dc, dc)
    zc = proj(h, 4 * da + 3 * dc, dc)
    conv_ref[0] = (bg * y * _silu(zc)).astype(BF16)

    qn = _head_norm(proj(h, 0, da), qg_ref[...], bd)
    qpl_ref[0] = qn.astype(BF16)
    qrot_ref[0] = _rope(qn, cos, slo, shi).astype(BF16)

    kn = _head_norm(proj(h, da, da), kg_ref[...], bd)
    k_ref[0] = _rope(kn, cos, slo, shi).astype(BF16)

    ga_ref[0] = _silu(proj(h, 3 * da, da)).astype(BF16)
    v_ref[0] = proj(h, 2 * da, da).astype(BF16)


def _in_proj(x, ada3, norm_g, w_in, qg, kg, cos, slo, shi, bd, conv_w, conv_b, *, tm):
    b, s, d = x.shape
    dc = conv_w.shape[1]
    da = (w_in.shape[1] - 4 * dc) // 4
    nh = tm // HALO
    last_halo = s // HALO - 1
    tok = lambda width: pl.BlockSpec((1, tm, width), lambda i, j: (i, j, 0))
    const = lambda shape: pl.BlockSpec(shape, lambda i, j: (0,) * len(shape))
    tab = pl.BlockSpec((tm, LANES), lambda i, j: (j, 0))
    out_tok = jax.ShapeDtypeStruct((b, s, da), BF16)
    return pl.pallas_call(
        functools.partial(_in_proj_kernel, tm=tm, da=da, dc=dc),
        grid=(b, s // tm),
        in_specs=[tok(d),
                  pl.BlockSpec((1, HALO, d), lambda i, j: (i, jnp.maximum(j * nh - 1, 0), 0)),
                  pl.BlockSpec((1, HALO, d), lambda i, j: (i, jnp.minimum((j + 1) * nh, last_halo), 0)),
                  pl.BlockSpec((1, 3, d), lambda i, j: (i, 0, 0)),
                  const((1, d)), const(w_in.shape), const((1, da)), const((1, da)),
                  tab, tab, tab, const((da, da)), const((CONV_K, dc)), const((1, dc))],
        out_specs=[tok(da), tok(da), tok(da), tok(da), tok(da), tok(dc)],
        out_shape=[out_tok, out_tok, out_tok, out_tok, out_tok,
                   jax.ShapeDtypeStruct((b, s, dc), BF16)],
        scratch_shapes=[pltpu.VMEM((tm + 2 * HALO, d), BF16),
                        pltpu.VMEM((tm + 2 * HALO, dc), F32)],
        compiler_params=pltpu.CompilerParams(vmem_limit_bytes=VMEM_LIMIT),
        name="in_proj",
    )(x, x, x, ada3, norm_g, w_in, qg, kg, cos, slo, shi, bd, conv_w, conv_b)


def _dot_nt(a, b):
    return lax.dot_general(a, b, (((1,), (1,)), ((), ())), preferred_element_type=F32)


def _attn_kernel(qrot_ref, qpl_ref, k_ref, v_ref, kc_ref, vc_ref, ga_ref, bias_ref, o_ref,
                 s_lat_ref, s_ctx_ref, p_lat_ref, p_ctx_ref, rl_ref, *, rows, group, n_batch, steps):
    n_units = n_batch * rows // group
    nk = WIN_H * GRID_W

    lane = lax.broadcasted_iota(jnp.int32, (GRID_W, LANES), 1)
    first_head = lane < HEAD_DIM

    def stack_heads(q2):
        zero = jnp.zeros_like(q2)
        return jnp.concatenate([jnp.where(first_head, q2, zero), jnp.where(first_head, zero, q2)], axis=0)

    def geometry(u, g):
        r = u * group + g
        if isinstance(r, int):
            bb, i = divmod(r, rows)
            rs = min(max(i - WIN_H // 2, 0), rows - WIN_H)
            return bb, i - rs, i * GRID_W, rs * GRID_W
        bb = r // rows
        i = r % rows
        rs = jnp.clip(i - WIN_H // 2, 0, rows - WIN_H)
        return bb, i - rs, pl.multiple_of(i * GRID_W, GRID_W), pl.multiple_of(rs * GRID_W, GRID_W)


    def scores_matmul(u):
        out = []
        for g in range(group):
            bb, off, tok0, key0 = geometry(u, g)
            qs = stack_heads(qrot_ref[bb, pl.ds(tok0, GRID_W), :])
            kband = k_ref[bb, pl.ds(key0, nk), :]
            qp = stack_heads(qpl_ref[bb, pl.ds(tok0, GRID_W), :])
            out.append((_dot_nt(qs, kband), off, _dot_nt(qp, kc_ref[bb])))
        return out


    def scores_store(slot, vals):
        for g, (s_lat, off, s_ctx) in enumerate(vals):
            s_lat_ref[slot, g] = s_lat + bias_ref[0, off]
            s_ctx_ref[slot, g] = s_ctx

    def softmax(slot):
        for g in range(group):
            s_lat = s_lat_ref[slot, g]
            s_ctx = s_ctx_ref[slot, g]
            m = jnp.maximum(jnp.max(s_lat, axis=-1, keepdims=True), jnp.max(s_ctx, axis=-1, keepdims=True))
            e_lat = jnp.exp2(s_lat - m)
            e_ctx = jnp.exp2(s_ctx - m)
            denom = jnp.sum(e_lat, axis=-1, keepdims=True) + jnp.sum(e_ctx, axis=-1, keepdims=True)
            p_lat_ref[slot, g] = e_lat.astype(BF16)
            p_ctx_ref[slot, g] = e_ctx.astype(BF16)
            rl_ref[slot, g] = jnp.broadcast_to(1.0 / denom, (2 * GRID_W, LANES))

    def pv_matmul(u, slot):
        out = []
        for g in range(group):
            bb, _, _, key0 = geometry(u, g)
            o = (jnp.dot(p_lat_ref[slot, g], v_ref[bb, pl.ds(key0, nk), :], preferred_element_type=F32)
                 + jnp.dot(p_ctx_ref[slot, g], vc_ref[bb], preferred_element_type=F32))
            out.append((o, rl_ref[slot, g]))
        return out

    def pv_store(u, vals):
        for g, (o, rl) in enumerate(vals):
            bb, _, tok0, _ = geometry(u, g)
            o = o * rl
            o2 = jnp.where(first_head, o[0:GRID_W], o[GRID_W:2 * GRID_W])
            gate = ga_ref[bb, pl.ds(tok0, GRID_W), :].astype(F32)
            o_ref[bb, pl.ds(tok0, GRID_W), :] = (o2 * gate).astype(BF16)

    scores_store(0, scores_matmul(0))
    sc = scores_matmul(1)
    softmax(0)
    scores_store(1, sc)

    def step(t, slot):
        sc = scores_matmul(t)
        pv = pv_matmul(t - 2, slot)
        softmax(1 - slot)
        scores_store(slot, sc)
        pv_store(t - 2, pv)

    n_steady, n_left = divmod(n_units - 2, steps)

    def steady(n, carry):
        for j in range(steps):
            step(steps * n + 2 + j, j % 2)
        return carry

    lax.fori_loop(0, n_steady, steady, 0)
    for j in range(n_left):
        step(steps * n_steady + 2 + j, j % 2)

    pv = pv_matmul(n_units - 2, 0)
    softmax(1)
    pv_store(n_units - 2, pv)
    pv_store(n_units - 1, pv_matmul(n_units - 1, 1))


def _attention(qrot, qpl, k, v, kc, vc, ga, bias, *, group, n_batch, steps):
    b, s, da = qrot.shape
    l = vc.shape[1]
    rows = s // GRID_W
    n_hp = da // LANES
    nq = 2 * GRID_W
    nk = WIN_H * GRID_W
    assert rows % (2 * group) == 0 and b % n_batch == 0 and steps % 2 == 0
    tok = pl.BlockSpec((n_batch, s, LANES), lambda hp, i: (i, 0, hp))
    ctx_tok = pl.BlockSpec((n_batch, l, LANES), lambda hp, i: (i, 0, hp))
    return pl.pallas_call(
        functools.partial(_attn_kernel, rows=rows, group=group, n_batch=n_batch, steps=steps),
        grid=(n_hp, b // n_batch),
        in_specs=[tok, tok, tok, tok, ctx_tok, ctx_tok, tok,
                  pl.BlockSpec((1,) + bias.shape[1:], lambda hp, i: (hp, 0, 0, 0))],
        out_specs=tok,
        out_shape=jax.ShapeDtypeStruct((b, s, da), BF16),
        scratch_shapes=[pltpu.VMEM((2, group, nq, nk), F32),
                        pltpu.VMEM((2, group, nq, l), F32),
                        pltpu.VMEM((2, group, nq, nk), BF16),
                        pltpu.VMEM((2, group, nq, l), BF16),
                        pltpu.VMEM((2, group, nq, LANES), F32)],
        compiler_params=pltpu.CompilerParams(vmem_limit_bytes=VMEM_LIMIT),
        name="attn",
    )(qrot, qpl, k, v, kc, vc, ga, bias)


def _out_proj_kernel(x_ref, a_ref, c_ref, ada_ref, w_ref, o_ref, *, da):
    gate = ada_ref[0, 2:3, :]
    upd = (jnp.dot(a_ref[0], w_ref[0:da, :], preferred_element_type=F32)
           + jnp.dot(c_ref[0], w_ref[da:, :], preferred_element_type=F32))
    o_ref[0] = x_ref[0] + gate * upd


def _out_proj(x, attn, conv, ada3, w_out, *, tm):
    b, s, d = x.shape
    da = attn.shape[2]
    dc = conv.shape[2]
    return pl.pallas_call(
        functools.partial(_out_proj_kernel, da=da),
        grid=(b, s // tm),
        in_specs=[pl.BlockSpec((1, tm, d), lambda i, j: (i, j, 0)),
                  pl.BlockSpec((1, tm, da), lambda i, j: (i, j, 0)),
                  pl.BlockSpec((1, tm, dc), lambda i, j: (i, j, 0)),
                  pl.BlockSpec((1, 3, d), lambda i, j: (i, 0, 0)),
                  pl.BlockSpec(w_out.shape, lambda i, j: (0, 0))],
        out_specs=pl.BlockSpec((1, tm, d), lambda i, j: (i, j, 0)),
        out_shape=jax.ShapeDtypeStruct((b, s, d), F32),
        compiler_params=pltpu.CompilerParams(vmem_limit_bytes=VMEM_LIMIT),
        name="out_proj",
    )(x, attn, conv, ada3, w_out)


def _rope_tables(s):
    nf = HEAD_DIM // 4
    inv = (ROPE_THETA ** (-np.arange(nf, dtype=np.float32) / nf)).astype(np.float32)
    pos = np.arange(s)
    lane = np.arange(LANES)
    d = lane % HEAD_DIM
    axis = d // (2 * nf)
    half = (d % (2 * nf)) // nf
    coord = np.where(axis[None, :] == 0, (pos // GRID_W)[:, None], (pos % GRID_W)[:, None]).astype(np.float32)
    ang = (coord * inv[d % nf][None, :]).astype(np.float32)
    cos = np.cos(ang).astype(np.float32)
    sin = np.sin(ang).astype(np.float32)
    sin_lo = np.where(half[None, :] == 0, -sin, 0.0).astype(np.float32)
    sin_hi = np.where(half[None, :] == 1, sin, 0.0).astype(np.float32)
    return jnp.asarray(cos), jnp.asarray(sin_lo), jnp.asarray(sin_hi)


def _bias_kernel(rpb_ref, o_ref):
    off = pl.program_id(1)
    cq = lax.broadcasted_iota(jnp.int32, (GRID_W, LANES), 0)
    ck = lax.broadcasted_iota(jnp.int32, (GRID_W, LANES), 1) % GRID_W
    col_start = jnp.clip(cq - WIN_W // 2, 0, GRID_W - WIN_W)
    valid = (ck >= col_start) & (ck < col_start + WIN_W)
    for a in range(2):
        for t in range(WIN_H // 2):
            dr0 = 2 * t - off + (WIN_H - 1)
            two = rpb_ref[a, pl.ds(dr0, 2), :]
            lanes = jnp.broadcast_to(jnp.concatenate([two[0:1], two[1:2]], axis=1), (GRID_W, LANES))
            toeplitz = pltpu.roll(lanes, LANES - (WIN_W - 1), 1, stride=1, stride_axis=0)
            o_ref[0, 0, a * GRID_W:(a + 1) * GRID_W, t * LANES:(t + 1) * LANES] = jnp.where(
                valid, toeplitz * LOG2_E, MASK_VALUE)


def _bias_table(rpb):
    nh, ndr, ndc = rpb.shape
    assert ndr == 2 * WIN_H - 1 and ndc == 2 * WIN_W - 1 and ndc <= GRID_W
    rpb_pad = jnp.pad(rpb, ((0, 0), (0, 0), (0, GRID_W - ndc)))
    return pl.pallas_call(
        _bias_kernel,
        grid=(nh // 2, WIN_H),
        in_specs=[pl.BlockSpec((2, ndr, GRID_W), lambda hp, off: (hp, 0, 0))],
        out_specs=pl.BlockSpec((1, 1, 2 * GRID_W, WIN_H * GRID_W), lambda hp, off: (hp, off, 0, 0)),
        out_shape=jax.ShapeDtypeStruct((nh // 2, WIN_H, 2 * GRID_W, WIN_H * GRID_W), F32),
        name="bias_table",
    )(rpb_pad)


def kernel(x, c, ctx, c_ctx, w_ada, b_ada, norm_g, w_in, q_norm_g, k_norm_g, rpb, conv_w, conv_b, w_out):
    depth = w_ada.shape[0]
    b, s, d = x.shape
    dc = conv_w.shape[2]
    da = (w_in.shape[2] - 4 * dc) // 4
    n_heads = da // HEAD_DIM
    rows = s // GRID_W
    assert depth == 1 and s % GRID_W == 0 and rows >= WIN_H and da % LANES == 0

    cos, slo, shi = _rope_tables(s)
    seg = np.arange(da) // HEAD_DIM
    bd = jnp.asarray((seg[:, None] == seg[None, :]).astype(np.float32), dtype=BF16)

    cond_rows = -(-(b + 1) // 8) * 8
    cond = jnp.zeros((cond_rows, d), F32).at[:b].set(c).at[b].set(c_ctx)
    ada3 = _adaln(cond, w_ada[0], b_ada[0]).reshape(cond_rows, 3, d)

    w_in_b = w_in[0].astype(BF16)
    ng = norm_g[0].reshape(1, d)
    qg = jnp.tile(q_norm_g[0] * (HEAD_DIM ** -0.5 * LOG2_E), n_heads).reshape(1, da)
    kg = jnp.tile(k_norm_g[0], n_heads).reshape(1, da)

    kc, vc = _ctx_kv(ctx, ada3, b, ng, w_in_b[:, da:3 * da], kg, bd)
    qrot, qpl, k, v, ga, conv = _in_proj(x, ada3, ng, w_in_b, qg, kg, cos, slo, shi, bd,
                                         conv_w[0], conv_b[0].reshape(1, dc), tm=512)
    attn = _attention(qrot, qpl, k, v, kc, vc, ga, _bias_table(rpb[0]), group=2, n_batch=4, steps=4)
    return _out_proj(x, attn, conv, ada3, w_out[0].astype(BF16), tm=512)
```

```python
import functools

import numpy as np
import jax
import jax.numpy as jnp
from jax import lax
from jax.experimental import pallas as pl
from jax.experimental.pallas import tpu as pltpu

F32 = jnp.float32
BF16 = jnp.bfloat16

HEAD_DIM = 64
GRID_W = 64
WIN_H = 8
WIN_W = 16
CONV_K = 3
ROPE_THETA = 10000.0
RMS_EPS = 1e-6
MASK_VALUE = -1e30
LOG2_E = 1.4426950408889634

LANES = 128
MXU_DIM = 256
BF16_ROWS = 16
HALO = BF16_ROWS
VMEM_LIMIT = 56 * 1024 * 1024


def _silu(z):
    return z * jax.nn.sigmoid(z)


def _adaln_kernel(cond_ref, w_ref, b_ref, o_ref):
    a = _silu(cond_ref[...]).astype(BF16)
    o_ref[...] = jnp.dot(a, w_ref[...].astype(BF16), preferred_element_type=F32) + b_ref[...]


def _adaln(cond, w_ada, b_ada):
    rows, d = cond.shape
    n = w_ada.shape[1]
    tn = 512
    return pl.pallas_call(
        _adaln_kernel,
        grid=(n // tn,),
        in_specs=[pl.BlockSpec((rows, d), lambda i: (0, 0)),
                  pl.BlockSpec((d, tn), lambda i: (0, i)),
                  pl.BlockSpec((1, tn), lambda i: (0, i))],
        out_specs=pl.BlockSpec((rows, tn), lambda i: (0, i)),
        out_shape=jax.ShapeDtypeStruct((rows, n), F32),
        compiler_params=pltpu.CompilerParams(vmem_limit_bytes=VMEM_LIMIT),
        name="adaln",
    )(cond, w_ada, b_ada.reshape(1, n))


def _modulated_norm(xt, mult, shift):
    ms = jnp.mean(xt * xt, axis=-1, keepdims=True)
    return (xt * lax.rsqrt(ms + RMS_EPS) * mult + shift).astype(BF16)


def _head_norm(t, gain, bd):
    t2 = (t * t).astype(BF16)
    w = bd.shape[0]
    ss = jnp.concatenate([jnp.dot(t2[:, c:c + w], bd, preferred_element_type=F32)
                          for c in range(0, t.shape[1], w)], axis=1)
    return t * lax.rsqrt(ss * (1.0 / HEAD_DIM) + RMS_EPS) * gain


def _rope(t, cos, sin_lo, sin_hi):
    outs = []
    for c in range(t.shape[1] // LANES):
        tc = t[:, c * LANES:(c + 1) * LANES]
        up = pltpu.roll(tc, LANES - HEAD_DIM // 4, 1)
        dn = pltpu.roll(tc, HEAD_DIM // 4, 1)
        outs.append(tc * cos + up * sin_lo + dn * sin_hi)
    return jnp.concatenate(outs, axis=1)


def _ctx_kernel(ctx_ref, ada_ref, ng_ref, wk_ref, wv_ref, kg_ref, bd_ref, kc_ref, vc_ref):
    shift = ada_ref[0, 0:1, :]
    scale = ada_ref[0, 1:2, :]
    mult = ng_ref[...] * (1.0 + scale)
    h = _modulated_norm(ctx_ref[0], mult, shift)
    kf = jnp.dot(h, wk_ref[...], preferred_element_type=F32)
    kn = _head_norm(kf, kg_ref[...], bd_ref[...])
    kc_ref[0] = kn.astype(BF16)
    vc_ref[0] = jnp.dot(h, wv_ref[...], preferred_element_type=F32).astype(BF16)


def _ctx_kv(ctx, ada3, ctx_row, norm_g, w_in, kg, bd):
    b, l, d = ctx.shape
    d_attn = kg.shape[1]
    return pl.pallas_call(
        _ctx_kernel,
        grid=(b,),
        in_specs=[pl.BlockSpec((1, l, d), lambda i: (i, 0, 0)),
                  pl.BlockSpec((1, 3, d), lambda i: (ctx_row, 0, 0)),
                  pl.BlockSpec((1, d), lambda i: (0, 0)),
                  pl.BlockSpec((d, d_attn), lambda i: (0, 1)),
                  pl.BlockSpec((d, d_attn), lambda i: (0, 2)),
                  pl.BlockSpec((1, d_attn), lambda i: (0, 0)),
                  pl.BlockSpec(bd.shape, lambda i: (0, 0))],
        out_specs=[pl.BlockSpec((1, l, d_attn), lambda i: (i, 0, 0)),
                   pl.BlockSpec((1, l, d_attn), lambda i: (i, 0, 0))],
        out_shape=[jax.ShapeDtypeStruct((b, l, d_attn), BF16),
                   jax.ShapeDtypeStruct((b, l, d_attn), BF16)],
        compiler_params=pltpu.CompilerParams(vmem_limit_bytes=VMEM_LIMIT),
        name="ctx_kv",
    )(ctx, ada3, norm_g, w_in, w_in, kg, bd)


def _in_proj_kernel(x_ref, xp_ref, xn_ref, ada_ref, ng_ref, w_ref, qg_ref, kg_ref,
                    cos_ref, slo_ref, shi_ref, bd_ref, cw_ref, cb_ref,
                    qrot_ref, qpl_ref, k_ref, v_ref, ga_ref, conv_ref,
                    hext_ref, cu_ref, *, tm, sub, da, dc):
    j = pl.program_id(1)
    nj = pl.num_programs(1)
    shift = ada_ref[0, 0:1, :]
    scale = ada_ref[0, 1:2, :]
    mult = ng_ref[...] * (1.0 + scale)

    def proj(src, lo, width):
        return jnp.dot(src, w_ref[:, lo:lo + width], preferred_element_type=F32)

    bd = bd_ref[...]
    hext_ref[0:HALO, :] = _modulated_norm(xp_ref[0], mult, shift)
    hext_ref[HALO + tm:, :] = _modulated_norm(xn_ref[0], mult, shift)

    for n in range(tm // sub):
        r0 = n * sub
        hext_ref[HALO + r0:HALO + r0 + sub, :] = _modulated_norm(x_ref[0, r0:r0 + sub, :], mult, shift)
    for n in range(tm // sub):
        r0 = n * sub
        rows = slice(r0, r0 + sub)
        h = hext_ref[HALO + r0:HALO + r0 + sub, :]
        cos, slo, shi = cos_ref[rows, :], slo_ref[rows, :], shi_ref[rows, :]


        hx = hext_ref[r0:r0 + sub + 2 * HALO, :]
        cu = proj(hx, 4 * da, dc) * proj(hx, 4 * da + 2 * dc, dc)
        row = lax.broadcasted_iota(jnp.int32, (sub + 2 * HALO, 1), 0) + r0
        inside = ((row >= HALO) | (j > 0)) & ((row < HALO + tm) | (j < nj - 1))
        cu_ref[n] = jnp.where(inside, cu, 0.0)
        y = (cb_ref[...]
             + cw_ref[0:1, :] * cu_ref[n, HALO - 1:HALO - 1 + sub, :]
             + cw_ref[1:2, :] * cu_ref[n, HALO:HALO + sub, :]
             + cw_ref[2:3, :] * cu_ref[n, HALO + 1:HALO + 1 + sub, :])
        bg = proj(h, 4 * da + dc, dc)
        zc = proj(h, 4 * da + 3 * dc, dc)
        conv_ref[0, rows, :] = (bg * y * _silu(zc)).astype(BF16)

        qn = _head_norm(proj(h, 0, da), qg_ref[...], bd)
        qpl_ref[0, rows, :] = qn.astype(BF16)
        qrot_ref[0, rows, :] = _rope(qn, cos, slo, shi).astype(BF16)

        kn = _head_norm(proj(h, da, da), kg_ref[...], bd)
        k_ref[0, rows, :] = _rope(kn, cos, slo, shi).astype(BF16)

        ga_ref[0, rows, :] = _silu(proj(h, 3 * da, da)).astype(BF16)
        v_ref[0, rows, :] = proj(h, 2 * da, da).astype(BF16)


def _in_proj(x, ada3, norm_g, w_in, qg, kg, cos, slo, shi, bd, conv_w, conv_b, *, tm, sub):
    b, s, d = x.shape
    assert s % tm == 0 and tm % sub == 0 and sub % HALO == 0
    dc = conv_w.shape[1]
    da = (w_in.shape[1] - 4 * dc) // 4
    nh = tm // HALO
    last_halo = s // HALO - 1
    tok = lambda width: pl.BlockSpec((1, tm, width), lambda i, j: (i, j, 0))
    const = lambda shape: pl.BlockSpec(shape, lambda i, j: (0,) * len(shape))
    tab = pl.BlockSpec((tm, LANES), lambda i, j: (j, 0))
    out_tok = jax.ShapeDtypeStruct((b, s, da), BF16)
    return pl.pallas_call(
        functools.partial(_in_proj_kernel, tm=tm, sub=sub, da=da, dc=dc),
        grid=(b, s // tm),
        in_specs=[tok(d),
                  pl.BlockSpec((1, HALO, d), lambda i, j: (i, jnp.maximum(j * nh - 1, 0), 0)),
                  pl.BlockSpec((1, HALO, d), lambda i, j: (i, jnp.minimum((j + 1) * nh, last_halo), 0)),
                  pl.BlockSpec((1, 3, d), lambda i, j: (i, 0, 0)),
                  const((1, d)), const(w_in.shape), const((1, da)), const((1, da)),
                  tab, tab, tab, const(bd.shape), const((CONV_K, dc)), const((1, dc))],
        out_specs=[tok(da), tok(da), tok(da), tok(da), tok(da), tok(dc)],
        out_shape=[out_tok, out_tok, out_tok, out_tok, out_tok,
                   jax.ShapeDtypeStruct((b, s, dc), BF16)],
        scratch_shapes=[pltpu.VMEM((tm + 2 * HALO, d), BF16),
                        pltpu.VMEM((tm // sub, sub + 2 * HALO, dc), F32)],
        compiler_params=pltpu.CompilerParams(vmem_limit_bytes=VMEM_LIMIT),
        name="in_proj",
    )(x, x, x, ada3, norm_g, w_in, qg, kg, cos, slo, shi, bd, conv_w, conv_b)


def _dot_nt(a, b):
    return lax.dot_general(a, b, (((1,), (1,)), ((), ())), preferred_element_type=F32)


def _attn_kernel(qrot_ref, qpl_ref, k_ref, v_ref, kc_ref, vc_ref, ga_ref, bias_ref, o_ref,
                 s_lat_ref, s_ctx_ref, p_lat_ref, p_ctx_ref, rl_ref, *, rows, group, n_batch, steps):
    n_units = n_batch * rows // group
    nk = WIN_H * GRID_W

    lane = lax.broadcasted_iota(jnp.int32, (GRID_W, LANES), 1)
    first_head = lane < HEAD_DIM

    def stack_heads(q2):
        zero = jnp.zeros_like(q2)
        return jnp.concatenate([jnp.where(first_head, q2, zero), jnp.where(first_head, zero, q2)], axis=0)

    def geometry(u, g):
        r = u * group + g
        if isinstance(r, int):
            bb, i = divmod(r, rows)
            rs = min(max(i - WIN_H // 2, 0), rows - WIN_H)
            return bb, i - rs, i * GRID_W, rs * GRID_W
        bb = r // rows
        i = r % rows
        rs = jnp.clip(i - WIN_H // 2, 0, rows - WIN_H)
        return bb, i - rs, pl.multiple_of(i * GRID_W, GRID_W), pl.multiple_of(rs * GRID_W, GRID_W)


    def scores_matmul(u):
        out = []
        for g in range(group):
            bb, off, tok0, key0 = geometry(u, g)
            qs = stack_heads(qrot_ref[bb, pl.ds(tok0, GRID_W), :])
            kband = k_ref[bb, pl.ds(key0, nk), :]
            qp = stack_heads(qpl_ref[bb, pl.ds(tok0, GRID_W), :])
            out.append((_dot_nt(qs, kband), off, _dot_nt(qp, kc_ref[bb])))
        return out


    def scores_store(slot, vals):
        for g, (s_lat, off, s_ctx) in enumerate(vals):
            s_lat_ref[slot, g] = s_lat + bias_ref[0, off]
            s_ctx_ref[slot, g] = s_ctx

    def softmax(slot):
        for g in range(group):
            s_lat = s_lat_ref[slot, g]
            s_ctx = s_ctx_ref[slot, g]
            m = jnp.maximum(jnp.max(s_lat, axis=-1, keepdims=True), jnp.max(s_ctx, axis=-1, keepdims=True))
            e_lat = jnp.exp2(s_lat - m)
            e_ctx = jnp.exp2(s_ctx - m)
            denom = jnp.sum(e_lat, axis=-1, keepdims=True) + jnp.sum(e_ctx, axis=-1, keepdims=True)
            p_lat_ref[slot, g] = e_lat.astype(BF16)
            p_ctx_ref[slot, g] = e_ctx.astype(BF16)
            rl_ref[slot, g] = jnp.broadcast_to(1.0 / denom, (2 * GRID_W, LANES))

    def pv_matmul(u, slot):
        out = []
        for g in range(group):
            bb, _, _, key0 = geometry(u, g)
            o = (jnp.dot(p_lat_ref[slot, g], v_ref[bb, pl.ds(key0, nk), :], preferred_element_type=F32)
                 + jnp.dot(p_ctx_ref[slot, g], vc_ref[bb], preferred_element_type=F32))
            out.append((o, rl_ref[slot, g]))
        return out

    def pv_store(u, vals):
        for g, (o, rl) in enumerate(vals):
            bb, _, tok0, _ = geometry(u, g)
            o = o * rl
            o2 = jnp.where(first_head, o[0:GRID_W], o[GRID_W:2 * GRID_W])
            gate = ga_ref[bb, pl.ds(tok0, GRID_W), :].astype(F32)
            o_ref[bb, pl.ds(tok0, GRID_W), :] = (o2 * gate).astype(BF16)

    scores_store(0, scores_matmul(0))
    sc = scores_matmul(1)
    softmax(0)
    scores_store(1, sc)

    def step(t, slot):
        sc = scores_matmul(t)
        pv = pv_matmul(t - 2, slot)
        softmax(1 - slot)
        scores_store(slot, sc)
        pv_store(t - 2, pv)

    n_steady, n_left = divmod(n_units - 2, steps)

    def steady(n, carry):
        for j in range(steps):
            step(steps * n + 2 + j, j % 2)
        return carry

    lax.fori_loop(0, n_steady, steady, 0)
    for j in range(n_left):
        step(steps * n_steady + 2 + j, j % 2)

    pv = pv_matmul(n_units - 2, 0)
    softmax(1)
    pv_store(n_units - 2, pv)
    pv_store(n_units - 1, pv_matmul(n_units - 1, 1))


def _attention(qrot, qpl, k, v, kc, vc, ga, bias, *, group, n_batch, steps):
    b, s, da = qrot.shape
    l = vc.shape[1]
    rows = s // GRID_W
    n_hp = da // LANES
    nq = 2 * GRID_W
    nk = WIN_H * GRID_W
    assert rows % (2 * group) == 0 and b % n_batch == 0 and steps % 2 == 0
    tok = pl.BlockSpec((n_batch, s, LANES), lambda hp, i: (i, 0, hp))
    ctx_tok = pl.BlockSpec((n_batch, l, LANES), lambda hp, i: (i, 0, hp))
    return pl.pallas_call(
        functools.partial(_attn_kernel, rows=rows, group=group, n_batch=n_batch, steps=steps),
        grid=(n_hp, b // n_batch),
        in_specs=[tok, tok, tok, tok, ctx_tok, ctx_tok, tok,
                  pl.BlockSpec((1,) + bias.shape[1:], lambda hp, i: (hp, 0, 0, 0))],
        out_specs=tok,
        out_shape=jax.ShapeDtypeStruct((b, s, da), BF16),
        scratch_shapes=[pltpu.VMEM((2, group, nq, nk), F32),
                        pltpu.VMEM((2, group, nq, l), F32),
                        pltpu.VMEM((2, group, nq, nk), BF16),
                        pltpu.VMEM((2, group, nq, l), BF16),
                        pltpu.VMEM((2, group, nq, LANES), F32)],
        compiler_params=pltpu.CompilerParams(vmem_limit_bytes=VMEM_LIMIT),
        name="attn",
    )(qrot, qpl, k, v, kc, vc, ga, bias)


def _out_proj_kernel(x_ref, a_ref, c_ref, ada_ref, w_ref, o_ref, *, da):
    gate = ada_ref[0, 2:3, :]
    upd = (jnp.dot(a_ref[0], w_ref[0:da, :], preferred_element_type=F32)
           + jnp.dot(c_ref[0], w_ref[da:, :], preferred_element_type=F32))
    o_ref[0] = x_ref[0] + gate * upd


def _out_proj(x, attn, conv, ada3, w_out, *, tm):
    b, s, d = x.shape
    da = attn.shape[2]
    dc = conv.shape[2]
    return pl.pallas_call(
        functools.partial(_out_proj_kernel, da=da),
        grid=(b, s // tm),
        in_specs=[pl.BlockSpec((1, tm, d), lambda i, j: (i, j, 0)),
                  pl.BlockSpec((1, tm, da), lambda i, j: (i, j, 0)),
                  pl.BlockSpec((1, tm, dc), lambda i, j: (i, j, 0)),
                  pl.BlockSpec((1, 3, d), lambda i, j: (i, 0, 0)),
                  pl.BlockSpec(w_out.shape, lambda i, j: (0, 0))],
        out_specs=pl.BlockSpec((1, tm, d), lambda i, j: (i, j, 0)),
        out_shape=jax.ShapeDtypeStruct((b, s, d), F32),
        compiler_params=pltpu.CompilerParams(vmem_limit_bytes=VMEM_LIMIT),
        name="out_proj",
    )(x, attn, conv, ada3, w_out)


def _rope_tables(s):
    nf = HEAD_DIM // 4
    inv = (ROPE_THETA ** (-np.arange(nf, dtype=np.float32) / nf)).astype(np.float32)
    pos = np.arange(s)
    lane = np.arange(LANES)
    d = lane % HEAD_DIM
    axis = d // (2 * nf)
    half = (d % (2 * nf)) // nf
    coord = np.where(axis[None, :] == 0, (pos // GRID_W)[:, None], (pos % GRID_W)[:, None]).astype(np.float32)
    ang = (coord * inv[d % nf][None, :]).astype(np.float32)
    cos = np.cos(ang).astype(np.float32)
    sin = np.sin(ang).astype(np.float32)
    sin_lo = np.where(half[None, :] == 0, -sin, 0.0).astype(np.float32)
    sin_hi = np.where(half[None, :] == 1, sin, 0.0).astype(np.float32)
    return jnp.asarray(cos), jnp.asarray(sin_lo), jnp.asarray(sin_hi)


def _bias_kernel(rpb_ref, o_ref):
    off = pl.program_id(1)
    cq = lax.broadcasted_iota(jnp.int32, (GRID_W, LANES), 0)
    ck = lax.broadcasted_iota(jnp.int32, (GRID_W, LANES), 1) % GRID_W
    col_start = jnp.clip(cq - WIN_W // 2, 0, GRID_W - WIN_W)
    valid = (ck >= col_start) & (ck < col_start + WIN_W)
    for a in range(2):
        for t in range(WIN_H // 2):
            dr0 = 2 * t - off + (WIN_H - 1)
            two = rpb_ref[a, pl.ds(dr0, 2), :]
            lanes = jnp.broadcast_to(jnp.concatenate([two[0:1], two[1:2]], axis=1), (GRID_W, LANES))
            toeplitz = pltpu.roll(lanes, LANES - (WIN_W - 1), 1, stride=1, stride_axis=0)
            o_ref[0, 0, a * GRID_W:(a + 1) * GRID_W, t * LANES:(t + 1) * LANES] = jnp.where(
                valid, toeplitz * LOG2_E, MASK_VALUE)


def _bias_table(rpb):
    nh, ndr, ndc = rpb.shape
    assert ndr == 2 * WIN_H - 1 and ndc == 2 * WIN_W - 1 and ndc <= GRID_W
    rpb_pad = jnp.pad(rpb, ((0, 0), (0, 0), (0, GRID_W - ndc)))
    return pl.pallas_call(
        _bias_kernel,
        grid=(nh // 2, WIN_H),
        in_specs=[pl.BlockSpec((2, ndr, GRID_W), lambda hp, off: (hp, 0, 0))],
        out_specs=pl.BlockSpec((1, 1, 2 * GRID_W, WIN_H * GRID_W), lambda hp, off: (hp, off, 0, 0)),
        out_shape=jax.ShapeDtypeStruct((nh // 2, WIN_H, 2 * GRID_W, WIN_H * GRID_W), F32),
        name="bias_table",
    )(rpb_pad)


def kernel(x, c, ctx, c_ctx, w_ada, b_ada, norm_g, w_in, q_norm_g, k_norm_g, rpb, conv_w, conv_b, w_out):
    depth = w_ada.shape[0]
    b, s, d = x.shape
    dc = conv_w.shape[2]
    da = (w_in.shape[2] - 4 * dc) // 4
    n_heads = da // HEAD_DIM
    rows = s // GRID_W
    assert depth == 1 and s % GRID_W == 0 and rows >= WIN_H and da % LANES == 0

    cos, slo, shi = _rope_tables(s)
    assert da % MXU_DIM == 0 and MXU_DIM % HEAD_DIM == 0
    seg = np.arange(MXU_DIM) // HEAD_DIM
    bd = jnp.asarray((seg[:, None] == seg[None, :]).astype(np.float32), dtype=BF16)

    cond_rows = -(-(b + 1) // 8) * 8
    cond = jnp.zeros((cond_rows, d), F32).at[:b].set(c).at[b].set(c_ctx)
    ada3 = _adaln(cond, w_ada[0], b_ada[0]).reshape(cond_rows, 3, d)

    w_in_b = w_in[0].astype(BF16)
    ng = norm_g[0].reshape(1, d)
    qg = jnp.tile(q_norm_g[0] * (HEAD_DIM ** -0.5 * LOG2_E), n_heads).reshape(1, da)
    kg = jnp.tile(k_norm_g[0], n_heads).reshape(1, da)

    kc, vc = _ctx_kv(ctx, ada3, b, ng, w_in_b, kg, bd)
    qrot, qpl, k, v, ga, conv = _in_proj(x, ada3, ng, w_in_b, qg, kg, cos, slo, shi, bd,
                                         conv_w[0], conv_b[0].reshape(1, dc), tm=1024, sub=512)
    attn = _attention(qrot, qpl, k, v, kc, vc, ga, _bias_table(rpb[0]), group=2, n_batch=4, steps=4)
    return _out_proj(x, attn, conv, ada3, w_out[0].astype(BF16), tm=1024)
```

```python
import functools

import numpy as np
import jax
import jax.numpy as jnp
from jax import lax
from jax.experimental import pallas as pl
from jax.experimental.pallas import tpu as pltpu

F32 = jnp.float32
BF16 = jnp.bfloat16

HEAD_DIM = 64
GRID_W = 64
WIN_H = 8
WIN_W = 16
CONV_K = 3
ROPE_THETA = 10000.0
RMS_EPS = 1e-6
MASK_VALUE = -1e30
LOG2_E = 1.4426950408889634

LANES = 128
MXU_DIM = 256
BF16_ROWS = 16
HALO = BF16_ROWS
VMEM_LIMIT = 56 * 1024 * 1024


def _silu(z):
    return z * jax.nn.sigmoid(z)


def _adaln_kernel(cond_ref, w_ref, b_ref, o_ref):
    a = _silu(cond_ref[...]).astype(BF16)
    o_ref[...] = jnp.dot(a, w_ref[...].astype(BF16), preferred_element_type=F32) + b_ref[...]


def _adaln(cond, w_ada, b_ada):
    rows, d = cond.shape
    n = w_ada.shape[1]
    tn = 1024
    return pl.pallas_call(
        _adaln_kernel,
        grid=(n // tn,),
        in_specs=[pl.BlockSpec((rows, d), lambda i: (0, 0)),
                  pl.BlockSpec((d, tn), lambda i: (0, i)),
                  pl.BlockSpec((1, tn), lambda i: (0, i))],
        out_specs=pl.BlockSpec((rows, tn), lambda i: (0, i)),
        out_shape=jax.ShapeDtypeStruct((rows, n), F32),
        compiler_params=pltpu.CompilerParams(vmem_limit_bytes=VMEM_LIMIT),
        name="adaln",
    )(cond, w_ada, b_ada.reshape(1, n))


def _modulated_norm(xt, mult, shift):
    ms = jnp.mean(xt * xt, axis=-1, keepdims=True)
    return (xt * lax.rsqrt(ms + RMS_EPS) * mult + shift).astype(BF16)


def _head_norm(t, gain, bd):
    t2 = (t * t).astype(BF16)
    w = bd.shape[0]
    ss = jnp.concatenate([jnp.dot(t2[:, c:c + w], bd, preferred_element_type=F32)
                          for c in range(0, t.shape[1], w)], axis=1)
    return t * lax.rsqrt(ss * (1.0 / HEAD_DIM) + RMS_EPS) * gain


def _rope(t, cos, sin_lo, sin_hi):
    outs = []
    for c in range(t.shape[1] // LANES):
        tc = t[:, c * LANES:(c + 1) * LANES]
        up = pltpu.roll(tc, LANES - HEAD_DIM // 4, 1)
        dn = pltpu.roll(tc, HEAD_DIM // 4, 1)
        outs.append(tc * cos + up * sin_lo + dn * sin_hi)
    return jnp.concatenate(outs, axis=1)


def _ctx_kernel(ctx_ref, ada_ref, ng_ref, wk_ref, wv_ref, kg_ref, bd_ref, kc_ref, vc_ref):
    shift = ada_ref[0, 0:1, :]
    scale = ada_ref[0, 1:2, :]
    mult = ng_ref[...] * (1.0 + scale)
    h = _modulated_norm(ctx_ref[...], mult, shift)
    kf = jnp.dot(h, wk_ref[...], preferred_element_type=F32)
    kn = _head_norm(kf, kg_ref[...], bd_ref[...])
    kc_ref[...] = kn.astype(BF16)
    vc_ref[...] = jnp.dot(h, wv_ref[...], preferred_element_type=F32).astype(BF16)


def _ctx_kv(ctx, ada3, ctx_row, norm_g, w_in, kg, bd, *, tm):
    b, l, d = ctx.shape
    d_attn = kg.shape[1]
    n = b * l
    assert n % tm == 0
    kc, vc = pl.pallas_call(
        _ctx_kernel,
        grid=(n // tm,),
        in_specs=[pl.BlockSpec((tm, d), lambda i: (i, 0)),
                  pl.BlockSpec((1, 3, d), lambda i: (ctx_row, 0, 0)),
                  pl.BlockSpec((1, d), lambda i: (0, 0)),
                  pl.BlockSpec((d, d_attn), lambda i: (0, 1)),
                  pl.BlockSpec((d, d_attn), lambda i: (0, 2)),
                  pl.BlockSpec((1, d_attn), lambda i: (0, 0)),
                  pl.BlockSpec(bd.shape, lambda i: (0, 0))],
        out_specs=[pl.BlockSpec((tm, d_attn), lambda i: (i, 0)),
                   pl.BlockSpec((tm, d_attn), lambda i: (i, 0))],
        out_shape=[jax.ShapeDtypeStruct((n, d_attn), BF16),
                   jax.ShapeDtypeStruct((n, d_attn), BF16)],
        compiler_params=pltpu.CompilerParams(vmem_limit_bytes=VMEM_LIMIT),
        name="ctx_kv",
    )(ctx.reshape(n, d), ada3, norm_g, w_in, w_in, kg, bd)
    return kc.reshape(b, l, d_attn), vc.reshape(b, l, d_attn)


def _in_proj_kernel(x_ref, xp_ref, xn_ref, ada_ref, ng_ref, w_ref, qg_ref, kg_ref,
                    cos_ref, slo_ref, shi_ref, bd_ref, cw_ref, cb_ref,
                    qrot_ref, qpl_ref, k_ref, v_ref, ga_ref, conv_ref,
                    hext_ref, cu_ref, *, tm, sub, da, dc):
    j = pl.program_id(1)
    nj = pl.num_programs(1)
    shift = ada_ref[0, 0:1, :]
    scale = ada_ref[0, 1:2, :]
    mult = ng_ref[...] * (1.0 + scale)

    def proj(src, lo, width):
        return jnp.dot(src, w_ref[:, lo:lo + width], preferred_element_type=F32)

    bd = bd_ref[...]
    hext_ref[0:HALO, :] = _modulated_norm(xp_ref[0], mult, shift)
    hext_ref[HALO + tm:, :] = _modulated_norm(xn_ref[0], mult, shift)

    for n in range(tm // sub):
        r0 = n * sub
        hext_ref[HALO + r0:HALO + r0 + sub, :] = _modulated_norm(x_ref[0, r0:r0 + sub, :], mult, shift)
    for n in range(tm // sub):
        r0 = n * sub
        rows = slice(r0, r0 + sub)
        h = hext_ref[HALO + r0:HALO + r0 + sub, :]
        cos, slo, shi = cos_ref[rows, :], slo_ref[rows, :], shi_ref[rows, :]


        hx = hext_ref[r0:r0 + sub + 2 * HALO, :]
        cu = proj(hx, 4 * da, dc) * proj(hx, 4 * da + 2 * dc, dc)
        row = lax.broadcasted_iota(jnp.int32, (sub + 2 * HALO, 1), 0) + r0
        inside = ((row >= HALO) | (j > 0)) & ((row < HALO + tm) | (j < nj - 1))
        cu_ref[n] = jnp.where(inside, cu, 0.0)
        y = (cb_ref[...]
             + cw_ref[0:1, :] * cu_ref[n, HALO - 1:HALO - 1 + sub, :]
             + cw_ref[1:2, :] * cu_ref[n, HALO:HALO + sub, :]
             + cw_ref[2:3, :] * cu_ref[n, HALO + 1:HALO + 1 + sub, :])
        bg = proj(h, 4 * da + dc, dc)
        zc = proj(h, 4 * da + 3 * dc, dc)
        conv_ref[0, rows, :] = (bg * y * _silu(zc)).astype(BF16)

        qn = _head_norm(proj(h, 0, da), qg_ref[...], bd)
        qpl_ref[0, rows, :] = qn.astype(BF16)
        qrot_ref[0, rows, :] = _rope(qn, cos, slo, shi).astype(BF16)

        kn = _head_norm(proj(h, da, da), kg_ref[...], bd)
        k_ref[0, rows, :] = _rope(kn, cos, slo, shi).astype(BF16)

        ga_ref[0, rows, :] = _silu(proj(h, 3 * da, da)).astype(BF16)
        v_ref[0, rows, :] = proj(h, 2 * da, da).astype(BF16)


def _in_proj(x, ada3, norm_g, w_in, qg, kg, cos, slo, shi, bd, conv_w, conv_b, *, tm, sub):
    b, s, d = x.shape
    assert s % tm == 0 and tm % sub == 0 and sub % HALO == 0
    dc = conv_w.shape[1]
    da = (w_in.shape[1] - 4 * dc) // 4
    nh = tm // HALO
    last_halo = s // HALO - 1
    tok = lambda width: pl.BlockSpec((1, tm, width), lambda i, j: (i, j, 0))
    const = lambda shape: pl.BlockSpec(shape, lambda i, j: (0,) * len(shape))
    tab = pl.BlockSpec((tm, LANES), lambda i, j: (j, 0))
    out_tok = jax.ShapeDtypeStruct((b, s, da), BF16)
    return pl.pallas_call(
        functools.partial(_in_proj_kernel, tm=tm, sub=sub, da=da, dc=dc),
        grid=(b, s // tm),
        in_specs=[tok(d),
                  pl.BlockSpec((1, HALO, d), lambda i, j: (i, jnp.maximum(j * nh - 1, 0), 0)),
                  pl.BlockSpec((1, HALO, d), lambda i, j: (i, jnp.minimum((j + 1) * nh, last_halo), 0)),
                  pl.BlockSpec((1, 3, d), lambda i, j: (i, 0, 0)),
                  const((1, d)), const(w_in.shape), const((1, da)), const((1, da)),
                  tab, tab, tab, const(bd.shape), const((CONV_K, dc)), const((1, dc))],
        out_specs=[tok(da), tok(da), tok(da), tok(da), tok(da), tok(dc)],
        out_shape=[out_tok, out_tok, out_tok, out_tok, out_tok,
                   jax.ShapeDtypeStruct((b, s, dc), BF16)],
        scratch_shapes=[pltpu.VMEM((tm + 2 * HALO, d), BF16),
                        pltpu.VMEM((tm // sub, sub + 2 * HALO, dc), F32)],
        compiler_params=pltpu.CompilerParams(vmem_limit_bytes=VMEM_LIMIT),
        name="in_proj",
    )(x, x, x, ada3, norm_g, w_in, qg, kg, cos, slo, shi, bd, conv_w, conv_b)


def _dot_nt(a, b):
    return lax.dot_general(a, b, (((1,), (1,)), ((), ())), preferred_element_type=F32)


def _attn_kernel(qrot_ref, qpl_ref, k_ref, v_ref, kc_ref, vc_ref, ga_ref, bias_ref, o_ref,
                 s_lat_ref, s_ctx_ref, p_lat_ref, p_ctx_ref, rl_ref, *, rows, group, n_batch, steps):
    n_units = n_batch * rows // group
    nk = WIN_H * GRID_W

    lane = lax.broadcasted_iota(jnp.int32, (GRID_W, LANES), 1)
    first_head = lane < HEAD_DIM

    def stack_heads(q2):
        zero = jnp.zeros_like(q2)
        return jnp.concatenate([jnp.where(first_head, q2, zero), jnp.where(first_head, zero, q2)], axis=0)

    def geometry(u, g):
        r = u * group + g
        if isinstance(r, int):
            bb, i = divmod(r, rows)
            rs = min(max(i - WIN_H // 2, 0), rows - WIN_H)
            return bb, i - rs, i * GRID_W, rs * GRID_W
        bb = r // rows
        i = r % rows
        rs = jnp.clip(i - WIN_H // 2, 0, rows - WIN_H)
        return bb, i - rs, pl.multiple_of(i * GRID_W, GRID_W), pl.multiple_of(rs * GRID_W, GRID_W)


    def scores_matmul(u):
        out = []
        for g in range(group):
            bb, off, tok0, key0 = geometry(u, g)
            qs = stack_heads(qrot_ref[bb, pl.ds(tok0, GRID_W), :])
            kband = k_ref[bb, pl.ds(key0, nk), :]
            qp = stack_heads(qpl_ref[bb, pl.ds(tok0, GRID_W), :])
            out.append((_dot_nt(qs, kband), off, _dot_nt(qp, kc_ref[bb])))
        return out


    def scores_store(slot, vals):
        for g, (s_lat, off, s_ctx) in enumerate(vals):
            s_lat_ref[slot, g] = s_lat + bias_ref[0, off]
            s_ctx_ref[slot, g] = s_ctx

    def softmax(slot):
        for g in range(group):
            s_lat = s_lat_ref[slot, g]
            s_ctx = s_ctx_ref[slot, g]
            m = jnp.maximum(jnp.max(s_lat, axis=-1, keepdims=True), jnp.max(s_ctx, axis=-1, keepdims=True))
            e_lat = jnp.exp2(s_lat - m)
            e_ctx = jnp.exp2(s_ctx - m)
            denom = jnp.sum(e_lat, axis=-1, keepdims=True) + jnp.sum(e_ctx, axis=-1, keepdims=True)
            p_lat_ref[slot, g] = e_lat.astype(BF16)
            p_ctx_ref[slot, g] = e_ctx.astype(BF16)
            rl_ref[slot, g] = jnp.broadcast_to(1.0 / denom, (2 * GRID_W, LANES))

    def pv_matmul(u, slot):
        out = []
        for g in range(group):
            bb, _, _, key0 = geometry(u, g)
            o = (jnp.dot(p_lat_ref[slot, g], v_ref[bb, pl.ds(key0, nk), :], preferred_element_type=F32)
                 + jnp.dot(p_ctx_ref[slot, g], vc_ref[bb], preferred_element_type=F32))
            out.append((o, rl_ref[slot, g]))
        return out

    def pv_store(u, vals):
        for g, (o, rl) in enumerate(vals):
            bb, _, tok0, _ = geometry(u, g)
            o = o * rl
            o2 = jnp.where(first_head, o[0:GRID_W], o[GRID_W:2 * GRID_W])
            gate = ga_ref[bb, pl.ds(tok0, GRID_W), :].astype(F32)
            o_ref[bb, pl.ds(tok0, GRID_W), :] = (o2 * gate).astype(BF16)

    scores_store(0, scores_matmul(0))
    sc = scores_matmul(1)
    softmax(0)
    scores_store(1, sc)

    def step(t, slot):
        sc = scores_matmul(t)
        pv = pv_matmul(t - 2, slot)
        softmax(1 - slot)
        scores_store(slot, sc)
        pv_store(t - 2, pv)

    n_steady, n_left = divmod(n_units - 2, steps)

    def steady(n, carry):
        for j in range(steps):
            step(steps * n + 2 + j, j % 2)
        return carry

    lax.fori_loop(0, n_steady, steady, 0)
    for j in range(n_left):
        step(steps * n_steady + 2 + j, j % 2)

    pv = pv_matmul(n_units - 2, 0)
    softmax(1)
    pv_store(n_units - 2, pv)
    pv_store(n_units - 1, pv_matmul(n_units - 1, 1))


def _attention(qrot, qpl, k, v, kc, vc, ga, bias, *, group, n_batch, steps):
    b, s, da = qrot.shape
    l = vc.shape[1]
    rows = s // GRID_W
    n_hp = da // LANES
    nq = 2 * GRID_W
    nk = WIN_H * GRID_W
    assert rows % (2 * group) == 0 and b % n_batch == 0 and steps % 2 == 0
    tok = pl.BlockSpec((n_batch, s, LANES), lambda hp, i: (i, 0, hp))
    ctx_tok = pl.BlockSpec((n_batch, l, LANES), lambda hp, i: (i, 0, hp))
    return pl.pallas_call(
        functools.partial(_attn_kernel, rows=rows, group=group, n_batch=n_batch, steps=steps),
        grid=(n_hp, b // n_batch),
        in_specs=[tok, tok, tok, tok, ctx_tok, ctx_tok, tok,
                  pl.BlockSpec((1,) + bias.shape[1:], lambda hp, i: (hp, 0, 0, 0))],
        out_specs=tok,
        out_shape=jax.ShapeDtypeStruct((b, s, da), BF16),
        scratch_shapes=[pltpu.VMEM((2, group, nq, nk), F32),
                        pltpu.VMEM((2, group, nq, l), F32),
                        pltpu.VMEM((2, group, nq, nk), BF16),
                        pltpu.VMEM((2, group, nq, l), BF16),
                        pltpu.VMEM((2, group, nq, LANES), F32)],
        compiler_params=pltpu.CompilerParams(vmem_limit_bytes=VMEM_LIMIT),
        name="attn",
    )(qrot, qpl, k, v, kc, vc, ga, bias)


def _out_proj_kernel(x_ref, a_ref, c_ref, ada_ref, w_ref, o_ref, *, da):
    gate = ada_ref[0, 2:3, :]
    upd = (jnp.dot(a_ref[0], w_ref[0:da, :], preferred_element_type=F32)
           + jnp.dot(c_ref[0], w_ref[da:, :], preferred_element_type=F32))
    o_ref[0] = x_ref[0] + gate * upd


def _out_proj(x, attn, conv, ada3, w_out, *, tm):
    b, s, d = x.shape
    da = attn.shape[2]
    dc = conv.shape[2]
    return pl.pallas_call(
        functools.partial(_out_proj_kernel, da=da),
        grid=(b, s // tm),
        in_specs=[pl.BlockSpec((1, tm, d), lambda i, j: (i, j, 0)),
                  pl.BlockSpec((1, tm, da), lambda i, j: (i, j, 0)),
                  pl.BlockSpec((1, tm, dc), lambda i, j: (i, j, 0)),
                  pl.BlockSpec((1, 3, d), lambda i, j: (i, 0, 0)),
                  pl.BlockSpec(w_out.shape, lambda i, j: (0, 0))],
        out_specs=pl.BlockSpec((1, tm, d), lambda i, j: (i, j, 0)),
        out_shape=jax.ShapeDtypeStruct((b, s, d), F32),
        compiler_params=pltpu.CompilerParams(vmem_limit_bytes=VMEM_LIMIT),
        name="out_proj",
    )(x, attn, conv, ada3, w_out)


def _rope_tables(s):
    nf = HEAD_DIM // 4
    inv = (ROPE_THETA ** (-np.arange(nf, dtype=np.float32) / nf)).astype(np.float32)
    pos = np.arange(s)
    lane = np.arange(LANES)
    d = lane % HEAD_DIM
    axis = d // (2 * nf)
    half = (d % (2 * nf)) // nf
    coord = np.where(axis[None, :] == 0, (pos // GRID_W)[:, None], (pos % GRID_W)[:, None]).astype(np.float32)
    ang = (coord * inv[d % nf][None, :]).astype(np.float32)
    cos = np.cos(ang).astype(np.float32)
    sin = np.sin(ang).astype(np.float32)
    sin_lo = np.where(half[None, :] == 0, -sin, 0.0).astype(np.float32)
    sin_hi = np.where(half[None, :] == 1, sin, 0.0).astype(np.float32)
    return jnp.asarray(cos), jnp.asarray(sin_lo), jnp.asarray(sin_hi)


def _bias_kernel(rpb_ref, o_ref):
    cq = lax.broadcasted_iota(jnp.int32, (GRID_W, LANES), 0)
    ck = lax.broadcasted_iota(jnp.int32, (GRID_W, LANES), 1) % GRID_W
    col_start = jnp.clip(cq - WIN_W // 2, 0, GRID_W - WIN_W)
    valid = (ck >= col_start) & (ck < col_start + WIN_W)

    def one_offset(off, carry):
        for a in range(2):
            for t in range(WIN_H // 2):
                dr0 = 2 * t - off + (WIN_H - 1)
                two = rpb_ref[a, pl.ds(dr0, 2), :]
                lanes = jnp.broadcast_to(jnp.concatenate([two[0:1], two[1:2]], axis=1), (GRID_W, LANES))
                toeplitz = pltpu.roll(lanes, LANES - (WIN_W - 1), 1, stride=1, stride_axis=0)
                o_ref[0, off, a * GRID_W:(a + 1) * GRID_W, t * LANES:(t + 1) * LANES] = jnp.where(
                    valid, toeplitz * LOG2_E, MASK_VALUE)
        return carry

    lax.fori_loop(0, WIN_H, one_offset, 0)


def _bias_table(rpb):
    nh, ndr, ndc = rpb.shape
    assert ndr == 2 * WIN_H - 1 and ndc == 2 * WIN_W - 1 and ndc <= GRID_W
    rpb_pad = jnp.pad(rpb, ((0, 0), (0, 0), (0, GRID_W - ndc)))
    return pl.pallas_call(
        _bias_kernel,
        grid=(nh // 2,),
        in_specs=[pl.BlockSpec((2, ndr, GRID_W), lambda hp: (hp, 0, 0))],
        out_specs=pl.BlockSpec((1, WIN_H, 2 * GRID_W, WIN_H * GRID_W), lambda hp: (hp, 0, 0, 0)),
        out_shape=jax.ShapeDtypeStruct((nh // 2, WIN_H, 2 * GRID_W, WIN_H * GRID_W), F32),
        name="bias_table",
    )(rpb_pad)


def kernel(x, c, ctx, c_ctx, w_ada, b_ada, norm_g, w_in, q_norm_g, k_norm_g, rpb, conv_w, conv_b, w_out):
    depth = w_ada.shape[0]
    b, s, d = x.shape
    dc = conv_w.shape[2]
    da = (w_in.shape[2] - 4 * dc) // 4
    n_heads = da // HEAD_DIM
    rows = s // GRID_W
    assert depth == 1 and s % GRID_W == 0 and rows >= WIN_H and da % LANES == 0

    cos, slo, shi = _rope_tables(s)
    assert da % MXU_DIM == 0 and MXU_DIM % HEAD_DIM == 0
    seg = np.arange(MXU_DIM) // HEAD_DIM
    bd = jnp.asarray((seg[:, None] == seg[None, :]).astype(np.float32), dtype=BF16)

    cond_rows = -(-(b + 1) // 8) * 8
    cond = jnp.zeros((cond_rows, d), F32).at[:b].set(c).at[b].set(c_ctx)
    ada3 = _adaln(cond, w_ada[0], b_ada[0]).reshape(cond_rows, 3, d)

    w_in_b = w_in[0].astype(BF16)
    ng = norm_g[0].reshape(1, d)
    qg = jnp.tile(q_norm_g[0] * (HEAD_DIM ** -0.5 * LOG2_E), n_heads).reshape(1, da)
    kg = jnp.tile(k_norm_g[0], n_heads).reshape(1, da)

    kc, vc = _ctx_kv(ctx, ada3, b, ng, w_in_b, kg, bd, tm=512)
    qrot, qpl, k, v, ga, conv = _in_proj(x, ada3, ng, w_in_b, qg, kg, cos, slo, shi, bd,
                                         conv_w[0], conv_b[0].reshape(1, dc), tm=1024, sub=512)
    attn = _attention(qrot, qpl, k, v, kc, vc, ga, _bias_table(rpb[0]), group=2, n_batch=4, steps=4)
    return _out_proj(x, attn, conv, ada3, w_out[0].astype(BF16), tm=1024)
```

```python
import functools

import numpy as np
import jax
import jax.numpy as jnp
from jax import lax
from jax.experimental import pallas as pl
from jax.experimental.pallas import tpu as pltpu

F32 = jnp.float32
BF16 = jnp.bfloat16

HEAD_DIM = 64
GRID_W = 64
WIN_H = 8
WIN_W = 16
CONV_K = 3
ROPE_THETA = 10000.0
RMS_EPS = 1e-6
MASK_VALUE = -1e30
LOG2_E = 1.4426950408889634

LANES = 128
MXU_DIM = 256
BF16_ROWS = 16
HALO = BF16_ROWS
VMEM_LIMIT = 56 * 1024 * 1024


def _silu(z):
    return z * jax.nn.sigmoid(z)


def _adaln_kernel(cond_ref, w_ref, b_ref, o_ref):
    a = _silu(cond_ref[...]).astype(BF16)
    o_ref[...] = jnp.dot(a, w_ref[...].astype(BF16), preferred_element_type=F32) + b_ref[...]


def _adaln(cond, w_ada, b_ada):
    rows, d = cond.shape
    n = w_ada.shape[1]
    tn = 1024
    return pl.pallas_call(
        _adaln_kernel,
        grid=(n // tn,),
        in_specs=[pl.BlockSpec((rows, d), lambda i: (0, 0)),
                  pl.BlockSpec((d, tn), lambda i: (0, i)),
                  pl.BlockSpec((1, tn), lambda i: (0, i))],
        out_specs=pl.BlockSpec((rows, tn), lambda i: (0, i)),
        out_shape=jax.ShapeDtypeStruct((rows, n), F32),
        compiler_params=pltpu.CompilerParams(vmem_limit_bytes=VMEM_LIMIT),
        name="adaln",
    )(cond, w_ada, b_ada.reshape(1, n))


def _modulated_norm(xt, mult, shift):
    ms = jnp.mean(xt * xt, axis=-1, keepdims=True)
    return (xt * lax.rsqrt(ms + RMS_EPS) * mult + shift).astype(BF16)


def _head_norm(t, gain, bd):
    t2 = (t * t).astype(BF16)
    w = bd.shape[0]
    ss = jnp.concatenate([jnp.dot(t2[:, c:c + w], bd, preferred_element_type=F32)
                          for c in range(0, t.shape[1], w)], axis=1)
    return t * lax.rsqrt(ss * (1.0 / HEAD_DIM) + RMS_EPS) * gain


def _rope(t, cos, sin_lo, sin_hi):
    outs = []
    for c in range(t.shape[1] // LANES):
        tc = t[:, c * LANES:(c + 1) * LANES]
        up = pltpu.roll(tc, LANES - HEAD_DIM // 4, 1)
        dn = pltpu.roll(tc, HEAD_DIM // 4, 1)
        outs.append(tc * cos + up * sin_lo + dn * sin_hi)
    return jnp.concatenate(outs, axis=1)


def _ctx_kernel(ctx_ref, ada_ref, ng_ref, wk_ref, wv_ref, kg_ref, bd_ref, kc_ref, vc_ref):
    shift = ada_ref[0, 0:1, :]
    scale = ada_ref[0, 1:2, :]
    mult = ng_ref[...] * (1.0 + scale)
    h = _modulated_norm(ctx_ref[...], mult, shift)
    kf = jnp.dot(h, wk_ref[...], preferred_element_type=F32)
    kn = _head_norm(kf, kg_ref[...], bd_ref[...])
    kc_ref[...] = kn.astype(BF16)
    vc_ref[...] = jnp.dot(h, wv_ref[...], preferred_element_type=F32).astype(BF16)


def _ctx_kv(ctx, ada3, ctx_row, norm_g, w_in, kg, bd, *, tm):
    b, l, d = ctx.shape
    d_attn = kg.shape[1]
    n = b * l
    assert n % tm == 0
    kc, vc = pl.pallas_call(
        _ctx_kernel,
        grid=(n // tm,),
        in_specs=[pl.BlockSpec((tm, d), lambda i: (i, 0)),
                  pl.BlockSpec((1, 3, d), lambda i: (ctx_row, 0, 0)),
                  pl.BlockSpec((1, d), lambda i: (0, 0)),
                  pl.BlockSpec((d, d_attn), lambda i: (0, 1)),
                  pl.BlockSpec((d, d_attn), lambda i: (0, 2)),
                  pl.BlockSpec((1, d_attn), lambda i: (0, 0)),
                  pl.BlockSpec(bd.shape, lambda i: (0, 0))],
        out_specs=[pl.BlockSpec((tm, d_attn), lambda i: (i, 0)),
                   pl.BlockSpec((tm, d_attn), lambda i: (i, 0))],
        out_shape=[jax.ShapeDtypeStruct((n, d_attn), BF16),
                   jax.ShapeDtypeStruct((n, d_attn), BF16)],
        compiler_params=pltpu.CompilerParams(vmem_limit_bytes=VMEM_LIMIT),
        name="ctx_kv",
    )(ctx.reshape(n, d), ada3, norm_g, w_in, w_in, kg, bd)
    return kc.reshape(b, l, d_attn), vc.reshape(b, l, d_attn)


def _in_proj_kernel(x_ref, xp_ref, xn_ref, ada_ref, ng_ref, w_ref, qg_ref, kg_ref,
                    cos_ref, slo_ref, shi_ref, bd_ref, cw_ref, cb_ref,
                    qrot_ref, qpl_ref, k_ref, v_ref, ga_ref, conv_ref,
                    hext_ref, cu_ref, *, tm, sub, da, dc):
    j = pl.program_id(1)
    nj = pl.num_programs(1)
    shift = ada_ref[0, 0:1, :]
    scale = ada_ref[0, 1:2, :]
    mult = ng_ref[...] * (1.0 + scale)

    def proj(src, lo, width):
        return jnp.dot(src, w_ref[:, lo:lo + width], preferred_element_type=F32)

    bd = bd_ref[...]
    hext_ref[0:HALO, :] = _modulated_norm(xp_ref[0], mult, shift)
    hext_ref[HALO + tm:, :] = _modulated_norm(xn_ref[0], mult, shift)

    for n in range(tm // sub):
        r0 = n * sub
        hext_ref[HALO + r0:HALO + r0 + sub, :] = _modulated_norm(x_ref[0, r0:r0 + sub, :], mult, shift)
    for n in range(tm // sub):
        r0 = n * sub
        rows = slice(r0, r0 + sub)
        h = hext_ref[HALO + r0:HALO + r0 + sub, :]
        cos, slo, shi = cos_ref[rows, :], slo_ref[rows, :], shi_ref[rows, :]


        hx = hext_ref[r0:r0 + sub + 2 * HALO, :]
        cu = proj(hx, 4 * da, dc) * proj(hx, 4 * da + 2 * dc, dc)
        row = lax.broadcasted_iota(jnp.int32, (sub + 2 * HALO, 1), 0) + r0
        inside = ((row >= HALO) | (j > 0)) & ((row < HALO + tm) | (j < nj - 1))
        cu_ref[n] = jnp.where(inside, cu, 0.0)
        y = (cb_ref[...]
             + cw_ref[0:1, :] * cu_ref[n, HALO - 1:HALO - 1 + sub, :]
             + cw_ref[1:2, :] * cu_ref[n, HALO:HALO + sub, :]
             + cw_ref[2:3, :] * cu_ref[n, HALO + 1:HALO + 1 + sub, :])
        bg = proj(h, 4 * da + dc, dc)
        zc = proj(h, 4 * da + 3 * dc, dc)
        conv_ref[0, rows, :] = (bg * y * _silu(zc)).astype(BF16)

        qn = _head_norm(proj(h, 0, da), qg_ref[...], bd)
        qpl_ref[0, rows, :] = qn.astype(BF16)
        qrot_ref[0, rows, :] = _rope(qn, cos, slo, shi).astype(BF16)

        kn = _head_norm(proj(h, da, da), kg_ref[...], bd)
        k_ref[0, rows, :] = _rope(kn, cos, slo, shi).astype(BF16)

        ga_ref[0, rows, :] = _silu(proj(h, 3 * da, da)).astype(BF16)
        v_ref[0, rows, :] = proj(h, 2 * da, da).astype(BF16)


def _in_proj(x, ada3, norm_g, w_in, qg, kg, cos, slo, shi, bd, conv_w, conv_b, *, tm, sub):
    b, s, d = x.shape
    assert s % tm == 0 and tm % sub == 0 and sub % HALO == 0
    dc = conv_w.shape[1]
    da = (w_in.shape[1] - 4 * dc) // 4
    nh = tm // HALO
    last_halo = s // HALO - 1
    tok = lambda width: pl.BlockSpec((1, tm, width), lambda i, j: (i, j, 0))
    const = lambda shape: pl.BlockSpec(shape, lambda i, j: (0,) * len(shape))
    tab = pl.BlockSpec((tm, LANES), lambda i, j: (j, 0))
    out_tok = jax.ShapeDtypeStruct((b, s, da), BF16)
    return pl.pallas_call(
        functools.partial(_in_proj_kernel, tm=tm, sub=sub, da=da, dc=dc),
        grid=(b, s // tm),
        in_specs=[tok(d),
                  pl.BlockSpec((1, HALO, d), lambda i, j: (i, jnp.maximum(j * nh - 1, 0), 0)),
                  pl.BlockSpec((1, HALO, d), lambda i, j: (i, jnp.minimum((j + 1) * nh, last_halo), 0)),
                  pl.BlockSpec((1, 3, d), lambda i, j: (i, 0, 0)),
                  const((1, d)), const(w_in.shape), const((1, da)), const((1, da)),
                  tab, tab, tab, const(bd.shape), const((CONV_K, dc)), const((1, dc))],
        out_specs=[tok(da), tok(da), tok(da), tok(da), tok(da), tok(dc)],
        out_shape=[out_tok, out_tok, out_tok, out_tok, out_tok,
                   jax.ShapeDtypeStruct((b, s, dc), BF16)],
        scratch_shapes=[pltpu.VMEM((tm + 2 * HALO, d), BF16),
                        pltpu.VMEM((tm // sub, sub + 2 * HALO, dc), F32)],
        compiler_params=pltpu.CompilerParams(vmem_limit_bytes=VMEM_LIMIT),
        name="in_proj",
    )(x, x, x, ada3, norm_g, w_in, qg, kg, cos, slo, shi, bd, conv_w, conv_b)


def _dot_nt(a, b):
    return lax.dot_general(a, b, (((1,), (1,)), ((), ())), preferred_element_type=F32)


def _attn_kernel(qrot_ref, qpl_ref, k_ref, v_ref, kc_ref, vc_ref, ga_ref, bias_ref, o_ref,
                 s_lat_ref, s_ctx_ref, p_lat_ref, p_ctx_ref, *, rows, group, n_batch, steps):
    n_units = n_batch * rows // group
    nk = WIN_H * GRID_W

    lane = lax.broadcasted_iota(jnp.int32, (GRID_W, LANES), 1)
    first_head = lane < HEAD_DIM

    def stack_heads(q2):
        zero = jnp.zeros_like(q2)
        return jnp.concatenate([jnp.where(first_head, q2, zero), jnp.where(first_head, zero, q2)], axis=0)

    def geometry(u, g):
        r = u * group + g
        if isinstance(r, int):
            bb, i = divmod(r, rows)
            rs = min(max(i - WIN_H // 2, 0), rows - WIN_H)
            return bb, i - rs, i * GRID_W, rs * GRID_W
        bb = r // rows
        i = r % rows
        rs = jnp.clip(i - WIN_H // 2, 0, rows - WIN_H)
        return bb, i - rs, pl.multiple_of(i * GRID_W, GRID_W), pl.multiple_of(rs * GRID_W, GRID_W)


    def scores_matmul(u):
        out = []
        for g in range(group):
            bb, off, tok0, key0 = geometry(u, g)
            qs = stack_heads(qrot_ref[bb, pl.ds(tok0, GRID_W), :])
            kband = k_ref[bb, pl.ds(key0, nk), :]
            qp = stack_heads(qpl_ref[bb, pl.ds(tok0, GRID_W), :])
            out.append((_dot_nt(qs, kband), off, _dot_nt(qp, kc_ref[bb])))
        return out


    def scores_store(slot, vals):
        for g, (s_lat, off, s_ctx) in enumerate(vals):
            s_lat_ref[slot, g] = s_lat + bias_ref[0, off]
            s_ctx_ref[slot, g] = s_ctx

    def softmax(slot):
        for g in range(group):
            s_lat = s_lat_ref[slot, g]
            s_ctx = s_ctx_ref[slot, g]
            m = jnp.maximum(jnp.max(s_lat, axis=-1, keepdims=True), jnp.max(s_ctx, axis=-1, keepdims=True))
            p_lat_ref[slot, g] = jnp.exp2(s_lat - m).astype(BF16)
            p_ctx_ref[slot, g] = jnp.exp2(s_ctx - m).astype(BF16)

    ones_lat = jnp.ones((nk, LANES), BF16)
    ones_ctx = jnp.ones((kc_ref.shape[1], LANES), BF16)

    def pv_matmul(u, slot):
        out = []
        for g in range(group):
            bb, _, _, key0 = geometry(u, g)
            v_lat = jnp.concatenate([v_ref[bb, pl.ds(key0, nk), :], ones_lat], axis=1)
            v_ctx = jnp.concatenate([vc_ref[bb], ones_ctx], axis=1)
            out.append(jnp.dot(p_lat_ref[slot, g], v_lat, preferred_element_type=F32)
                       + jnp.dot(p_ctx_ref[slot, g], v_ctx, preferred_element_type=F32))
        return out

    def pv_store(u, vals):
        for g, ol in enumerate(vals):
            bb, _, tok0, _ = geometry(u, g)
            o = ol[:, 0:LANES] * (1.0 / ol[:, LANES:])
            o2 = jnp.where(first_head, o[0:GRID_W], o[GRID_W:2 * GRID_W])
            gate = ga_ref[bb, pl.ds(tok0, GRID_W), :].astype(F32)
            o_ref[bb, pl.ds(tok0, GRID_W), :] = (o2 * gate).astype(BF16)

    scores_store(0, scores_matmul(0))
    sc = scores_matmul(1)
    softmax(0)
    scores_store(1, sc)

    def step(t, slot):
        sc = scores_matmul(t)
        pv = pv_matmul(t - 2, slot)
        softmax(1 - slot)
        scores_store(slot, sc)
        pv_store(t - 2, pv)

    n_steady, n_left = divmod(n_units - 2, steps)

    def steady(n, carry):
        for j in range(steps):
            step(steps * n + 2 + j, j % 2)
        return carry

    lax.fori_loop(0, n_steady, steady, 0)
    for j in range(n_left):
        step(steps * n_steady + 2 + j, j % 2)

    pv = pv_matmul(n_units - 2, 0)
    softmax(1)
    pv_store(n_units - 2, pv)
    pv_store(n_units - 1, pv_matmul(n_units - 1, 1))


def _attention(qrot, qpl, k, v, kc, vc, ga, bias, *, group, n_batch, steps):
    b, s, da = qrot.shape
    l = vc.shape[1]
    rows = s // GRID_W
    n_hp = da // LANES
    nq = 2 * GRID_W
    nk = WIN_H * GRID_W
    assert rows % (2 * group) == 0 and b % n_batch == 0 and steps % 2 == 0
    tok = pl.BlockSpec((n_batch, s, LANES), lambda hp, i: (i, 0, hp))
    ctx_tok = pl.BlockSpec((n_batch, l, LANES), lambda hp, i: (i, 0, hp))
    return pl.pallas_call(
        functools.partial(_attn_kernel, rows=rows, group=group, n_batch=n_batch, steps=steps),
        grid=(n_hp, b // n_batch),
        in_specs=[tok, tok, tok, tok, ctx_tok, ctx_tok, tok,
                  pl.BlockSpec((1,) + bias.shape[1:], lambda hp, i: (hp, 0, 0, 0))],
        out_specs=tok,
        out_shape=jax.ShapeDtypeStruct((b, s, da), BF16),
        scratch_shapes=[pltpu.VMEM((2, group, nq, nk), F32),
                        pltpu.VMEM((2, group, nq, l), F32),
                        pltpu.VMEM((2, group, nq, nk), BF16),
                        pltpu.VMEM((2, group, nq, l), BF16)],
        compiler_params=pltpu.CompilerParams(vmem_limit_bytes=VMEM_LIMIT),
        name="attn",
    )(qrot, qpl, k, v, kc, vc, ga, bias)


def _out_proj_kernel(x_ref, a_ref, c_ref, ada_ref, w_ref, o_ref, *, da):
    gate = ada_ref[0, 2:3, :]
    upd = (jnp.dot(a_ref[0], w_ref[0:da, :], preferred_element_type=F32)
           + jnp.dot(c_ref[0], w_ref[da:, :], preferred_element_type=F32))
    o_ref[0] = x_ref[0] + gate * upd


def _out_proj(x, attn, conv, ada3, w_out, *, tm):
    b, s, d = x.shape
    da = attn.shape[2]
    dc = conv.shape[2]
    return pl.pallas_call(
        functools.partial(_out_proj_kernel, da=da),
        grid=(b, s // tm),
        in_specs=[pl.BlockSpec((1, tm, d), lambda i, j: (i, j, 0)),
                  pl.BlockSpec((1, tm, da), lambda i, j: (i, j, 0)),
                  pl.BlockSpec((1, tm, dc), lambda i, j: (i, j, 0)),
                  pl.BlockSpec((1, 3, d), lambda i, j: (i, 0, 0)),
                  pl.BlockSpec(w_out.shape, lambda i, j: (0, 0))],
        out_specs=pl.BlockSpec((1, tm, d), lambda i, j: (i, j, 0)),
        out_shape=jax.ShapeDtypeStruct((b, s, d), F32),
        compiler_params=pltpu.CompilerParams(vmem_limit_bytes=VMEM_LIMIT),
        name="out_proj",
    )(x, attn, conv, ada3, w_out)


def _rope_tables(s):
    nf = HEAD_DIM // 4
    inv = (ROPE_THETA ** (-np.arange(nf, dtype=np.float32) / nf)).astype(np.float32)
    pos = np.arange(s)
    lane = np.arange(LANES)
    d = lane % HEAD_DIM
    axis = d // (2 * nf)
    half = (d % (2 * nf)) // nf
    coord = np.where(axis[None, :] == 0, (pos // GRID_W)[:, None], (pos % GRID_W)[:, None]).astype(np.float32)
    ang = (coord * inv[d % nf][None, :]).astype(np.float32)
    cos = np.cos(ang).astype(np.float32)
    sin = np.sin(ang).astype(np.float32)
    sin_lo = np.where(half[None, :] == 0, -sin, 0.0).astype(np.float32)
    sin_hi = np.where(half[None, :] == 1, sin, 0.0).astype(np.float32)
    return jnp.asarray(cos), jnp.asarray(sin_lo), jnp.asarray(sin_hi)


def _bias_kernel(rpb_ref, o_ref):
    cq = lax.broadcasted_iota(jnp.int32, (GRID_W, LANES), 0)
    ck = lax.broadcasted_iota(jnp.int32, (GRID_W, LANES), 1) % GRID_W
    col_start = jnp.clip(cq - WIN_W // 2, 0, GRID_W - WIN_W)
    valid = (ck >= col_start) & (ck < col_start + WIN_W)

    def one_offset(off, carry):
        for a in range(2):
            for t in range(WIN_H // 2):
                dr0 = 2 * t - off + (WIN_H - 1)
                two = rpb_ref[a, pl.ds(dr0, 2), :]
                lanes = jnp.broadcast_to(jnp.concatenate([two[0:1], two[1:2]], axis=1), (GRID_W, LANES))
                toeplitz = pltpu.roll(lanes, LANES - (WIN_W - 1), 1, stride=1, stride_axis=0)
                o_ref[0, off, a * GRID_W:(a + 1) * GRID_W, t * LANES:(t + 1) * LANES] = jnp.where(
                    valid, toeplitz * LOG2_E, MASK_VALUE)
        return carry

    lax.fori_loop(0, WIN_H, one_offset, 0)


def _bias_table(rpb):
    nh, ndr, ndc = rpb.shape
    assert ndr == 2 * WIN_H - 1 and ndc == 2 * WIN_W - 1 and ndc <= GRID_W
    rpb_pad = jnp.pad(rpb, ((0, 0), (0, 0), (0, GRID_W - ndc)))
    return pl.pallas_call(
        _bias_kernel,
        grid=(nh // 2,),
        in_specs=[pl.BlockSpec((2, ndr, GRID_W), lambda hp: (hp, 0, 0))],
        out_specs=pl.BlockSpec((1, WIN_H, 2 * GRID_W, WIN_H * GRID_W), lambda hp: (hp, 0, 0, 0)),
        out_shape=jax.ShapeDtypeStruct((nh // 2, WIN_H, 2 * GRID_W, WIN_H * GRID_W), F32),
        name="bias_table",
    )(rpb_pad)


def kernel(x, c, ctx, c_ctx, w_ada, b_ada, norm_g, w_in, q_norm_g, k_norm_g, rpb, conv_w, conv_b, w_out):
    depth = w_ada.shape[0]
    b, s, d = x.shape
    dc = conv_w.shape[2]
    da = (w_in.shape[2] - 4 * dc) // 4
    n_heads = da // HEAD_DIM
    rows = s // GRID_W
    assert depth == 1 and s % GRID_W == 0 and rows >= WIN_H and da % LANES == 0

    cos, slo, shi = _rope_tables(s)
    assert da % MXU_DIM == 0 and MXU_DIM % HEAD_DIM == 0
    seg = np.arange(MXU_DIM) // HEAD_DIM
    bd = jnp.asarray((seg[:, None] == seg[None, :]).astype(np.float32), dtype=BF16)

    cond_rows = -(-(b + 1) // 8) * 8
    cond = jnp.zeros((cond_rows, d), F32).at[:b].set(c).at[b].set(c_ctx)
    ada3 = _adaln(cond, w_ada[0], b_ada[0]).reshape(cond_rows, 3, d)

    w_in_b = w_in[0].astype(BF16)
    ng = norm_g[0].reshape(1, d)
    qg = jnp.tile(q_norm_g[0] * (HEAD_DIM ** -0.5 * LOG2_E), n_heads).reshape(1, da)
    kg = jnp.tile(k_norm_g[0], n_heads).reshape(1, da)

    kc, vc = _ctx_kv(ctx, ada3, b, ng, w_in_b, kg, bd, tm=512)
    qrot, qpl, k, v, ga, conv = _in_proj(x, ada3, ng, w_in_b, qg, kg, cos, slo, shi, bd,
                                         conv_w[0], conv_b[0].reshape(1, dc), tm=1024, sub=512)
    attn = _attention(qrot, qpl, k, v, kc, vc, ga, _bias_table(rpb[0]), group=2, n_batch=4, steps=4)
    return _out_proj(x, attn, conv, ada3, w_out[0].astype(BF16), tm=1024)
```

```python
import functools

import numpy as np
import jax
import jax.numpy as jnp
from jax import lax
from jax.experimental import pallas as pl
from jax.experimental.pallas import tpu as pltpu

F32 = jnp.float32
BF16 = jnp.bfloat16

HEAD_DIM = 64
GRID_W = 64
WIN_H = 8
WIN_W = 16
CONV_K = 3
ROPE_THETA = 10000.0
RMS_EPS = 1e-6
MASK_VALUE = -1e30
LOG2_E = 1.4426950408889634

LANES = 128
MXU_DIM = 256
BF16_ROWS = 16
HALO = BF16_ROWS
VMEM_LIMIT = 56 * 1024 * 1024


def _silu(z):
    return z * jax.nn.sigmoid(z)


def _prep_kernel(cond_ref, w_ref, b_ref, rpb_ref, ada_ref, bias_ref):
    a = _silu(cond_ref[...]).astype(BF16)
    ada_ref[...] = jnp.dot(a, w_ref[...].astype(BF16), preferred_element_type=F32) + b_ref[...]
    _bias_block(rpb_ref, bias_ref)


def _prep(cond, w_ada, b_ada, rpb):
    rows, d = cond.shape
    n = w_ada.shape[1]
    nh, ndr, ndc = rpb.shape
    steps = nh // 2
    tn = n // steps
    assert ndr == 2 * WIN_H - 1 and ndc == 2 * WIN_W - 1 and ndc <= GRID_W and tn % LANES == 0
    rpb_pad = jnp.pad(rpb, ((0, 0), (0, 0), (0, GRID_W - ndc)))
    return pl.pallas_call(
        _prep_kernel,
        grid=(steps,),
        in_specs=[pl.BlockSpec((rows, d), lambda i: (0, 0)),
                  pl.BlockSpec((d, tn), lambda i: (0, i)),
                  pl.BlockSpec((1, tn), lambda i: (0, i)),
                  pl.BlockSpec((2, ndr, GRID_W), lambda i: (i, 0, 0))],
        out_specs=[pl.BlockSpec((rows, tn), lambda i: (0, i)),
                   pl.BlockSpec((1, WIN_H, 2 * GRID_W, WIN_H * GRID_W), lambda i: (i, 0, 0, 0))],
        out_shape=[jax.ShapeDtypeStruct((rows, n), F32),
                   jax.ShapeDtypeStruct((steps, WIN_H, 2 * GRID_W, WIN_H * GRID_W), F32)],
        compiler_params=pltpu.CompilerParams(vmem_limit_bytes=VMEM_LIMIT),
        name="prep",
    )(cond, w_ada, b_ada.reshape(1, n), rpb_pad)


def _modulated_norm(xt, mult, shift):
    ms = jnp.mean(xt * xt, axis=-1, keepdims=True)
    return (xt * lax.rsqrt(ms + RMS_EPS) * mult + shift).astype(BF16)


def _head_norm(t, gain, bd):
    t2 = (t * t).astype(BF16)
    w = bd.shape[0]
    ss = jnp.concatenate([jnp.dot(t2[:, c:c + w], bd, preferred_element_type=F32)
                          for c in range(0, t.shape[1], w)], axis=1)
    return t * lax.rsqrt(ss * (1.0 / HEAD_DIM) + RMS_EPS) * gain


def _rope(t, cos, sin_lo, sin_hi):
    outs = []
    for c in range(t.shape[1] // LANES):
        tc = t[:, c * LANES:(c + 1) * LANES]
        up = pltpu.roll(tc, LANES - HEAD_DIM // 4, 1)
        dn = pltpu.roll(tc, HEAD_DIM // 4, 1)
        outs.append(tc * cos + up * sin_lo + dn * sin_hi)
    return jnp.concatenate(outs, axis=1)


def _ctx_kernel(ctx_ref, ada_ref, ng_ref, wk_ref, wv_ref, kg_ref, bd_ref, kc_ref, vc_ref):
    shift = ada_ref[0, 0:1, :]
    scale = ada_ref[0, 1:2, :]
    mult = ng_ref[...] * (1.0 + scale)
    h = _modulated_norm(ctx_ref[...], mult, shift)
    kf = jnp.dot(h, wk_ref[...], preferred_element_type=F32)
    kn = _head_norm(kf, kg_ref[...], bd_ref[...])
    kc_ref[...] = kn.astype(BF16)
    vc_ref[...] = jnp.dot(h, wv_ref[...], preferred_element_type=F32).astype(BF16)


def _ctx_kv(ctx, ada3, ctx_row, norm_g, w_in, kg, bd, *, tm):
    b, l, d = ctx.shape
    d_attn = kg.shape[1]
    n = b * l
    assert n % tm == 0
    kc, vc = pl.pallas_call(
        _ctx_kernel,
        grid=(n // tm,),
        in_specs=[pl.BlockSpec((tm, d), lambda i: (i, 0)),
                  pl.BlockSpec((1, 3, d), lambda i: (ctx_row, 0, 0)),
                  pl.BlockSpec((1, d), lambda i: (0, 0)),
                  pl.BlockSpec((d, d_attn), lambda i: (0, 1)),
                  pl.BlockSpec((d, d_attn), lambda i: (0, 2)),
                  pl.BlockSpec((1, d_attn), lambda i: (0, 0)),
                  pl.BlockSpec(bd.shape, lambda i: (0, 0))],
        out_specs=[pl.BlockSpec((tm, d_attn), lambda i: (i, 0)),
                   pl.BlockSpec((tm, d_attn), lambda i: (i, 0))],
        out_shape=[jax.ShapeDtypeStruct((n, d_attn), BF16),
                   jax.ShapeDtypeStruct((n, d_attn), BF16)],
        compiler_params=pltpu.CompilerParams(vmem_limit_bytes=VMEM_LIMIT),
        name="ctx_kv",
    )(ctx.reshape(n, d), ada3, norm_g, w_in, w_in, kg, bd)
    return kc.reshape(b, l, d_attn), vc.reshape(b, l, d_attn)


def _in_proj_kernel(x_ref, xp_ref, xn_ref, ada_ref, ng_ref, w_ref, qg_ref, kg_ref,
                    cos_ref, slo_ref, shi_ref, bd_ref, cw_ref, cb_ref,
                    qrot_ref, qpl_ref, k_ref, v_ref, ga_ref, conv_ref,
                    hext_ref, cu_ref, *, tm, sub, da, dc):
    j = pl.program_id(1)
    nj = pl.num_programs(1)
    shift = ada_ref[0, 0:1, :]
    scale = ada_ref[0, 1:2, :]
    mult = ng_ref[...] * (1.0 + scale)

    def proj(src, lo, width):
        return jnp.dot(src, w_ref[:, lo:lo + width], preferred_element_type=F32)

    bd = bd_ref[...]
    hext_ref[0:HALO, :] = _modulated_norm(xp_ref[0], mult, shift)
    hext_ref[HALO + tm:, :] = _modulated_norm(xn_ref[0], mult, shift)

    for n in range(tm // sub):
        r0 = n * sub
        hext_ref[HALO + r0:HALO + r0 + sub, :] = _modulated_norm(x_ref[0, r0:r0 + sub, :], mult, shift)
    for n in range(tm // sub):
        r0 = n * sub
        rows = slice(r0, r0 + sub)
        h = hext_ref[HALO + r0:HALO + r0 + sub, :]
        cos, slo, shi = cos_ref[rows, :], slo_ref[rows, :], shi_ref[rows, :]


        hx = hext_ref[r0:r0 + sub + 2 * HALO, :]
        cu = proj(hx, 4 * da, dc) * proj(hx, 4 * da + 2 * dc, dc)
        row = lax.broadcasted_iota(jnp.int32, (sub + 2 * HALO, 1), 0) + r0
        inside = ((row >= HALO) | (j > 0)) & ((row < HALO + tm) | (j < nj - 1))
        cu_ref[n] = jnp.where(inside, cu, 0.0)
        y = (cb_ref[...]
             + cw_ref[0:1, :] * cu_ref[n, HALO - 1:HALO - 1 + sub, :]
             + cw_ref[1:2, :] * cu_ref[n, HALO:HALO + sub, :]
             + cw_ref[2:3, :] * cu_ref[n, HALO + 1:HALO + 1 + sub, :])
        bg = proj(h, 4 * da + dc, dc)
        zc = proj(h, 4 * da + 3 * dc, dc)
        conv_ref[0, rows, :] = (bg * y * _silu(zc)).astype(BF16)

        qn = _head_norm(proj(h, 0, da), qg_ref[...], bd)
        qpl_ref[0, rows, :] = qn.astype(BF16)
        qrot_ref[0, rows, :] = _rope(qn, cos, slo, shi).astype(BF16)

        kn = _head_norm(proj(h, da, da), kg_ref[...], bd)
        k_ref[0, rows, :] = _rope(kn, cos, slo, shi).astype(BF16)

        ga_ref[0, rows, :] = _silu(proj(h, 3 * da, da)).astype(BF16)
        v_ref[0, rows, :] = proj(h, 2 * da, da).astype(BF16)


def _in_proj(x, ada3, norm_g, w_in, qg, kg, cos, slo, shi, bd, conv_w, conv_b, *, tm, sub):
    b, s, d = x.shape
    assert s % tm == 0 and tm % sub == 0 and sub % HALO == 0
    dc = conv_w.shape[1]
    da = (w_in.shape[1] - 4 * dc) // 4
    nh = tm // HALO
    last_halo = s // HALO - 1
    tok = lambda width: pl.BlockSpec((1, tm, width), lambda i, j: (i, j, 0))
    const = lambda shape: pl.BlockSpec(shape, lambda i, j: (0,) * len(shape))
    tab = pl.BlockSpec((tm, LANES), lambda i, j: (j, 0))
    out_tok = jax.ShapeDtypeStruct((b, s, da), BF16)
    return pl.pallas_call(
        functools.partial(_in_proj_kernel, tm=tm, sub=sub, da=da, dc=dc),
        grid=(b, s // tm),
        in_specs=[tok(d),
                  pl.BlockSpec((1, HALO, d), lambda i, j: (i, jnp.maximum(j * nh - 1, 0), 0)),
                  pl.BlockSpec((1, HALO, d), lambda i, j: (i, jnp.minimum((j + 1) * nh, last_halo), 0)),
                  pl.BlockSpec((1, 3, d), lambda i, j: (i, 0, 0)),
                  const((1, d)), const(w_in.shape), const((1, da)), const((1, da)),
                  tab, tab, tab, const(bd.shape), const((CONV_K, dc)), const((1, dc))],
        out_specs=[tok(da), tok(da), tok(da), tok(da), tok(da), tok(dc)],
        out_shape=[out_tok, out_tok, out_tok, out_tok, out_tok,
                   jax.ShapeDtypeStruct((b, s, dc), BF16)],
        scratch_shapes=[pltpu.VMEM((tm + 2 * HALO, d), BF16),
                        pltpu.VMEM((tm // sub, sub + 2 * HALO, dc), F32)],
        compiler_params=pltpu.CompilerParams(vmem_limit_bytes=VMEM_LIMIT),
        name="in_proj",
    )(x, x, x, ada3, norm_g, w_in, qg, kg, cos, slo, shi, bd, conv_w, conv_b)


def _dot_nt(a, b):
    return lax.dot_general(a, b, (((1,), (1,)), ((), ())), preferred_element_type=F32)


def _attn_kernel(qrot_ref, qpl_ref, k_ref, v_ref, kc_ref, vc_ref, ga_ref, bias_ref, o_ref,
                 s_lat_ref, s_ctx_ref, p_lat_ref, p_ctx_ref, *, rows, group, n_batch, steps):
    n_units = n_batch * rows // group
    nk = WIN_H * GRID_W

    lane = lax.broadcasted_iota(jnp.int32, (GRID_W, LANES), 1)
    first_head = lane < HEAD_DIM

    def stack_heads(q2):
        zero = jnp.zeros_like(q2)
        return jnp.concatenate([jnp.where(first_head, q2, zero), jnp.where(first_head, zero, q2)], axis=0)

    def geometry(u, g):
        r = u * group + g
        if isinstance(r, int):
            bb, i = divmod(r, rows)
            rs = min(max(i - WIN_H // 2, 0), rows - WIN_H)
            return bb, i - rs, i * GRID_W, rs * GRID_W
        bb = r // rows
        i = r % rows
        rs = jnp.clip(i - WIN_H // 2, 0, rows - WIN_H)
        return bb, i - rs, pl.multiple_of(i * GRID_W, GRID_W), pl.multiple_of(rs * GRID_W, GRID_W)


    def scores_matmul(u):
        out = []
        for g in range(group):
            bb, off, tok0, key0 = geometry(u, g)
            qs = stack_heads(qrot_ref[bb, pl.ds(tok0, GRID_W), :])
            kband = k_ref[bb, pl.ds(key0, nk), :]
            qp = stack_heads(qpl_ref[bb, pl.ds(tok0, GRID_W), :])
            out.append((_dot_nt(qs, kband), off, _dot_nt(qp, kc_ref[bb])))
        return out


    def scores_store(slot, vals):
        for g, (s_lat, off, s_ctx) in enumerate(vals):
            s_lat_ref[slot, g] = s_lat + bias_ref[0, off]
            s_ctx_ref[slot, g] = s_ctx

    def softmax(slot):
        for g in range(group):
            s_lat = s_lat_ref[slot, g]
            s_ctx = s_ctx_ref[slot, g]
            m = jnp.maximum(jnp.max(s_lat, axis=-1, keepdims=True), jnp.max(s_ctx, axis=-1, keepdims=True))
            p_lat_ref[slot, g] = jnp.exp2(s_lat - m).astype(BF16)
            p_ctx_ref[slot, g] = jnp.exp2(s_ctx - m).astype(BF16)

    ones_lat = jnp.ones((nk, LANES), BF16)
    ones_ctx = jnp.ones((kc_ref.shape[1], LANES), BF16)

    def pv_matmul(u, slot):
        out = []
        for g in range(group):
            bb, _, _, key0 = geometry(u, g)
            v_lat = jnp.concatenate([v_ref[bb, pl.ds(key0, nk), :], ones_lat], axis=1)
            v_ctx = jnp.concatenate([vc_ref[bb], ones_ctx], axis=1)
            out.append(jnp.dot(p_lat_ref[slot, g], v_lat, preferred_element_type=F32)
                       + jnp.dot(p_ctx_ref[slot, g], v_ctx, preferred_element_type=F32))
        return out

    def pv_store(u, vals):
        for g, ol in enumerate(vals):
            bb, _, tok0, _ = geometry(u, g)
            o = ol[:, 0:LANES] * (1.0 / ol[:, LANES:])
            o2 = jnp.where(first_head, o[0:GRID_W], o[GRID_W:2 * GRID_W])
            gate = ga_ref[bb, pl.ds(tok0, GRID_W), :].astype(F32)
            o_ref[bb, pl.ds(tok0, GRID_W), :] = (o2 * gate).astype(BF16)

    scores_store(0, scores_matmul(0))
    sc = scores_matmul(1)
    softmax(0)
    scores_store(1, sc)

    def step(t, slot):
        sc = scores_matmul(t)
        pv = pv_matmul(t - 2, slot)
        softmax(1 - slot)
        scores_store(slot, sc)
        pv_store(t - 2, pv)

    n_steady, n_left = divmod(n_units - 2, steps)

    def steady(n, carry):
        for j in range(steps):
            step(steps * n + 2 + j, j % 2)
        return carry

    lax.fori_loop(0, n_steady, steady, 0)
    for j in range(n_left):
        step(steps * n_steady + 2 + j, j % 2)

    pv = pv_matmul(n_units - 2, 0)
    softmax(1)
    pv_store(n_units - 2, pv)
    pv_store(n_units - 1, pv_matmul(n_units - 1, 1))


def _attention(qrot, qpl, k, v, kc, vc, ga, bias, *, group, n_batch, steps):
    b, s, da = qrot.shape
    l = vc.shape[1]
    rows = s // GRID_W
    n_hp = da // LANES
    nq = 2 * GRID_W
    nk = WIN_H * GRID_W
    assert rows % (2 * group) == 0 and b % n_batch == 0 and steps % 2 == 0
    tok = pl.BlockSpec((n_batch, s, LANES), lambda hp, i: (i, 0, hp))
    ctx_tok = pl.BlockSpec((n_batch, l, LANES), lambda hp, i: (i, 0, hp))
    return pl.pallas_call(
        functools.partial(_attn_kernel, rows=rows, group=group, n_batch=n_batch, steps=steps),
        grid=(n_hp, b // n_batch),
        in_specs=[tok, tok, tok, tok, ctx_tok, ctx_tok, tok,
                  pl.BlockSpec((1,) + bias.shape[1:], lambda hp, i: (hp, 0, 0, 0))],
        out_specs=tok,
        out_shape=jax.ShapeDtypeStruct((b, s, da), BF16),
        scratch_shapes=[pltpu.VMEM((2, group, nq, nk), F32),
                        pltpu.VMEM((2, group, nq, l), F32),
                        pltpu.VMEM((2, group, nq, nk), BF16),
                        pltpu.VMEM((2, group, nq, l), BF16)],
        compiler_params=pltpu.CompilerParams(vmem_limit_bytes=VMEM_LIMIT),
        name="attn",
    )(qrot, qpl, k, v, kc, vc, ga, bias)


def _out_proj_kernel(x_ref, a_ref, c_ref, ada_ref, w_ref, o_ref, *, da):
    gate = ada_ref[0, 2:3, :]
    upd = (jnp.dot(a_ref[0], w_ref[0:da, :], preferred_element_type=F32)
           + jnp.dot(c_ref[0], w_ref[da:, :], preferred_element_type=F32))
    o_ref[0] = x_ref[0] + gate * upd


def _out_proj(x, attn, conv, ada3, w_out, *, tm):
    b, s, d = x.shape
    da = attn.shape[2]
    dc = conv.shape[2]
    return pl.pallas_call(
        functools.partial(_out_proj_kernel, da=da),
        grid=(b, s // tm),
        in_specs=[pl.BlockSpec((1, tm, d), lambda i, j: (i, j, 0)),
                  pl.BlockSpec((1, tm, da), lambda i, j: (i, j, 0)),
                  pl.BlockSpec((1, tm, dc), lambda i, j: (i, j, 0)),
                  pl.BlockSpec((1, 3, d), lambda i, j: (i, 0, 0)),
                  pl.BlockSpec(w_out.shape, lambda i, j: (0, 0))],
        out_specs=pl.BlockSpec((1, tm, d), lambda i, j: (i, j, 0)),
        out_shape=jax.ShapeDtypeStruct((b, s, d), F32),
        compiler_params=pltpu.CompilerParams(vmem_limit_bytes=VMEM_LIMIT),
        name="out_proj",
    )(x, attn, conv, ada3, w_out)


def _rope_tables(s):
    nf = HEAD_DIM // 4
    inv = (ROPE_THETA ** (-np.arange(nf, dtype=np.float32) / nf)).astype(np.float32)
    pos = np.arange(s)
    lane = np.arange(LANES)
    d = lane % HEAD_DIM
    axis = d // (2 * nf)
    half = (d % (2 * nf)) // nf
    coord = np.where(axis[None, :] == 0, (pos // GRID_W)[:, None], (pos % GRID_W)[:, None]).astype(np.float32)
    ang = (coord * inv[d % nf][None, :]).astype(np.float32)
    cos = np.cos(ang).astype(np.float32)
    sin = np.sin(ang).astype(np.float32)
    sin_lo = np.where(half[None, :] == 0, -sin, 0.0).astype(np.float32)
    sin_hi = np.where(half[None, :] == 1, sin, 0.0).astype(np.float32)
    return jnp.asarray(cos), jnp.asarray(sin_lo), jnp.asarray(sin_hi)


def _bias_block(rpb_ref, o_ref):
    cq = lax.broadcasted_iota(jnp.int32, (GRID_W, LANES), 0)
    ck = lax.broadcasted_iota(jnp.int32, (GRID_W, LANES), 1) % GRID_W
    col_start = jnp.clip(cq - WIN_W // 2, 0, GRID_W - WIN_W)
    valid = (ck >= col_start) & (ck < col_start + WIN_W)

    def one_offset(off, carry):
        for a in range(2):
            for t in range(WIN_H // 2):
                dr0 = 2 * t - off + (WIN_H - 1)
                two = rpb_ref[a, pl.ds(dr0, 2), :]
                lanes = jnp.broadcast_to(jnp.concatenate([two[0:1], two[1:2]], axis=1), (GRID_W, LANES))
                toeplitz = pltpu.roll(lanes, LANES - (WIN_W - 1), 1, stride=1, stride_axis=0)
                o_ref[0, off, a * GRID_W:(a + 1) * GRID_W, t * LANES:(t + 1) * LANES] = jnp.where(
                    valid, toeplitz * LOG2_E, MASK_VALUE)
        return carry

    lax.fori_loop(0, WIN_H, one_offset, 0)


def kernel(x, c, ctx, c_ctx, w_ada, b_ada, norm_g, w_in, q_norm_g, k_norm_g, rpb, conv_w, conv_b, w_out):
    depth = w_ada.shape[0]
    b, s, d = x.shape
    dc = conv_w.shape[2]
    da = (w_in.shape[2] - 4 * dc) // 4
    n_heads = da // HEAD_DIM
    rows = s // GRID_W
    assert depth == 1 and s % GRID_W == 0 and rows >= WIN_H and da % LANES == 0

    cos, slo, shi = _rope_tables(s)
    assert da % MXU_DIM == 0 and MXU_DIM % HEAD_DIM == 0
    seg = np.arange(MXU_DIM) // HEAD_DIM
    bd = jnp.asarray((seg[:, None] == seg[None, :]).astype(np.float32), dtype=BF16)

    cond_rows = -(-(b + 1) // 8) * 8
    cond = jnp.zeros((cond_rows, d), F32).at[:b].set(c).at[b].set(c_ctx)
    ada, bias = _prep(cond, w_ada[0], b_ada[0], rpb[0])
    ada3 = ada.reshape(cond_rows, 3, d)

    w_in_b = w_in[0].astype(BF16)
    ng = norm_g[0].reshape(1, d)
    qg = jnp.tile(q_norm_g[0] * (HEAD_DIM ** -0.5 * LOG2_E), n_heads).reshape(1, da)
    kg = jnp.tile(k_norm_g[0], n_heads).reshape(1, da)

    kc, vc = _ctx_kv(ctx, ada3, b, ng, w_in_b, kg, bd, tm=512)
    qrot, qpl, k, v, ga, conv = _in_proj(x, ada3, ng, w_in_b, qg, kg, cos, slo, shi, bd,
                                         conv_w[0], conv_b[0].reshape(1, dc), tm=1024, sub=512)
    attn = _attention(qrot, qpl, k, v, kc, vc, ga, bias, group=2, n_batch=4, steps=4)
    return _out_proj(x, attn, conv, ada3, w_out[0].astype(BF16), tm=2048)
```

```python
import functools

import numpy as np
import jax
import jax.numpy as jnp
from jax import lax
from jax.experimental import pallas as pl
from jax.experimental.pallas import tpu as pltpu

F32 = jnp.float32
BF16 = jnp.bfloat16

HEAD_DIM = 64
GRID_W = 64
WIN_H = 8
WIN_W = 16
CONV_K = 3
ROPE_THETA = 10000.0
RMS_EPS = 1e-6
MASK_VALUE = -1e30
LOG2_E = 1.4426950408889634

LANES = 128
MXU_DIM = 256
F32_ROWS = 8
BF16_ROWS = 16
HALO = BF16_ROWS
VMEM_LIMIT = 56 * 1024 * 1024


def _silu(z):
    return z * jax.nn.sigmoid(z)


def _prep_kernel(c_ref, cctx_ref, w_ref, b_ref, rpb_ref, ada_ref, bias_ref):
    nb, d = c_ref.shape
    cond = jnp.concatenate([c_ref[...], jnp.broadcast_to(cctx_ref[...], (ada_ref.shape[0] - nb, d))], axis=0)
    a = _silu(cond).astype(BF16)
    ada_ref[...] = jnp.dot(a, w_ref[...].astype(BF16), preferred_element_type=F32) + b_ref[...]
    _bias_block(rpb_ref, bias_ref)


def _prep(c, c_ctx, w_ada, b_ada, rpb):
    nb, d = c.shape
    assert nb % F32_ROWS == 0
    rows = nb + F32_ROWS
    n = w_ada.shape[1]
    nh, ndr, ndc = rpb.shape
    steps = nh // 2
    tn = n // steps
    assert ndr == 2 * WIN_H - 1 and ndc == 2 * WIN_W - 1 and ndc <= GRID_W and tn % LANES == 0
    rpb_pad = jnp.pad(rpb, ((0, 0), (0, 0), (0, GRID_W - ndc)))
    return pl.pallas_call(
        _prep_kernel,
        grid=(steps,),
        in_specs=[pl.BlockSpec((nb, d), lambda i: (0, 0)),
                  pl.BlockSpec((1, d), lambda i: (0, 0)),
                  pl.BlockSpec((d, tn), lambda i: (0, i)),
                  pl.BlockSpec((1, tn), lambda i: (0, i)),
                  pl.BlockSpec((2, ndr, GRID_W), lambda i: (i, 0, 0))],
        out_specs=[pl.BlockSpec((rows, tn), lambda i: (0, i)),
                   pl.BlockSpec((1, WIN_H, 2 * GRID_W, WIN_H * GRID_W), lambda i: (i, 0, 0, 0))],
        out_shape=[jax.ShapeDtypeStruct((rows, n), F32),
                   jax.ShapeDtypeStruct((steps, WIN_H, 2 * GRID_W, WIN_H * GRID_W), F32)],
        compiler_params=pltpu.CompilerParams(vmem_limit_bytes=VMEM_LIMIT),
        name="prep",
    )(c, c_ctx.reshape(1, d), w_ada, b_ada.reshape(1, n), rpb_pad)


def _ada_row(ada_ref, row):
    r = ada_ref[pl.ds(row, 1), :]
    d = r.shape[1] // 3
    return r[:, 0:d], r[:, d:2 * d], r[:, 2 * d:]


def _modulated_norm(xt, mult, shift):
    ms = jnp.mean(xt * xt, axis=-1, keepdims=True)
    return (xt * lax.rsqrt(ms + RMS_EPS) * mult + shift).astype(BF16)


def _head_norm(t, gain, bd):
    t2 = (t * t).astype(BF16)
    w = bd.shape[0]
    ss = jnp.concatenate([jnp.dot(t2[:, c:c + w], bd, preferred_element_type=F32)
                          for c in range(0, t.shape[1], w)], axis=1)
    return t * lax.rsqrt(ss * (1.0 / HEAD_DIM) + RMS_EPS) * gain


def _rope(t, cos, sin_lo, sin_hi):
    outs = []
    for c in range(t.shape[1] // LANES):
        tc = t[:, c * LANES:(c + 1) * LANES]
        up = pltpu.roll(tc, LANES - HEAD_DIM // 4, 1)
        dn = pltpu.roll(tc, HEAD_DIM // 4, 1)
        outs.append(tc * cos + up * sin_lo + dn * sin_hi)
    return jnp.concatenate(outs, axis=1)


def _ctx_kernel(ctx_ref, ada_ref, ng_ref, wk_ref, wv_ref, wcol_ref, g_ref, bd_ref,
                kc_ref, vc_ref, wb_ref, *, ctx_row):
    wb_ref[...] = wcol_ref[...].astype(BF16)
    shift, scale, _ = _ada_row(ada_ref, ctx_row)
    mult = ng_ref[...] * (1.0 + scale)
    h = _modulated_norm(ctx_ref[...], mult, shift)
    kf = jnp.dot(h, wk_ref[...].astype(BF16), preferred_element_type=F32)
    kn = _head_norm(kf, g_ref[1:2, :], bd_ref[...])
    kc_ref[...] = kn.astype(BF16)
    vc_ref[...] = jnp.dot(h, wv_ref[...].astype(BF16), preferred_element_type=F32).astype(BF16)


def _ctx_kv(ctx, ada, ctx_row, norm_g, w_in, gains, bd):
    b, l, d = ctx.shape
    d_attn = gains.shape[1]
    n = b * l
    steps = 4
    tm = n // steps
    tn = w_in.shape[1] // steps
    assert n % steps == 0 and tm % BF16_ROWS == 0 and w_in.shape[1] % steps == 0 and tn % LANES == 0
    kc, vc, w_in_b = pl.pallas_call(
        functools.partial(_ctx_kernel, ctx_row=ctx_row),
        grid=(steps,),
        in_specs=[pl.BlockSpec((tm, d), lambda i: (i, 0)),
                  pl.BlockSpec(ada.shape, lambda i: (0, 0)),
                  pl.BlockSpec((1, d), lambda i: (0, 0)),
                  pl.BlockSpec((d, d_attn), lambda i: (0, 1)),
                  pl.BlockSpec((d, d_attn), lambda i: (0, 2)),
                  pl.BlockSpec((d, tn), lambda i: (0, i)),
                  pl.BlockSpec(gains.shape, lambda i: (0, 0)),
                  pl.BlockSpec(bd.shape, lambda i: (0, 0))],
        out_specs=[pl.BlockSpec((tm, d_attn), lambda i: (i, 0)),
                   pl.BlockSpec((tm, d_attn), lambda i: (i, 0)),
                   pl.BlockSpec((d, tn), lambda i: (0, i))],
        out_shape=[jax.ShapeDtypeStruct((n, d_attn), BF16),
                   jax.ShapeDtypeStruct((n, d_attn), BF16),
                   jax.ShapeDtypeStruct(w_in.shape, BF16)],
        compiler_params=pltpu.CompilerParams(vmem_limit_bytes=VMEM_LIMIT),
        name="ctx_kv",
    )(ctx.reshape(n, d), ada, norm_g, w_in, w_in, w_in, gains, bd)
    return kc.reshape(b, l, d_attn), vc.reshape(b, l, d_attn), w_in_b


def _in_proj_kernel(x_ref, xp_ref, xn_ref, ada_ref, ng_ref, w_ref, g_ref,
                    cos_ref, slo_ref, shi_ref, bd_ref, cw_ref, cb_ref,
                    qrot_ref, qpl_ref, k_ref, v_ref, ga_ref, conv_ref,
                    hext_ref, cu_ref, *, tm, sub, da, dc):
    j = pl.program_id(1)
    nj = pl.num_programs(1)
    shift, scale, _ = _ada_row(ada_ref, pl.program_id(0))
    mult = ng_ref[...] * (1.0 + scale)

    def proj(src, lo, width):
        return jnp.dot(src, w_ref[:, lo:lo + width], preferred_element_type=F32)

    bd = bd_ref[...]
    hext_ref[0:HALO, :] = _modulated_norm(xp_ref[0], mult, shift)
    hext_ref[HALO + tm:, :] = _modulated_norm(xn_ref[0], mult, shift)

    for n in range(tm // sub):
        r0 = n * sub
        hext_ref[HALO + r0:HALO + r0 + sub, :] = _modulated_norm(x_ref[0, r0:r0 + sub, :], mult, shift)
    for n in range(tm // sub):
        r0 = n * sub
        rows = slice(r0, r0 + sub)
        h = hext_ref[HALO + r0:HALO + r0 + sub, :]
        cos, slo, shi = cos_ref[rows, :], slo_ref[rows, :], shi_ref[rows, :]


        hx = hext_ref[r0:r0 + sub + 2 * HALO, :]
        cu = proj(hx, 4 * da, dc) * proj(hx, 4 * da + 2 * dc, dc)
        row = lax.broadcasted_iota(jnp.int32, (sub + 2 * HALO, 1), 0) + r0
        inside = ((row >= HALO) | (j > 0)) & ((row < HALO + tm) | (j < nj - 1))
        cu_ref[n] = jnp.where(inside, cu, 0.0)
        y = (cb_ref[...]
             + cw_ref[0:1, :] * cu_ref[n, HALO - 1:HALO - 1 + sub, :]
             + cw_ref[1:2, :] * cu_ref[n, HALO:HALO + sub, :]
             + cw_ref[2:3, :] * cu_ref[n, HALO + 1:HALO + 1 + sub, :])
        bg = proj(h, 4 * da + dc, dc)
        zc = proj(h, 4 * da + 3 * dc, dc)
        conv_ref[0, rows, :] = (bg * y * _silu(zc)).astype(BF16)

        qn = _head_norm(proj(h, 0, da), g_ref[0:1, :], bd)
        qpl_ref[0, rows, :] = qn.astype(BF16)
        qrot_ref[0, rows, :] = _rope(qn, cos, slo, shi).astype(BF16)

        kn = _head_norm(proj(h, da, da), g_ref[1:2, :], bd)
        k_ref[0, rows, :] = _rope(kn, cos, slo, shi).astype(BF16)

        ga_ref[0, rows, :] = _silu(proj(h, 3 * da, da)).astype(BF16)
        v_ref[0, rows, :] = proj(h, 2 * da, da).astype(BF16)


def _in_proj(x, ada, norm_g, w_in, gains, cos, slo, shi, bd, conv_w, conv_b, *, tm, sub):
    b, s, d = x.shape
    assert s % tm == 0 and tm % sub == 0 and sub % HALO == 0
    dc = conv_w.shape[1]
    da = (w_in.shape[1] - 4 * dc) // 4
    nh = tm // HALO
    last_halo = s // HALO - 1
    tok = lambda width: pl.BlockSpec((1, tm, width), lambda i, j: (i, j, 0))
    const = lambda shape: pl.BlockSpec(shape, lambda i, j: (0,) * len(shape))
    tab = pl.BlockSpec((tm, LANES), lambda i, j: (j, 0))
    out_tok = jax.ShapeDtypeStruct((b, s, da), BF16)
    return pl.pallas_call(
        functools.partial(_in_proj_kernel, tm=tm, sub=sub, da=da, dc=dc),
        grid=(b, s // tm),
        in_specs=[tok(d),
                  pl.BlockSpec((1, HALO, d), lambda i, j: (i, jnp.maximum(j * nh - 1, 0), 0)),
                  pl.BlockSpec((1, HALO, d), lambda i, j: (i, jnp.minimum((j + 1) * nh, last_halo), 0)),
                  const(ada.shape),
                  const((1, d)), const(w_in.shape), const(gains.shape),
                  tab, tab, tab, const(bd.shape), const((CONV_K, dc)), const((1, dc))],
        out_specs=[tok(da), tok(da), tok(da), tok(da), tok(da), tok(dc)],
        out_shape=[out_tok, out_tok, out_tok, out_tok, out_tok,
                   jax.ShapeDtypeStruct((b, s, dc), BF16)],
        scratch_shapes=[pltpu.VMEM((tm + 2 * HALO, d), BF16),
                        pltpu.VMEM((tm // sub, sub + 2 * HALO, dc), F32)],
        compiler_params=pltpu.CompilerParams(vmem_limit_bytes=VMEM_LIMIT),
        name="in_proj",
    )(x, x, x, ada, norm_g, w_in, gains, cos, slo, shi, bd, conv_w, conv_b)


def _dot_nt(a, b):
    return lax.dot_general(a, b, (((1,), (1,)), ((), ())), preferred_element_type=F32)


def _attn_kernel(qrot_ref, qpl_ref, k_ref, v_ref, kc_ref, vc_ref, ga_ref, bias_ref, o_ref,
                 s_lat_ref, s_ctx_ref, p_lat_ref, p_ctx_ref, *, rows, group, n_batch, steps):
    n_units = n_batch * rows // group
    nk = WIN_H * GRID_W

    lane = lax.broadcasted_iota(jnp.int32, (GRID_W, LANES), 1)
    first_head = lane < HEAD_DIM

    def stack_heads(q2):
        zero = jnp.zeros_like(q2)
        return jnp.concatenate([jnp.where(first_head, q2, zero), jnp.where(first_head, zero, q2)], axis=0)

    def geometry(u, g):
        r = u * group + g
        if isinstance(r, int):
            bb, i = divmod(r, rows)
            rs = min(max(i - WIN_H // 2, 0), rows - WIN_H)
            return bb, i - rs, i * GRID_W, rs * GRID_W
        bb = r // rows
        i = r % rows
        rs = jnp.clip(i - WIN_H // 2, 0, rows - WIN_H)
        return bb, i - rs, pl.multiple_of(i * GRID_W, GRID_W), pl.multiple_of(rs * GRID_W, GRID_W)


    def scores_matmul(u):
        out = []
        for g in range(group):
            bb, off, tok0, key0 = geometry(u, g)
            qs = stack_heads(qrot_ref[bb, pl.ds(tok0, GRID_W), :])
            kband = k_ref[bb, pl.ds(key0, nk), :]
            qp = stack_heads(qpl_ref[bb, pl.ds(tok0, GRID_W), :])
            out.append((_dot_nt(qs, kband), off, _dot_nt(qp, kc_ref[bb])))
        return out


    def scores_store(slot, vals):
        for g, (s_lat, off, s_ctx) in enumerate(vals):
            s_lat_ref[slot, g] = s_lat + bias_ref[0, off]
            s_ctx_ref[slot, g] = s_ctx

    def softmax(slot):
        for g in range(group):
            s_lat = s_lat_ref[slot, g]
            s_ctx = s_ctx_ref[slot, g]
            m = jnp.maximum(jnp.max(s_lat, axis=-1, keepdims=True), jnp.max(s_ctx, axis=-1, keepdims=True))
            p_lat_ref[slot, g] = jnp.exp2(s_lat - m).astype(BF16)
            p_ctx_ref[slot, g] = jnp.exp2(s_ctx - m).astype(BF16)

    ones_lat = jnp.ones((nk, LANES), BF16)
    ones_ctx = jnp.ones((kc_ref.shape[1], LANES), BF16)

    def pv_matmul(u, slot):
        out = []
        for g in range(group):
            bb, _, _, key0 = geometry(u, g)
            v_lat = jnp.concatenate([v_ref[bb, pl.ds(key0, nk), :], ones_lat], axis=1)
            v_ctx = jnp.concatenate([vc_ref[bb], ones_ctx], axis=1)
            out.append(jnp.dot(p_lat_ref[slot, g], v_lat, preferred_element_type=F32)
                       + jnp.dot(p_ctx_ref[slot, g], v_ctx, preferred_element_type=F32))
        return out

    def pv_store(u, vals):
        for g, ol in enumerate(vals):
            bb, _, tok0, _ = geometry(u, g)
            o = ol[:, 0:LANES] * (1.0 / ol[:, LANES:])
            o2 = jnp.where(first_head, o[0:GRID_W], o[GRID_W:2 * GRID_W])
            gate = ga_ref[bb, pl.ds(tok0, GRID_W), :].astype(F32)
            o_ref[bb, pl.ds(tok0, GRID_W), :] = (o2 * gate).astype(BF16)

    scores_store(0, scores_matmul(0))
    sc = scores_matmul(1)
    softmax(0)
    scores_store(1, sc)

    def step(t, slot):
        sc = scores_matmul(t)
        pv = pv_matmul(t - 2, slot)
        softmax(1 - slot)
        scores_store(slot, sc)
        pv_store(t - 2, pv)

    n_steady, n_left = divmod(n_units - 2, steps)

    def steady(n, carry):
        for j in range(steps):
            step(steps * n + 2 + j, j % 2)
        return carry

    lax.fori_loop(0, n_steady, steady, 0)
    for j in range(n_left):
        step(steps * n_steady + 2 + j, j % 2)

    pv = pv_matmul(n_units - 2, 0)
    softmax(1)
    pv_store(n_units - 2, pv)
    pv_store(n_units - 1, pv_matmul(n_units - 1, 1))


def _attention(qrot, qpl, k, v, kc, vc, ga, bias, *, group, n_batch, steps):
    b, s, da = qrot.shape
    l = vc.shape[1]
    rows = s // GRID_W
    n_hp = da // LANES
    nq = 2 * GRID_W
    nk = WIN_H * GRID_W
    assert rows % (2 * group) == 0 and b % n_batch == 0 and steps % 2 == 0
    tok = pl.BlockSpec((n_batch, s, LANES), lambda hp, i: (i, 0, hp))
    ctx_tok = pl.BlockSpec((n_batch, l, LANES), lambda hp, i: (i, 0, hp))
    return pl.pallas_call(
        functools.partial(_attn_kernel, rows=rows, group=group, n_batch=n_batch, steps=steps),
        grid=(n_hp, b // n_batch),
        in_specs=[tok, tok, tok, tok, ctx_tok, ctx_tok, tok,
                  pl.BlockSpec((1,) + bias.shape[1:], lambda hp, i: (hp, 0, 0, 0))],
        out_specs=tok,
        out_shape=jax.ShapeDtypeStruct((b, s, da), BF16),
        scratch_shapes=[pltpu.VMEM((2, group, nq, nk), F32),
                        pltpu.VMEM((2, group, nq, l), F32),
                        pltpu.VMEM((2, group, nq, nk), BF16),
                        pltpu.VMEM((2, group, nq, l), BF16)],
        compiler_params=pltpu.CompilerParams(vmem_limit_bytes=VMEM_LIMIT),
        name="attn",
    )(qrot, qpl, k, v, kc, vc, ga, bias)


def _out_proj_kernel(x_ref, a_ref, c_ref, ada_ref, w_ref, o_ref, *, da):
    _, _, gate = _ada_row(ada_ref, pl.program_id(0))
    upd = (jnp.dot(a_ref[0], w_ref[0:da, :].astype(BF16), preferred_element_type=F32)
           + jnp.dot(c_ref[0], w_ref[da:, :].astype(BF16), preferred_element_type=F32))
    o_ref[0] = x_ref[0] + gate * upd


def _out_proj(x, attn, conv, ada, w_out, *, tm):
    b, s, d = x.shape
    da = attn.shape[2]
    dc = conv.shape[2]
    return pl.pallas_call(
        functools.partial(_out_proj_kernel, da=da),
        grid=(b, s // tm),
        in_specs=[pl.BlockSpec((1, tm, d), lambda i, j: (i, j, 0)),
                  pl.BlockSpec((1, tm, da), lambda i, j: (i, j, 0)),
                  pl.BlockSpec((1, tm, dc), lambda i, j: (i, j, 0)),
                  pl.BlockSpec(ada.shape, lambda i, j: (0, 0)),
                  pl.BlockSpec(w_out.shape, lambda i, j: (0, 0))],
        out_specs=pl.BlockSpec((1, tm, d), lambda i, j: (i, j, 0)),
        out_shape=jax.ShapeDtypeStruct((b, s, d), F32),
        compiler_params=pltpu.CompilerParams(vmem_limit_bytes=VMEM_LIMIT),
        name="out_proj",
    )(x, attn, conv, ada, w_out)


def _rope_tables(s):
    nf = HEAD_DIM // 4
    inv = (ROPE_THETA ** (-np.arange(nf, dtype=np.float32) / nf)).astype(np.float32)
    pos = np.arange(s)
    lane = np.arange(LANES)
    d = lane % HEAD_DIM
    axis = d // (2 * nf)
    half = (d % (2 * nf)) // nf
    coord = np.where(axis[None, :] == 0, (pos // GRID_W)[:, None], (pos % GRID_W)[:, None]).astype(np.float32)
    ang = (coord * inv[d % nf][None, :]).astype(np.float32)
    cos = np.cos(ang).astype(np.float32)
    sin = np.sin(ang).astype(np.float32)
    sin_lo = np.where(half[None, :] == 0, -sin, 0.0).astype(np.float32)
    sin_hi = np.where(half[None, :] == 1, sin, 0.0).astype(np.float32)
    return jnp.asarray(cos), jnp.asarray(sin_lo), jnp.asarray(sin_hi)


def _bias_block(rpb_ref, o_ref):
    cq = lax.broadcasted_iota(jnp.int32, (GRID_W, LANES), 0)
    ck = lax.broadcasted_iota(jnp.int32, (GRID_W, LANES), 1) % GRID_W
    col_start = jnp.clip(cq - WIN_W // 2, 0, GRID_W - WIN_W)
    valid = (ck >= col_start) & (ck < col_start + WIN_W)

    def one_offset(off, carry):
        for a in range(2):
            for t in range(WIN_H // 2):
                dr0 = 2 * t - off + (WIN_H - 1)
                two = rpb_ref[a, pl.ds(dr0, 2), :]
                lanes = jnp.broadcast_to(jnp.concatenate([two[0:1], two[1:2]], axis=1), (GRID_W, LANES))
                toeplitz = pltpu.roll(lanes, LANES - (WIN_W - 1), 1, stride=1, stride_axis=0)
                o_ref[0, off, a * GRID_W:(a + 1) * GRID_W, t * LANES:(t + 1) * LANES] = jnp.where(
                    valid, toeplitz * LOG2_E, MASK_VALUE)
        return carry

    lax.fori_loop(0, WIN_H, one_offset, 0)


def kernel(x, c, ctx, c_ctx, w_ada, b_ada, norm_g, w_in, q_norm_g, k_norm_g, rpb, conv_w, conv_b, w_out):
    depth = w_ada.shape[0]
    b, s, d = x.shape
    dc = conv_w.shape[2]
    da = (w_in.shape[2] - 4 * dc) // 4
    n_heads = da // HEAD_DIM
    rows = s // GRID_W
    assert depth == 1 and s % GRID_W == 0 and rows >= WIN_H and da % LANES == 0

    cos, slo, shi = _rope_tables(s)
    assert da % MXU_DIM == 0 and MXU_DIM % HEAD_DIM == 0
    seg = np.arange(MXU_DIM) // HEAD_DIM
    bd = jnp.asarray((seg[:, None] == seg[None, :]).astype(np.float32), dtype=BF16)

    ada, bias = _prep(c, c_ctx, w_ada[0], b_ada[0], rpb[0])

    ng = norm_g[0].reshape(1, d)
    gains = jnp.tile(jnp.stack([q_norm_g[0] * (HEAD_DIM ** -0.5 * LOG2_E), k_norm_g[0]]), (1, n_heads))

    kc, vc, w_in_b = _ctx_kv(ctx, ada, b, ng, w_in[0], gains, bd)
    qrot, qpl, k, v, ga, conv = _in_proj(x, ada, ng, w_in_b, gains, cos, slo, shi, bd,
                                         conv_w[0], conv_b[0].reshape(1, dc), tm=1024, sub=512)
    attn = _attention(qrot, qpl, k, v, kc, vc, ga, bias, group=2, n_batch=4, steps=4)
    return _out_proj(x, attn, conv, ada, w_out[0], tm=2048)
```

```python
import functools

import numpy as np
import jax
import jax.numpy as jnp
from jax import lax
from jax.experimental import pallas as pl
from jax.experimental.pallas import tpu as pltpu

F32 = jnp.float32
BF16 = jnp.bfloat16

HEAD_DIM = 64
GRID_W = 64
WIN_H = 8
WIN_W = 16
CONV_K = 3
ROPE_THETA = 10000.0
RMS_EPS = 1e-6
MASK_VALUE = -1e30
LOG2_E = 1.4426950408889634

LANES = 128
MXU_DIM = 256
F32_ROWS = 8
BF16_ROWS = 16
HALO = BF16_ROWS
VMEM_LIMIT = 56 * 1024 * 1024


def _silu(z):
    return z * jax.nn.sigmoid(z)


def _prep_kernel(c_ref, cctx_ref, w_ref, b_ref, rpb_ref, ada_ref, bias_ref):
    nb, d = c_ref.shape
    cond = jnp.concatenate([c_ref[...], jnp.broadcast_to(cctx_ref[...], (ada_ref.shape[0] - nb, d))], axis=0)
    a = _silu(cond).astype(BF16)
    ada_ref[...] = jnp.dot(a, w_ref[...].astype(BF16), preferred_element_type=F32) + b_ref[...]
    _bias_block(rpb_ref, bias_ref)


def _prep(c, c_ctx, w_ada, b_ada, rpb):
    nb, d = c.shape
    assert nb % F32_ROWS == 0
    rows = nb + F32_ROWS
    n = w_ada.shape[1]
    nh, ndr, ndc = rpb.shape
    steps = nh // 2
    tn = n // steps
    assert ndr == 2 * WIN_H - 1 and ndc == 2 * WIN_W - 1 and ndc <= GRID_W and tn % LANES == 0
    rpb_pad = jnp.pad(rpb, ((0, 0), (0, 0), (0, GRID_W - ndc)))
    return pl.pallas_call(
        _prep_kernel,
        grid=(steps,),
        in_specs=[pl.BlockSpec((nb, d), lambda i: (0, 0)),
                  pl.BlockSpec((1, d), lambda i: (0, 0)),
                  pl.BlockSpec((d, tn), lambda i: (0, i)),
                  pl.BlockSpec((1, tn), lambda i: (0, i)),
                  pl.BlockSpec((2, ndr, GRID_W), lambda i: (i, 0, 0))],
        out_specs=[pl.BlockSpec((rows, tn), lambda i: (0, i)),
                   pl.BlockSpec((1, WIN_H, 2 * GRID_W, WIN_H * GRID_W), lambda i: (i, 0, 0, 0))],
        out_shape=[jax.ShapeDtypeStruct((rows, n), F32),
                   jax.ShapeDtypeStruct((steps, WIN_H, 2 * GRID_W, WIN_H * GRID_W), F32)],
        compiler_params=pltpu.CompilerParams(vmem_limit_bytes=VMEM_LIMIT),
        name="prep",
    )(c, c_ctx.reshape(1, d), w_ada, b_ada.reshape(1, n), rpb_pad)


def _ada_row(ada_ref, row):
    r = ada_ref[pl.ds(row, 1), :]
    d = r.shape[1] // 3
    return r[:, 0:d], r[:, d:2 * d], r[:, 2 * d:]


def _modulated_norm(xt, mult, shift):
    ms = jnp.mean(xt * xt, axis=-1, keepdims=True)
    return (xt * lax.rsqrt(ms + RMS_EPS) * mult + shift).astype(BF16)


def _head_norm(t, gain, bd):
    t2 = (t * t).astype(BF16)
    w = bd.shape[0]
    ss = jnp.concatenate([jnp.dot(t2[:, c:c + w], bd, preferred_element_type=F32)
                          for c in range(0, t.shape[1], w)], axis=1)
    return t * lax.rsqrt(ss * (1.0 / HEAD_DIM) + RMS_EPS) * gain


def _rope(t, cos, sin_lo, sin_hi):
    outs = []
    for c in range(t.shape[1] // LANES):
        tc = t[:, c * LANES:(c + 1) * LANES]
        up = pltpu.roll(tc, LANES - HEAD_DIM // 4, 1)
        dn = pltpu.roll(tc, HEAD_DIM // 4, 1)
        outs.append(tc * cos + up * sin_lo + dn * sin_hi)
    return jnp.concatenate(outs, axis=1)


def _ctx_kernel(ctx_ref, ada_ref, ng_ref, wk_ref, wv_ref, wcol_ref, g_ref, bd_ref,
                kc_ref, vc_ref, wb_ref, *, ctx_row):
    wb_ref[...] = wcol_ref[...].astype(BF16)
    shift, scale, _ = _ada_row(ada_ref, ctx_row)
    mult = ng_ref[...] * (1.0 + scale)
    h = _modulated_norm(ctx_ref[...], mult, shift)
    kf = jnp.dot(h, wk_ref[...].astype(BF16), preferred_element_type=F32)
    kn = _head_norm(kf, g_ref[1:2, :], bd_ref[...])
    kc_ref[...] = kn.astype(BF16)
    vc_ref[...] = jnp.dot(h, wv_ref[...].astype(BF16), preferred_element_type=F32).astype(BF16)


def _ctx_kv(ctx, ada, ctx_row, norm_g, w_in, gains, bd):
    b, l, d = ctx.shape
    d_attn = gains.shape[1]
    n = b * l
    steps = 4
    tm = n // steps
    tn = w_in.shape[1] // steps
    assert n % steps == 0 and tm % BF16_ROWS == 0 and w_in.shape[1] % steps == 0 and tn % LANES == 0
    kc, vc, w_in_b = pl.pallas_call(
        functools.partial(_ctx_kernel, ctx_row=ctx_row),
        grid=(steps,),
        in_specs=[pl.BlockSpec((tm, d), lambda i: (i, 0)),
                  pl.BlockSpec(ada.shape, lambda i: (0, 0)),
                  pl.BlockSpec((1, d), lambda i: (0, 0)),
                  pl.BlockSpec((d, d_attn), lambda i: (0, 1)),
                  pl.BlockSpec((d, d_attn), lambda i: (0, 2)),
                  pl.BlockSpec((d, tn), lambda i: (0, i)),
                  pl.BlockSpec(gains.shape, lambda i: (0, 0)),
                  pl.BlockSpec(bd.shape, lambda i: (0, 0))],
        out_specs=[pl.BlockSpec((tm, d_attn), lambda i: (i, 0)),
                   pl.BlockSpec((tm, d_attn), lambda i: (i, 0)),
                   pl.BlockSpec((d, tn), lambda i: (0, i))],
        out_shape=[jax.ShapeDtypeStruct((n, d_attn), BF16),
                   jax.ShapeDtypeStruct((n, d_attn), BF16),
                   jax.ShapeDtypeStruct(w_in.shape, BF16)],
        compiler_params=pltpu.CompilerParams(vmem_limit_bytes=VMEM_LIMIT),
        name="ctx_kv",
    )(ctx.reshape(n, d), ada, norm_g, w_in, w_in, w_in, gains, bd)
    return kc.reshape(b, l, d_attn), vc.reshape(b, l, d_attn), w_in_b


def _in_proj_kernel(x_ref, xp_ref, xn_ref, ada_ref, ng_ref, w_ref, g_ref,
                    cos_ref, slo_ref, shi_ref, bd_ref, cw_ref, cb_ref,
                    qrot_ref, qpl_ref, k_ref, v_ref, ga_ref, conv_ref,
                    hext_ref, cu_ref, *, tm, sub, da, dc):
    j = pl.program_id(1)
    nj = pl.num_programs(1)
    shift, scale, _ = _ada_row(ada_ref, pl.program_id(0))
    mult = ng_ref[...] * (1.0 + scale)

    def proj(src, lo, width):
        return jnp.dot(src, w_ref[:, lo:lo + width], preferred_element_type=F32)

    bd = bd_ref[...]
    hext_ref[0:HALO, :] = _modulated_norm(xp_ref[0], mult, shift)
    hext_ref[HALO + tm:, :] = _modulated_norm(xn_ref[0], mult, shift)

    for n in range(tm // sub):
        r0 = n * sub
        hext_ref[HALO + r0:HALO + r0 + sub, :] = _modulated_norm(x_ref[0, r0:r0 + sub, :], mult, shift)
    for n in range(tm // sub):
        r0 = n * sub
        rows = slice(r0, r0 + sub)
        h = hext_ref[HALO + r0:HALO + r0 + sub, :]
        cos, slo, shi = cos_ref[rows, :], slo_ref[rows, :], shi_ref[rows, :]


        hx = hext_ref[r0:r0 + sub + 2 * HALO, :]
        cu = proj(hx, 4 * da, dc) * proj(hx, 4 * da + 2 * dc, dc)
        row = lax.broadcasted_iota(jnp.int32, (sub + 2 * HALO, 1), 0) + r0
        inside = ((row >= HALO) | (j > 0)) & ((row < HALO + tm) | (j < nj - 1))
        cu_ref[n] = jnp.where(inside, cu, 0.0)
        y = (cb_ref[...]
             + cw_ref[0:1, :] * cu_ref[n, HALO - 1:HALO - 1 + sub, :]
             + cw_ref[1:2, :] * cu_ref[n, HALO:HALO + sub, :]
             + cw_ref[2:3, :] * cu_ref[n, HALO + 1:HALO + 1 + sub, :])
        bg = proj(h, 4 * da + dc, dc)
        zc = proj(h, 4 * da + 3 * dc, dc)
        conv_ref[0, rows, :] = (bg * y * _silu(zc)).astype(BF16)

        qn = _head_norm(proj(h, 0, da), g_ref[0:1, :], bd)
        qpl_ref[0, rows, :] = qn.astype(BF16)
        qrot_ref[0, rows, :] = _rope(qn, cos, slo, shi).astype(BF16)

        kn = _head_norm(proj(h, da, da), g_ref[1:2, :], bd)
        k_ref[0, rows, :] = _rope(kn, cos, slo, shi).astype(BF16)

        ga_ref[0, rows, :] = _silu(proj(h, 3 * da, da)).astype(BF16)
        v_ref[0, rows, :] = proj(h, 2 * da, da).astype(BF16)


def _in_proj(x, ada, norm_g, w_in, gains, cos, slo, shi, bd, conv_w, conv_b, *, tm, sub):
    b, s, d = x.shape
    assert s % tm == 0 and tm % sub == 0 and sub % HALO == 0
    dc = conv_w.shape[1]
    da = (w_in.shape[1] - 4 * dc) // 4
    nh = tm // HALO
    last_halo = s // HALO - 1
    tok = lambda width: pl.BlockSpec((1, tm, width), lambda i, j: (i, j, 0))
    const = lambda shape: pl.BlockSpec(shape, lambda i, j: (0,) * len(shape))
    tab = pl.BlockSpec((tm, LANES), lambda i, j: (j, 0))
    out_tok = jax.ShapeDtypeStruct((b, s, da), BF16)
    return pl.pallas_call(
        functools.partial(_in_proj_kernel, tm=tm, sub=sub, da=da, dc=dc),
        grid=(b, s // tm),
        in_specs=[tok(d),
                  pl.BlockSpec((1, HALO, d), lambda i, j: (i, jnp.maximum(j * nh - 1, 0), 0)),
                  pl.BlockSpec((1, HALO, d), lambda i, j: (i, jnp.minimum((j + 1) * nh, last_halo), 0)),
                  const(ada.shape),
                  const((1, d)), const(w_in.shape), const(gains.shape),
                  tab, tab, tab, const(bd.shape), const((CONV_K, dc)), const((1, dc))],
        out_specs=[tok(da), tok(da), tok(da), tok(da), tok(da), tok(dc)],
        out_shape=[out_tok, out_tok, out_tok, out_tok, out_tok,
                   jax.ShapeDtypeStruct((b, s, dc), BF16)],
        scratch_shapes=[pltpu.VMEM((tm + 2 * HALO, d), BF16),
                        pltpu.VMEM((tm // sub, sub + 2 * HALO, dc), F32)],
        compiler_params=pltpu.CompilerParams(vmem_limit_bytes=VMEM_LIMIT),
        name="in_proj",
    )(x, x, x, ada, norm_g, w_in, gains, cos, slo, shi, bd, conv_w, conv_b)


def _dot_nt(a, b):
    return lax.dot_general(a, b, (((1,), (1,)), ((), ())), preferred_element_type=F32)


def _attn_kernel(qrot_ref, qpl_ref, k_ref, v_ref, kc_ref, vc_ref, ga_ref, bias_ref, o_ref,
                 s_lat_ref, s_ctx_ref, p_lat_ref, p_ctx_ref, *, rows, group, n_batch, steps):
    n_units = n_batch * rows // group
    nk = WIN_H * GRID_W

    lane = lax.broadcasted_iota(jnp.int32, (GRID_W, LANES), 1)
    first_head = lane < HEAD_DIM

    def stack_heads(q2):
        zero = jnp.zeros_like(q2)
        return jnp.concatenate([jnp.where(first_head, q2, zero), jnp.where(first_head, zero, q2)], axis=0)

    def geometry(u, g):
        r = u * group + g
        if isinstance(r, int):
            bb, i = divmod(r, rows)
            rs = min(max(i - WIN_H // 2, 0), rows - WIN_H)
            return bb, i - rs, i * GRID_W, rs * GRID_W
        bb = r // rows
        i = r % rows
        rs = jnp.clip(i - WIN_H // 2, 0, rows - WIN_H)
        return bb, i - rs, pl.multiple_of(i * GRID_W, GRID_W), pl.multiple_of(rs * GRID_W, GRID_W)


    def scores_matmul(u):
        out = []
        for g in range(group):
            bb, off, tok0, key0 = geometry(u, g)
            qs = stack_heads(qrot_ref[bb, pl.ds(tok0, GRID_W), :])
            kband = k_ref[bb, pl.ds(key0, nk), :]
            qp = stack_heads(qpl_ref[bb, pl.ds(tok0, GRID_W), :])
            out.append((_dot_nt(qs, kband), off, _dot_nt(qp, kc_ref[bb])))
        return out


    def scores_store(slot, vals):
        for g, (s_lat, off, s_ctx) in enumerate(vals):
            s_lat_ref[slot, g] = s_lat + bias_ref[0, off]
            s_ctx_ref[slot, g] = s_ctx

    def softmax(slot):
        for g in range(group):
            s_lat = s_lat_ref[slot, g]
            s_ctx = s_ctx_ref[slot, g]
            m = jnp.maximum(jnp.max(s_lat, axis=-1, keepdims=True), jnp.max(s_ctx, axis=-1, keepdims=True))
            p_lat_ref[slot, g] = jnp.exp2(s_lat - m).astype(BF16)
            p_ctx_ref[slot, g] = jnp.exp2(s_ctx - m).astype(BF16)

    ones_lat = jnp.ones((nk, LANES), BF16)
    ones_ctx = jnp.ones((kc_ref.shape[1], LANES), BF16)

    def pv_matmul(u, slot):
        out = []
        for g in range(group):
            bb, _, _, key0 = geometry(u, g)
            v_lat = jnp.concatenate([v_ref[bb, pl.ds(key0, nk), :], ones_lat], axis=1)
            v_ctx = jnp.concatenate([vc_ref[bb], ones_ctx], axis=1)
            out.append(jnp.dot(p_lat_ref[slot, g], v_lat, preferred_element_type=F32)
                       + jnp.dot(p_ctx_ref[slot, g], v_ctx, preferred_element_type=F32))
        return out

    def pv_store(u, vals):
        for g, ol in enumerate(vals):
            bb, _, tok0, _ = geometry(u, g)
            o = ol[:, 0:LANES] * (1.0 / ol[:, LANES:])
            o2 = jnp.where(first_head, o[0:GRID_W], o[GRID_W:2 * GRID_W])
            gate = ga_ref[bb, pl.ds(tok0, GRID_W), :].astype(F32)
            o_ref[bb, pl.ds(tok0, GRID_W), :] = (o2 * gate).astype(BF16)

    scores_store(0, scores_matmul(0))
    sc = scores_matmul(1)
    softmax(0)
    scores_store(1, sc)

    def step(t, slot):
        sc = scores_matmul(t)
        pv = pv_matmul(t - 2, slot)
        softmax(1 - slot)
        scores_store(slot, sc)
        pv_store(t - 2, pv)

    n_steady, n_left = divmod(n_units - 2, steps)

    def steady(n, carry):
        for j in range(steps):
            step(steps * n + 2 + j, j % 2)
        return carry

    lax.fori_loop(0, n_steady, steady, 0)
    for j in range(n_left):
        step(steps * n_steady + 2 + j, j % 2)

    pv = pv_matmul(n_units - 2, 0)
    softmax(1)
    pv_store(n_units - 2, pv)
    pv_store(n_units - 1, pv_matmul(n_units - 1, 1))


def _attention(qrot, qpl, k, v, kc, vc, ga, bias, *, group, n_batch, steps):
    b, s, da = qrot.shape
    l = vc.shape[1]
    rows = s // GRID_W
    n_hp = da // LANES
    nq = 2 * GRID_W
    nk = WIN_H * GRID_W
    assert rows % (2 * group) == 0 and b % n_batch == 0 and steps % 2 == 0
    tok = pl.BlockSpec((n_batch, s, LANES), lambda hp, i: (i, 0, hp))
    ctx_tok = pl.BlockSpec((n_batch, l, LANES), lambda hp, i: (i, 0, hp))
    return pl.pallas_call(
        functools.partial(_attn_kernel, rows=rows, group=group, n_batch=n_batch, steps=steps),
        grid=(n_hp, b // n_batch),
        in_specs=[tok, tok, tok, tok, ctx_tok, ctx_tok, tok,
                  pl.BlockSpec((1,) + bias.shape[1:], lambda hp, i: (hp, 0, 0, 0))],
        out_specs=tok,
        out_shape=jax.ShapeDtypeStruct((b, s, da), BF16),
        scratch_shapes=[pltpu.VMEM((2, group, nq, nk), F32),
                        pltpu.VMEM((2, group, nq, l), F32),
                        pltpu.VMEM((2, group, nq, nk), BF16),
                        pltpu.VMEM((2, group, nq, l), BF16)],
        compiler_params=pltpu.CompilerParams(vmem_limit_bytes=VMEM_LIMIT),
        name="attn",
    )(qrot, qpl, k, v, kc, vc, ga, bias)


def _out_proj_kernel(x_ref, a_ref, c_ref, ada_ref, w_ref, o_ref, *, da):
    _, _, gate = _ada_row(ada_ref, pl.program_id(0))
    upd = (jnp.dot(a_ref[0], w_ref[0:da, :].astype(BF16), preferred_element_type=F32)
           + jnp.dot(c_ref[0], w_ref[da:, :].astype(BF16), preferred_element_type=F32))
    o_ref[0] = x_ref[0] + gate * upd


def _out_proj(x, attn, conv, ada, w_out, *, tm):
    b, s, d = x.shape
    da = attn.shape[2]
    dc = conv.shape[2]
    return pl.pallas_call(
        functools.partial(_out_proj_kernel, da=da),
        grid=(b, s // tm),
        in_specs=[pl.BlockSpec((1, tm, d), lambda i, j: (i, j, 0)),
                  pl.BlockSpec((1, tm, da), lambda i, j: (i, j, 0)),
                  pl.BlockSpec((1, tm, dc), lambda i, j: (i, j, 0)),
                  pl.BlockSpec(ada.shape, lambda i, j: (0, 0)),
                  pl.BlockSpec(w_out.shape, lambda i, j: (0, 0))],
        out_specs=pl.BlockSpec((1, tm, d), lambda i, j: (i, j, 0)),
        out_shape=jax.ShapeDtypeStruct((b, s, d), F32),
        compiler_params=pltpu.CompilerParams(vmem_limit_bytes=VMEM_LIMIT),
        name="out_proj",
    )(x, attn, conv, ada, w_out)


def _rope_tables(s):
    nf = HEAD_DIM // 4
    inv = (ROPE_THETA ** (-np.arange(nf, dtype=np.float32) / nf)).astype(np.float32)
    pos = np.arange(s)
    lane = np.arange(LANES)
    d = lane % HEAD_DIM
    axis = d // (2 * nf)
    half = (d % (2 * nf)) // nf
    coord = np.where(axis[None, :] == 0, (pos // GRID_W)[:, None], (pos % GRID_W)[:, None]).astype(np.float32)
    ang = (coord * inv[d % nf][None, :]).astype(np.float32)
    cos = np.cos(ang).astype(np.float32)
    sin = np.sin(ang).astype(np.float32)
    sin_lo = np.where(half[None, :] == 0, -sin, 0.0).astype(np.float32)
    sin_hi = np.where(half[None, :] == 1, sin, 0.0).astype(np.float32)
    return jnp.asarray(cos), jnp.asarray(sin_lo), jnp.asarray(sin_hi)


def _bias_block(rpb_ref, o_ref):
    cq = lax.broadcasted_iota(jnp.int32, (GRID_W, LANES), 0)
    ck = lax.broadcasted_iota(jnp.int32, (GRID_W, LANES), 1) % GRID_W
    col_start = jnp.clip(cq - WIN_W // 2, 0, GRID_W - WIN_W)
    valid = (ck >= col_start) & (ck < col_start + WIN_W)

    def one_offset(off, carry):
        for a in range(2):
            for t in range(WIN_H // 2):
                dr0 = 2 * t - off + (WIN_H - 1)
                two = rpb_ref[a, pl.ds(dr0, 2), :]
                lanes = jnp.broadcast_to(jnp.concatenate([two[0:1], two[1:2]], axis=1), (GRID_W, LANES))
                toeplitz = pltpu.roll(lanes, LANES - (WIN_W - 1), 1, stride=1, stride_axis=0)
                o_ref[0, off, a * GRID_W:(a + 1) * GRID_W, t * LANES:(t + 1) * LANES] = jnp.where(
                    valid, toeplitz * LOG2_E, MASK_VALUE)
        return carry

    lax.fori_loop(0, WIN_H, one_offset, 0)


def kernel(x, c, ctx, c_ctx, w_ada, b_ada, norm_g, w_in, q_norm_g, k_norm_g, rpb, conv_w, conv_b, w_out):
    depth = w_ada.shape[0]
    b, s, d = x.shape
    dc = conv_w.shape[2]
    da = (w_in.shape[2] - 4 * dc) // 4
    n_heads = da // HEAD_DIM
    rows = s // GRID_W
    assert depth == 1 and s % GRID_W == 0 and rows >= WIN_H and da % LANES == 0

    cos, slo, shi = _rope_tables(s)
    assert da % MXU_DIM == 0 and MXU_DIM % HEAD_DIM == 0
    seg = np.arange(MXU_DIM) // HEAD_DIM
    bd = jnp.asarray((seg[:, None] == seg[None, :]).astype(np.float32), dtype=BF16)

    ada, bias = _prep(c, c_ctx, w_ada[0], b_ada[0], rpb[0])

    ng = norm_g[0].reshape(1, d)
    gains = jnp.tile(jnp.stack([q_norm_g[0] * (HEAD_DIM ** -0.5 * LOG2_E), k_norm_g[0]]), (1, n_heads))

    kc, vc, w_in_b = _ctx_kv(ctx, ada, b, ng, w_in[0], gains, bd)
    qrot, qpl, k, v, ga, conv = _in_proj(x, ada, ng, w_in_b, gains, cos, slo, shi, bd,
                                         conv_w[0], conv_b[0].reshape(1, dc), tm=1024, sub=512)
    attn = _attention(qrot, qpl, k, v, kc, vc, ga, bias, group=2, n_batch=4, steps=20)
    return _out_proj(x, attn, conv, ada, w_out[0], tm=2048)
```

```python
import functools

import numpy as np
import jax
import jax.numpy as jnp
from jax import lax
from jax.experimental import pallas as pl
from jax.experimental.pallas import tpu as pltpu

F32 = jnp.float32
BF16 = jnp.bfloat16

HEAD_DIM = 64
GRID_W = 64
WIN_H = 8
WIN_W = 16
CONV_K = 3
ROPE_THETA = 10000.0
RMS_EPS = 1e-6
MASK_VALUE = -1e30
LOG2_E = 1.4426950408889634

LANES = 128
MXU_DIM = 256
F32_ROWS = 8
BF16_ROWS = 16
HALO = BF16_ROWS
VMEM_LIMIT = 56 * 1024 * 1024


def _silu(z):
    return z * jax.nn.sigmoid(z)


def _prep_kernel(c_ref, cctx_ref, w_ref, b_ref, rpb_ref, win_ref, ada_ref, bias_ref, winb_ref):
    nb, d = c_ref.shape
    cond = jnp.concatenate([c_ref[...], jnp.broadcast_to(cctx_ref[...], (ada_ref.shape[0] - nb, d))], axis=0)
    a = _silu(cond).astype(BF16)
    ada_ref[...] = jnp.dot(a, w_ref[...].astype(BF16), preferred_element_type=F32) + b_ref[...]
    winb_ref[...] = win_ref[...].astype(BF16)
    _bias_block(rpb_ref, bias_ref)


def _prep(c, c_ctx, w_ada, b_ada, rpb, w_in):
    nb, d = c.shape
    assert nb % F32_ROWS == 0
    rows = nb + F32_ROWS
    n = w_ada.shape[1]
    nh, ndr, ndc = rpb.shape
    steps = nh // 2
    tn = n // steps
    tw = w_in.shape[1] // steps
    assert ndr == 2 * WIN_H - 1 and ndc == 2 * WIN_W - 1 and ndc <= GRID_W and tn % LANES == 0 and tw % LANES == 0
    rpb_pad = jnp.pad(rpb, ((0, 0), (0, 0), (0, GRID_W - ndc)))
    return pl.pallas_call(
        _prep_kernel,
        grid=(steps,),
        in_specs=[pl.BlockSpec((nb, d), lambda i: (0, 0)),
                  pl.BlockSpec((1, d), lambda i: (0, 0)),
                  pl.BlockSpec((d, tn), lambda i: (0, i)),
                  pl.BlockSpec((1, tn), lambda i: (0, i)),
                  pl.BlockSpec((2, ndr, GRID_W), lambda i: (i, 0, 0)),
                  pl.BlockSpec((d, tw), lambda i: (0, i))],
        out_specs=[pl.BlockSpec((rows, tn), lambda i: (0, i)),
                   pl.BlockSpec((1, WIN_H, 2 * GRID_W, WIN_H * GRID_W), lambda i: (i, 0, 0, 0)),
                   pl.BlockSpec((d, tw), lambda i: (0, i))],
        out_shape=[jax.ShapeDtypeStruct((rows, n), F32),
                   jax.ShapeDtypeStruct((steps, WIN_H, 2 * GRID_W, WIN_H * GRID_W), F32),
                   jax.ShapeDtypeStruct(w_in.shape, BF16)],
        compiler_params=pltpu.CompilerParams(vmem_limit_bytes=VMEM_LIMIT),
        name="prep",
    )(c, c_ctx.reshape(1, d), w_ada, b_ada.reshape(1, n), rpb_pad, w_in)


def _ada_row(ada_ref, row):
    r = ada_ref[pl.ds(row, 1), :]
    d = r.shape[1] // 3
    return r[:, 0:d], r[:, d:2 * d], r[:, 2 * d:]


def _modulated_norm(xt, mult, shift):
    ms = jnp.mean(xt * xt, axis=-1, keepdims=True)
    return (xt * lax.rsqrt(ms + RMS_EPS) * mult + shift).astype(BF16)


def _head_norm(t, gain, bd):
    t2 = (t * t).astype(BF16)
    w = bd.shape[0]
    ss = jnp.concatenate([jnp.dot(t2[:, c:c + w], bd, preferred_element_type=F32)
                          for c in range(0, t.shape[1], w)], axis=1)
    return t * lax.rsqrt(ss * (1.0 / HEAD_DIM) + RMS_EPS) * gain


def _rope(t, cos, sin_lo, sin_hi):
    outs = []
    for c in range(t.shape[1] // LANES):
        tc = t[:, c * LANES:(c + 1) * LANES]
        up = pltpu.roll(tc, LANES - HEAD_DIM // 4, 1)
        dn = pltpu.roll(tc, HEAD_DIM // 4, 1)
        outs.append(tc * cos + up * sin_lo + dn * sin_hi)
    return jnp.concatenate(outs, axis=1)


def _ctx_kernel(ctx_ref, ada_ref, ng_ref, wk_ref, wv_ref, g_ref, bd_ref, kc_ref, vc_ref, *, ctx_row):
    shift, scale, _ = _ada_row(ada_ref, ctx_row)
    mult = ng_ref[...] * (1.0 + scale)
    h = _modulated_norm(ctx_ref[...], mult, shift)
    kf = jnp.dot(h, wk_ref[...], preferred_element_type=F32)
    kn = _head_norm(kf, g_ref[1:2, :], bd_ref[...])
    kc_ref[...] = kn.astype(BF16)
    vc_ref[...] = jnp.dot(h, wv_ref[...], preferred_element_type=F32).astype(BF16)


def _ctx_kv(ctx, ada, ctx_row, norm_g, w_in, gains, bd, *, tm):
    b, l, d = ctx.shape
    d_attn = gains.shape[1]
    n = b * l
    assert n % tm == 0
    kc, vc = pl.pallas_call(
        functools.partial(_ctx_kernel, ctx_row=ctx_row),
        grid=(n // tm,),
        in_specs=[pl.BlockSpec((tm, d), lambda i: (i, 0)),
                  pl.BlockSpec(ada.shape, lambda i: (0, 0)),
                  pl.BlockSpec((1, d), lambda i: (0, 0)),
                  pl.BlockSpec((d, d_attn), lambda i: (0, 1)),
                  pl.BlockSpec((d, d_attn), lambda i: (0, 2)),
                  pl.BlockSpec(gains.shape, lambda i: (0, 0)),
                  pl.BlockSpec(bd.shape, lambda i: (0, 0))],
        out_specs=[pl.BlockSpec((tm, d_attn), lambda i: (i, 0)),
                   pl.BlockSpec((tm, d_attn), lambda i: (i, 0))],
        out_shape=[jax.ShapeDtypeStruct((n, d_attn), BF16),
                   jax.ShapeDtypeStruct((n, d_attn), BF16)],
        compiler_params=pltpu.CompilerParams(vmem_limit_bytes=VMEM_LIMIT),
        name="ctx_kv",
    )(ctx.reshape(n, d), ada, norm_g, w_in, w_in, gains, bd)
    return kc.reshape(b, l, d_attn), vc.reshape(b, l, d_attn)


def _in_proj_kernel(x_ref, xp_ref, xn_ref, ada_ref, ng_ref, w_ref, g_ref,
                    cos_ref, slo_ref, shi_ref, bd_ref, cw_ref, cb_ref,
                    qrot_ref, qpl_ref, k_ref, v_ref, ga_ref, conv_ref,
                    hext_ref, cu_ref, *, tm, sub, da, dc):
    j = pl.program_id(1)
    nj = pl.num_programs(1)
    shift, scale, _ = _ada_row(ada_ref, pl.program_id(0))
    mult = ng_ref[...] * (1.0 + scale)

    def proj(src, lo, width):
        return jnp.dot(hext_ref[src, :], w_ref[:, lo:lo + width], preferred_element_type=F32)

    bd = bd_ref[...]
    hext_ref[0:HALO, :] = _modulated_norm(xp_ref[0], mult, shift)
    hext_ref[HALO + tm:, :] = _modulated_norm(xn_ref[0], mult, shift)

    for n in range(tm // sub):
        r0 = n * sub
        hext_ref[HALO + r0:HALO + r0 + sub, :] = _modulated_norm(x_ref[0, r0:r0 + sub, :], mult, shift)
    for n in range(tm // sub):
        r0 = n * sub
        rows = slice(r0, r0 + sub)
        h = slice(HALO + r0, HALO + r0 + sub)
        cos, slo, shi = cos_ref[rows, :], slo_ref[rows, :], shi_ref[rows, :]


        hx = slice(r0, r0 + sub + 2 * HALO)
        cu = proj(hx, 4 * da, dc) * proj(hx, 4 * da + 2 * dc, dc)
        row = lax.broadcasted_iota(jnp.int32, (sub + 2 * HALO, 1), 0) + r0
        inside = ((row >= HALO) | (j > 0)) & ((row < HALO + tm) | (j < nj - 1))
        cu_ref[n] = jnp.where(inside, cu, 0.0)
        y = (cb_ref[...]
             + cw_ref[0:1, :] * cu_ref[n, HALO - 1:HALO - 1 + sub, :]
             + cw_ref[1:2, :] * cu_ref[n, HALO:HALO + sub, :]
             + cw_ref[2:3, :] * cu_ref[n, HALO + 1:HALO + 1 + sub, :])
        bg = proj(h, 4 * da + dc, dc)
        zc = proj(h, 4 * da + 3 * dc, dc)
        conv_ref[0, rows, :] = (bg * y * _silu(zc)).astype(BF16)

        qn = _head_norm(proj(h, 0, da), g_ref[0:1, :], bd)
        qpl_ref[0, rows, :] = qn.astype(BF16)
        qrot_ref[0, rows, :] = _rope(qn, cos, slo, shi).astype(BF16)

        kn = _head_norm(proj(h, da, da), g_ref[1:2, :], bd)
        k_ref[0, rows, :] = _rope(kn, cos, slo, shi).astype(BF16)

        ga_ref[0, rows, :] = _silu(proj(h, 3 * da, da)).astype(BF16)
        v_ref[0, rows, :] = proj(h, 2 * da, da).astype(BF16)


def _in_proj(x, ada, norm_g, w_in, gains, cos, slo, shi, bd, conv_w, conv_b, *, tm, sub):
    b, s, d = x.shape
    assert s % tm == 0 and tm % sub == 0 and sub % HALO == 0
    dc = conv_w.shape[1]
    da = (w_in.shape[1] - 4 * dc) // 4
    nh = tm // HALO
    last_halo = s // HALO - 1
    tok = lambda width: pl.BlockSpec((1, tm, width), lambda i, j: (i, j, 0))
    const = lambda shape: pl.BlockSpec(shape, lambda i, j: (0,) * len(shape))
    tab = pl.BlockSpec((tm, LANES), lambda i, j: (j, 0))
    out_tok = jax.ShapeDtypeStruct((b, s, da), BF16)
    return pl.pallas_call(
        functools.partial(_in_proj_kernel, tm=tm, sub=sub, da=da, dc=dc),
        grid=(b, s // tm),
        in_specs=[tok(d),
                  pl.BlockSpec((1, HALO, d), lambda i, j: (i, jnp.maximum(j * nh - 1, 0), 0)),
                  pl.BlockSpec((1, HALO, d), lambda i, j: (i, jnp.minimum((j + 1) * nh, last_halo), 0)),
                  const(ada.shape),
                  const((1, d)), const(w_in.shape), const(gains.shape),
                  tab, tab, tab, const(bd.shape), const((CONV_K, dc)), const((1, dc))],
        out_specs=[tok(da), tok(da), tok(da), tok(da), tok(da), tok(dc)],
        out_shape=[out_tok, out_tok, out_tok, out_tok, out_tok,
                   jax.ShapeDtypeStruct((b, s, dc), BF16)],
        scratch_shapes=[pltpu.VMEM((tm + 2 * HALO, d), BF16),
                        pltpu.VMEM((tm // sub, sub + 2 * HALO, dc), F32)],
        compiler_params=pltpu.CompilerParams(vmem_limit_bytes=VMEM_LIMIT),
        name="in_proj",
    )(x, x, x, ada, norm_g, w_in, gains, cos, slo, shi, bd, conv_w, conv_b)


def _dot_nt(a, b):
    return lax.dot_general(a, b, (((1,), (1,)), ((), ())), preferred_element_type=F32)


def _attn_kernel(qrot_ref, qpl_ref, k_ref, v_ref, kc_ref, vc_ref, ga_ref, bias_ref, o_ref,
                 s_lat_ref, s_ctx_ref, p_lat_ref, p_ctx_ref, *, rows, group, n_batch, steps):
    n_units = n_batch * rows // group
    nk = WIN_H * GRID_W

    lane = lax.broadcasted_iota(jnp.int32, (GRID_W, LANES), 1)
    first_head = lane < HEAD_DIM

    def stack_heads(q2):
        zero = jnp.zeros_like(q2)
        return jnp.concatenate([jnp.where(first_head, q2, zero), jnp.where(first_head, zero, q2)], axis=0)

    def geometry(u, g):
        r = u * group + g
        if isinstance(r, int):
            bb, i = divmod(r, rows)
            rs = min(max(i - WIN_H // 2, 0), rows - WIN_H)
            return bb, i - rs, i * GRID_W, rs * GRID_W
        bb = r // rows
        i = r % rows
        rs = jnp.clip(i - WIN_H // 2, 0, rows - WIN_H)
        return bb, i - rs, pl.multiple_of(i * GRID_W, GRID_W), pl.multiple_of(rs * GRID_W, GRID_W)


    def scores_matmul(u):
        out = []
        for g in range(group):
            bb, off, tok0, key0 = geometry(u, g)
            qs = stack_heads(qrot_ref[bb, pl.ds(tok0, GRID_W), :])
            kband = k_ref[bb, pl.ds(key0, nk), :]
            qp = stack_heads(qpl_ref[bb, pl.ds(tok0, GRID_W), :])
            out.append((_dot_nt(qs, kband), off, _dot_nt(qp, kc_ref[bb])))
        return out


    def scores_store(slot, vals):
        for g, (s_lat, off, s_ctx) in enumerate(vals):
            s_lat_ref[slot, g] = s_lat + bias_ref[0, off]
            s_ctx_ref[slot, g] = s_ctx

    def softmax(slot):
        for g in range(group):
            s_lat = s_lat_ref[slot, g]
            s_ctx = s_ctx_ref[slot, g]
            m = jnp.maximum(jnp.max(s_lat, axis=-1, keepdims=True), jnp.max(s_ctx, axis=-1, keepdims=True))
            p_lat_ref[slot, g] = jnp.exp2(s_lat - m).astype(BF16)
            p_ctx_ref[slot, g] = jnp.exp2(s_ctx - m).astype(BF16)

    ones_lat = jnp.ones((nk, LANES), BF16)
    ones_ctx = jnp.ones((kc_ref.shape[1], LANES), BF16)

    def pv_matmul(u, slot):
        out = []
        for g in range(group):
            bb, _, _, key0 = geometry(u, g)
            v_lat = jnp.concatenate([v_ref[bb, pl.ds(key0, nk), :], ones_lat], axis=1)
            v_ctx = jnp.concatenate([vc_ref[bb], ones_ctx], axis=1)
            out.append(jnp.dot(p_lat_ref[slot, g], v_lat, preferred_element_type=F32)
                       + jnp.dot(p_ctx_ref[slot, g], v_ctx, preferred_element_type=F32))
        return out

    def pv_store(u, vals):
        for g, ol in enumerate(vals):
            bb, _, tok0, _ = geometry(u, g)
            o = ol[:, 0:LANES] * (1.0 / ol[:, LANES:])
            o2 = jnp.where(first_head, o[0:GRID_W], o[GRID_W:2 * GRID_W])
            gate = ga_ref[bb, pl.ds(tok0, GRID_W), :].astype(F32)
            o_ref[bb, pl.ds(tok0, GRID_W), :] = (o2 * gate).astype(BF16)

    scores_store(0, scores_matmul(0))
    sc = scores_matmul(1)
    softmax(0)
    scores_store(1, sc)

    def step(t, slot):
        sc = scores_matmul(t)
        pv = pv_matmul(t - 2, slot)
        softmax(1 - slot)
        scores_store(slot, sc)
        pv_store(t - 2, pv)

    n_steady, n_left = divmod(n_units - 2, steps)

    def steady(n, carry):
        for j in range(steps):
            step(steps * n + 2 + j, j % 2)
        return carry

    lax.fori_loop(0, n_steady, steady, 0)
    for j in range(n_left):
        step(steps * n_steady + 2 + j, j % 2)

    pv = pv_matmul(n_units - 2, 0)
    softmax(1)
    pv_store(n_units - 2, pv)
    pv_store(n_units - 1, pv_matmul(n_units - 1, 1))


def _attention(qrot, qpl, k, v, kc, vc, ga, bias, *, group, n_batch, steps):
    b, s, da = qrot.shape
    l = vc.shape[1]
    rows = s // GRID_W
    n_hp = da // LANES
    nq = 2 * GRID_W
    nk = WIN_H * GRID_W
    assert rows % (2 * group) == 0 and b % n_batch == 0 and steps % 2 == 0
    tok = pl.BlockSpec((n_batch, s, LANES), lambda hp, i: (i, 0, hp))
    ctx_tok = pl.BlockSpec((n_batch, l, LANES), lambda hp, i: (i, 0, hp))
    return pl.pallas_call(
        functools.partial(_attn_kernel, rows=rows, group=group, n_batch=n_batch, steps=steps),
        grid=(n_hp, b // n_batch),
        in_specs=[tok, tok, tok, tok, ctx_tok, ctx_tok, tok,
                  pl.BlockSpec((1,) + bias.shape[1:], lambda hp, i: (hp, 0, 0, 0))],
        out_specs=tok,
        out_shape=jax.ShapeDtypeStruct((b, s, da), BF16),
        scratch_shapes=[pltpu.VMEM((2, group, nq, nk), F32),
                        pltpu.VMEM((2, group, nq, l), F32),
                        pltpu.VMEM((2, group, nq, nk), BF16),
                        pltpu.VMEM((2, group, nq, l), BF16)],
        compiler_params=pltpu.CompilerParams(vmem_limit_bytes=VMEM_LIMIT),
        name="attn",
    )(qrot, qpl, k, v, kc, vc, ga, bias)


def _out_proj_kernel(x_ref, a_ref, c_ref, ada_ref, w_ref, o_ref, *, da):
    _, _, gate = _ada_row(ada_ref, pl.program_id(0))
    upd = (jnp.dot(a_ref[0], w_ref[0:da, :].astype(BF16), preferred_element_type=F32)
           + jnp.dot(c_ref[0], w_ref[da:, :].astype(BF16), preferred_element_type=F32))
    o_ref[0] = x_ref[0] + gate * upd


def _out_proj(x, attn, conv, ada, w_out, *, tm):
    b, s, d = x.shape
    da = attn.shape[2]
    dc = conv.shape[2]
    return pl.pallas_call(
        functools.partial(_out_proj_kernel, da=da),
        grid=(b, s // tm),
        in_specs=[pl.BlockSpec((1, tm, d), lambda i, j: (i, j, 0)),
                  pl.BlockSpec((1, tm, da), lambda i, j: (i, j, 0)),
                  pl.BlockSpec((1, tm, dc), lambda i, j: (i, j, 0)),
                  pl.BlockSpec(ada.shape, lambda i, j: (0, 0)),
                  pl.BlockSpec(w_out.shape, lambda i, j: (0, 0))],
        out_specs=pl.BlockSpec((1, tm, d), lambda i, j: (i, j, 0)),
        out_shape=jax.ShapeDtypeStruct((b, s, d), F32),
        compiler_params=pltpu.CompilerParams(vmem_limit_bytes=VMEM_LIMIT),
        name="out_proj",
    )(x, attn, conv, ada, w_out)


def _rope_tables(s):
    nf = HEAD_DIM // 4
    inv = (ROPE_THETA ** (-np.arange(nf, dtype=np.float32) / nf)).astype(np.float32)
    pos = np.arange(s)
    lane = np.arange(LANES)
    d = lane % HEAD_DIM
    axis = d // (2 * nf)
    half = (d % (2 * nf)) // nf
    coord = np.where(axis[None, :] == 0, (pos // GRID_W)[:, None], (pos % GRID_W)[:, None]).astype(np.float32)
    ang = (coord * inv[d % nf][None, :]).astype(np.float32)
    cos = np.cos(ang).astype(np.float32)
    sin = np.sin(ang).astype(np.float32)
    sin_lo = np.where(half[None, :] == 0, -sin, 0.0).astype(np.float32)
    sin_hi = np.where(half[None, :] == 1, sin, 0.0).astype(np.float32)
    return jnp.asarray(cos), jnp.asarray(sin_lo), jnp.asarray(sin_hi)


def _bias_block(rpb_ref, o_ref):
    cq = lax.broadcasted_iota(jnp.int32, (GRID_W, LANES), 0)
    ck = lax.broadcasted_iota(jnp.int32, (GRID_W, LANES), 1) % GRID_W
    col_start = jnp.clip(cq - WIN_W // 2, 0, GRID_W - WIN_W)
    valid = (ck >= col_start) & (ck < col_start + WIN_W)

    def one_offset(off, carry):
        for a in range(2):
            for t in range(WIN_H // 2):
                dr0 = 2 * t - off + (WIN_H - 1)
                two = rpb_ref[a, pl.ds(dr0, 2), :]
                lanes = jnp.broadcast_to(jnp.concatenate([two[0:1], two[1:2]], axis=1), (GRID_W, LANES))
                toeplitz = pltpu.roll(lanes, LANES - (WIN_W - 1), 1, stride=1, stride_axis=0)
                o_ref[0, off, a * GRID_W:(a + 1) * GRID_W, t * LANES:(t + 1) * LANES] = jnp.where(
                    valid, toeplitz * LOG2_E, MASK_VALUE)
        return carry

    lax.fori_loop(0, WIN_H, one_offset, 0)


def kernel(x, c, ctx, c_ctx, w_ada, b_ada, norm_g, w_in, q_norm_g, k_norm_g, rpb, conv_w, conv_b, w_out):
    depth = w_ada.shape[0]
    b, s, d = x.shape
    dc = conv_w.shape[2]
    da = (w_in.shape[2] - 4 * dc) // 4
    n_heads = da // HEAD_DIM
    rows = s // GRID_W
    assert depth == 1 and s % GRID_W == 0 and rows >= WIN_H and da % LANES == 0

    cos, slo, shi = _rope_tables(s)
    assert da % MXU_DIM == 0 and MXU_DIM % HEAD_DIM == 0
    seg = np.arange(MXU_DIM) // HEAD_DIM
    bd = jnp.asarray((seg[:, None] == seg[None, :]).astype(np.float32), dtype=BF16)

    ada, bias, w_in_b = _prep(c, c_ctx, w_ada[0], b_ada[0], rpb[0], w_in[0])

    ng = norm_g[0].reshape(1, d)
    gains = jnp.tile(jnp.stack([q_norm_g[0] * (HEAD_DIM ** -0.5 * LOG2_E), k_norm_g[0]]), (1, n_heads))

    kc, vc = _ctx_kv(ctx, ada, b, ng, w_in_b, gains, bd, tm=512)
    qrot, qpl, k, v, ga, conv = _in_proj(x, ada, ng, w_in_b, gains, cos, slo, shi, bd,
                                         conv_w[0], conv_b[0].reshape(1, dc), tm=1024, sub=512)
    attn = _attention(qrot, qpl, k, v, kc, vc, ga, bias, group=2, n_batch=4, steps=20)
    return _out_proj(x, attn, conv, ada, w_out[0], tm=2048)
```

```python
import functools

import numpy as np
import jax
import jax.numpy as jnp
from jax import lax
from jax.experimental import pallas as pl
from jax.experimental.pallas import tpu as pltpu

F32 = jnp.float32
BF16 = jnp.bfloat16

HEAD_DIM = 64
GRID_W = 64
WIN_H = 8
WIN_W = 16
CONV_K = 3
ROPE_THETA = 10000.0
RMS_EPS = 1e-6
MASK_VALUE = -1e30
LOG2_E = 1.4426950408889634

LANES = 128
MXU_DIM = 256
F32_ROWS = 8
BF16_ROWS = 16
HALO = BF16_ROWS
VMEM_LIMIT = 56 * 1024 * 1024


def _silu(z):
    return z * jax.nn.sigmoid(z)


def _prep_kernel(c_ref, cctx_ref, w_ref, b_ref, rpb_ref, win_ref, ada_ref, bias_ref, winb_ref):
    nb, d = c_ref.shape
    cond = jnp.concatenate([c_ref[...], jnp.broadcast_to(cctx_ref[...], (ada_ref.shape[0] - nb, d))], axis=0)
    a = _silu(cond).astype(BF16)
    ada_ref[...] = jnp.dot(a, w_ref[...].astype(BF16), preferred_element_type=F32) + b_ref[...]
    winb_ref[...] = win_ref[...].astype(BF16)
    _bias_block(rpb_ref, bias_ref)


def _prep(c, c_ctx, w_ada, b_ada, rpb, w_in):
    nb, d = c.shape
    assert nb % F32_ROWS == 0
    rows = nb + F32_ROWS
    n = w_ada.shape[1]
    nh, ndr, ndc = rpb.shape
    steps = nh // 2
    tn = n // steps
    tw = w_in.shape[1] // steps
    assert ndr == 2 * WIN_H - 1 and ndc == 2 * WIN_W - 1 and ndc <= GRID_W and tn % LANES == 0 and tw % LANES == 0
    rpb_pad = jnp.pad(rpb, ((0, 0), (0, 0), (0, GRID_W - ndc)))
    return pl.pallas_call(
        _prep_kernel,
        grid=(steps,),
        in_specs=[pl.BlockSpec((nb, d), lambda i: (0, 0)),
                  pl.BlockSpec((1, d), lambda i: (0, 0)),
                  pl.BlockSpec((d, tn), lambda i: (0, i)),
                  pl.BlockSpec((1, tn), lambda i: (0, i)),
                  pl.BlockSpec((2, ndr, GRID_W), lambda i: (i, 0, 0)),
                  pl.BlockSpec((d, tw), lambda i: (0, i))],
        out_specs=[pl.BlockSpec((rows, tn), lambda i: (0, i)),
                   pl.BlockSpec((1, ndr - 1, 2 * GRID_W, 2 * GRID_W), lambda i: (i, 0, 0, 0)),
                   pl.BlockSpec((d, tw), lambda i: (0, i))],
        out_shape=[jax.ShapeDtypeStruct((rows, n), F32),
                   jax.ShapeDtypeStruct((steps, ndr - 1, 2 * GRID_W, 2 * GRID_W), F32),
                   jax.ShapeDtypeStruct(w_in.shape, BF16)],
        compiler_params=pltpu.CompilerParams(vmem_limit_bytes=VMEM_LIMIT),
        name="prep",
    )(c, c_ctx.reshape(1, d), w_ada, b_ada.reshape(1, n), rpb_pad, w_in)


def _ada_row(ada_ref, row):
    r = ada_ref[pl.ds(row, 1), :]
    d = r.shape[1] // 3
    return r[:, 0:d], r[:, d:2 * d], r[:, 2 * d:]


def _modulated_norm(xt, mult, shift):
    ms = jnp.mean(xt * xt, axis=-1, keepdims=True)
    return (xt * lax.rsqrt(ms + RMS_EPS) * mult + shift).astype(BF16)


def _head_norm(t, gain, bd):
    t2 = (t * t).astype(BF16)
    w = bd.shape[0]
    ss = jnp.concatenate([jnp.dot(t2[:, c:c + w], bd, preferred_element_type=F32)
                          for c in range(0, t.shape[1], w)], axis=1)
    return t * lax.rsqrt(ss * (1.0 / HEAD_DIM) + RMS_EPS) * gain


def _rope(t, cos, sin_lo, sin_hi):
    outs = []
    for c in range(t.shape[1] // LANES):
        tc = t[:, c * LANES:(c + 1) * LANES]
        up = pltpu.roll(tc, LANES - HEAD_DIM // 4, 1)
        dn = pltpu.roll(tc, HEAD_DIM // 4, 1)
        outs.append(tc * cos + up * sin_lo + dn * sin_hi)
    return jnp.concatenate(outs, axis=1)


def _ctx_kernel(ctx_ref, ada_ref, ng_ref, wk_ref, wv_ref, g_ref, bd_ref, kc_ref, vc_ref, *, ctx_row):
    shift, scale, _ = _ada_row(ada_ref, ctx_row)
    mult = ng_ref[...] * (1.0 + scale)
    h = _modulated_norm(ctx_ref[...], mult, shift)
    kf = jnp.dot(h, wk_ref[...], preferred_element_type=F32)
    kn = _head_norm(kf, g_ref[1:2, :], bd_ref[...])
    kc_ref[...] = kn.astype(BF16)
    vc_ref[...] = jnp.dot(h, wv_ref[...], preferred_element_type=F32).astype(BF16)


def _ctx_kv(ctx, ada, ctx_row, norm_g, w_in, gains, bd, *, tm):
    b, l, d = ctx.shape
    d_attn = gains.shape[1]
    n = b * l
    assert n % tm == 0
    kc, vc = pl.pallas_call(
        functools.partial(_ctx_kernel, ctx_row=ctx_row),
        grid=(n // tm,),
        in_specs=[pl.BlockSpec((tm, d), lambda i: (i, 0)),
                  pl.BlockSpec(ada.shape, lambda i: (0, 0)),
                  pl.BlockSpec((1, d), lambda i: (0, 0)),
                  pl.BlockSpec((d, d_attn), lambda i: (0, 1)),
                  pl.BlockSpec((d, d_attn), lambda i: (0, 2)),
                  pl.BlockSpec(gains.shape, lambda i: (0, 0)),
                  pl.BlockSpec(bd.shape, lambda i: (0, 0))],
        out_specs=[pl.BlockSpec((tm, d_attn), lambda i: (i, 0)),
                   pl.BlockSpec((tm, d_attn), lambda i: (i, 0))],
        out_shape=[jax.ShapeDtypeStruct((n, d_attn), BF16),
                   jax.ShapeDtypeStruct((n, d_attn), BF16)],
        compiler_params=pltpu.CompilerParams(vmem_limit_bytes=VMEM_LIMIT),
        name="ctx_kv",
    )(ctx.reshape(n, d), ada, norm_g, w_in, w_in, gains, bd)
    return kc.reshape(b, l, d_attn), vc.reshape(b, l, d_attn)


def _in_proj_kernel(x_ref, xp_ref, xn_ref, ada_ref, ng_ref, w_ref, g_ref,
                    cos_ref, slo_ref, shi_ref, bd_ref, cw_ref, cb_ref,
                    qrot_ref, qpl_ref, k_ref, v_ref, ga_ref, conv_ref,
                    hext_ref, cu_ref, *, tm, sub, da, dc):
    j = pl.program_id(1)
    nj = pl.num_programs(1)
    shift, scale, _ = _ada_row(ada_ref, pl.program_id(0))
    mult = ng_ref[...] * (1.0 + scale)

    def proj(src, lo, width):
        return jnp.dot(hext_ref[src, :], w_ref[:, lo:lo + width], preferred_element_type=F32)

    bd = bd_ref[...]
    hext_ref[0:HALO, :] = _modulated_norm(xp_ref[0], mult, shift)
    hext_ref[HALO + tm:, :] = _modulated_norm(xn_ref[0], mult, shift)

    for n in range(tm // sub):
        r0 = n * sub
        hext_ref[HALO + r0:HALO + r0 + sub, :] = _modulated_norm(x_ref[0, r0:r0 + sub, :], mult, shift)
    for n in range(tm // sub):
        r0 = n * sub
        rows = slice(r0, r0 + sub)
        h = slice(HALO + r0, HALO + r0 + sub)
        cos, slo, shi = cos_ref[rows, :], slo_ref[rows, :], shi_ref[rows, :]


        hx = slice(r0, r0 + sub + 2 * HALO)
        cu = proj(hx, 4 * da, dc) * proj(hx, 4 * da + 2 * dc, dc)
        row = lax.broadcasted_iota(jnp.int32, (sub + 2 * HALO, 1), 0) + r0
        inside = ((row >= HALO) | (j > 0)) & ((row < HALO + tm) | (j < nj - 1))
        cu_ref[n] = jnp.where(inside, cu, 0.0)
        y = (cb_ref[...]
             + cw_ref[0:1, :] * cu_ref[n, HALO - 1:HALO - 1 + sub, :]
             + cw_ref[1:2, :] * cu_ref[n, HALO:HALO + sub, :]
             + cw_ref[2:3, :] * cu_ref[n, HALO + 1:HALO + 1 + sub, :])
        bg = proj(h, 4 * da + dc, dc)
        zc = proj(h, 4 * da + 3 * dc, dc)
        conv_ref[0, rows, :] = (bg * y * _silu(zc)).astype(BF16)

        qn = _head_norm(proj(h, 0, da), g_ref[0:1, :], bd)
        qpl_ref[0, rows, :] = qn.astype(BF16)
        qrot_ref[0, rows, :] = _rope(qn, cos, slo, shi).astype(BF16)

        kn = _head_norm(proj(h, da, da), g_ref[1:2, :], bd)
        k_ref[0, rows, :] = _rope(kn, cos, slo, shi).astype(BF16)

        ga_ref[0, rows, :] = _silu(proj(h, 3 * da, da)).astype(BF16)
        v_ref[0, rows, :] = proj(h, 2 * da, da).astype(BF16)


def _in_proj(x, ada, norm_g, w_in, gains, cos, slo, shi, bd, conv_w, conv_b, *, tm, sub):
    b, s, d = x.shape
    assert s % tm == 0 and tm % sub == 0 and sub % HALO == 0
    dc = conv_w.shape[1]
    da = (w_in.shape[1] - 4 * dc) // 4
    nh = tm // HALO
    last_halo = s // HALO - 1
    tok = lambda width: pl.BlockSpec((1, tm, width), lambda i, j: (i, j, 0))
    const = lambda shape: pl.BlockSpec(shape, lambda i, j: (0,) * len(shape))
    tab = pl.BlockSpec((tm, LANES), lambda i, j: (j, 0))
    out_tok = jax.ShapeDtypeStruct((b, s, da), BF16)
    return pl.pallas_call(
        functools.partial(_in_proj_kernel, tm=tm, sub=sub, da=da, dc=dc),
        grid=(b, s // tm),
        in_specs=[tok(d),
                  pl.BlockSpec((1, HALO, d), lambda i, j: (i, jnp.maximum(j * nh - 1, 0), 0)),
                  pl.BlockSpec((1, HALO, d), lambda i, j: (i, jnp.minimum((j + 1) * nh, last_halo), 0)),
                  const(ada.shape),
                  const((1, d)), const(w_in.shape), const(gains.shape),
                  tab, tab, tab, const(bd.shape), const((CONV_K, dc)), const((1, dc))],
        out_specs=[tok(da), tok(da), tok(da), tok(da), tok(da), tok(dc)],
        out_shape=[out_tok, out_tok, out_tok, out_tok, out_tok,
                   jax.ShapeDtypeStruct((b, s, dc), BF16)],
        scratch_shapes=[pltpu.VMEM((tm + 2 * HALO, d), BF16),
                        pltpu.VMEM((tm // sub, sub + 2 * HALO, dc), F32)],
        compiler_params=pltpu.CompilerParams(vmem_limit_bytes=VMEM_LIMIT),
        name="in_proj",
    )(x, x, x, ada, norm_g, w_in, gains, cos, slo, shi, bd, conv_w, conv_b)


def _dot_nt(a, b):
    return lax.dot_general(a, b, (((1,), (1,)), ((), ())), preferred_element_type=F32)


def _attn_kernel(qrot_ref, qpl_ref, k_ref, v_ref, kc_ref, vc_ref, ga_ref, bias_ref, o_ref,
                 s_lat_ref, s_ctx_ref, p_lat_ref, p_ctx_ref, *, rows, group, n_batch, steps):
    n_units = n_batch * rows // group
    nk = WIN_H * GRID_W

    lane = lax.broadcasted_iota(jnp.int32, (GRID_W, LANES), 1)
    first_head = lane < HEAD_DIM

    def stack_heads(q2):
        zero = jnp.zeros_like(q2)
        return jnp.concatenate([jnp.where(first_head, q2, zero), jnp.where(first_head, zero, q2)], axis=0)

    def geometry(u, g):
        r = u * group + g
        if isinstance(r, int):
            bb, i = divmod(r, rows)
            rs = min(max(i - WIN_H // 2, 0), rows - WIN_H)
            return bb, i - rs, i * GRID_W, rs * GRID_W
        bb = r // rows
        i = r % rows
        rs = jnp.clip(i - WIN_H // 2, 0, rows - WIN_H)
        return bb, i - rs, pl.multiple_of(i * GRID_W, GRID_W), pl.multiple_of(rs * GRID_W, GRID_W)


    def scores_matmul(u):
        out = []
        for g in range(group):
            bb, off, tok0, key0 = geometry(u, g)
            qs = stack_heads(qrot_ref[bb, pl.ds(tok0, GRID_W), :])
            kband = k_ref[bb, pl.ds(key0, nk), :]
            qp = stack_heads(qpl_ref[bb, pl.ds(tok0, GRID_W), :])
            out.append((_dot_nt(qs, kband), off, _dot_nt(qp, kc_ref[bb])))
        return out


    def scores_store(slot, vals):
        for g, (s_lat, off, s_ctx) in enumerate(vals):
            first = (WIN_H - 1) - off
            bias = jnp.concatenate([bias_ref[0, first + 2 * t] for t in range(WIN_H // 2)], axis=1)
            s_lat_ref[slot, g] = s_lat + bias
            s_ctx_ref[slot, g] = s_ctx

    def softmax(slot):
        for g in range(group):
            s_lat = s_lat_ref[slot, g]
            s_ctx = s_ctx_ref[slot, g]
            m = jnp.maximum(jnp.max(s_lat, axis=-1, keepdims=True), jnp.max(s_ctx, axis=-1, keepdims=True))
            p_lat_ref[slot, g] = jnp.exp2(s_lat - m).astype(BF16)
            p_ctx_ref[slot, g] = jnp.exp2(s_ctx - m).astype(BF16)

    ones_lat = jnp.ones((nk, LANES), BF16)
    ones_ctx = jnp.ones((kc_ref.shape[1], LANES), BF16)

    def pv_matmul(u, slot):
        out = []
        for g in range(group):
            bb, _, _, key0 = geometry(u, g)
            v_lat = jnp.concatenate([v_ref[bb, pl.ds(key0, nk), :], ones_lat], axis=1)
            v_ctx = jnp.concatenate([vc_ref[bb], ones_ctx], axis=1)
            out.append(jnp.dot(p_lat_ref[slot, g], v_lat, preferred_element_type=F32)
                       + jnp.dot(p_ctx_ref[slot, g], v_ctx, preferred_element_type=F32))
        return out

    def pv_store(u, vals):
        for g, ol in enumerate(vals):
            bb, _, tok0, _ = geometry(u, g)
            o = ol[:, 0:LANES] * (1.0 / ol[:, LANES:])
            o2 = jnp.where(first_head, o[0:GRID_W], o[GRID_W:2 * GRID_W])
            gate = ga_ref[bb, pl.ds(tok0, GRID_W), :].astype(F32)
            o_ref[bb, pl.ds(tok0, GRID_W), :] = (o2 * gate).astype(BF16)

    scores_store(0, scores_matmul(0))
    sc = scores_matmul(1)
    softmax(0)
    scores_store(1, sc)

    def step(t, slot):
        sc = scores_matmul(t)
        pv = pv_matmul(t - 2, slot)
        softmax(1 - slot)
        scores_store(slot, sc)
        pv_store(t - 2, pv)

    n_steady, n_left = divmod(n_units - 2, steps)

    def steady(n, carry):
        for j in range(steps):
            step(steps * n + 2 + j, j % 2)
        return carry

    lax.fori_loop(0, n_steady, steady, 0)
    for j in range(n_left):
        step(steps * n_steady + 2 + j, j % 2)

    pv = pv_matmul(n_units - 2, 0)
    softmax(1)
    pv_store(n_units - 2, pv)
    pv_store(n_units - 1, pv_matmul(n_units - 1, 1))


def _attention(qrot, qpl, k, v, kc, vc, ga, bias, *, group, n_batch, steps):
    b, s, da = qrot.shape
    l = vc.shape[1]
    rows = s // GRID_W
    n_hp = da // LANES
    nq = 2 * GRID_W
    nk = WIN_H * GRID_W
    assert rows % (2 * group) == 0 and b % n_batch == 0 and steps % 2 == 0
    tok = pl.BlockSpec((n_batch, s, LANES), lambda hp, i: (i, 0, hp))
    ctx_tok = pl.BlockSpec((n_batch, l, LANES), lambda hp, i: (i, 0, hp))
    return pl.pallas_call(
        functools.partial(_attn_kernel, rows=rows, group=group, n_batch=n_batch, steps=steps),
        grid=(n_hp, b // n_batch),
        in_specs=[tok, tok, tok, tok, ctx_tok, ctx_tok, tok,
                  pl.BlockSpec((1,) + bias.shape[1:], lambda hp, i: (hp, 0, 0, 0))],
        out_specs=tok,
        out_shape=jax.ShapeDtypeStruct((b, s, da), BF16),
        scratch_shapes=[pltpu.VMEM((2, group, nq, nk), F32),
                        pltpu.VMEM((2, group, nq, l), F32),
                        pltpu.VMEM((2, group, nq, nk), BF16),
                        pltpu.VMEM((2, group, nq, l), BF16)],
        compiler_params=pltpu.CompilerParams(vmem_limit_bytes=VMEM_LIMIT),
        name="attn",
    )(qrot, qpl, k, v, kc, vc, ga, bias)


def _out_proj_kernel(x_ref, a_ref, c_ref, ada_ref, w_ref, o_ref, *, da):
    _, _, gate = _ada_row(ada_ref, pl.program_id(0))
    upd = (jnp.dot(a_ref[0], w_ref[0:da, :].astype(BF16), preferred_element_type=F32)
           + jnp.dot(c_ref[0], w_ref[da:, :].astype(BF16), preferred_element_type=F32))
    o_ref[0] = x_ref[0] + gate * upd


def _out_proj(x, attn, conv, ada, w_out, *, tm):
    b, s, d = x.shape
    da = attn.shape[2]
    dc = conv.shape[2]
    return pl.pallas_call(
        functools.partial(_out_proj_kernel, da=da),
        grid=(b, s // tm),
        in_specs=[pl.BlockSpec((1, tm, d), lambda i, j: (i, j, 0)),
                  pl.BlockSpec((1, tm, da), lambda i, j: (i, j, 0)),
                  pl.BlockSpec((1, tm, dc), lambda i, j: (i, j, 0)),
                  pl.BlockSpec(ada.shape, lambda i, j: (0, 0)),
                  pl.BlockSpec(w_out.shape, lambda i, j: (0, 0))],
        out_specs=pl.BlockSpec((1, tm, d), lambda i, j: (i, j, 0)),
        out_shape=jax.ShapeDtypeStruct((b, s, d), F32),
        compiler_params=pltpu.CompilerParams(vmem_limit_bytes=VMEM_LIMIT),
        name="out_proj",
    )(x, attn, conv, ada, w_out)


def _rope_tables(s):
    nf = HEAD_DIM // 4
    inv = (ROPE_THETA ** (-np.arange(nf, dtype=np.float32) / nf)).astype(np.float32)
    pos = np.arange(s)
    lane = np.arange(LANES)
    d = lane % HEAD_DIM
    axis = d // (2 * nf)
    half = (d % (2 * nf)) // nf
    coord = np.where(axis[None, :] == 0, (pos // GRID_W)[:, None], (pos % GRID_W)[:, None]).astype(np.float32)
    ang = (coord * inv[d % nf][None, :]).astype(np.float32)
    cos = np.cos(ang).astype(np.float32)
    sin = np.sin(ang).astype(np.float32)
    sin_lo = np.where(half[None, :] == 0, -sin, 0.0).astype(np.float32)
    sin_hi = np.where(half[None, :] == 1, sin, 0.0).astype(np.float32)
    return jnp.asarray(cos), jnp.asarray(sin_lo), jnp.asarray(sin_hi)


def _bias_block(rpb_ref, o_ref):
    cq = lax.broadcasted_iota(jnp.int32, (GRID_W, LANES), 0)
    ck = lax.broadcasted_iota(jnp.int32, (GRID_W, LANES), 1) % GRID_W
    col_start = jnp.clip(cq - WIN_W // 2, 0, GRID_W - WIN_W)
    valid = (ck >= col_start) & (ck < col_start + WIN_W)

    def one_pair(dr, carry):
        for a in range(2):
            two = rpb_ref[a, pl.ds(dr, 2), :]
            lanes = jnp.broadcast_to(jnp.concatenate([two[0:1], two[1:2]], axis=1), (GRID_W, LANES))
            toeplitz = pltpu.roll(lanes, LANES - (WIN_W - 1), 1, stride=1, stride_axis=0)
            o_ref[0, dr, a * GRID_W:(a + 1) * GRID_W, :] = jnp.where(valid, toeplitz * LOG2_E, MASK_VALUE)
        return carry

    lax.fori_loop(0, o_ref.shape[1], one_pair, 0, unroll=2)


def kernel(x, c, ctx, c_ctx, w_ada, b_ada, norm_g, w_in, q_norm_g, k_norm_g, rpb, conv_w, conv_b, w_out):
    depth = w_ada.shape[0]
    b, s, d = x.shape
    dc = conv_w.shape[2]
    da = (w_in.shape[2] - 4 * dc) // 4
    n_heads = da // HEAD_DIM
    rows = s // GRID_W
    assert depth == 1 and s % GRID_W == 0 and rows >= WIN_H and da % LANES == 0

    cos, slo, shi = _rope_tables(s)
    assert da % MXU_DIM == 0 and MXU_DIM % HEAD_DIM == 0
    seg = np.arange(MXU_DIM) // HEAD_DIM
    bd = jnp.asarray((seg[:, None] == seg[None, :]).astype(np.float32), dtype=BF16)

    ada, bias, w_in_b = _prep(c, c_ctx, w_ada[0], b_ada[0], rpb[0], w_in[0])

    ng = norm_g[0].reshape(1, d)
    gains = jnp.tile(jnp.stack([q_norm_g[0] * (HEAD_DIM ** -0.5 * LOG2_E), k_norm_g[0]]), (1, n_heads))

    kc, vc = _ctx_kv(ctx, ada, b, ng, w_in_b, gains, bd, tm=512)
    qrot, qpl, k, v, ga, conv = _in_proj(x, ada, ng, w_in_b, gains, cos, slo, shi, bd,
                                         conv_w[0], conv_b[0].reshape(1, dc), tm=1024, sub=512)
    attn = _attention(qrot, qpl, k, v, kc, vc, ga, bias, group=2, n_batch=4, steps=20)
    return _out_proj(x, attn, conv, ada, w_out[0], tm=2048)
```

```python
import functools

import numpy as np
import jax
import jax.numpy as jnp
from jax import lax
from jax.experimental import pallas as pl
from jax.experimental.pallas import tpu as pltpu

F32 = jnp.float32
BF16 = jnp.bfloat16

HEAD_DIM = 64
GRID_W = 64
WIN_H = 8
WIN_W = 16
CONV_K = 3
ROPE_THETA = 10000.0
RMS_EPS = 1e-6
MASK_VALUE = -1e30
LOG2_E = 1.4426950408889634

LANES = 128
MXU_DIM = 256
F32_ROWS = 8
BF16_ROWS = 16
HALO = BF16_ROWS
VMEM_CAPACITY = 64 * 1024 * 1024
VMEM_LIMIT = VMEM_CAPACITY * 7 // 8

IN_PROJ_ROWS, IN_PROJ_SUB = 1024, 512
CTX_ROWS = 512
OUT_PROJ_ROWS = 2048
ATTN_GROUP, ATTN_BATCH, ATTN_STEPS = 2, 4, 30


def _silu(z):
    return z * jax.nn.sigmoid(z)


def _prep_kernel(c_ref, cctx_ref, w_ref, b_ref, rpb_ref, win_ref, ada_ref, bias_ref, winb_ref):
    nb, d = c_ref.shape
    cond = jnp.concatenate([c_ref[...], jnp.broadcast_to(cctx_ref[...], (ada_ref.shape[0] - nb, d))], axis=0)
    a = _silu(cond).astype(BF16)
    ada_ref[...] = jnp.dot(a, w_ref[...].astype(BF16), preferred_element_type=F32) + b_ref[...]
    winb_ref[...] = win_ref[...].astype(BF16)
    _bias_block(rpb_ref, bias_ref)


def _prep(c, c_ctx, w_ada, b_ada, rpb, w_in):
    nb, d = c.shape
    assert nb % F32_ROWS == 0
    rows = nb + F32_ROWS
    n = w_ada.shape[1]
    nh, ndr, ndc = rpb.shape
    steps = nh // 2
    tn = n // steps
    tw = w_in.shape[1] // steps
    assert ndr == 2 * WIN_H - 1 and ndc == 2 * WIN_W - 1 and ndc <= GRID_W and tn % LANES == 0 and tw % LANES == 0
    rpb_pad = jnp.pad(rpb, ((0, 0), (0, 0), (0, GRID_W - ndc)))
    return pl.pallas_call(
        _prep_kernel,
        grid=(steps,),
        in_specs=[pl.BlockSpec((nb, d), lambda i: (0, 0)),
                  pl.BlockSpec((1, d), lambda i: (0, 0)),
                  pl.BlockSpec((d, tn), lambda i: (0, i)),
                  pl.BlockSpec((1, tn), lambda i: (0, i)),
                  pl.BlockSpec((2, ndr, GRID_W), lambda i: (i, 0, 0)),
                  pl.BlockSpec((d, tw), lambda i: (0, i))],
        out_specs=[pl.BlockSpec((rows, tn), lambda i: (0, i)),
                   pl.BlockSpec((1, ndr - 1, 2 * GRID_W, 2 * GRID_W), lambda i: (i, 0, 0, 0)),
                   pl.BlockSpec((d, tw), lambda i: (0, i))],
        out_shape=[jax.ShapeDtypeStruct((rows, n), F32),
                   jax.ShapeDtypeStruct((steps, ndr - 1, 2 * GRID_W, 2 * GRID_W), F32),
                   jax.ShapeDtypeStruct(w_in.shape, BF16)],
        compiler_params=pltpu.CompilerParams(vmem_limit_bytes=VMEM_LIMIT),
        name="prep",
    )(c, c_ctx.reshape(1, d), w_ada, b_ada.reshape(1, n), rpb_pad, w_in)


def _ada_row(ada_ref, row):
    r = ada_ref[pl.ds(row, 1), :]
    d = r.shape[1] // 3
    return r[:, 0:d], r[:, d:2 * d], r[:, 2 * d:]


def _modulated_norm(xt, mult, shift):
    ms = jnp.mean(xt * xt, axis=-1, keepdims=True)
    return (xt * lax.rsqrt(ms + RMS_EPS) * mult + shift).astype(BF16)


def _head_norm(t, gain, bd):
    t2 = (t * t).astype(BF16)
    w = bd.shape[0]
    ss = jnp.concatenate([jnp.dot(t2[:, c:c + w], bd, preferred_element_type=F32)
                          for c in range(0, t.shape[1], w)], axis=1)
    return t * lax.rsqrt(ss * (1.0 / HEAD_DIM) + RMS_EPS) * gain


def _rope(t, cos, sin_lo, sin_hi):
    outs = []
    for c in range(t.shape[1] // LANES):
        tc = t[:, c * LANES:(c + 1) * LANES]
        up = pltpu.roll(tc, LANES - HEAD_DIM // 4, 1)
        dn = pltpu.roll(tc, HEAD_DIM // 4, 1)
        outs.append(tc * cos + up * sin_lo + dn * sin_hi)
    return jnp.concatenate(outs, axis=1)


def _ctx_kernel(ctx_ref, ada_ref, ng_ref, wk_ref, wv_ref, g_ref, bd_ref, kc_ref, vc_ref, *, ctx_row):
    shift, scale, _ = _ada_row(ada_ref, ctx_row)
    mult = ng_ref[...] * (1.0 + scale)
    h = _modulated_norm(ctx_ref[...], mult, shift)
    kf = jnp.dot(h, wk_ref[...], preferred_element_type=F32)
    kn = _head_norm(kf, g_ref[1:2, :], bd_ref[...])
    kc_ref[...] = kn.astype(BF16)
    vc_ref[...] = jnp.dot(h, wv_ref[...], preferred_element_type=F32).astype(BF16)


def _ctx_kv(ctx, ada, ctx_row, norm_g, w_in, gains, bd, *, tm):
    b, l, d = ctx.shape
    d_attn = gains.shape[1]
    n = b * l
    assert n % tm == 0
    kc, vc = pl.pallas_call(
        functools.partial(_ctx_kernel, ctx_row=ctx_row),
        grid=(n // tm,),
        in_specs=[pl.BlockSpec((tm, d), lambda i: (i, 0)),
                  pl.BlockSpec(ada.shape, lambda i: (0, 0)),
                  pl.BlockSpec((1, d), lambda i: (0, 0)),
                  pl.BlockSpec((d, d_attn), lambda i: (0, 1)),
                  pl.BlockSpec((d, d_attn), lambda i: (0, 2)),
                  pl.BlockSpec(gains.shape, lambda i: (0, 0)),
                  pl.BlockSpec(bd.shape, lambda i: (0, 0))],
        out_specs=[pl.BlockSpec((tm, d_attn), lambda i: (i, 0)),
                   pl.BlockSpec((tm, d_attn), lambda i: (i, 0))],
        out_shape=[jax.ShapeDtypeStruct((n, d_attn), BF16),
                   jax.ShapeDtypeStruct((n, d_attn), BF16)],
        compiler_params=pltpu.CompilerParams(vmem_limit_bytes=VMEM_LIMIT),
        name="ctx_kv",
    )(ctx.reshape(n, d), ada, norm_g, w_in, w_in, gains, bd)
    return kc.reshape(b, l, d_attn), vc.reshape(b, l, d_attn)


def _in_proj_kernel(x_ref, xp_ref, xn_ref, ada_ref, ng_ref, w_ref, g_ref,
                    cos_ref, slo_ref, shi_ref, bd_ref, cw_ref, cb_ref,
                    qrot_ref, qpl_ref, k_ref, v_ref, ga_ref, conv_ref,
                    hext_ref, cu_ref, *, tm, sub, da, dc):
    j = pl.program_id(1)
    nj = pl.num_programs(1)
    shift, scale, _ = _ada_row(ada_ref, pl.program_id(0))
    mult = ng_ref[...] * (1.0 + scale)

    def proj(src, lo, width):
        return jnp.dot(hext_ref[src, :], w_ref[:, lo:lo + width], preferred_element_type=F32)

    bd = bd_ref[...]
    hext_ref[0:HALO, :] = _modulated_norm(xp_ref[0], mult, shift)
    hext_ref[HALO + tm:, :] = _modulated_norm(xn_ref[0], mult, shift)

    for n in range(tm // sub):
        r0 = n * sub
        hext_ref[HALO + r0:HALO + r0 + sub, :] = _modulated_norm(x_ref[0, r0:r0 + sub, :], mult, shift)
    for n in range(tm // sub):
        r0 = n * sub
        rows = slice(r0, r0 + sub)
        h = slice(HALO + r0, HALO + r0 + sub)
        cos, slo, shi = cos_ref[rows, :], slo_ref[rows, :], shi_ref[rows, :]


        hx = slice(r0, r0 + sub + 2 * HALO)
        cu = proj(hx, 4 * da, dc) * proj(hx, 4 * da + 2 * dc, dc)
        row = lax.broadcasted_iota(jnp.int32, (sub + 2 * HALO, 1), 0) + r0
        inside = ((row >= HALO) | (j > 0)) & ((row < HALO + tm) | (j < nj - 1))
        cu_ref[n] = jnp.where(inside, cu, 0.0)
        y = (cb_ref[...]
             + cw_ref[0:1, :] * cu_ref[n, HALO - 1:HALO - 1 + sub, :]
             + cw_ref[1:2, :] * cu_ref[n, HALO:HALO + sub, :]
             + cw_ref[2:3, :] * cu_ref[n, HALO + 1:HALO + 1 + sub, :])
        bg = proj(h, 4 * da + dc, dc)
        zc = proj(h, 4 * da + 3 * dc, dc)
        conv_ref[0, rows, :] = (bg * y * _silu(zc)).astype(BF16)

        qn = _head_norm(proj(h, 0, da), g_ref[0:1, :], bd)
        qpl_ref[0, rows, :] = qn.astype(BF16)
        qrot_ref[0, rows, :] = _rope(qn, cos, slo, shi).astype(BF16)

        kn = _head_norm(proj(h, da, da), g_ref[1:2, :], bd)
        k_ref[0, rows, :] = _rope(kn, cos, slo, shi).astype(BF16)

        ga_ref[0, rows, :] = _silu(proj(h, 3 * da, da)).astype(BF16)
        v_ref[0, rows, :] = proj(h, 2 * da, da).astype(BF16)


def _in_proj(x, ada, norm_g, w_in, gains, cos, slo, shi, bd, conv_w, conv_b, *, tm, sub):
    b, s, d = x.shape
    assert s % tm == 0 and tm % sub == 0 and sub % HALO == 0
    dc = conv_w.shape[1]
    da = (w_in.shape[1] - 4 * dc) // 4
    nh = tm // HALO
    last_halo = s // HALO - 1
    tok = lambda width: pl.BlockSpec((1, tm, width), lambda i, j: (i, j, 0))
    const = lambda shape: pl.BlockSpec(shape, lambda i, j: (0,) * len(shape))
    tab = pl.BlockSpec((tm, LANES), lambda i, j: (j, 0))
    out_tok = jax.ShapeDtypeStruct((b, s, da), BF16)
    return pl.pallas_call(
        functools.partial(_in_proj_kernel, tm=tm, sub=sub, da=da, dc=dc),
        grid=(b, s // tm),
        in_specs=[tok(d),
                  pl.BlockSpec((1, HALO, d), lambda i, j: (i, jnp.maximum(j * nh - 1, 0), 0)),
                  pl.BlockSpec((1, HALO, d), lambda i, j: (i, jnp.minimum((j + 1) * nh, last_halo), 0)),
                  const(ada.shape),
                  const((1, d)), const(w_in.shape), const(gains.shape),
                  tab, tab, tab, const(bd.shape), const((CONV_K, dc)), const((1, dc))],
        out_specs=[tok(da), tok(da), tok(da), tok(da), tok(da), tok(dc)],
        out_shape=[out_tok, out_tok, out_tok, out_tok, out_tok,
                   jax.ShapeDtypeStruct((b, s, dc), BF16)],
        scratch_shapes=[pltpu.VMEM((tm + 2 * HALO, d), BF16),
                        pltpu.VMEM((tm // sub, sub + 2 * HALO, dc), F32)],
        compiler_params=pltpu.CompilerParams(vmem_limit_bytes=VMEM_LIMIT),
        name="in_proj",
    )(x, x, x, ada, norm_g, w_in, gains, cos, slo, shi, bd, conv_w, conv_b)


def _dot_nt(a, b):
    return lax.dot_general(a, b, (((1,), (1,)), ((), ())), preferred_element_type=F32)


def _attn_kernel(qrot_ref, qpl_ref, k_ref, v_ref, kc_ref, vc_ref, ga_ref, bias_ref, o_ref,
                 s_lat_ref, s_ctx_ref, p_lat_ref, p_ctx_ref, *, rows, group, n_batch, steps):
    n_units = n_batch * rows // group
    nk = WIN_H * GRID_W

    lane = lax.broadcasted_iota(jnp.int32, (GRID_W, LANES), 1)
    first_head = lane < HEAD_DIM

    def stack_heads(q2):
        zero = jnp.zeros_like(q2)
        return jnp.concatenate([jnp.where(first_head, q2, zero), jnp.where(first_head, zero, q2)], axis=0)

    def geometry(u, g):
        r = u * group + g
        if isinstance(r, int):
            bb, i = divmod(r, rows)
            rs = min(max(i - WIN_H // 2, 0), rows - WIN_H)
            return bb, i - rs, i * GRID_W, rs * GRID_W
        bb = r // rows
        i = r % rows
        rs = jnp.clip(i - WIN_H // 2, 0, rows - WIN_H)
        return bb, i - rs, pl.multiple_of(i * GRID_W, GRID_W), pl.multiple_of(rs * GRID_W, GRID_W)


    def scores_matmul(u):
        out = []
        for g in range(group):
            bb, off, tok0, key0 = geometry(u, g)
            qs = stack_heads(qrot_ref[bb, pl.ds(tok0, GRID_W), :])
            kband = k_ref[bb, pl.ds(key0, nk), :]
            qp = stack_heads(qpl_ref[bb, pl.ds(tok0, GRID_W), :])
            out.append((_dot_nt(qs, kband), off, _dot_nt(qp, kc_ref[bb])))
        return out


    def scores_store(slot, vals):
        for g, (s_lat, off, s_ctx) in enumerate(vals):
            first = (WIN_H - 1) - off
            bias = jnp.concatenate([bias_ref[0, first + 2 * t] for t in range(WIN_H // 2)], axis=1)
            s_lat_ref[slot, g] = s_lat + bias
            s_ctx_ref[slot, g] = s_ctx

    def softmax(slot):
        for g in range(group):
            s_lat = s_lat_ref[slot, g]
            s_ctx = s_ctx_ref[slot, g]
            m = jnp.maximum(jnp.max(s_lat, axis=-1, keepdims=True), jnp.max(s_ctx, axis=-1, keepdims=True))
            p_lat_ref[slot, g] = jnp.exp2(s_lat - m).astype(BF16)
            p_ctx_ref[slot, g] = jnp.exp2(s_ctx - m).astype(BF16)

    ones_lat = jnp.ones((nk, LANES), BF16)
    ones_ctx = jnp.ones((kc_ref.shape[1], LANES), BF16)

    def pv_matmul(u, slot):
        out = []
        for g in range(group):
            bb, _, _, key0 = geometry(u, g)
            v_lat = jnp.concatenate([v_ref[bb, pl.ds(key0, nk), :], ones_lat], axis=1)
            v_ctx = jnp.concatenate([vc_ref[bb], ones_ctx], axis=1)
            out.append(jnp.dot(p_lat_ref[slot, g], v_lat, preferred_element_type=F32)
                       + jnp.dot(p_ctx_ref[slot, g], v_ctx, preferred_element_type=F32))
        return out

    def pv_store(u, vals):
        for g, ol in enumerate(vals):
            bb, _, tok0, _ = geometry(u, g)
            o = ol[:, 0:LANES] * (1.0 / ol[:, LANES:])
            o2 = jnp.where(first_head, o[0:GRID_W], o[GRID_W:2 * GRID_W])
            gate = ga_ref[bb, pl.ds(tok0, GRID_W), :].astype(F32)
            o_ref[bb, pl.ds(tok0, GRID_W), :] = (o2 * gate).astype(BF16)

    scores_store(0, scores_matmul(0))
    sc = scores_matmul(1)
    softmax(0)
    scores_store(1, sc)

    def step(t, slot):
        sc = scores_matmul(t)
        pv = pv_matmul(t - 2, slot)
        softmax(1 - slot)
        scores_store(slot, sc)
        pv_store(t - 2, pv)

    n_steady, n_left = divmod(n_units - 2, steps)

    def steady(n, carry):
        for j in range(steps):
            step(steps * n + 2 + j, j % 2)
        return carry

    lax.fori_loop(0, n_steady, steady, 0)
    for j in range(n_left):
        step(steps * n_steady + 2 + j, j % 2)

    pv = pv_matmul(n_units - 2, 0)
    softmax(1)
    pv_store(n_units - 2, pv)
    pv_store(n_units - 1, pv_matmul(n_units - 1, 1))


def _attention(qrot, qpl, k, v, kc, vc, ga, bias, *, group, n_batch, steps):
    b, s, da = qrot.shape
    l = vc.shape[1]
    rows = s // GRID_W
    n_hp = da // LANES
    nq = 2 * GRID_W
    nk = WIN_H * GRID_W
    assert rows % (2 * group) == 0 and b % n_batch == 0 and steps % 2 == 0
    tok = pl.BlockSpec((n_batch, s, LANES), lambda hp, i: (i, 0, hp))
    ctx_tok = pl.BlockSpec((n_batch, l, LANES), lambda hp, i: (i, 0, hp))
    return pl.pallas_call(
        functools.partial(_attn_kernel, rows=rows, group=group, n_batch=n_batch, steps=steps),
        grid=(n_hp, b // n_batch),
        in_specs=[tok, tok, tok, tok, ctx_tok, ctx_tok, tok,
                  pl.BlockSpec((1,) + bias.shape[1:], lambda hp, i: (hp, 0, 0, 0))],
        out_specs=tok,
        out_shape=jax.ShapeDtypeStruct((b, s, da), BF16),
        scratch_shapes=[pltpu.VMEM((2, group, nq, nk), F32),
                        pltpu.VMEM((2, group, nq, l), F32),
                        pltpu.VMEM((2, group, nq, nk), BF16),
                        pltpu.VMEM((2, group, nq, l), BF16)],
        compiler_params=pltpu.CompilerParams(vmem_limit_bytes=VMEM_LIMIT),
        name="attn",
    )(qrot, qpl, k, v, kc, vc, ga, bias)


def _out_proj_kernel(x_ref, a_ref, c_ref, ada_ref, w_ref, o_ref, *, da):
    _, _, gate = _ada_row(ada_ref, pl.program_id(0))
    upd = (jnp.dot(a_ref[0], w_ref[0:da, :].astype(BF16), preferred_element_type=F32)
           + jnp.dot(c_ref[0], w_ref[da:, :].astype(BF16), preferred_element_type=F32))
    o_ref[0] = x_ref[0] + gate * upd


def _out_proj(x, attn, conv, ada, w_out, *, tm):
    b, s, d = x.shape
    da = attn.shape[2]
    dc = conv.shape[2]
    return pl.pallas_call(
        functools.partial(_out_proj_kernel, da=da),
        grid=(b, s // tm),
        in_specs=[pl.BlockSpec((1, tm, d), lambda i, j: (i, j, 0)),
                  pl.BlockSpec((1, tm, da), lambda i, j: (i, j, 0)),
                  pl.BlockSpec((1, tm, dc), lambda i, j: (i, j, 0)),
                  pl.BlockSpec(ada.shape, lambda i, j: (0, 0)),
                  pl.BlockSpec(w_out.shape, lambda i, j: (0, 0))],
        out_specs=pl.BlockSpec((1, tm, d), lambda i, j: (i, j, 0)),
        out_shape=jax.ShapeDtypeStruct((b, s, d), F32),
        compiler_params=pltpu.CompilerParams(vmem_limit_bytes=VMEM_LIMIT),
        name="out_proj",
    )(x, attn, conv, ada, w_out)


def _rope_tables(s):
    nf = HEAD_DIM // 4
    inv = (ROPE_THETA ** (-np.arange(nf, dtype=np.float32) / nf)).astype(np.float32)
    pos = np.arange(s)
    lane = np.arange(LANES)
    d = lane % HEAD_DIM
    axis = d // (2 * nf)
    half = (d % (2 * nf)) // nf
    coord = np.where(axis[None, :] == 0, (pos // GRID_W)[:, None], (pos % GRID_W)[:, None]).astype(np.float32)
    ang = (coord * inv[d % nf][None, :]).astype(np.float32)
    cos = np.cos(ang).astype(np.float32)
    sin = np.sin(ang).astype(np.float32)
    sin_lo = np.where(half[None, :] == 0, -sin, 0.0).astype(np.float32)
    sin_hi = np.where(half[None, :] == 1, sin, 0.0).astype(np.float32)
    return jnp.asarray(cos), jnp.asarray(sin_lo), jnp.asarray(sin_hi)


def _bias_block(rpb_ref, o_ref):
    cq = lax.broadcasted_iota(jnp.int32, (GRID_W, LANES), 0)
    ck = lax.broadcasted_iota(jnp.int32, (GRID_W, LANES), 1) % GRID_W
    col_start = jnp.clip(cq - WIN_W // 2, 0, GRID_W - WIN_W)
    valid = (ck >= col_start) & (ck < col_start + WIN_W)

    def one_pair(dr, carry):
        for a in range(2):
            two = rpb_ref[a, pl.ds(dr, 2), :]
            lanes = jnp.broadcast_to(jnp.concatenate([two[0:1], two[1:2]], axis=1), (GRID_W, LANES))
            toeplitz = pltpu.roll(lanes, LANES - (WIN_W - 1), 1, stride=1, stride_axis=0)
            o_ref[0, dr, a * GRID_W:(a + 1) * GRID_W, :] = jnp.where(valid, toeplitz * LOG2_E, MASK_VALUE)
        return carry

    lax.fori_loop(0, o_ref.shape[1], one_pair, 0, unroll=2)


def kernel(x, c, ctx, c_ctx, w_ada, b_ada, norm_g, w_in, q_norm_g, k_norm_g, rpb, conv_w, conv_b, w_out):
    depth = w_ada.shape[0]
    b, s, d = x.shape
    dc = conv_w.shape[2]
    da = (w_in.shape[2] - 4 * dc) // 4
    n_heads = da // HEAD_DIM
    rows = s // GRID_W
    assert depth == 1 and s % GRID_W == 0 and rows >= WIN_H and da % LANES == 0

    cos, slo, shi = _rope_tables(s)
    assert da % MXU_DIM == 0 and MXU_DIM % HEAD_DIM == 0
    seg = np.arange(MXU_DIM) // HEAD_DIM
    bd = jnp.asarray((seg[:, None] == seg[None, :]).astype(np.float32), dtype=BF16)

    ada, bias, w_in_b = _prep(c, c_ctx, w_ada[0], b_ada[0], rpb[0], w_in[0])

    ng = norm_g[0].reshape(1, d)
    gains = jnp.tile(jnp.stack([q_norm_g[0] * (HEAD_DIM ** -0.5 * LOG2_E), k_norm_g[0]]), (1, n_heads))

    kc, vc = _ctx_kv(ctx, ada, b, ng, w_in_b, gains, bd, tm=CTX_ROWS)
    qrot, qpl, k, v, ga, conv = _in_proj(x, ada, ng, w_in_b, gains, cos, slo, shi, bd,
                                         conv_w[0], conv_b[0].reshape(1, dc), tm=IN_PROJ_ROWS, sub=IN_PROJ_SUB)
    attn = _attention(qrot, qpl, k, v, kc, vc, ga, bias, group=ATTN_GROUP, n_batch=ATTN_BATCH, steps=ATTN_STEPS)
    return _out_proj(x, attn, conv, ada, w_out[0], tm=OUT_PROJ_ROWS)
```

```python
import functools

import numpy as np
import jax
import jax.numpy as jnp
from jax import lax
from jax.experimental import pallas as pl
from jax.experimental.pallas import tpu as pltpu

F32 = jnp.float32
BF16 = jnp.bfloat16

HEAD_DIM = 64
GRID_W = 64
WIN_H = 8
WIN_W = 16
CONV_K = 3
ROPE_THETA = 10000.0
RMS_EPS = 1e-6
MASK_VALUE = -1e30
LOG2_E = 1.4426950408889634

LANES = 128
MXU_DIM = 256
F32_ROWS = 8
BF16_ROWS = 16
HALO = BF16_ROWS
VMEM_CAPACITY = 64 * 1024 * 1024
VMEM_LIMIT = VMEM_CAPACITY * 7 // 8

IN_PROJ_ROWS, IN_PROJ_SUB = 1024, 512
OUT_PROJ_ROWS = 2048
ATTN_GROUP, ATTN_BATCH, ATTN_STEPS = 2, 4, 30


def _silu(z):
    return z * jax.nn.sigmoid(z)


def _prep_kernel(c_ref, cctx_ref, w_ref, b_ref, rpb_ref, win_ref, ada_ref, bias_ref, winb_ref):
    nb, d = c_ref.shape
    cond = jnp.concatenate([c_ref[...], jnp.broadcast_to(cctx_ref[...], (ada_ref.shape[0] - nb, d))], axis=0)
    a = _silu(cond).astype(BF16)
    ada_ref[...] = jnp.dot(a, w_ref[...].astype(BF16), preferred_element_type=F32) + b_ref[...]
    winb_ref[...] = win_ref[...].astype(BF16)
    _bias_block(rpb_ref, bias_ref)


def _prep(c, c_ctx, w_ada, b_ada, rpb, w_in):
    nb, d = c.shape
    assert nb % F32_ROWS == 0
    rows = nb + F32_ROWS
    n = w_ada.shape[1]
    nh, ndr, ndc = rpb.shape
    steps = nh // 2
    tn = n // steps
    tw = w_in.shape[1] // steps
    assert ndr == 2 * WIN_H - 1 and ndc == 2 * WIN_W - 1 and ndc <= GRID_W and tn % LANES == 0 and tw % LANES == 0
    rpb_pad = jnp.pad(rpb, ((0, 0), (0, 0), (0, GRID_W - ndc)))
    return pl.pallas_call(
        _prep_kernel,
        grid=(steps,),
        in_specs=[pl.BlockSpec((nb, d), lambda i: (0, 0)),
                  pl.BlockSpec((1, d), lambda i: (0, 0)),
                  pl.BlockSpec((d, tn), lambda i: (0, i)),
                  pl.BlockSpec((1, tn), lambda i: (0, i)),
                  pl.BlockSpec((2, ndr, GRID_W), lambda i: (i, 0, 0)),
                  pl.BlockSpec((d, tw), lambda i: (0, i))],
        out_specs=[pl.BlockSpec((rows, tn), lambda i: (0, i)),
                   pl.BlockSpec((1, ndr - 1, 2 * GRID_W, 2 * GRID_W), lambda i: (i, 0, 0, 0)),
                   pl.BlockSpec((d, tw), lambda i: (0, i))],
        out_shape=[jax.ShapeDtypeStruct((rows, n), F32),
                   jax.ShapeDtypeStruct((steps, ndr - 1, 2 * GRID_W, 2 * GRID_W), F32),
                   jax.ShapeDtypeStruct(w_in.shape, BF16)],
        compiler_params=pltpu.CompilerParams(vmem_limit_bytes=VMEM_LIMIT),
        name="prep",
    )(c, c_ctx.reshape(1, d), w_ada, b_ada.reshape(1, n), rpb_pad, w_in)


def _ada_row(ada_ref, row):
    r = ada_ref[pl.ds(row, 1), :]
    d = r.shape[1] // 3
    return r[:, 0:d], r[:, d:2 * d], r[:, 2 * d:]


def _modulated_norm(xt, mult, shift):
    ms = jnp.mean(xt * xt, axis=-1, keepdims=True)
    return (xt * lax.rsqrt(ms + RMS_EPS) * mult + shift).astype(BF16)


def _head_norm(t, gain, bd):
    t2 = (t * t).astype(BF16)
    w = bd.shape[0]
    ss = jnp.concatenate([jnp.dot(t2[:, c:c + w], bd, preferred_element_type=F32)
                          for c in range(0, t.shape[1], w)], axis=1)
    return t * lax.rsqrt(ss * (1.0 / HEAD_DIM) + RMS_EPS) * gain


def _rope(t, cos, sin_lo, sin_hi):
    outs = []
    for c in range(t.shape[1] // LANES):
        tc = t[:, c * LANES:(c + 1) * LANES]
        up = pltpu.roll(tc, LANES - HEAD_DIM // 4, 1)
        dn = pltpu.roll(tc, HEAD_DIM // 4, 1)
        outs.append(tc * cos + up * sin_lo + dn * sin_hi)
    return jnp.concatenate(outs, axis=1)


def _in_proj_kernel(x_ref, xp_ref, xn_ref, ctx_ref, ada_ref, ng_ref, w_ref, g_ref,
                    cos_ref, slo_ref, shi_ref, bd_ref, cw_ref, cb_ref,
                    qrot_ref, qpl_ref, k_ref, v_ref, ga_ref, conv_ref, kc_ref, vc_ref,
                    hext_ref, cu_ref, *, tm, sub, da, dc, ctx_row):
    j = pl.program_id(1)
    nj = pl.num_programs(1)
    shift, scale, _ = _ada_row(ada_ref, pl.program_id(0))
    mult = ng_ref[...] * (1.0 + scale)
    n_sub = tm // sub
    cr = ctx_ref.shape[1]

    def proj(src, lo, width):
        return jnp.dot(hext_ref[src, :], w_ref[:, lo:lo + width], preferred_element_type=F32)

    bd = bd_ref[...]
    hext_ref[0:HALO, :] = _modulated_norm(xp_ref[0], mult, shift)
    hext_ref[HALO + tm:2 * HALO + tm, :] = _modulated_norm(xn_ref[0], mult, shift)
    shift_c, scale_c, _ = _ada_row(ada_ref, ctx_row)
    hext_ref[2 * HALO + tm:, :] = _modulated_norm(ctx_ref[0], ng_ref[...] * (1.0 + scale_c), shift_c)

    for n in range(n_sub):
        r0 = n * sub
        hext_ref[HALO + r0:HALO + r0 + sub, :] = _modulated_norm(x_ref[0, r0:r0 + sub, :], mult, shift)
    for n in range(n_sub):
        r0 = n * sub
        rows = slice(r0, r0 + sub)
        h = slice(HALO + r0, HALO + r0 + sub)
        with_ctx = n == n_sub - 1
        h_kv = slice(HALO + r0, 2 * HALO + tm + cr) if with_ctx else h
        cos, slo, shi = cos_ref[rows, :], slo_ref[rows, :], shi_ref[rows, :]


        hx = slice(r0, r0 + sub + 2 * HALO)
        cu = proj(hx, 4 * da, dc) * proj(hx, 4 * da + 2 * dc, dc)
        row = lax.broadcasted_iota(jnp.int32, (sub + 2 * HALO, 1), 0) + r0
        inside = ((row >= HALO) | (j > 0)) & ((row < HALO + tm) | (j < nj - 1))
        cu_ref[n] = jnp.where(inside, cu, 0.0)
        y = (cb_ref[...]
             + cw_ref[0:1, :] * cu_ref[n, HALO - 1:HALO - 1 + sub, :]
             + cw_ref[1:2, :] * cu_ref[n, HALO:HALO + sub, :]
             + cw_ref[2:3, :] * cu_ref[n, HALO + 1:HALO + 1 + sub, :])
        bg = proj(h, 4 * da + dc, dc)
        zc = proj(h, 4 * da + 3 * dc, dc)
        conv_ref[0, rows, :] = (bg * y * _silu(zc)).astype(BF16)

        qn = _head_norm(proj(h, 0, da), g_ref[0:1, :], bd)
        qpl_ref[0, rows, :] = qn.astype(BF16)
        qrot_ref[0, rows, :] = _rope(qn, cos, slo, shi).astype(BF16)

        kf = proj(h_kv, da, da)
        kn = _head_norm(kf[0:sub], g_ref[1:2, :], bd)
        k_ref[0, rows, :] = _rope(kn, cos, slo, shi).astype(BF16)
        if with_ctx:
            kc_ref[0] = _head_norm(kf[sub + HALO:], g_ref[1:2, :], bd).astype(BF16)

        ga_ref[0, rows, :] = _silu(proj(h, 3 * da, da)).astype(BF16)
        vf = proj(h_kv, 2 * da, da)
        v_ref[0, rows, :] = vf[0:sub].astype(BF16)
        if with_ctx:
            vc_ref[0] = vf[sub + HALO:].astype(BF16)


def _in_proj(x, ctx, ada, ctx_row, norm_g, w_in, gains, cos, slo, shi, bd, conv_w, conv_b, *, tm, sub):
    b, s, d = x.shape
    l = ctx.shape[1]
    assert s % tm == 0 and tm % sub == 0 and sub % HALO == 0
    cr = l // (s // tm)
    assert l % (s // tm) == 0 and cr % BF16_ROWS == 0
    dc = conv_w.shape[1]
    da = (w_in.shape[1] - 4 * dc) // 4
    nh = tm // HALO
    last_halo = s // HALO - 1
    tok = lambda width: pl.BlockSpec((1, tm, width), lambda i, j: (i, j, 0))
    const = lambda shape: pl.BlockSpec(shape, lambda i, j: (0,) * len(shape))
    tab = pl.BlockSpec((tm, LANES), lambda i, j: (j, 0))
    out_tok = jax.ShapeDtypeStruct((b, s, da), BF16)
    out_ctx = jax.ShapeDtypeStruct((b, l, da), BF16)
    ctx_spec = lambda width: pl.BlockSpec((1, cr, width), lambda i, j: (i, j, 0))
    return pl.pallas_call(
        functools.partial(_in_proj_kernel, tm=tm, sub=sub, da=da, dc=dc, ctx_row=ctx_row),
        grid=(b, s // tm),
        in_specs=[tok(d),
                  pl.BlockSpec((1, HALO, d), lambda i, j: (i, jnp.maximum(j * nh - 1, 0), 0)),
                  pl.BlockSpec((1, HALO, d), lambda i, j: (i, jnp.minimum((j + 1) * nh, last_halo), 0)),
                  ctx_spec(d),
                  const(ada.shape),
                  const((1, d)), const(w_in.shape), const(gains.shape),
                  tab, tab, tab, const(bd.shape), const((CONV_K, dc)), const((1, dc))],
        out_specs=[tok(da), tok(da), tok(da), tok(da), tok(da), tok(dc), ctx_spec(da), ctx_spec(da)],
        out_shape=[out_tok, out_tok, out_tok, out_tok, out_tok,
                   jax.ShapeDtypeStruct((b, s, dc), BF16), out_ctx, out_ctx],
        scratch_shapes=[pltpu.VMEM((tm + 2 * HALO + cr, d), BF16),
                        pltpu.VMEM((tm // sub, sub + 2 * HALO, dc), F32)],
        compiler_params=pltpu.CompilerParams(vmem_limit_bytes=VMEM_LIMIT),
        name="in_proj",
    )(x, x, x, ctx, ada, norm_g, w_in, gains, cos, slo, shi, bd, conv_w, conv_b)


def _dot_nt(a, b):
    return lax.dot_general(a, b, (((1,), (1,)), ((), ())), preferred_element_type=F32)


def _attn_kernel(qrot_ref, qpl_ref, k_ref, v_ref, kc_ref, vc_ref, ga_ref, bias_ref, o_ref,
                 s_lat_ref, s_ctx_ref, p_lat_ref, p_ctx_ref, *, rows, group, n_batch, steps):
    n_units = n_batch * rows // group
    nk = WIN_H * GRID_W

    lane = lax.broadcasted_iota(jnp.int32, (GRID_W, LANES), 1)
    first_head = lane < HEAD_DIM

    def stack_heads(q2):
        zero = jnp.zeros_like(q2)
        return jnp.concatenate([jnp.where(first_head, q2, zero), jnp.where(first_head, zero, q2)], axis=0)

    def geometry(u, g):
        r = u * group + g
        if isinstance(r, int):
            bb, i = divmod(r, rows)
            rs = min(max(i - WIN_H // 2, 0), rows - WIN_H)
            return bb, i - rs, i * GRID_W, rs * GRID_W
        bb = r // rows
        i = r % rows
        rs = jnp.clip(i - WIN_H // 2, 0, rows - WIN_H)
        return bb, i - rs, pl.multiple_of(i * GRID_W, GRID_W), pl.multiple_of(rs * GRID_W, GRID_W)


    def scores_matmul(u):
        out = []
        for g in range(group):
            bb, off, tok0, key0 = geometry(u, g)
            qs = stack_heads(qrot_ref[bb, pl.ds(tok0, GRID_W), :])
            kband = k_ref[bb, pl.ds(key0, nk), :]
            qp = stack_heads(qpl_ref[bb, pl.ds(tok0, GRID_W), :])
            out.append((_dot_nt(qs, kband), off, _dot_nt(qp, kc_ref[bb])))
        return out


    def scores_store(slot, vals):
        for g, (s_lat, off, s_ctx) in enumerate(vals):
            first = (WIN_H - 1) - off
            bias = jnp.concatenate([bias_ref[0, first + 2 * t] for t in range(WIN_H // 2)], axis=1)
            s_lat_ref[slot, g] = s_lat + bias
            s_ctx_ref[slot, g] = s_ctx

    def softmax(slot):
        for g in range(group):
            s_lat = s_lat_ref[slot, g]
            s_ctx = s_ctx_ref[slot, g]
            m = jnp.maximum(jnp.max(s_lat, axis=-1, keepdims=True), jnp.max(s_ctx, axis=-1, keepdims=True))
            p_lat_ref[slot, g] = jnp.exp2(s_lat - m).astype(BF16)
            p_ctx_ref[slot, g] = jnp.exp2(s_ctx - m).astype(BF16)

    ones_lat = jnp.ones((nk, LANES), BF16)
    ones_ctx = jnp.ones((kc_ref.shape[1], LANES), BF16)

    def pv_matmul(u, slot):
        out = []
        for g in range(group):
            bb, _, _, key0 = geometry(u, g)
            v_lat = jnp.concatenate([v_ref[bb, pl.ds(key0, nk), :], ones_lat], axis=1)
            v_ctx = jnp.concatenate([vc_ref[bb], ones_ctx], axis=1)
            out.append(jnp.dot(p_lat_ref[slot, g], v_lat, preferred_element_type=F32)
                       + jnp.dot(p_ctx_ref[slot, g], v_ctx, preferred_element_type=F32))
        return out

    def pv_store(u, vals):
        for g, ol in enumerate(vals):
            bb, _, tok0, _ = geometry(u, g)
            o = ol[:, 0:LANES] * (1.0 / ol[:, LANES:])
            o2 = jnp.where(first_head, o[0:GRID_W], o[GRID_W:2 * GRID_W])
            gate = ga_ref[bb, pl.ds(tok0, GRID_W), :].astype(F32)
            o_ref[bb, pl.ds(tok0, GRID_W), :] = (o2 * gate).astype(BF16)

    scores_store(0, scores_matmul(0))
    sc = scores_matmul(1)
    softmax(0)
    scores_store(1, sc)

    def step(t, slot):
        sc = scores_matmul(t)
        pv = pv_matmul(t - 2, slot)
        softmax(1 - slot)
        scores_store(slot, sc)
        pv_store(t - 2, pv)

    n_steady, n_left = divmod(n_units - 2, steps)

    def steady(n, carry):
        for j in range(steps):
            step(steps * n + 2 + j, j % 2)
        return carry

    lax.fori_loop(0, n_steady, steady, 0)
    for j in range(n_left):
        step(steps * n_steady + 2 + j, j % 2)

    pv = pv_matmul(n_units - 2, 0)
    softmax(1)
    pv_store(n_units - 2, pv)
    pv_store(n_units - 1, pv_matmul(n_units - 1, 1))


def _attention(qrot, qpl, k, v, kc, vc, ga, bias, *, group, n_batch, steps):
    b, s, da = qrot.shape
    l = vc.shape[1]
    rows = s // GRID_W
    n_hp = da // LANES
    nq = 2 * GRID_W
    nk = WIN_H * GRID_W
    assert rows % (2 * group) == 0 and b % n_batch == 0 and steps % 2 == 0
    tok = pl.BlockSpec((n_batch, s, LANES), lambda hp, i: (i, 0, hp))
    ctx_tok = pl.BlockSpec((n_batch, l, LANES), lambda hp, i: (i, 0, hp))
    return pl.pallas_call(
        functools.partial(_attn_kernel, rows=rows, group=group, n_batch=n_batch, steps=steps),
        grid=(n_hp, b // n_batch),
        in_specs=[tok, tok, tok, tok, ctx_tok, ctx_tok, tok,
                  pl.BlockSpec((1,) + bias.shape[1:], lambda hp, i: (hp, 0, 0, 0))],
        out_specs=tok,
        out_shape=jax.ShapeDtypeStruct((b, s, da), BF16),
        scratch_shapes=[pltpu.VMEM((2, group, nq, nk), F32),
                        pltpu.VMEM((2, group, nq, l), F32),
                        pltpu.VMEM((2, group, nq, nk), BF16),
                        pltpu.VMEM((2, group, nq, l), BF16)],
        compiler_params=pltpu.CompilerParams(vmem_limit_bytes=VMEM_LIMIT),
        name="attn",
    )(qrot, qpl, k, v, kc, vc, ga, bias)


def _out_proj_kernel(x_ref, a_ref, c_ref, ada_ref, w_ref, o_ref, *, da):
    _, _, gate = _ada_row(ada_ref, pl.program_id(0))
    upd = (jnp.dot(a_ref[0], w_ref[0:da, :].astype(BF16), preferred_element_type=F32)
           + jnp.dot(c_ref[0], w_ref[da:, :].astype(BF16), preferred_element_type=F32))
    o_ref[0] = x_ref[0] + gate * upd


def _out_proj(x, attn, conv, ada, w_out, *, tm):
    b, s, d = x.shape
    da = attn.shape[2]
    dc = conv.shape[2]
    return pl.pallas_call(
        functools.partial(_out_proj_kernel, da=da),
        grid=(b, s // tm),
        in_specs=[pl.BlockSpec((1, tm, d), lambda i, j: (i, j, 0)),
                  pl.BlockSpec((1, tm, da), lambda i, j: (i, j, 0)),
                  pl.BlockSpec((1, tm, dc), lambda i, j: (i, j, 0)),
                  pl.BlockSpec(ada.shape, lambda i, j: (0, 0)),
                  pl.BlockSpec(w_out.shape, lambda i, j: (0, 0))],
        out_specs=pl.BlockSpec((1, tm, d), lambda i, j: (i, j, 0)),
        out_shape=jax.ShapeDtypeStruct((b, s, d), F32),
        compiler_params=pltpu.CompilerParams(vmem_limit_bytes=VMEM_LIMIT),
        name="out_proj",
    )(x, attn, conv, ada, w_out)


def _rope_tables(s):
    nf = HEAD_DIM // 4
    inv = (ROPE_THETA ** (-np.arange(nf, dtype=np.float32) / nf)).astype(np.float32)
    pos = np.arange(s)
    lane = np.arange(LANES)
    d = lane % HEAD_DIM
    axis = d // (2 * nf)
    half = (d % (2 * nf)) // nf
    coord = np.where(axis[None, :] == 0, (pos // GRID_W)[:, None], (pos % GRID_W)[:, None]).astype(np.float32)
    ang = (coord * inv[d % nf][None, :]).astype(np.float32)
    cos = np.cos(ang).astype(np.float32)
    sin = np.sin(ang).astype(np.float32)
    sin_lo = np.where(half[None, :] == 0, -sin, 0.0).astype(np.float32)
    sin_hi = np.where(half[None, :] == 1, sin, 0.0).astype(np.float32)
    return jnp.asarray(cos), jnp.asarray(sin_lo), jnp.asarray(sin_hi)


def _bias_block(rpb_ref, o_ref):
    cq = lax.broadcasted_iota(jnp.int32, (GRID_W, LANES), 0)
    ck = lax.broadcasted_iota(jnp.int32, (GRID_W, LANES), 1) % GRID_W
    col_start = jnp.clip(cq - WIN_W // 2, 0, GRID_W - WIN_W)
    valid = (ck >= col_start) & (ck < col_start + WIN_W)

    def one_pair(dr, carry):
        for a in range(2):
            two = rpb_ref[a, pl.ds(dr, 2), :]
            lanes = jnp.broadcast_to(jnp.concatenate([two[0:1], two[1:2]], axis=1), (GRID_W, LANES))
            toeplitz = pltpu.roll(lanes, LANES - (WIN_W - 1), 1, stride=1, stride_axis=0)
            o_ref[0, dr, a * GRID_W:(a + 1) * GRID_W, :] = jnp.where(valid, toeplitz * LOG2_E, MASK_VALUE)
        return carry

    lax.fori_loop(0, o_ref.shape[1], one_pair, 0, unroll=2)


def kernel(x, c, ctx, c_ctx, w_ada, b_ada, norm_g, w_in, q_norm_g, k_norm_g, rpb, conv_w, conv_b, w_out):
    depth = w_ada.shape[0]
    b, s, d = x.shape
    dc = conv_w.shape[2]
    da = (w_in.shape[2] - 4 * dc) // 4
    n_heads = da // HEAD_DIM
    rows = s // GRID_W
    assert depth == 1 and s % GRID_W == 0 and rows >= WIN_H and da % LANES == 0

    cos, slo, shi = _rope_tables(s)
    assert da % MXU_DIM == 0 and MXU_DIM % HEAD_DIM == 0
    seg = np.arange(MXU_DIM) // HEAD_DIM
    bd = jnp.asarray((seg[:, None] == seg[None, :]).astype(np.float32), dtype=BF16)

    ada, bias, w_in_b = _prep(c, c_ctx, w_ada[0], b_ada[0], rpb[0], w_in[0])

    ng = norm_g[0].reshape(1, d)
    gains = jnp.tile(jnp.stack([q_norm_g[0] * (HEAD_DIM ** -0.5 * LOG2_E), k_norm_g[0]]), (1, n_heads))

    qrot, qpl, k, v, ga, conv, kc, vc = _in_proj(x, ctx, ada, b, ng, w_in_b, gains, cos, slo, shi, bd,
                                                 conv_w[0], conv_b[0].reshape(1, dc),
                                                 tm=IN_PROJ_ROWS, sub=IN_PROJ_SUB)
    attn = _attention(qrot, qpl, k, v, kc, vc, ga, bias, group=ATTN_GROUP, n_batch=ATTN_BATCH, steps=ATTN_STEPS)
    return _out_proj(x, attn, conv, ada, w_out[0], tm=OUT_PROJ_ROWS)
```

```python
import functools

import numpy as np
import jax
import jax.numpy as jnp
from jax import lax
from jax.experimental import pallas as pl
from jax.experimental.pallas import tpu as pltpu

F32 = jnp.float32
BF16 = jnp.bfloat16

HEAD_DIM = 64
GRID_W = 64
WIN_H = 8
WIN_W = 16
CONV_K = 3
ROPE_THETA = 10000.0
RMS_EPS = 1e-6
MASK_VALUE = -1e30
LOG2_E = 1.4426950408889634

LANES = 128
MXU_DIM = 256
F32_ROWS = 8
BF16_ROWS = 16
HALO = BF16_ROWS
VMEM_CAPACITY = 64 * 1024 * 1024
VMEM_LIMIT = VMEM_CAPACITY * 7 // 8

IN_PROJ_ROWS, IN_PROJ_SUB = 1024, 512
OUT_PROJ_ROWS = 2048
ATTN_GROUP, ATTN_BATCH, ATTN_STEPS = 2, 4, 30


def _silu(z):
    return z * jax.nn.sigmoid(z)


def _prep_kernel(c_ref, cctx_ref, w_ref, b_ref, rpb_ref, win_ref, qg_ref, kg_ref,
                 ada_ref, bias_ref, winb_ref, gains_ref):
    nb, d = c_ref.shape
    cond = jnp.concatenate([c_ref[...], jnp.broadcast_to(cctx_ref[...], (ada_ref.shape[0] - nb, d))], axis=0)
    a = _silu(cond).astype(BF16)
    ada_ref[...] = jnp.dot(a, w_ref[...].astype(BF16), preferred_element_type=F32) + b_ref[...]
    winb_ref[...] = win_ref[...].astype(BF16)
    _bias_block(rpb_ref, bias_ref)
    g = jnp.concatenate([qg_ref[...] * (HEAD_DIM ** -0.5 * LOG2_E), kg_ref[...]], axis=0)
    gains_ref[...] = jnp.concatenate([g] * (gains_ref.shape[1] // HEAD_DIM), axis=1)


def _prep(c, c_ctx, w_ada, b_ada, rpb, w_in, q_gain, k_gain, d_attn):
    nb, d = c.shape
    assert nb % F32_ROWS == 0
    rows = nb + F32_ROWS
    n = w_ada.shape[1]
    nh, ndr, ndc = rpb.shape
    steps = nh // 2
    tn = n // steps
    tw = w_in.shape[1] // steps
    assert ndr == 2 * WIN_H - 1 and ndc == 2 * WIN_W - 1 and ndc <= GRID_W and tn % LANES == 0 and tw % LANES == 0
    rpb_pad = jnp.pad(rpb, ((0, 0), (0, 0), (0, GRID_W - ndc)))
    return pl.pallas_call(
        _prep_kernel,
        grid=(steps,),
        in_specs=[pl.BlockSpec((nb, d), lambda i: (0, 0)),
                  pl.BlockSpec((1, d), lambda i: (0, 0)),
                  pl.BlockSpec((d, tn), lambda i: (0, i)),
                  pl.BlockSpec((1, tn), lambda i: (0, i)),
                  pl.BlockSpec((2, ndr, GRID_W), lambda i: (i, 0, 0)),
                  pl.BlockSpec((d, tw), lambda i: (0, i)),
                  pl.BlockSpec((1, HEAD_DIM), lambda i: (0, 0)),
                  pl.BlockSpec((1, HEAD_DIM), lambda i: (0, 0))],
        out_specs=[pl.BlockSpec((rows, tn), lambda i: (0, i)),
                   pl.BlockSpec((1, ndr - 1, 2 * GRID_W, 2 * GRID_W), lambda i: (i, 0, 0, 0)),
                   pl.BlockSpec((d, tw), lambda i: (0, i)),
                   pl.BlockSpec((2, d_attn), lambda i: (0, 0))],
        out_shape=[jax.ShapeDtypeStruct((rows, n), F32),
                   jax.ShapeDtypeStruct((steps, ndr - 1, 2 * GRID_W, 2 * GRID_W), F32),
                   jax.ShapeDtypeStruct(w_in.shape, BF16),
                   jax.ShapeDtypeStruct((2, d_attn), F32)],
        compiler_params=pltpu.CompilerParams(vmem_limit_bytes=VMEM_LIMIT),
        name="prep",
    )(c, c_ctx.reshape(1, d), w_ada, b_ada, rpb_pad, w_in, q_gain, k_gain)


def _ada_row(ada_ref, row):
    r = ada_ref[pl.ds(row, 1), :]
    d = r.shape[1] // 3
    return r[:, 0:d], r[:, d:2 * d], r[:, 2 * d:]


def _modulated_norm(xt, mult, shift):
    ms = jnp.mean(xt * xt, axis=-1, keepdims=True)
    return (xt * lax.rsqrt(ms + RMS_EPS) * mult + shift).astype(BF16)


def _head_norm(t, gain, bd):
    t2 = (t * t).astype(BF16)
    w = bd.shape[0]
    ss = jnp.concatenate([jnp.dot(t2[:, c:c + w], bd, preferred_element_type=F32)
                          for c in range(0, t.shape[1], w)], axis=1)
    return t * lax.rsqrt(ss * (1.0 / HEAD_DIM) + RMS_EPS) * gain


def _rope(t, cos, sin_lo, sin_hi):
    outs = []
    for c in range(t.shape[1] // LANES):
        tc = t[:, c * LANES:(c + 1) * LANES]
        up = pltpu.roll(tc, LANES - HEAD_DIM // 4, 1)
        dn = pltpu.roll(tc, HEAD_DIM // 4, 1)
        outs.append(tc * cos + up * sin_lo + dn * sin_hi)
    return jnp.concatenate(outs, axis=1)


def _in_proj_kernel(x_ref, xp_ref, xn_ref, ctx_ref, ada_ref, ng_ref, w_ref, g_ref,
                    cos_ref, slo_ref, shi_ref, bd_ref, cw_ref, cb_ref,
                    qrot_ref, qpl_ref, k_ref, v_ref, ga_ref, conv_ref, kc_ref, vc_ref,
                    hext_ref, cu_ref, *, tm, sub, da, dc, ctx_row):
    j = pl.program_id(1)
    nj = pl.num_programs(1)
    shift, scale, _ = _ada_row(ada_ref, pl.program_id(0))
    mult = ng_ref[...] * (1.0 + scale)
    n_sub = tm // sub
    cr = ctx_ref.shape[1]

    def proj(src, lo, width):
        return jnp.dot(hext_ref[src, :], w_ref[:, lo:lo + width], preferred_element_type=F32)

    bd = bd_ref[...]
    hext_ref[0:HALO, :] = _modulated_norm(xp_ref[0], mult, shift)
    hext_ref[HALO + tm:2 * HALO + tm, :] = _modulated_norm(xn_ref[0], mult, shift)
    shift_c, scale_c, _ = _ada_row(ada_ref, ctx_row)
    hext_ref[2 * HALO + tm:, :] = _modulated_norm(ctx_ref[0], ng_ref[...] * (1.0 + scale_c), shift_c)

    for n in range(n_sub):
        r0 = n * sub
        hext_ref[HALO + r0:HALO + r0 + sub, :] = _modulated_norm(x_ref[0, r0:r0 + sub, :], mult, shift)
    for n in range(n_sub):
        r0 = n * sub
        rows = slice(r0, r0 + sub)
        h = slice(HALO + r0, HALO + r0 + sub)
        with_ctx = n == n_sub - 1
        h_kv = slice(HALO + r0, 2 * HALO + tm + cr) if with_ctx else h
        cos, slo, shi = cos_ref[rows, :], slo_ref[rows, :], shi_ref[rows, :]


        hx = slice(r0, r0 + sub + 2 * HALO)
        cu = proj(hx, 4 * da, dc) * proj(hx, 4 * da + 2 * dc, dc)
        row = lax.broadcasted_iota(jnp.int32, (sub + 2 * HALO, 1), 0) + r0
        inside = ((row >= HALO) | (j > 0)) & ((row < HALO + tm) | (j < nj - 1))
        cu_ref[n] = jnp.where(inside, cu, 0.0)
        y = (cb_ref[...]
             + cw_ref[0:1, :] * cu_ref[n, HALO - 1:HALO - 1 + sub, :]
             + cw_ref[1:2, :] * cu_ref[n, HALO:HALO + sub, :]
             + cw_ref[2:3, :] * cu_ref[n, HALO + 1:HALO + 1 + sub, :])
        bg = proj(h, 4 * da + dc, dc)
        zc = proj(h, 4 * da + 3 * dc, dc)
        conv_ref[0, rows, :] = (bg * y * _silu(zc)).astype(BF16)

        qn = _head_norm(proj(h, 0, da), g_ref[0:1, :], bd)
        qpl_ref[0, rows, :] = qn.astype(BF16)
        qrot_ref[0, rows, :] = _rope(qn, cos, slo, shi).astype(BF16)

        kf = proj(h_kv, da, da)
        kn = _head_norm(kf[0:sub], g_ref[1:2, :], bd)
        k_ref[0, rows, :] = _rope(kn, cos, slo, shi).astype(BF16)
        if with_ctx:
            kc_ref[0] = _head_norm(kf[sub + HALO:], g_ref[1:2, :], bd).astype(BF16)

        ga_ref[0, rows, :] = _silu(proj(h, 3 * da, da)).astype(BF16)
        vf = proj(h_kv, 2 * da, da)
        v_ref[0, rows, :] = vf[0:sub].astype(BF16)
        if with_ctx:
            vc_ref[0] = vf[sub + HALO:].astype(BF16)


def _in_proj(x, ctx, ada, ctx_row, norm_g, w_in, gains, cos, slo, shi, bd, conv_w, conv_b, *, tm, sub):
    b, s, d = x.shape
    l = ctx.shape[1]
    assert s % tm == 0 and tm % sub == 0 and sub % HALO == 0
    cr = l // (s // tm)
    assert l % (s // tm) == 0 and cr % BF16_ROWS == 0
    dc = conv_w.shape[1]
    da = (w_in.shape[1] - 4 * dc) // 4
    nh = tm // HALO
    last_halo = s // HALO - 1
    tok = lambda width: pl.BlockSpec((1, tm, width), lambda i, j: (i, j, 0))
    const = lambda shape: pl.BlockSpec(shape, lambda i, j: (0,) * len(shape))
    tab = pl.BlockSpec((tm, LANES), lambda i, j: (j, 0))
    out_tok = jax.ShapeDtypeStruct((b, s, da), BF16)
    out_ctx = jax.ShapeDtypeStruct((b, l, da), BF16)
    ctx_spec = lambda width: pl.BlockSpec((1, cr, width), lambda i, j: (i, j, 0))
    return pl.pallas_call(
        functools.partial(_in_proj_kernel, tm=tm, sub=sub, da=da, dc=dc, ctx_row=ctx_row),
        grid=(b, s // tm),
        in_specs=[tok(d),
                  pl.BlockSpec((1, HALO, d), lambda i, j: (i, jnp.maximum(j * nh - 1, 0), 0)),
                  pl.BlockSpec((1, HALO, d), lambda i, j: (i, jnp.minimum((j + 1) * nh, last_halo), 0)),
                  ctx_spec(d),
                  const(ada.shape),
                  const((1, d)), const(w_in.shape), const(gains.shape),
                  tab, tab, tab, const(bd.shape), const((CONV_K, dc)), const((1, dc))],
        out_specs=[tok(da), tok(da), tok(da), tok(da), tok(da), tok(dc), ctx_spec(da), ctx_spec(da)],
        out_shape=[out_tok, out_tok, out_tok, out_tok, out_tok,
                   jax.ShapeDtypeStruct((b, s, dc), BF16), out_ctx, out_ctx],
        scratch_shapes=[pltpu.VMEM((tm + 2 * HALO + cr, d), BF16),
                        pltpu.VMEM((tm // sub, sub + 2 * HALO, dc), F32)],
        compiler_params=pltpu.CompilerParams(vmem_limit_bytes=VMEM_LIMIT),
        name="in_proj",
    )(x, x, x, ctx, ada, norm_g, w_in, gains, cos, slo, shi, bd, conv_w, conv_b)


def _dot_nt(a, b):
    return lax.dot_general(a, b, (((1,), (1,)), ((), ())), preferred_element_type=F32)


def _attn_kernel(qrot_ref, qpl_ref, k_ref, v_ref, kc_ref, vc_ref, ga_ref, bias_ref, o_ref,
                 s_lat_ref, s_ctx_ref, p_lat_ref, p_ctx_ref, *, rows, group, n_batch, steps):
    n_units = n_batch * rows // group
    nk = WIN_H * GRID_W

    lane = lax.broadcasted_iota(jnp.int32, (GRID_W, LANES), 1)
    first_head = lane < HEAD_DIM

    def stack_heads(q2):
        zero = jnp.zeros_like(q2)
        return jnp.concatenate([jnp.where(first_head, q2, zero), jnp.where(first_head, zero, q2)], axis=0)

    def geometry(u, g):
        r = u * group + g
        if isinstance(r, int):
            bb, i = divmod(r, rows)
            rs = min(max(i - WIN_H // 2, 0), rows - WIN_H)
            return bb, i - rs, i * GRID_W, rs * GRID_W
        bb = r // rows
        i = r % rows
        rs = jnp.clip(i - WIN_H // 2, 0, rows - WIN_H)
        return bb, i - rs, pl.multiple_of(i * GRID_W, GRID_W), pl.multiple_of(rs * GRID_W, GRID_W)


    def scores_matmul(u):
        out = []
        for g in range(group):
            bb, off, tok0, key0 = geometry(u, g)
            qs = stack_heads(qrot_ref[bb, pl.ds(tok0, GRID_W), :])
            kband = k_ref[bb, pl.ds(key0, nk), :]
            qp = stack_heads(qpl_ref[bb, pl.ds(tok0, GRID_W), :])
            out.append((_dot_nt(qs, kband), off, _dot_nt(qp, kc_ref[bb])))
        return out


    def scores_store(slot, vals):
        for g, (s_lat, off, s_ctx) in enumerate(vals):
            first = (WIN_H - 1) - off
            bias = jnp.concatenate([bias_ref[0, first + 2 * t] for t in range(WIN_H // 2)], axis=1)
            s_lat_ref[slot, g] = s_lat + bias
            s_ctx_ref[slot, g] = s_ctx

    def softmax(slot):
        for g in range(group):
            s_lat = s_lat_ref[slot, g]
            s_ctx = s_ctx_ref[slot, g]
            m = jnp.maximum(jnp.max(s_lat, axis=-1, keepdims=True), jnp.max(s_ctx, axis=-1, keepdims=True))
            p_lat_ref[slot, g] = jnp.exp2(s_lat - m).astype(BF16)
            p_ctx_ref[slot, g] = jnp.exp2(s_ctx - m).astype(BF16)

    ones_lat = jnp.ones((nk, LANES), BF16)
    ones_ctx = jnp.ones((kc_ref.shape[1], LANES), BF16)

    def pv_matmul(u, slot):
        out = []
        for g in range(group):
            bb, _, _, key0 = geometry(u, g)
            v_lat = jnp.concatenate([v_ref[bb, pl.ds(key0, nk), :], ones_lat], axis=1)
            v_ctx = jnp.concatenate([vc_ref[bb], ones_ctx], axis=1)
            out.append(jnp.dot(p_lat_ref[slot, g], v_lat, preferred_element_type=F32)
                       + jnp.dot(p_ctx_ref[slot, g], v_ctx, preferred_element_type=F32))
        return out

    def pv_store(u, vals):
        for g, ol in enumerate(vals):
            bb, _, tok0, _ = geometry(u, g)
            o = ol[:, 0:LANES] * (1.0 / ol[:, LANES:])
            o2 = jnp.where(first_head, o[0:GRID_W], o[GRID_W:2 * GRID_W])
            gate = ga_ref[bb, pl.ds(tok0, GRID_W), :].astype(F32)
            o_ref[bb, pl.ds(tok0, GRID_W), :] = (o2 * gate).astype(BF16)

    scores_store(0, scores_matmul(0))
    sc = scores_matmul(1)
    softmax(0)
    scores_store(1, sc)

    def step(t, slot):
        sc = scores_matmul(t)
        pv = pv_matmul(t - 2, slot)
        softmax(1 - slot)
        scores_store(slot, sc)
        pv_store(t - 2, pv)

    n_steady, n_left = divmod(n_units - 2, steps)

    def steady(n, carry):
        for j in range(steps):
            step(steps * n + 2 + j, j % 2)
        return carry

    lax.fori_loop(0, n_steady, steady, 0)
    for j in range(n_left):
        step(steps * n_steady + 2 + j, j % 2)

    pv = pv_matmul(n_units - 2, 0)
    softmax(1)
    pv_store(n_units - 2, pv)
    pv_store(n_units - 1, pv_matmul(n_units - 1, 1))


def _attention(qrot, qpl, k, v, kc, vc, ga, bias, *, group, n_batch, steps):
    b, s, da = qrot.shape
    l = vc.shape[1]
    rows = s // GRID_W
    n_hp = da // LANES
    nq = 2 * GRID_W
    nk = WIN_H * GRID_W
    assert rows % (2 * group) == 0 and b % n_batch == 0 and steps % 2 == 0
    tok = pl.BlockSpec((n_batch, s, LANES), lambda hp, i: (i, 0, hp))
    ctx_tok = pl.BlockSpec((n_batch, l, LANES), lambda hp, i: (i, 0, hp))
    return pl.pallas_call(
        functools.partial(_attn_kernel, rows=rows, group=group, n_batch=n_batch, steps=steps),
        grid=(n_hp, b // n_batch),
        in_specs=[tok, tok, tok, tok, ctx_tok, ctx_tok, tok,
                  pl.BlockSpec((1,) + bias.shape[1:], lambda hp, i: (hp, 0, 0, 0))],
        out_specs=tok,
        out_shape=jax.ShapeDtypeStruct((b, s, da), BF16),
        scratch_shapes=[pltpu.VMEM((2, group, nq, nk), F32),
                        pltpu.VMEM((2, group, nq, l), F32),
                        pltpu.VMEM((2, group, nq, nk), BF16),
                        pltpu.VMEM((2, group, nq, l), BF16)],
        compiler_params=pltpu.CompilerParams(vmem_limit_bytes=VMEM_LIMIT),
        name="attn",
    )(qrot, qpl, k, v, kc, vc, ga, bias)


def _out_proj_kernel(x_ref, a_ref, c_ref, ada_ref, w_ref, o_ref, *, da):
    _, _, gate = _ada_row(ada_ref, pl.program_id(0))
    upd = (jnp.dot(a_ref[0], w_ref[0:da, :].astype(BF16), preferred_element_type=F32)
           + jnp.dot(c_ref[0], w_ref[da:, :].astype(BF16), preferred_element_type=F32))
    o_ref[0] = x_ref[0] + gate * upd


def _out_proj(x, attn, conv, ada, w_out, *, tm):
    b, s, d = x.shape
    da = attn.shape[2]
    dc = conv.shape[2]
    return pl.pallas_call(
        functools.partial(_out_proj_kernel, da=da),
        grid=(b, s // tm),
        in_specs=[pl.BlockSpec((1, tm, d), lambda i, j: (i, j, 0)),
                  pl.BlockSpec((1, tm, da), lambda i, j: (i, j, 0)),
                  pl.BlockSpec((1, tm, dc), lambda i, j: (i, j, 0)),
                  pl.BlockSpec(ada.shape, lambda i, j: (0, 0)),
                  pl.BlockSpec(w_out.shape, lambda i, j: (0, 0))],
        out_specs=pl.BlockSpec((1, tm, d), lambda i, j: (i, j, 0)),
        out_shape=jax.ShapeDtypeStruct((b, s, d), F32),
        compiler_params=pltpu.CompilerParams(vmem_limit_bytes=VMEM_LIMIT),
        name="out_proj",
    )(x, attn, conv, ada, w_out)


def _rope_tables(s):
    nf = HEAD_DIM // 4
    inv = (ROPE_THETA ** (-np.arange(nf, dtype=np.float32) / nf)).astype(np.float32)
    pos = np.arange(s)
    lane = np.arange(LANES)
    d = lane % HEAD_DIM
    axis = d // (2 * nf)
    half = (d % (2 * nf)) // nf
    coord = np.where(axis[None, :] == 0, (pos // GRID_W)[:, None], (pos % GRID_W)[:, None]).astype(np.float32)
    ang = (coord * inv[d % nf][None, :]).astype(np.float32)
    cos = np.cos(ang).astype(np.float32)
    sin = np.sin(ang).astype(np.float32)
    sin_lo = np.where(half[None, :] == 0, -sin, 0.0).astype(np.float32)
    sin_hi = np.where(half[None, :] == 1, sin, 0.0).astype(np.float32)
    return jnp.asarray(cos), jnp.asarray(sin_lo), jnp.asarray(sin_hi)


def _bias_block(rpb_ref, o_ref):
    cq = lax.broadcasted_iota(jnp.int32, (GRID_W, LANES), 0)
    ck = lax.broadcasted_iota(jnp.int32, (GRID_W, LANES), 1) % GRID_W
    col_start = jnp.clip(cq - WIN_W // 2, 0, GRID_W - WIN_W)
    valid = (ck >= col_start) & (ck < col_start + WIN_W)

    def one_pair(dr, carry):
        for a in range(2):
            two = rpb_ref[a, pl.ds(dr, 2), :]
            lanes = jnp.broadcast_to(jnp.concatenate([two[0:1], two[1:2]], axis=1), (GRID_W, LANES))
            toeplitz = pltpu.roll(lanes, LANES - (WIN_W - 1), 1, stride=1, stride_axis=0)
            o_ref[0, dr, a * GRID_W:(a + 1) * GRID_W, :] = jnp.where(valid, toeplitz * LOG2_E, MASK_VALUE)
        return carry

    lax.fori_loop(0, o_ref.shape[1], one_pair, 0, unroll=2)


def kernel(x, c, ctx, c_ctx, w_ada, b_ada, norm_g, w_in, q_norm_g, k_norm_g, rpb, conv_w, conv_b, w_out):
    depth = w_ada.shape[0]
    b, s, d = x.shape
    dc = conv_w.shape[2]
    da = (w_in.shape[2] - 4 * dc) // 4
    rows = s // GRID_W
    assert depth == 1 and s % GRID_W == 0 and rows >= WIN_H and da % LANES == 0

    cos, slo, shi = _rope_tables(s)
    assert da % MXU_DIM == 0 and MXU_DIM % HEAD_DIM == 0
    seg = np.arange(MXU_DIM) // HEAD_DIM
    bd = jnp.asarray((seg[:, None] == seg[None, :]).astype(np.float32), dtype=BF16)

    ada, bias, w_in_b, gains = _prep(c, c_ctx, w_ada[0], b_ada, rpb[0], w_in[0], q_norm_g, k_norm_g, da)

    qrot, qpl, k, v, ga, conv, kc, vc = _in_proj(x, ctx, ada, b, norm_g, w_in_b, gains, cos, slo, shi, bd,
                                                 conv_w[0], conv_b, tm=IN_PROJ_ROWS, sub=IN_PROJ_SUB)
    attn = _attention(qrot, qpl, k, v, kc, vc, ga, bias, group=ATTN_GROUP, n_batch=ATTN_BATCH, steps=ATTN_STEPS)
    return _out_proj(x, attn, conv, ada, w_out[0], tm=OUT_PROJ_ROWS)
```

```python
import functools

import numpy as np
import jax
import jax.numpy as jnp
from jax import lax
from jax.experimental import pallas as pl
from jax.experimental.pallas import tpu as pltpu

F32 = jnp.float32
BF16 = jnp.bfloat16

HEAD_DIM = 64
GRID_W = 64
WIN_H = 8
WIN_W = 16
CONV_K = 3
ROPE_THETA = 10000.0
RMS_EPS = 1e-6
MASK_VALUE = -1e30
LOG2_E = 1.4426950408889634

LANES = 128
MXU_DIM = 256
F32_ROWS = 8
BF16_ROWS = 16
HALO = BF16_ROWS
VMEM_CAPACITY = 64 * 1024 * 1024
VMEM_LIMIT = VMEM_CAPACITY * 7 // 8

IN_PROJ_ROWS, IN_PROJ_SUB = 1024, 512
OUT_PROJ_ROWS = 2048
ATTN_GROUP, ATTN_BATCH, ATTN_STEPS = 2, 4, 30


def _silu(z):
    return z * jax.nn.sigmoid(z)


def _prep_kernel(c_ref, cctx_ref, w_ref, b_ref, rpb_ref, win_ref, qg_ref, kg_ref,
                 ada_ref, bias_ref, winb_ref, gains_ref):
    nb, d = c_ref.shape
    cctx = cctx_ref[...].reshape(1, d)
    cond = jnp.concatenate([c_ref[...], jnp.broadcast_to(cctx, (ada_ref.shape[0] - nb, d))], axis=0)
    a = _silu(cond).astype(BF16)
    ada_ref[...] = jnp.dot(a, w_ref[...].astype(BF16), preferred_element_type=F32) + b_ref[...]
    winb_ref[...] = win_ref[...].astype(BF16)
    _bias_block(rpb_ref, bias_ref)
    g = jnp.concatenate([qg_ref[...] * (HEAD_DIM ** -0.5 * LOG2_E), kg_ref[...]], axis=0)
    gains_ref[...] = jnp.concatenate([g] * (gains_ref.shape[1] // HEAD_DIM), axis=1)


def _prep(c, c_ctx, w_ada, b_ada, rpb, w_in, q_gain, k_gain, d_attn):
    nb, d = c.shape
    assert nb % F32_ROWS == 0
    rows = nb + F32_ROWS
    n = w_ada.shape[1]
    nh, ndr, ndc = rpb.shape
    steps = nh // 2
    tn = n // steps
    tw = w_in.shape[1] // steps
    assert ndr == 2 * WIN_H - 1 and ndc == 2 * WIN_W - 1 and ndc <= GRID_W and tn % LANES == 0 and tw % LANES == 0
    rpb_pad = jnp.pad(rpb, ((0, 0), (0, 0), (0, GRID_W - ndc)))
    return pl.pallas_call(
        _prep_kernel,
        grid=(steps,),
        in_specs=[pl.BlockSpec((nb, d), lambda i: (0, 0)),
                  pl.BlockSpec((d,), lambda i: (0,)),
                  pl.BlockSpec((d, tn), lambda i: (0, i)),
                  pl.BlockSpec((1, tn), lambda i: (0, i)),
                  pl.BlockSpec((2, ndr, GRID_W), lambda i: (i, 0, 0)),
                  pl.BlockSpec((d, tw), lambda i: (0, i)),
                  pl.BlockSpec((1, HEAD_DIM), lambda i: (0, 0)),
                  pl.BlockSpec((1, HEAD_DIM), lambda i: (0, 0))],
        out_specs=[pl.BlockSpec((rows, tn), lambda i: (0, i)),
                   pl.BlockSpec((1, ndr - 1, 2 * GRID_W, 2 * GRID_W), lambda i: (i, 0, 0, 0)),
                   pl.BlockSpec((d, tw), lambda i: (0, i)),
                   pl.BlockSpec((2, d_attn), lambda i: (0, 0))],
        out_shape=[jax.ShapeDtypeStruct((rows, n), F32),
                   jax.ShapeDtypeStruct((steps, ndr - 1, 2 * GRID_W, 2 * GRID_W), F32),
                   jax.ShapeDtypeStruct(w_in.shape, BF16),
                   jax.ShapeDtypeStruct((2, d_attn), F32)],
        compiler_params=pltpu.CompilerParams(vmem_limit_bytes=VMEM_LIMIT),
        name="prep",
    )(c, c_ctx, w_ada, b_ada, rpb_pad, w_in, q_gain, k_gain)


def _ada_row(ada_ref, row):
    r = ada_ref[pl.ds(row, 1), :]
    d = r.shape[1] // 3
    return r[:, 0:d], r[:, d:2 * d], r[:, 2 * d:]


def _modulated_norm(xt, mult, shift):
    ms = jnp.mean(xt * xt, axis=-1, keepdims=True)
    return (xt * lax.rsqrt(ms + RMS_EPS) * mult + shift).astype(BF16)


def _head_norm(t, gain, bd):
    t2 = (t * t).astype(BF16)
    w = bd.shape[0]
    ss = jnp.concatenate([jnp.dot(t2[:, c:c + w], bd, preferred_element_type=F32)
                          for c in range(0, t.shape[1], w)], axis=1)
    return t * lax.rsqrt(ss * (1.0 / HEAD_DIM) + RMS_EPS) * gain


def _rope(t, cos, sin_lo, sin_hi):
    outs = []
    for c in range(t.shape[1] // LANES):
        tc = t[:, c * LANES:(c + 1) * LANES]
        up = pltpu.roll(tc, LANES - HEAD_DIM // 4, 1)
        dn = pltpu.roll(tc, HEAD_DIM // 4, 1)
        outs.append(tc * cos + up * sin_lo + dn * sin_hi)
    return jnp.concatenate(outs, axis=1)


def _in_proj_kernel(x_ref, xp_ref, xn_ref, ctx_ref, ada_ref, ng_ref, w_ref, g_ref,
                    cos_ref, slo_ref, shi_ref, bd_ref, cw_ref, cb_ref,
                    qrot_ref, qpl_ref, k_ref, v_ref, ga_ref, conv_ref, kc_ref, vc_ref,
                    hext_ref, cu_ref, *, tm, sub, da, dc, ctx_row):
    j = pl.program_id(1)
    nj = pl.num_programs(1)
    shift, scale, _ = _ada_row(ada_ref, pl.program_id(0))
    mult = ng_ref[...] * (1.0 + scale)
    n_sub = tm // sub
    cr = ctx_ref.shape[1]

    def proj(src, lo, width):
        return jnp.dot(hext_ref[src, :], w_ref[:, lo:lo + width], preferred_element_type=F32)

    bd = bd_ref[...]
    hext_ref[0:HALO, :] = _modulated_norm(xp_ref[0], mult, shift)
    hext_ref[HALO + tm:2 * HALO + tm, :] = _modulated_norm(xn_ref[0], mult, shift)
    shift_c, scale_c, _ = _ada_row(ada_ref, ctx_row)
    hext_ref[2 * HALO + tm:, :] = _modulated_norm(ctx_ref[0], ng_ref[...] * (1.0 + scale_c), shift_c)

    for n in range(n_sub):
        r0 = n * sub
        hext_ref[HALO + r0:HALO + r0 + sub, :] = _modulated_norm(x_ref[0, r0:r0 + sub, :], mult, shift)
    for n in range(n_sub):
        r0 = n * sub
        rows = slice(r0, r0 + sub)
        h = slice(HALO + r0, HALO + r0 + sub)
        with_ctx = n == n_sub - 1
        h_kv = slice(HALO + r0, 2 * HALO + tm + cr) if with_ctx else h
        cos, slo, shi = cos_ref[rows, :], slo_ref[rows, :], shi_ref[rows, :]


        hx = slice(r0, r0 + sub + 2 * HALO)
        cu = proj(hx, 4 * da, dc) * proj(hx, 4 * da + 2 * dc, dc)
        row = lax.broadcasted_iota(jnp.int32, (sub + 2 * HALO, 1), 0) + r0
        inside = ((row >= HALO) | (j > 0)) & ((row < HALO + tm) | (j < nj - 1))
        cu_ref[n] = jnp.where(inside, cu, 0.0)
        y = (cb_ref[...]
             + cw_ref[0, 0:1, :] * cu_ref[n, HALO - 1:HALO - 1 + sub, :]
             + cw_ref[0, 1:2, :] * cu_ref[n, HALO:HALO + sub, :]
             + cw_ref[0, 2:3, :] * cu_ref[n, HALO + 1:HALO + 1 + sub, :])
        bg = proj(h, 4 * da + dc, dc)
        zc = proj(h, 4 * da + 3 * dc, dc)
        conv_ref[0, rows, :] = (bg * y * _silu(zc)).astype(BF16)

        qn = _head_norm(proj(h, 0, da), g_ref[0:1, :], bd)
        qpl_ref[0, rows, :] = qn.astype(BF16)
        qrot_ref[0, rows, :] = _rope(qn, cos, slo, shi).astype(BF16)

        kf = proj(h_kv, da, da)
        kn = _head_norm(kf[0:sub], g_ref[1:2, :], bd)
        k_ref[0, rows, :] = _rope(kn, cos, slo, shi).astype(BF16)
        if with_ctx:
            kc_ref[0] = _head_norm(kf[sub + HALO:], g_ref[1:2, :], bd).astype(BF16)

        ga_ref[0, rows, :] = _silu(proj(h, 3 * da, da)).astype(BF16)
        vf = proj(h_kv, 2 * da, da)
        v_ref[0, rows, :] = vf[0:sub].astype(BF16)
        if with_ctx:
            vc_ref[0] = vf[sub + HALO:].astype(BF16)


def _in_proj(x, ctx, ada, ctx_row, norm_g, w_in, gains, cos, slo, shi, bd, conv_w, conv_b, *, tm, sub):
    b, s, d = x.shape
    l = ctx.shape[1]
    assert s % tm == 0 and tm % sub == 0 and sub % HALO == 0
    cr = l // (s // tm)
    assert l % (s // tm) == 0 and cr % BF16_ROWS == 0
    dc = conv_w.shape[2]
    da = (w_in.shape[1] - 4 * dc) // 4
    nh = tm // HALO
    last_halo = s // HALO - 1
    tok = lambda width: pl.BlockSpec((1, tm, width), lambda i, j: (i, j, 0))
    const = lambda shape: pl.BlockSpec(shape, lambda i, j: (0,) * len(shape))
    tab = pl.BlockSpec((tm, LANES), lambda i, j: (j, 0))
    out_tok = jax.ShapeDtypeStruct((b, s, da), BF16)
    out_ctx = jax.ShapeDtypeStruct((b, l, da), BF16)
    ctx_spec = lambda width: pl.BlockSpec((1, cr, width), lambda i, j: (i, j, 0))
    return pl.pallas_call(
        functools.partial(_in_proj_kernel, tm=tm, sub=sub, da=da, dc=dc, ctx_row=ctx_row),
        grid=(b, s // tm),
        in_specs=[tok(d),
                  pl.BlockSpec((1, HALO, d), lambda i, j: (i, jnp.maximum(j * nh - 1, 0), 0)),
                  pl.BlockSpec((1, HALO, d), lambda i, j: (i, jnp.minimum((j + 1) * nh, last_halo), 0)),
                  ctx_spec(d),
                  const(ada.shape),
                  const((1, d)), const(w_in.shape), const(gains.shape),
                  tab, tab, tab, const(bd.shape), const((1, CONV_K, dc)), const((1, dc))],
        out_specs=[tok(da), tok(da), tok(da), tok(da), tok(da), tok(dc), ctx_spec(da), ctx_spec(da)],
        out_shape=[out_tok, out_tok, out_tok, out_tok, out_tok,
                   jax.ShapeDtypeStruct((b, s, dc), BF16), out_ctx, out_ctx],
        scratch_shapes=[pltpu.VMEM((tm + 2 * HALO + cr, d), BF16),
                        pltpu.VMEM((tm // sub, sub + 2 * HALO, dc), F32)],
        compiler_params=pltpu.CompilerParams(vmem_limit_bytes=VMEM_LIMIT),
        name="in_proj",
    )(x, x, x, ctx, ada, norm_g, w_in, gains, cos, slo, shi, bd, conv_w, conv_b)


def _dot_nt(a, b):
    return lax.dot_general(a, b, (((1,), (1,)), ((), ())), preferred_element_type=F32)


def _attn_kernel(qrot_ref, qpl_ref, k_ref, v_ref, kc_ref, vc_ref, ga_ref, bias_ref, o_ref,
                 s_lat_ref, s_ctx_ref, p_lat_ref, p_ctx_ref, *, rows, group, n_batch, steps):
    n_units = n_batch * rows // group
    nk = WIN_H * GRID_W

    lane = lax.broadcasted_iota(jnp.int32, (GRID_W, LANES), 1)
    first_head = lane < HEAD_DIM

    def stack_heads(q2):
        zero = jnp.zeros_like(q2)
        return jnp.concatenate([jnp.where(first_head, q2, zero), jnp.where(first_head, zero, q2)], axis=0)

    def geometry(u, g):
        r = u * group + g
        if isinstance(r, int):
            bb, i = divmod(r, rows)
            rs = min(max(i - WIN_H // 2, 0), rows - WIN_H)
            return bb, i - rs, i * GRID_W, rs * GRID_W
        bb = r // rows
        i = r % rows
        rs = jnp.clip(i - WIN_H // 2, 0, rows - WIN_H)
        return bb, i - rs, pl.multiple_of(i * GRID_W, GRID_W), pl.multiple_of(rs * GRID_W, GRID_W)


    def scores_matmul(u):
        out = []
        for g in range(group):
            bb, off, tok0, key0 = geometry(u, g)
            qs = stack_heads(qrot_ref[bb, pl.ds(tok0, GRID_W), :])
            kband = k_ref[bb, pl.ds(key0, nk), :]
            qp = stack_heads(qpl_ref[bb, pl.ds(tok0, GRID_W), :])
            out.append((_dot_nt(qs, kband), off, _dot_nt(qp, kc_ref[bb])))
        return out


    def scores_store(slot, vals):
        for g, (s_lat, off, s_ctx) in enumerate(vals):
            first = (WIN_H - 1) - off
            bias = jnp.concatenate([bias_ref[0, first + 2 * t] for t in range(WIN_H // 2)], axis=1)
            s_lat_ref[slot, g] = s_lat + bias
            s_ctx_ref[slot, g] = s_ctx

    def softmax(slot):
        for g in range(group):
            s_lat = s_lat_ref[slot, g]
            s_ctx = s_ctx_ref[slot, g]
            m = jnp.maximum(jnp.max(s_lat, axis=-1, keepdims=True), jnp.max(s_ctx, axis=-1, keepdims=True))
            p_lat_ref[slot, g] = jnp.exp2(s_lat - m).astype(BF16)
            p_ctx_ref[slot, g] = jnp.exp2(s_ctx - m).astype(BF16)

    ones_lat = jnp.ones((nk, LANES), BF16)
    ones_ctx = jnp.ones((kc_ref.shape[1], LANES), BF16)

    def pv_matmul(u, slot):
        out = []
        for g in range(group):
            bb, _, _, key0 = geometry(u, g)
            v_lat = jnp.concatenate([v_ref[bb, pl.ds(key0, nk), :], ones_lat], axis=1)
            v_ctx = jnp.concatenate([vc_ref[bb], ones_ctx], axis=1)
            out.append(jnp.dot(p_lat_ref[slot, g], v_lat, preferred_element_type=F32)
                       + jnp.dot(p_ctx_ref[slot, g], v_ctx, preferred_element_type=F32))
        return out

    def pv_store(u, vals):
        for g, ol in enumerate(vals):
            bb, _, tok0, _ = geometry(u, g)
            o = ol[:, 0:LANES] * (1.0 / ol[:, LANES:])
            o2 = jnp.where(first_head, o[0:GRID_W], o[GRID_W:2 * GRID_W])
            gate = ga_ref[bb, pl.ds(tok0, GRID_W), :].astype(F32)
            o_ref[bb, pl.ds(tok0, GRID_W), :] = (o2 * gate).astype(BF16)

    scores_store(0, scores_matmul(0))
    sc = scores_matmul(1)
    softmax(0)
    scores_store(1, sc)

    def step(t, slot):
        sc = scores_matmul(t)
        pv = pv_matmul(t - 2, slot)
        softmax(1 - slot)
        scores_store(slot, sc)
        pv_store(t - 2, pv)

    n_steady, n_left = divmod(n_units - 2, steps)

    def steady(n, carry):
        for j in range(steps):
            step(steps * n + 2 + j, j % 2)
        return carry

    lax.fori_loop(0, n_steady, steady, 0)
    for j in range(n_left):
        step(steps * n_steady + 2 + j, j % 2)

    pv = pv_matmul(n_units - 2, 0)
    softmax(1)
    pv_store(n_units - 2, pv)
    pv_store(n_units - 1, pv_matmul(n_units - 1, 1))


def _attention(qrot, qpl, k, v, kc, vc, ga, bias, *, group, n_batch, steps):
    b, s, da = qrot.shape
    l = vc.shape[1]
    rows = s // GRID_W
    n_hp = da // LANES
    nq = 2 * GRID_W
    nk = WIN_H * GRID_W
    assert rows % (2 * group) == 0 and b % n_batch == 0 and steps % 2 == 0
    tok = pl.BlockSpec((n_batch, s, LANES), lambda hp, i: (i, 0, hp))
    ctx_tok = pl.BlockSpec((n_batch, l, LANES), lambda hp, i: (i, 0, hp))
    return pl.pallas_call(
        functools.partial(_attn_kernel, rows=rows, group=group, n_batch=n_batch, steps=steps),
        grid=(n_hp, b // n_batch),
        in_specs=[tok, tok, tok, tok, ctx_tok, ctx_tok, tok,
                  pl.BlockSpec((1,) + bias.shape[1:], lambda hp, i: (hp, 0, 0, 0))],
        out_specs=tok,
        out_shape=jax.ShapeDtypeStruct((b, s, da), BF16),
        scratch_shapes=[pltpu.VMEM((2, group, nq, nk), F32),
                        pltpu.VMEM((2, group, nq, l), F32),
                        pltpu.VMEM((2, group, nq, nk), BF16),
                        pltpu.VMEM((2, group, nq, l), BF16)],
        compiler_params=pltpu.CompilerParams(vmem_limit_bytes=VMEM_LIMIT),
        name="attn",
    )(qrot, qpl, k, v, kc, vc, ga, bias)


def _out_proj_kernel(x_ref, a_ref, c_ref, ada_ref, w_ref, o_ref, *, da):
    _, _, gate = _ada_row(ada_ref, pl.program_id(0))
    upd = (jnp.dot(a_ref[0], w_ref[0:da, :].astype(BF16), preferred_element_type=F32)
           + jnp.dot(c_ref[0], w_ref[da:, :].astype(BF16), preferred_element_type=F32))
    o_ref[0] = x_ref[0] + gate * upd


def _out_proj(x, attn, conv, ada, w_out, *, tm):
    b, s, d = x.shape
    da = attn.shape[2]
    dc = conv.shape[2]
    return pl.pallas_call(
        functools.partial(_out_proj_kernel, da=da),
        grid=(b, s // tm),
        in_specs=[pl.BlockSpec((1, tm, d), lambda i, j: (i, j, 0)),
                  pl.BlockSpec((1, tm, da), lambda i, j: (i, j, 0)),
                  pl.BlockSpec((1, tm, dc), lambda i, j: (i, j, 0)),
                  pl.BlockSpec(ada.shape, lambda i, j: (0, 0)),
                  pl.BlockSpec(w_out.shape, lambda i, j: (0, 0))],
        out_specs=pl.BlockSpec((1, tm, d), lambda i, j: (i, j, 0)),
        out_shape=jax.ShapeDtypeStruct((b, s, d), F32),
        compiler_params=pltpu.CompilerParams(vmem_limit_bytes=VMEM_LIMIT),
        name="out_proj",
    )(x, attn, conv, ada, w_out)


def _rope_tables(s):
    nf = HEAD_DIM // 4
    inv = (ROPE_THETA ** (-np.arange(nf, dtype=np.float32) / nf)).astype(np.float32)
    pos = np.arange(s)
    lane = np.arange(LANES)
    d = lane % HEAD_DIM
    axis = d // (2 * nf)
    half = (d % (2 * nf)) // nf
    coord = np.where(axis[None, :] == 0, (pos // GRID_W)[:, None], (pos % GRID_W)[:, None]).astype(np.float32)
    ang = (coord * inv[d % nf][None, :]).astype(np.float32)
    cos = np.cos(ang).astype(np.float32)
    sin = np.sin(ang).astype(np.float32)
    sin_lo = np.where(half[None, :] == 0, -sin, 0.0).astype(np.float32)
    sin_hi = np.where(half[None, :] == 1, sin, 0.0).astype(np.float32)
    return jnp.asarray(cos), jnp.asarray(sin_lo), jnp.asarray(sin_hi)


def _bias_block(rpb_ref, o_ref):
    cq = lax.broadcasted_iota(jnp.int32, (GRID_W, LANES), 0)
    ck = lax.broadcasted_iota(jnp.int32, (GRID_W, LANES), 1) % GRID_W
    col_start = jnp.clip(cq - WIN_W // 2, 0, GRID_W - WIN_W)
    valid = (ck >= col_start) & (ck < col_start + WIN_W)

    def one_pair(dr, carry):
        for a in range(2):
            two = rpb_ref[a, pl.ds(dr, 2), :]
            lanes = jnp.broadcast_to(jnp.concatenate([two[0:1], two[1:2]], axis=1), (GRID_W, LANES))
            toeplitz = pltpu.roll(lanes, LANES - (WIN_W - 1), 1, stride=1, stride_axis=0)
            o_ref[0, dr, a * GRID_W:(a + 1) * GRID_W, :] = jnp.where(valid, toeplitz * LOG2_E, MASK_VALUE)
        return carry

    lax.fori_loop(0, o_ref.shape[1], one_pair, 0, unroll=2)


def kernel(x, c, ctx, c_ctx, w_ada, b_ada, norm_g, w_in, q_norm_g, k_norm_g, rpb, conv_w, conv_b, w_out):
    depth = w_ada.shape[0]
    b, s, d = x.shape
    dc = conv_w.shape[2]
    da = (w_in.shape[2] - 4 * dc) // 4
    rows = s // GRID_W
    assert depth == 1 and s % GRID_W == 0 and rows >= WIN_H and da % LANES == 0

    cos, slo, shi = _rope_tables(s)
    assert da % MXU_DIM == 0 and MXU_DIM % HEAD_DIM == 0
    seg = np.arange(MXU_DIM) // HEAD_DIM
    bd = jnp.asarray((seg[:, None] == seg[None, :]).astype(np.float32), dtype=BF16)

    ada, bias, w_in_b, gains = _prep(c, c_ctx, w_ada[0], b_ada, rpb[0], w_in[0], q_norm_g, k_norm_g, da)

    qrot, qpl, k, v, ga, conv, kc, vc = _in_proj(x, ctx, ada, b, norm_g, w_in_b, gains, cos, slo, shi, bd,
                                                 conv_w, conv_b, tm=IN_PROJ_ROWS, sub=IN_PROJ_SUB)
    attn = _attention(qrot, qpl, k, v, kc, vc, ga, bias, group=ATTN_GROUP, n_batch=ATTN_BATCH, steps=ATTN_STEPS)
    return _out_proj(x, attn, conv, ada, w_out[0], tm=OUT_PROJ_ROWS)
```

```python
import functools

import numpy as np
import jax
import jax.numpy as jnp
from jax import lax
from jax.experimental import pallas as pl
from jax.experimental.pallas import tpu as pltpu

F32 = jnp.float32
BF16 = jnp.bfloat16

HEAD_DIM = 64
GRID_W = 64
WIN_H = 8
WIN_W = 16
CONV_K = 3
ROPE_THETA = 10000.0
RMS_EPS = 1e-6
MASK_VALUE = -1e30
LOG2_E = 1.4426950408889634

LANES = 128
MXU_DIM = 256
F32_ROWS = 8
BF16_ROWS = 16
HALO = BF16_ROWS
VMEM_CAPACITY = 64 * 1024 * 1024
VMEM_LIMIT = VMEM_CAPACITY * 7 // 8

IN_PROJ_ROWS, IN_PROJ_SUB = 1024, 512
OUT_PROJ_ROWS = 2048
ATTN_GROUP, ATTN_BATCH, ATTN_STEPS = 2, 4, 30


def _silu(z):
    return z * jax.nn.sigmoid(z)


def _prep_kernel(c_ref, cctx_ref, w_ref, b_ref, rpb_ref, win_ref, qg_ref, kg_ref,
                 ada_ref, bias_ref, winb_ref, gains_ref):
    nb, d = c_ref.shape
    cctx = cctx_ref[...].reshape(1, d)
    cond = jnp.concatenate([c_ref[...], jnp.broadcast_to(cctx, (ada_ref.shape[0] - nb, d))], axis=0)
    a = _silu(cond).astype(BF16)
    ada_ref[...] = jnp.dot(a, w_ref[...].astype(BF16), preferred_element_type=F32) + b_ref[...]
    winb_ref[...] = win_ref[...].astype(BF16)
    _bias_block(rpb_ref, bias_ref)
    g = jnp.concatenate([qg_ref[...] * (HEAD_DIM ** -0.5 * LOG2_E), kg_ref[...]], axis=0)
    gains_ref[...] = jnp.concatenate([g] * (gains_ref.shape[1] // HEAD_DIM), axis=1)


def _prep(c, c_ctx, w_ada, b_ada, rpb, w_in, q_gain, k_gain, d_attn):
    nb, d = c.shape
    assert nb % F32_ROWS == 0
    rows = nb + F32_ROWS
    n = w_ada.shape[1]
    _, nh, ndr, ndc = rpb.shape
    steps = nh // 2
    tn = n // steps
    tw = w_in.shape[1] // steps
    assert ndr == 2 * WIN_H - 1 and ndc == 2 * WIN_W - 1 and ndc <= GRID_W and tn % LANES == 0 and tw % LANES == 0
    rpb_pad = jnp.pad(rpb, ((0, 0), (0, 0), (0, 0), (0, GRID_W - ndc)))[0]
    return pl.pallas_call(
        _prep_kernel,
        grid=(steps,),
        in_specs=[pl.BlockSpec((nb, d), lambda i: (0, 0)),
                  pl.BlockSpec((d,), lambda i: (0,)),
                  pl.BlockSpec((d, tn), lambda i: (0, i)),
                  pl.BlockSpec((1, tn), lambda i: (0, i)),
                  pl.BlockSpec((2, ndr, GRID_W), lambda i: (i, 0, 0)),
                  pl.BlockSpec((d, tw), lambda i: (0, i)),
                  pl.BlockSpec((1, HEAD_DIM), lambda i: (0, 0)),
                  pl.BlockSpec((1, HEAD_DIM), lambda i: (0, 0))],
        out_specs=[pl.BlockSpec((rows, tn), lambda i: (0, i)),
                   pl.BlockSpec((1, ndr - 1, 2 * GRID_W, 2 * GRID_W), lambda i: (i, 0, 0, 0)),
                   pl.BlockSpec((d, tw), lambda i: (0, i)),
                   pl.BlockSpec((2, d_attn), lambda i: (0, 0))],
        out_shape=[jax.ShapeDtypeStruct((rows, n), F32),
                   jax.ShapeDtypeStruct((steps, ndr - 1, 2 * GRID_W, 2 * GRID_W), F32),
                   jax.ShapeDtypeStruct(w_in.shape, BF16),
                   jax.ShapeDtypeStruct((2, d_attn), F32)],
        compiler_params=pltpu.CompilerParams(vmem_limit_bytes=VMEM_LIMIT),
        name="prep",
    )(c, c_ctx, w_ada, b_ada, rpb_pad, w_in, q_gain, k_gain)


def _ada_row(ada_ref, row):
    r = ada_ref[pl.ds(row, 1), :]
    d = r.shape[1] // 3
    return r[:, 0:d], r[:, d:2 * d], r[:, 2 * d:]


def _modulated_norm(xt, mult, shift):
    ms = jnp.mean(xt * xt, axis=-1, keepdims=True)
    return (xt * lax.rsqrt(ms + RMS_EPS) * mult + shift).astype(BF16)


def _head_norm(t, gain, bd):
    t2 = (t * t).astype(BF16)
    w = bd.shape[0]
    ss = jnp.concatenate([jnp.dot(t2[:, c:c + w], bd, preferred_element_type=F32)
                          for c in range(0, t.shape[1], w)], axis=1)
    return t * lax.rsqrt(ss * (1.0 / HEAD_DIM) + RMS_EPS) * gain


def _rope(t, cos, sin_lo, sin_hi):
    outs = []
    for c in range(t.shape[1] // LANES):
        tc = t[:, c * LANES:(c + 1) * LANES]
        up = pltpu.roll(tc, LANES - HEAD_DIM // 4, 1)
        dn = pltpu.roll(tc, HEAD_DIM // 4, 1)
        outs.append(tc * cos + up * sin_lo + dn * sin_hi)
    return jnp.concatenate(outs, axis=1)


def _in_proj_kernel(x_ref, xp_ref, xn_ref, ctx_ref, ada_ref, ng_ref, w_ref, g_ref,
                    cos_ref, slo_ref, shi_ref, bd_ref, cw_ref, cb_ref,
                    qrot_ref, qpl_ref, k_ref, v_ref, ga_ref, conv_ref, kc_ref, vc_ref,
                    hext_ref, cu_ref, *, tm, sub, da, dc, ctx_row):
    j = pl.program_id(1)
    nj = pl.num_programs(1)
    shift, scale, _ = _ada_row(ada_ref, pl.program_id(0))
    mult = ng_ref[...] * (1.0 + scale)
    n_sub = tm // sub
    cr = ctx_ref.shape[1]

    def proj(src, lo, width):
        return jnp.dot(hext_ref[src, :], w_ref[:, lo:lo + width], preferred_element_type=F32)

    bd = bd_ref[...]
    hext_ref[0:HALO, :] = _modulated_norm(xp_ref[0], mult, shift)
    hext_ref[HALO + tm:2 * HALO + tm, :] = _modulated_norm(xn_ref[0], mult, shift)
    shift_c, scale_c, _ = _ada_row(ada_ref, ctx_row)
    hext_ref[2 * HALO + tm:, :] = _modulated_norm(ctx_ref[0], ng_ref[...] * (1.0 + scale_c), shift_c)

    for n in range(n_sub):
        r0 = n * sub
        hext_ref[HALO + r0:HALO + r0 + sub, :] = _modulated_norm(x_ref[0, r0:r0 + sub, :], mult, shift)
    for n in range(n_sub):
        r0 = n * sub
        rows = slice(r0, r0 + sub)
        h = slice(HALO + r0, HALO + r0 + sub)
        with_ctx = n == n_sub - 1
        h_kv = slice(HALO + r0, 2 * HALO + tm + cr) if with_ctx else h
        cos, slo, shi = cos_ref[rows, :], slo_ref[rows, :], shi_ref[rows, :]


        hx = slice(r0, r0 + sub + 2 * HALO)
        cu = proj(hx, 4 * da, dc) * proj(hx, 4 * da + 2 * dc, dc)
        row = lax.broadcasted_iota(jnp.int32, (sub + 2 * HALO, 1), 0) + r0
        inside = ((row >= HALO) | (j > 0)) & ((row < HALO + tm) | (j < nj - 1))
        cu_ref[n] = jnp.where(inside, cu, 0.0)
        y = (cb_ref[...]
             + cw_ref[0, 0:1, :] * cu_ref[n, HALO - 1:HALO - 1 + sub, :]
             + cw_ref[0, 1:2, :] * cu_ref[n, HALO:HALO + sub, :]
             + cw_ref[0, 2:3, :] * cu_ref[n, HALO + 1:HALO + 1 + sub, :])
        bg = proj(h, 4 * da + dc, dc)
        zc = proj(h, 4 * da + 3 * dc, dc)
        conv_ref[0, rows, :] = (bg * y * _silu(zc)).astype(BF16)

        qn = _head_norm(proj(h, 0, da), g_ref[0:1, :], bd)
        qpl_ref[0, rows, :] = qn.astype(BF16)
        qrot_ref[0, rows, :] = _rope(qn, cos, slo, shi).astype(BF16)

        kf = proj(h_kv, da, da)
        kn = _head_norm(kf[0:sub], g_ref[1:2, :], bd)
        k_ref[0, rows, :] = _rope(kn, cos, slo, shi).astype(BF16)
        if with_ctx:
            kc_ref[0] = _head_norm(kf[sub + HALO:], g_ref[1:2, :], bd).astype(BF16)

        ga_ref[0, rows, :] = _silu(proj(h, 3 * da, da)).astype(BF16)
        vf = proj(h_kv, 2 * da, da)
        v_ref[0, rows, :] = vf[0:sub].astype(BF16)
        if with_ctx:
            vc_ref[0] = vf[sub + HALO:].astype(BF16)


def _in_proj(x, ctx, ada, ctx_row, norm_g, w_in, gains, cos, slo, shi, bd, conv_w, conv_b, *, tm, sub):
    b, s, d = x.shape
    l = ctx.shape[1]
    assert s % tm == 0 and tm % sub == 0 and sub % HALO == 0
    cr = l // (s // tm)
    assert l % (s // tm) == 0 and cr % BF16_ROWS == 0
    dc = conv_w.shape[2]
    da = (w_in.shape[1] - 4 * dc) // 4
    nh = tm // HALO
    last_halo = s // HALO - 1
    tok = lambda width: pl.BlockSpec((1, tm, width), lambda i, j: (i, j, 0))
    const = lambda shape: pl.BlockSpec(shape, lambda i, j: (0,) * len(shape))
    tab = pl.BlockSpec((tm, LANES), lambda i, j: (j, 0))
    out_tok = jax.ShapeDtypeStruct((b, s, da), BF16)
    out_ctx = jax.ShapeDtypeStruct((b, l, da), BF16)
    ctx_spec = lambda width: pl.BlockSpec((1, cr, width), lambda i, j: (i, j, 0))
    return pl.pallas_call(
        functools.partial(_in_proj_kernel, tm=tm, sub=sub, da=da, dc=dc, ctx_row=ctx_row),
        grid=(b, s // tm),
        in_specs=[tok(d),
                  pl.BlockSpec((1, HALO, d), lambda i, j: (i, jnp.maximum(j * nh - 1, 0), 0)),
                  pl.BlockSpec((1, HALO, d), lambda i, j: (i, jnp.minimum((j + 1) * nh, last_halo), 0)),
                  ctx_spec(d),
                  const(ada.shape),
                  const((1, d)), const(w_in.shape), const(gains.shape),
                  tab, tab, tab, const(bd.shape), const((1, CONV_K, dc)), const((1, dc))],
        out_specs=[tok(da), tok(da), tok(da), tok(da), tok(da), tok(dc), ctx_spec(da), ctx_spec(da)],
        out_shape=[out_tok, out_tok, out_tok, out_tok, out_tok,
                   jax.ShapeDtypeStruct((b, s, dc), BF16), out_ctx, out_ctx],
        scratch_shapes=[pltpu.VMEM((tm + 2 * HALO + cr, d), BF16),
                        pltpu.VMEM((tm // sub, sub + 2 * HALO, dc), F32)],
        compiler_params=pltpu.CompilerParams(vmem_limit_bytes=VMEM_LIMIT),
        name="in_proj",
    )(x, x, x, ctx, ada, norm_g, w_in, gains, cos, slo, shi, bd, conv_w, conv_b)


def _dot_nt(a, b):
    return lax.dot_general(a, b, (((1,), (1,)), ((), ())), preferred_element_type=F32)


def _attn_kernel(qrot_ref, qpl_ref, k_ref, v_ref, kc_ref, vc_ref, ga_ref, bias_ref, o_ref,
                 s_lat_ref, s_ctx_ref, p_lat_ref, p_ctx_ref, *, rows, group, n_batch, steps):
    n_units = n_batch * rows // group
    nk = WIN_H * GRID_W

    lane = lax.broadcasted_iota(jnp.int32, (GRID_W, LANES), 1)
    first_head = lane < HEAD_DIM

    def stack_heads(q2):
        zero = jnp.zeros_like(q2)
        return jnp.concatenate([jnp.where(first_head, q2, zero), jnp.where(first_head, zero, q2)], axis=0)

    def geometry(u, g):
        r = u * group + g
        if isinstance(r, int):
            bb, i = divmod(r, rows)
            rs = min(max(i - WIN_H // 2, 0), rows - WIN_H)
            return bb, i - rs, i * GRID_W, rs * GRID_W
        bb = r // rows
        i = r % rows
        rs = jnp.clip(i - WIN_H // 2, 0, rows - WIN_H)
        return bb, i - rs, pl.multiple_of(i * GRID_W, GRID_W), pl.multiple_of(rs * GRID_W, GRID_W)


    def scores_matmul(u):
        out = []
        for g in range(group):
            bb, off, tok0, key0 = geometry(u, g)
            qs = stack_heads(qrot_ref[bb, pl.ds(tok0, GRID_W), :])
            kband = k_ref[bb, pl.ds(key0, nk), :]
            qp = stack_heads(qpl_ref[bb, pl.ds(tok0, GRID_W), :])
            out.append((_dot_nt(qs, kband), off, _dot_nt(qp, kc_ref[bb])))
        return out


    def scores_store(slot, vals):
        for g, (s_lat, off, s_ctx) in enumerate(vals):
            first = (WIN_H - 1) - off
            bias = jnp.concatenate([bias_ref[0, first + 2 * t] for t in range(WIN_H // 2)], axis=1)
            s_lat_ref[slot, g] = s_lat + bias
            s_ctx_ref[slot, g] = s_ctx

    def softmax(slot):
        for g in range(group):
            s_lat = s_lat_ref[slot, g]
            s_ctx = s_ctx_ref[slot, g]
            m = jnp.maximum(jnp.max(s_lat, axis=-1, keepdims=True), jnp.max(s_ctx, axis=-1, keepdims=True))
            p_lat_ref[slot, g] = jnp.exp2(s_lat - m).astype(BF16)
            p_ctx_ref[slot, g] = jnp.exp2(s_ctx - m).astype(BF16)

    ones_lat = jnp.ones((nk, LANES), BF16)
    ones_ctx = jnp.ones((kc_ref.shape[1], LANES), BF16)

    def pv_matmul(u, slot):
        out = []
        for g in range(group):
            bb, _, _, key0 = geometry(u, g)
            v_lat = jnp.concatenate([v_ref[bb, pl.ds(key0, nk), :], ones_lat], axis=1)
            v_ctx = jnp.concatenate([vc_ref[bb], ones_ctx], axis=1)
            out.append(jnp.dot(p_lat_ref[slot, g], v_lat, preferred_element_type=F32)
                       + jnp.dot(p_ctx_ref[slot, g], v_ctx, preferred_element_type=F32))
        return out

    def pv_store(u, vals):
        for g, ol in enumerate(vals):
            bb, _, tok0, _ = geometry(u, g)
            o = ol[:, 0:LANES] * (1.0 / ol[:, LANES:])
            o2 = jnp.where(first_head, o[0:GRID_W], o[GRID_W:2 * GRID_W])
            gate = ga_ref[bb, pl.ds(tok0, GRID_W), :].astype(F32)
            o_ref[bb, pl.ds(tok0, GRID_W), :] = (o2 * gate).astype(BF16)

    scores_store(0, scores_matmul(0))
    sc = scores_matmul(1)
    softmax(0)
    scores_store(1, sc)

    def step(t, slot):
        sc = scores_matmul(t)
        pv = pv_matmul(t - 2, slot)
        softmax(1 - slot)
        scores_store(slot, sc)
        pv_store(t - 2, pv)

    n_steady, n_left = divmod(n_units - 2, steps)

    def steady(n, carry):
        for j in range(steps):
            step(steps * n + 2 + j, j % 2)
        return carry

    lax.fori_loop(0, n_steady, steady, 0)
    for j in range(n_left):
        step(steps * n_steady + 2 + j, j % 2)

    pv = pv_matmul(n_units - 2, 0)
    softmax(1)
    pv_store(n_units - 2, pv)
    pv_store(n_units - 1, pv_matmul(n_units - 1, 1))


def _attention(qrot, qpl, k, v, kc, vc, ga, bias, *, group, n_batch, steps):
    b, s, da = qrot.shape
    l = vc.shape[1]
    rows = s // GRID_W
    n_hp = da // LANES
    nq = 2 * GRID_W
    nk = WIN_H * GRID_W
    assert rows % (2 * group) == 0 and b % n_batch == 0 and steps % 2 == 0
    tok = pl.BlockSpec((n_batch, s, LANES), lambda hp, i: (i, 0, hp))
    ctx_tok = pl.BlockSpec((n_batch, l, LANES), lambda hp, i: (i, 0, hp))
    return pl.pallas_call(
        functools.partial(_attn_kernel, rows=rows, group=group, n_batch=n_batch, steps=steps),
        grid=(n_hp, b // n_batch),
        in_specs=[tok, tok, tok, tok, ctx_tok, ctx_tok, tok,
                  pl.BlockSpec((1,) + bias.shape[1:], lambda hp, i: (hp, 0, 0, 0))],
        out_specs=tok,
        out_shape=jax.ShapeDtypeStruct((b, s, da), BF16),
        scratch_shapes=[pltpu.VMEM((2, group, nq, nk), F32),
                        pltpu.VMEM((2, group, nq, l), F32),
                        pltpu.VMEM((2, group, nq, nk), BF16),
                        pltpu.VMEM((2, group, nq, l), BF16)],
        compiler_params=pltpu.CompilerParams(vmem_limit_bytes=VMEM_LIMIT),
        name="attn",
    )(qrot, qpl, k, v, kc, vc, ga, bias)


def _out_proj_kernel(x_ref, a_ref, c_ref, ada_ref, w_ref, o_ref, *, da):
    _, _, gate = _ada_row(ada_ref, pl.program_id(0))
    upd = (jnp.dot(a_ref[0], w_ref[0:da, :].astype(BF16), preferred_element_type=F32)
           + jnp.dot(c_ref[0], w_ref[da:, :].astype(BF16), preferred_element_type=F32))
    o_ref[0] = x_ref[0] + gate * upd


def _out_proj(x, attn, conv, ada, w_out, *, tm):
    b, s, d = x.shape
    da = attn.shape[2]
    dc = conv.shape[2]
    return pl.pallas_call(
        functools.partial(_out_proj_kernel, da=da),
        grid=(b, s // tm),
        in_specs=[pl.BlockSpec((1, tm, d), lambda i, j: (i, j, 0)),
                  pl.BlockSpec((1, tm, da), lambda i, j: (i, j, 0)),
                  pl.BlockSpec((1, tm, dc), lambda i, j: (i, j, 0)),
                  pl.BlockSpec(ada.shape, lambda i, j: (0, 0)),
                  pl.BlockSpec(w_out.shape, lambda i, j: (0, 0))],
        out_specs=pl.BlockSpec((1, tm, d), lambda i, j: (i, j, 0)),
        out_shape=jax.ShapeDtypeStruct((b, s, d), F32),
        compiler_params=pltpu.CompilerParams(vmem_limit_bytes=VMEM_LIMIT),
        name="out_proj",
    )(x, attn, conv, ada, w_out)


def _rope_tables(s):
    nf = HEAD_DIM // 4
    inv = (ROPE_THETA ** (-np.arange(nf, dtype=np.float32) / nf)).astype(np.float32)
    pos = np.arange(s)
    lane = np.arange(LANES)
    d = lane % HEAD_DIM
    axis = d // (2 * nf)
    half = (d % (2 * nf)) // nf
    coord = np.where(axis[None, :] == 0, (pos // GRID_W)[:, None], (pos % GRID_W)[:, None]).astype(np.float32)
    ang = (coord * inv[d % nf][None, :]).astype(np.float32)
    cos = np.cos(ang).astype(np.float32)
    sin = np.sin(ang).astype(np.float32)
    sin_lo = np.where(half[None, :] == 0, -sin, 0.0).astype(np.float32)
    sin_hi = np.where(half[None, :] == 1, sin, 0.0).astype(np.float32)
    return jnp.asarray(cos), jnp.asarray(sin_lo), jnp.asarray(sin_hi)


def _bias_block(rpb_ref, o_ref):
    cq = lax.broadcasted_iota(jnp.int32, (GRID_W, LANES), 0)
    ck = lax.broadcasted_iota(jnp.int32, (GRID_W, LANES), 1) % GRID_W
    col_start = jnp.clip(cq - WIN_W // 2, 0, GRID_W - WIN_W)
    valid = (ck >= col_start) & (ck < col_start + WIN_W)

    def one_pair(dr, carry):
        for a in range(2):
            two = rpb_ref[a, pl.ds(dr, 2), :]
            lanes = jnp.broadcast_to(jnp.concatenate([two[0:1], two[1:2]], axis=1), (GRID_W, LANES))
            toeplitz = pltpu.roll(lanes, LANES - (WIN_W - 1), 1, stride=1, stride_axis=0)
            o_ref[0, dr, a * GRID_W:(a + 1) * GRID_W, :] = jnp.where(valid, toeplitz * LOG2_E, MASK_VALUE)
        return carry

    lax.fori_loop(0, o_ref.shape[1], one_pair, 0, unroll=2)


def kernel(x, c, ctx, c_ctx, w_ada, b_ada, norm_g, w_in, q_norm_g, k_norm_g, rpb, conv_w, conv_b, w_out):
    depth = w_ada.shape[0]
    b, s, d = x.shape
    dc = conv_w.shape[2]
    da = (w_in.shape[2] - 4 * dc) // 4
    rows = s // GRID_W
    assert depth == 1 and s % GRID_W == 0 and rows >= WIN_H and da % LANES == 0

    cos, slo, shi = _rope_tables(s)
    assert da % MXU_DIM == 0 and MXU_DIM % HEAD_DIM == 0
    seg = np.arange(MXU_DIM) // HEAD_DIM
    bd = jnp.asarray((seg[:, None] == seg[None, :]).astype(np.float32), dtype=BF16)

    ada, bias, w_in_b, gains = _prep(c, c_ctx, w_ada[0], b_ada, rpb, w_in[0], q_norm_g, k_norm_g, da)

    qrot, qpl, k, v, ga, conv, kc, vc = _in_proj(x, ctx, ada, b, norm_g, w_in_b, gains, cos, slo, shi, bd,
                                                 conv_w, conv_b, tm=IN_PROJ_ROWS, sub=IN_PROJ_SUB)
    attn = _attention(qrot, qpl, k, v, kc, vc, ga, bias, group=ATTN_GROUP, n_batch=ATTN_BATCH, steps=ATTN_STEPS)
    return _out_proj(x, attn, conv, ada, w_out[0], tm=OUT_PROJ_ROWS)
```

```python
import functools

import numpy as np
import jax
import jax.numpy as jnp
from jax import lax
from jax.experimental import pallas as pl
from jax.experimental.pallas import tpu as pltpu

F32 = jnp.float32
BF16 = jnp.bfloat16

HEAD_DIM = 64
GRID_W = 64
WIN_H = 8
WIN_W = 16
CONV_K = 3
ROPE_THETA = 10000.0
RMS_EPS = 1e-6
MASK_VALUE = -1e30
LOG2_E = 1.4426950408889634

LANES = 128
MXU_DIM = 256
F32_ROWS = 8
BF16_ROWS = 16
HALO = BF16_ROWS
VMEM_CAPACITY = 64 * 1024 * 1024
VMEM_LIMIT = VMEM_CAPACITY * 7 // 8

PREP_STEPS = 4
IN_PROJ_ROWS, IN_PROJ_SUB = 1024, 512
OUT_PROJ_ROWS = 2048
ATTN_GROUP, ATTN_BATCH, ATTN_STEPS = 2, 4, 30


def _silu(z):
    return z * jax.nn.sigmoid(z)


def _prep_kernel(c_ref, cctx_ref, w_ref, b_ref, win_ref, qg_ref, kg_ref, ada_ref, winb_ref, gains_ref):
    nb, d = c_ref.shape
    cctx = cctx_ref[...].reshape(1, d)
    cond = jnp.concatenate([c_ref[...], jnp.broadcast_to(cctx, (ada_ref.shape[0] - nb, d))], axis=0)
    a = _silu(cond).astype(BF16)
    ada_ref[...] = jnp.dot(a, w_ref[...].astype(BF16), preferred_element_type=F32) + b_ref[...]
    winb_ref[...] = win_ref[...].astype(BF16)
    g = jnp.concatenate([qg_ref[...] * (HEAD_DIM ** -0.5 * LOG2_E), kg_ref[...]], axis=0)
    gains_ref[...] = jnp.concatenate([g] * (gains_ref.shape[1] // HEAD_DIM), axis=1)


def _prep(c, c_ctx, w_ada, b_ada, w_in, q_gain, k_gain, d_attn, *, steps):
    nb, d = c.shape
    assert nb % F32_ROWS == 0
    rows = nb + F32_ROWS
    n = w_ada.shape[1]
    tn = n // steps
    tw = w_in.shape[1] // steps
    assert tn % LANES == 0 and tw % LANES == 0
    return pl.pallas_call(
        _prep_kernel,
        grid=(steps,),
        in_specs=[pl.BlockSpec((nb, d), lambda i: (0, 0)),
                  pl.BlockSpec((d,), lambda i: (0,)),
                  pl.BlockSpec((d, tn), lambda i: (0, i)),
                  pl.BlockSpec((1, tn), lambda i: (0, i)),
                  pl.BlockSpec((d, tw), lambda i: (0, i)),
                  pl.BlockSpec((1, HEAD_DIM), lambda i: (0, 0)),
                  pl.BlockSpec((1, HEAD_DIM), lambda i: (0, 0))],
        out_specs=[pl.BlockSpec((rows, tn), lambda i: (0, i)),
                   pl.BlockSpec((d, tw), lambda i: (0, i)),
                   pl.BlockSpec((2, d_attn), lambda i: (0, 0))],
        out_shape=[jax.ShapeDtypeStruct((rows, n), F32),
                   jax.ShapeDtypeStruct(w_in.shape, BF16),
                   jax.ShapeDtypeStruct((2, d_attn), F32)],
        compiler_params=pltpu.CompilerParams(vmem_limit_bytes=VMEM_LIMIT),
        name="prep",
    )(c, c_ctx, w_ada, b_ada, w_in, q_gain, k_gain)


def _ada_row(ada_ref, row):
    r = ada_ref[pl.ds(row, 1), :]
    d = r.shape[1] // 3
    return r[:, 0:d], r[:, d:2 * d], r[:, 2 * d:]


def _modulated_norm(xt, mult, shift):
    ms = jnp.mean(xt * xt, axis=-1, keepdims=True)
    return (xt * lax.rsqrt(ms + RMS_EPS) * mult + shift).astype(BF16)


def _head_norm(t, gain, bd):
    t2 = (t * t).astype(BF16)
    w = bd.shape[0]
    ss = jnp.concatenate([jnp.dot(t2[:, c:c + w], bd, preferred_element_type=F32)
                          for c in range(0, t.shape[1], w)], axis=1)
    return t * lax.rsqrt(ss * (1.0 / HEAD_DIM) + RMS_EPS) * gain


def _rope(t, cos, sin_lo, sin_hi):
    outs = []
    for c in range(t.shape[1] // LANES):
        tc = t[:, c * LANES:(c + 1) * LANES]
        up = pltpu.roll(tc, LANES - HEAD_DIM // 4, 1)
        dn = pltpu.roll(tc, HEAD_DIM // 4, 1)
        outs.append(tc * cos + up * sin_lo + dn * sin_hi)
    return jnp.concatenate(outs, axis=1)


def _in_proj_kernel(x_ref, xp_ref, xn_ref, ctx_ref, ada_ref, ng_ref, w_ref, g_ref,
                    cos_ref, slo_ref, shi_ref, bd_ref, cw_ref, cb_ref, rpb_ref,
                    qrot_ref, qpl_ref, k_ref, v_ref, ga_ref, conv_ref, kc_ref, vc_ref, bias_ref,
                    hext_ref, cu_ref, *, tm, sub, da, dc, ctx_row, bias_blocks):
    j = pl.program_id(1)
    nj = pl.num_programs(1)
    step = pl.program_id(0) * nj + j
    _bias_tiles(rpb_ref, bias_ref, jnp.minimum(step, bias_blocks - 1) * bias_ref.shape[0])
    shift, scale, _ = _ada_row(ada_ref, pl.program_id(0))
    mult = ng_ref[...] * (1.0 + scale)
    n_sub = tm // sub
    cr = ctx_ref.shape[1]

    def proj(src, lo, width):
        return jnp.dot(hext_ref[src, :], w_ref[:, lo:lo + width], preferred_element_type=F32)

    bd = bd_ref[...]
    hext_ref[0:HALO, :] = _modulated_norm(xp_ref[0], mult, shift)
    hext_ref[HALO + tm:2 * HALO + tm, :] = _modulated_norm(xn_ref[0], mult, shift)
    shift_c, scale_c, _ = _ada_row(ada_ref, ctx_row)
    hext_ref[2 * HALO + tm:, :] = _modulated_norm(ctx_ref[0], ng_ref[...] * (1.0 + scale_c), shift_c)

    for n in range(n_sub):
        r0 = n * sub
        hext_ref[HALO + r0:HALO + r0 + sub, :] = _modulated_norm(x_ref[0, r0:r0 + sub, :], mult, shift)
    for n in range(n_sub):
        r0 = n * sub
        rows = slice(r0, r0 + sub)
        h = slice(HALO + r0, HALO + r0 + sub)
        with_ctx = n == n_sub - 1
        h_kv = slice(HALO + r0, 2 * HALO + tm + cr) if with_ctx else h
        cos, slo, shi = cos_ref[rows, :], slo_ref[rows, :], shi_ref[rows, :]


        hx = slice(r0, r0 + sub + 2 * HALO)
        cu = proj(hx, 4 * da, dc) * proj(hx, 4 * da + 2 * dc, dc)
        row = lax.broadcasted_iota(jnp.int32, (sub + 2 * HALO, 1), 0) + r0
        inside = ((row >= HALO) | (j > 0)) & ((row < HALO + tm) | (j < nj - 1))
        cu_ref[n] = jnp.where(inside, cu, 0.0)
        y = (cb_ref[...]
             + cw_ref[0, 0:1, :] * cu_ref[n, HALO - 1:HALO - 1 + sub, :]
             + cw_ref[0, 1:2, :] * cu_ref[n, HALO:HALO + sub, :]
             + cw_ref[0, 2:3, :] * cu_ref[n, HALO + 1:HALO + 1 + sub, :])
        bg = proj(h, 4 * da + dc, dc)
        zc = proj(h, 4 * da + 3 * dc, dc)
        conv_ref[0, rows, :] = (bg * y * _silu(zc)).astype(BF16)

        qn = _head_norm(proj(h, 0, da), g_ref[0:1, :], bd)
        qpl_ref[0, rows, :] = qn.astype(BF16)
        qrot_ref[0, rows, :] = _rope(qn, cos, slo, shi).astype(BF16)

        kf = proj(h_kv, da, da)
        kn = _head_norm(kf[0:sub], g_ref[1:2, :], bd)
        k_ref[0, rows, :] = _rope(kn, cos, slo, shi).astype(BF16)
        if with_ctx:
            kc_ref[0] = _head_norm(kf[sub + HALO:], g_ref[1:2, :], bd).astype(BF16)

        ga_ref[0, rows, :] = _silu(proj(h, 3 * da, da)).astype(BF16)
        vf = proj(h_kv, 2 * da, da)
        v_ref[0, rows, :] = vf[0:sub].astype(BF16)
        if with_ctx:
            vc_ref[0] = vf[sub + HALO:].astype(BF16)


def _in_proj(x, ctx, ada, ctx_row, norm_g, w_in, gains, cos, slo, shi, bd, conv_w, conv_b, rpb, *, tm, sub):
    b, s, d = x.shape
    n_heads, ndr, ndc = rpb.shape
    assert ndr == 2 * WIN_H - 1 and ndc == 2 * WIN_W - 1 and ndc <= GRID_W
    rpb_pad = jnp.pad(rpb, ((0, 0), (0, 0), (0, GRID_W - ndc)))
    n_tiles = (n_heads // 2) * (ndr - 1)
    bias_blocks = max(k for k in range(1, b * (s // tm) + 1) if n_tiles % k == 0)
    tiles_per_step = n_tiles // bias_blocks
    l = ctx.shape[1]
    assert s % tm == 0 and tm % sub == 0 and sub % HALO == 0
    cr = l // (s // tm)
    assert l % (s // tm) == 0 and cr % BF16_ROWS == 0
    dc = conv_w.shape[2]
    da = (w_in.shape[1] - 4 * dc) // 4
    nh = tm // HALO
    last_halo = s // HALO - 1
    tok = lambda width: pl.BlockSpec((1, tm, width), lambda i, j: (i, j, 0))
    const = lambda shape: pl.BlockSpec(shape, lambda i, j: (0,) * len(shape))
    tab = pl.BlockSpec((tm, LANES), lambda i, j: (j, 0))
    out_tok = jax.ShapeDtypeStruct((b, s, da), BF16)
    out_ctx = jax.ShapeDtypeStruct((b, l, da), BF16)
    ctx_spec = lambda width: pl.BlockSpec((1, cr, width), lambda i, j: (i, j, 0))
    nj = s // tm
    bias_spec = pl.BlockSpec((tiles_per_step, 2 * GRID_W, 2 * GRID_W),
                             lambda i, j: (jnp.minimum(i * nj + j, bias_blocks - 1), 0, 0))
    outs = pl.pallas_call(
        functools.partial(_in_proj_kernel, tm=tm, sub=sub, da=da, dc=dc, ctx_row=ctx_row,
                          bias_blocks=bias_blocks),
        grid=(b, s // tm),
        in_specs=[tok(d),
                  pl.BlockSpec((1, HALO, d), lambda i, j: (i, jnp.maximum(j * nh - 1, 0), 0)),
                  pl.BlockSpec((1, HALO, d), lambda i, j: (i, jnp.minimum((j + 1) * nh, last_halo), 0)),
                  ctx_spec(d),
                  const(ada.shape),
                  const((1, d)), const(w_in.shape), const(gains.shape),
                  tab, tab, tab, const(bd.shape), const((1, CONV_K, dc)), const((1, dc)),
                  const(rpb_pad.shape)],
        out_specs=[tok(da), tok(da), tok(da), tok(da), tok(da), tok(dc), ctx_spec(da), ctx_spec(da), bias_spec],
        out_shape=[out_tok, out_tok, out_tok, out_tok, out_tok,
                   jax.ShapeDtypeStruct((b, s, dc), BF16), out_ctx, out_ctx,
                   jax.ShapeDtypeStruct((n_tiles, 2 * GRID_W, 2 * GRID_W), F32)],
        scratch_shapes=[pltpu.VMEM((tm + 2 * HALO + cr, d), BF16),
                        pltpu.VMEM((tm // sub, sub + 2 * HALO, dc), F32)],
        compiler_params=pltpu.CompilerParams(vmem_limit_bytes=VMEM_LIMIT),
        name="in_proj",
    )(x, x, x, ctx, ada, norm_g, w_in, gains, cos, slo, shi, bd, conv_w, conv_b, rpb_pad)
    bias = outs[-1].reshape(n_heads // 2, ndr - 1, 2 * GRID_W, 2 * GRID_W)
    return (*outs[:-1], bias)


def _dot_nt(a, b):
    return lax.dot_general(a, b, (((1,), (1,)), ((), ())), preferred_element_type=F32)


def _attn_kernel(qrot_ref, qpl_ref, k_ref, v_ref, kc_ref, vc_ref, ga_ref, bias_ref, o_ref,
                 s_lat_ref, s_ctx_ref, p_lat_ref, p_ctx_ref, *, rows, group, n_batch, steps):
    n_units = n_batch * rows // group
    nk = WIN_H * GRID_W

    lane = lax.broadcasted_iota(jnp.int32, (GRID_W, LANES), 1)
    first_head = lane < HEAD_DIM

    def stack_heads(q2):
        zero = jnp.zeros_like(q2)
        return jnp.concatenate([jnp.where(first_head, q2, zero), jnp.where(first_head, zero, q2)], axis=0)

    def geometry(u, g):
        r = u * group + g
        if isinstance(r, int):
            bb, i = divmod(r, rows)
            rs = min(max(i - WIN_H // 2, 0), rows - WIN_H)
            return bb, i - rs, i * GRID_W, rs * GRID_W
        bb = r // rows
        i = r % rows
        rs = jnp.clip(i - WIN_H // 2, 0, rows - WIN_H)
        return bb, i - rs, pl.multiple_of(i * GRID_W, GRID_W), pl.multiple_of(rs * GRID_W, GRID_W)


    def scores_matmul(u):
        out = []
        for g in range(group):
            bb, off, tok0, key0 = geometry(u, g)
            qs = stack_heads(qrot_ref[bb, pl.ds(tok0, GRID_W), :])
            kband = k_ref[bb, pl.ds(key0, nk), :]
            qp = stack_heads(qpl_ref[bb, pl.ds(tok0, GRID_W), :])
            out.append((_dot_nt(qs, kband), off, _dot_nt(qp, kc_ref[bb])))
        return out


    def scores_store(slot, vals):
        for g, (s_lat, off, s_ctx) in enumerate(vals):
            first = (WIN_H - 1) - off
            bias = jnp.concatenate([bias_ref[0, first + 2 * t] for t in range(WIN_H // 2)], axis=1)
            s_lat_ref[slot, g] = s_lat + bias
            s_ctx_ref[slot, g] = s_ctx

    def softmax(slot):
        for g in range(group):
            s_lat = s_lat_ref[slot, g]
            s_ctx = s_ctx_ref[slot, g]
            m = jnp.maximum(jnp.max(s_lat, axis=-1, keepdims=True), jnp.max(s_ctx, axis=-1, keepdims=True))
            p_lat_ref[slot, g] = jnp.exp2(s_lat - m).astype(BF16)
            p_ctx_ref[slot, g] = jnp.exp2(s_ctx - m).astype(BF16)

    ones_lat = jnp.ones((nk, LANES), BF16)
    ones_ctx = jnp.ones((kc_ref.shape[1], LANES), BF16)

    def pv_matmul(u, slot):
        out = []
        for g in range(group):
            bb, _, _, key0 = geometry(u, g)
            v_lat = jnp.concatenate([v_ref[bb, pl.ds(key0, nk), :], ones_lat], axis=1)
            v_ctx = jnp.concatenate([vc_ref[bb], ones_ctx], axis=1)
            out.append(jnp.dot(p_lat_ref[slot, g], v_lat, preferred_element_type=F32)
                       + jnp.dot(p_ctx_ref[slot, g], v_ctx, preferred_element_type=F32))
        return out

    def pv_store(u, vals):
        for g, ol in enumerate(vals):
            bb, _, tok0, _ = geometry(u, g)
            o = ol[:, 0:LANES] * (1.0 / ol[:, LANES:])
            o2 = jnp.where(first_head, o[0:GRID_W], o[GRID_W:2 * GRID_W])
            gate = ga_ref[bb, pl.ds(tok0, GRID_W), :].astype(F32)
            o_ref[bb, pl.ds(tok0, GRID_W), :] = (o2 * gate).astype(BF16)

    scores_store(0, scores_matmul(0))
    sc = scores_matmul(1)
    softmax(0)
    scores_store(1, sc)

    def step(t, slot):
        sc = scores_matmul(t)
        pv = pv_matmul(t - 2, slot)
        softmax(1 - slot)
        scores_store(slot, sc)
        pv_store(t - 2, pv)

    n_steady, n_left = divmod(n_units - 2, steps)

    def steady(n, carry):
        for j in range(steps):
            step(steps * n + 2 + j, j % 2)
        return carry

    lax.fori_loop(0, n_steady, steady, 0)
    for j in range(n_left):
        step(steps * n_steady + 2 + j, j % 2)

    pv = pv_matmul(n_units - 2, 0)
    softmax(1)
    pv_store(n_units - 2, pv)
    pv_store(n_units - 1, pv_matmul(n_units - 1, 1))


def _attention(qrot, qpl, k, v, kc, vc, ga, bias, *, group, n_batch, steps):
    b, s, da = qrot.shape
    l = vc.shape[1]
    rows = s // GRID_W
    n_hp = da // LANES
    nq = 2 * GRID_W
    nk = WIN_H * GRID_W
    assert rows % (2 * group) == 0 and b % n_batch == 0 and steps % 2 == 0
    tok = pl.BlockSpec((n_batch, s, LANES), lambda hp, i: (i, 0, hp))
    ctx_tok = pl.BlockSpec((n_batch, l, LANES), lambda hp, i: (i, 0, hp))
    return pl.pallas_call(
        functools.partial(_attn_kernel, rows=rows, group=group, n_batch=n_batch, steps=steps),
        grid=(n_hp, b // n_batch),
        in_specs=[tok, tok, tok, tok, ctx_tok, ctx_tok, tok,
                  pl.BlockSpec((1,) + bias.shape[1:], lambda hp, i: (hp, 0, 0, 0))],
        out_specs=tok,
        out_shape=jax.ShapeDtypeStruct((b, s, da), BF16),
        scratch_shapes=[pltpu.VMEM((2, group, nq, nk), F32),
                        pltpu.VMEM((2, group, nq, l), F32),
                        pltpu.VMEM((2, group, nq, nk), BF16),
                        pltpu.VMEM((2, group, nq, l), BF16)],
        compiler_params=pltpu.CompilerParams(vmem_limit_bytes=VMEM_LIMIT),
        name="attn",
    )(qrot, qpl, k, v, kc, vc, ga, bias)


def _out_proj_kernel(x_ref, a_ref, c_ref, ada_ref, w_ref, o_ref, *, da):
    _, _, gate = _ada_row(ada_ref, pl.program_id(0))
    upd = (jnp.dot(a_ref[0], w_ref[0:da, :].astype(BF16), preferred_element_type=F32)
           + jnp.dot(c_ref[0], w_ref[da:, :].astype(BF16), preferred_element_type=F32))
    o_ref[0] = x_ref[0] + gate * upd


def _out_proj(x, attn, conv, ada, w_out, *, tm):
    b, s, d = x.shape
    da = attn.shape[2]
    dc = conv.shape[2]
    return pl.pallas_call(
        functools.partial(_out_proj_kernel, da=da),
        grid=(b, s // tm),
        in_specs=[pl.BlockSpec((1, tm, d), lambda i, j: (i, j, 0)),
                  pl.BlockSpec((1, tm, da), lambda i, j: (i, j, 0)),
                  pl.BlockSpec((1, tm, dc), lambda i, j: (i, j, 0)),
                  pl.BlockSpec(ada.shape, lambda i, j: (0, 0)),
                  pl.BlockSpec(w_out.shape, lambda i, j: (0, 0))],
        out_specs=pl.BlockSpec((1, tm, d), lambda i, j: (i, j, 0)),
        out_shape=jax.ShapeDtypeStruct((b, s, d), F32),
        compiler_params=pltpu.CompilerParams(vmem_limit_bytes=VMEM_LIMIT),
        name="out_proj",
    )(x, attn, conv, ada, w_out)


def _rope_tables(s):
    nf = HEAD_DIM // 4
    inv = (ROPE_THETA ** (-np.arange(nf, dtype=np.float32) / nf)).astype(np.float32)
    pos = np.arange(s)
    lane = np.arange(LANES)
    d = lane % HEAD_DIM
    axis = d // (2 * nf)
    half = (d % (2 * nf)) // nf
    coord = np.where(axis[None, :] == 0, (pos // GRID_W)[:, None], (pos % GRID_W)[:, None]).astype(np.float32)
    ang = (coord * inv[d % nf][None, :]).astype(np.float32)
    cos = np.cos(ang).astype(np.float32)
    sin = np.sin(ang).astype(np.float32)
    sin_lo = np.where(half[None, :] == 0, -sin, 0.0).astype(np.float32)
    sin_hi = np.where(half[None, :] == 1, sin, 0.0).astype(np.float32)
    return jnp.asarray(cos), jnp.asarray(sin_lo), jnp.asarray(sin_hi)


def _bias_tiles(rpb_ref, o_ref, first_tile):
    n_pairs = rpb_ref.shape[1] - 1
    cq = lax.broadcasted_iota(jnp.int32, (GRID_W, LANES), 0)
    ck = lax.broadcasted_iota(jnp.int32, (GRID_W, LANES), 1) % GRID_W
    col_start = jnp.clip(cq - WIN_W // 2, 0, GRID_W - WIN_W)
    valid = (ck >= col_start) & (ck < col_start + WIN_W)
    for e in range(o_ref.shape[0]):
        t = first_tile + e
        hp = t // n_pairs
        dr = t % n_pairs
        for a in range(2):
            two = rpb_ref[2 * hp + a, pl.ds(dr, 2), :]
            lanes = jnp.broadcast_to(jnp.concatenate([two[0:1], two[1:2]], axis=1), (GRID_W, LANES))
            toeplitz = pltpu.roll(lanes, LANES - (WIN_W - 1), 1, stride=1, stride_axis=0)
            o_ref[e, a * GRID_W:(a + 1) * GRID_W, :] = jnp.where(valid, toeplitz * LOG2_E, MASK_VALUE)


def kernel(x, c, ctx, c_ctx, w_ada, b_ada, norm_g, w_in, q_norm_g, k_norm_g, rpb, conv_w, conv_b, w_out):
    depth = w_ada.shape[0]
    b, s, d = x.shape
    dc = conv_w.shape[2]
    da = (w_in.shape[2] - 4 * dc) // 4
    rows = s // GRID_W
    assert depth == 1 and s % GRID_W == 0 and rows >= WIN_H and da % LANES == 0

    cos, slo, shi = _rope_tables(s)
    assert da % MXU_DIM == 0 and MXU_DIM % HEAD_DIM == 0
    seg = np.arange(MXU_DIM) // HEAD_DIM
    bd = jnp.asarray((seg[:, None] == seg[None, :]).astype(np.float32), dtype=BF16)

    ada, w_in_b, gains = _prep(c, c_ctx, w_ada[0], b_ada, w_in[0], q_norm_g, k_norm_g, da, steps=PREP_STEPS)

    qrot, qpl, k, v, ga, conv, kc, vc, bias = _in_proj(x, ctx, ada, b, norm_g, w_in_b, gains, cos, slo, shi, bd,
                                                       conv_w, conv_b, rpb[0], tm=IN_PROJ_ROWS, sub=IN_PROJ_SUB)
    attn = _attention(qrot, qpl, k, v, kc, vc, ga, bias, group=ATTN_GROUP, n_batch=ATTN_BATCH, steps=ATTN_STEPS)
    return _out_proj(x, attn, conv, ada, w_out[0], tm=OUT_PROJ_ROWS)
```

```python
import functools

import numpy as np
import jax
import jax.numpy as jnp
from jax import lax
from jax.experimental import pallas as pl
from jax.experimental.pallas import tpu as pltpu

F32 = jnp.float32
BF16 = jnp.bfloat16

HEAD_DIM = 64
GRID_W = 64
WIN_H = 8
WIN_W = 16
CONV_K = 3
ROPE_THETA = 10000.0
RMS_EPS = 1e-6
MASK_VALUE = -1e30
LOG2_E = 1.4426950408889634

LANES = 128
MXU_DIM = 256
F32_ROWS = 8
BF16_ROWS = 16
HALO = BF16_ROWS
VMEM_CAPACITY = 64 * 1024 * 1024
VMEM_LIMIT = VMEM_CAPACITY * 7 // 8

PREP_STEPS = 4
IN_PROJ_ROWS, IN_PROJ_SUB = 1024, 512
OUT_PROJ_ROWS = 2048
ATTN_GROUP, ATTN_BATCH, ATTN_STEPS = 2, 4, 30


def _silu(z):
    return z * jax.nn.sigmoid(z)


def _prep_kernel(c_ref, cctx_ref, w_ref, b_ref, qg_ref, kg_ref, ada_ref, gains_ref):
    nb, d = c_ref.shape
    cctx = cctx_ref[...].reshape(1, d)
    cond = jnp.concatenate([c_ref[...], jnp.broadcast_to(cctx, (ada_ref.shape[0] - nb, d))], axis=0)
    a = _silu(cond).astype(BF16)
    ada_ref[...] = jnp.dot(a, w_ref[...].astype(BF16), preferred_element_type=F32) + b_ref[...]
    g = jnp.concatenate([qg_ref[...] * (HEAD_DIM ** -0.5 * LOG2_E), kg_ref[...]], axis=0)
    gains_ref[...] = jnp.concatenate([g] * (gains_ref.shape[1] // HEAD_DIM), axis=1)


def _prep(c, c_ctx, w_ada, b_ada, q_gain, k_gain, d_attn, *, steps):
    nb, d = c.shape
    assert nb % F32_ROWS == 0
    rows = nb + F32_ROWS
    n = w_ada.shape[1]
    tn = n // steps
    assert tn % LANES == 0
    return pl.pallas_call(
        _prep_kernel,
        grid=(steps,),
        in_specs=[pl.BlockSpec((nb, d), lambda i: (0, 0)),
                  pl.BlockSpec((d,), lambda i: (0,)),
                  pl.BlockSpec((d, tn), lambda i: (0, i)),
                  pl.BlockSpec((1, tn), lambda i: (0, i)),
                  pl.BlockSpec((1, HEAD_DIM), lambda i: (0, 0)),
                  pl.BlockSpec((1, HEAD_DIM), lambda i: (0, 0))],
        out_specs=[pl.BlockSpec((rows, tn), lambda i: (0, i)),
                   pl.BlockSpec((2, d_attn), lambda i: (0, 0))],
        out_shape=[jax.ShapeDtypeStruct((rows, n), F32),
                   jax.ShapeDtypeStruct((2, d_attn), F32)],
        compiler_params=pltpu.CompilerParams(vmem_limit_bytes=VMEM_LIMIT),
        name="prep",
    )(c, c_ctx, w_ada, b_ada, q_gain, k_gain)


def _ada_row(ada_ref, row):
    r = ada_ref[pl.ds(row, 1), :]
    d = r.shape[1] // 3
    return r[:, 0:d], r[:, d:2 * d], r[:, 2 * d:]


def _modulated_norm(xt, mult, shift):
    ms = jnp.mean(xt * xt, axis=-1, keepdims=True)
    return (xt * lax.rsqrt(ms + RMS_EPS) * mult + shift).astype(BF16)


def _head_norm(t, gain, bd):
    t2 = (t * t).astype(BF16)
    w = bd.shape[0]
    ss = jnp.concatenate([jnp.dot(t2[:, c:c + w], bd, preferred_element_type=F32)
                          for c in range(0, t.shape[1], w)], axis=1)
    return t * lax.rsqrt(ss * (1.0 / HEAD_DIM) + RMS_EPS) * gain


def _rope(t, cos, sin_lo, sin_hi):
    outs = []
    for c in range(t.shape[1] // LANES):
        tc = t[:, c * LANES:(c + 1) * LANES]
        up = pltpu.roll(tc, LANES - HEAD_DIM // 4, 1)
        dn = pltpu.roll(tc, HEAD_DIM // 4, 1)
        outs.append(tc * cos + up * sin_lo + dn * sin_hi)
    return jnp.concatenate(outs, axis=1)


def _in_proj_kernel(x_ref, xp_ref, xn_ref, ctx_ref, ada_ref, ng_ref, w_ref, g_ref,
                    cos_ref, slo_ref, shi_ref, bd_ref, cw_ref, cb_ref, rpb_ref,
                    qrot_ref, qpl_ref, k_ref, v_ref, ga_ref, conv_ref, kc_ref, vc_ref, bias_ref,
                    hext_ref, cu_ref, wb_ref, *, tm, sub, da, dc, ctx_row, bias_blocks):
    j = pl.program_id(1)
    nj = pl.num_programs(1)

    @pl.when((pl.program_id(0) == 0) & (j == 0))
    def _():
        for c in range(0, w_ref.shape[1], MXU_DIM):
            wb_ref[:, c:c + MXU_DIM] = w_ref[:, c:c + MXU_DIM].astype(BF16)

    step = pl.program_id(0) * nj + j
    _bias_tiles(rpb_ref, bias_ref, jnp.minimum(step, bias_blocks - 1) * bias_ref.shape[0])
    shift, scale, _ = _ada_row(ada_ref, pl.program_id(0))
    mult = ng_ref[...] * (1.0 + scale)
    n_sub = tm // sub
    cr = ctx_ref.shape[1]

    def proj(src, lo, width):
        return jnp.dot(hext_ref[src, :], wb_ref[:, lo:lo + width], preferred_element_type=F32)

    bd = bd_ref[...]
    hext_ref[0:HALO, :] = _modulated_norm(xp_ref[0], mult, shift)
    hext_ref[HALO + tm:2 * HALO + tm, :] = _modulated_norm(xn_ref[0], mult, shift)
    shift_c, scale_c, _ = _ada_row(ada_ref, ctx_row)
    hext_ref[2 * HALO + tm:, :] = _modulated_norm(ctx_ref[0], ng_ref[...] * (1.0 + scale_c), shift_c)

    for n in range(n_sub):
        r0 = n * sub
        hext_ref[HALO + r0:HALO + r0 + sub, :] = _modulated_norm(x_ref[0, r0:r0 + sub, :], mult, shift)
    for n in range(n_sub):
        r0 = n * sub
        rows = slice(r0, r0 + sub)
        h = slice(HALO + r0, HALO + r0 + sub)
        with_ctx = n == n_sub - 1
        h_kv = slice(HALO + r0, 2 * HALO + tm + cr) if with_ctx else h
        cos, slo, shi = cos_ref[rows, :], slo_ref[rows, :], shi_ref[rows, :]


        hx = slice(r0, r0 + sub + 2 * HALO)
        cu = proj(hx, 4 * da, dc) * proj(hx, 4 * da + 2 * dc, dc)
        row = lax.broadcasted_iota(jnp.int32, (sub + 2 * HALO, 1), 0) + r0
        inside = ((row >= HALO) | (j > 0)) & ((row < HALO + tm) | (j < nj - 1))
        cu_ref[n] = jnp.where(inside, cu, 0.0)
        y = (cb_ref[...]
             + cw_ref[0, 0:1, :] * cu_ref[n, HALO - 1:HALO - 1 + sub, :]
             + cw_ref[0, 1:2, :] * cu_ref[n, HALO:HALO + sub, :]
             + cw_ref[0, 2:3, :] * cu_ref[n, HALO + 1:HALO + 1 + sub, :])
        bg = proj(h, 4 * da + dc, dc)
        zc = proj(h, 4 * da + 3 * dc, dc)
        conv_ref[0, rows, :] = (bg * y * _silu(zc)).astype(BF16)

        qn = _head_norm(proj(h, 0, da), g_ref[0:1, :], bd)
        qpl_ref[0, rows, :] = qn.astype(BF16)
        qrot_ref[0, rows, :] = _rope(qn, cos, slo, shi).astype(BF16)

        kf = proj(h_kv, da, da)
        kn = _head_norm(kf[0:sub], g_ref[1:2, :], bd)
        k_ref[0, rows, :] = _rope(kn, cos, slo, shi).astype(BF16)
        if with_ctx:
            kc_ref[0] = _head_norm(kf[sub + HALO:], g_ref[1:2, :], bd).astype(BF16)

        ga_ref[0, rows, :] = _silu(proj(h, 3 * da, da)).astype(BF16)
        vf = proj(h_kv, 2 * da, da)
        v_ref[0, rows, :] = vf[0:sub].astype(BF16)
        if with_ctx:
            vc_ref[0] = vf[sub + HALO:].astype(BF16)


def _in_proj(x, ctx, ada, ctx_row, norm_g, w_in, gains, cos, slo, shi, bd, conv_w, conv_b, rpb, *, tm, sub):
    b, s, d = x.shape
    n_heads, ndr, ndc = rpb.shape
    assert ndr == 2 * WIN_H - 1 and ndc == 2 * WIN_W - 1 and ndc <= GRID_W
    rpb_pad = jnp.pad(rpb, ((0, 0), (0, 0), (0, GRID_W - ndc)))
    n_tiles = (n_heads // 2) * (ndr - 1)
    bias_blocks = max(k for k in range(1, b * (s // tm) + 1) if n_tiles % k == 0)
    tiles_per_step = n_tiles // bias_blocks
    l = ctx.shape[1]
    assert s % tm == 0 and tm % sub == 0 and sub % HALO == 0
    cr = l // (s // tm)
    assert l % (s // tm) == 0 and cr % BF16_ROWS == 0
    dc = conv_w.shape[2]
    da = (w_in.shape[1] - 4 * dc) // 4
    nh = tm // HALO
    last_halo = s // HALO - 1
    tok = lambda width: pl.BlockSpec((1, tm, width), lambda i, j: (i, j, 0))
    const = lambda shape: pl.BlockSpec(shape, lambda i, j: (0,) * len(shape))
    tab = pl.BlockSpec((tm, LANES), lambda i, j: (j, 0))
    out_tok = jax.ShapeDtypeStruct((b, s, da), BF16)
    out_ctx = jax.ShapeDtypeStruct((b, l, da), BF16)
    ctx_spec = lambda width: pl.BlockSpec((1, cr, width), lambda i, j: (i, j, 0))
    nj = s // tm
    bias_spec = pl.BlockSpec((tiles_per_step, 2 * GRID_W, 2 * GRID_W),
                             lambda i, j: (jnp.minimum(i * nj + j, bias_blocks - 1), 0, 0))
    outs = pl.pallas_call(
        functools.partial(_in_proj_kernel, tm=tm, sub=sub, da=da, dc=dc, ctx_row=ctx_row,
                          bias_blocks=bias_blocks),
        grid=(b, s // tm),
        in_specs=[tok(d),
                  pl.BlockSpec((1, HALO, d), lambda i, j: (i, jnp.maximum(j * nh - 1, 0), 0)),
                  pl.BlockSpec((1, HALO, d), lambda i, j: (i, jnp.minimum((j + 1) * nh, last_halo), 0)),
                  ctx_spec(d),
                  const(ada.shape),
                  const((1, d)), const(w_in.shape), const(gains.shape),
                  tab, tab, tab, const(bd.shape), const((1, CONV_K, dc)), const((1, dc)),
                  const(rpb_pad.shape)],
        out_specs=[tok(da), tok(da), tok(da), tok(da), tok(da), tok(dc), ctx_spec(da), ctx_spec(da), bias_spec],
        out_shape=[out_tok, out_tok, out_tok, out_tok, out_tok,
                   jax.ShapeDtypeStruct((b, s, dc), BF16), out_ctx, out_ctx,
                   jax.ShapeDtypeStruct((n_tiles, 2 * GRID_W, 2 * GRID_W), F32)],
        scratch_shapes=[pltpu.VMEM((tm + 2 * HALO + cr, d), BF16),
                        pltpu.VMEM((tm // sub, sub + 2 * HALO, dc), F32),
                        pltpu.VMEM(w_in.shape, BF16)],
        compiler_params=pltpu.CompilerParams(vmem_limit_bytes=VMEM_LIMIT,
                                             dimension_semantics=("arbitrary", "arbitrary")),
        name="in_proj",
    )(x, x, x, ctx, ada, norm_g, w_in, gains, cos, slo, shi, bd, conv_w, conv_b, rpb_pad)
    bias = outs[-1].reshape(n_heads // 2, ndr - 1, 2 * GRID_W, 2 * GRID_W)
    return (*outs[:-1], bias)


def _dot_nt(a, b):
    return lax.dot_general(a, b, (((1,), (1,)), ((), ())), preferred_element_type=F32)


def _attn_kernel(qrot_ref, qpl_ref, k_ref, v_ref, kc_ref, vc_ref, ga_ref, bias_ref, o_ref,
                 s_lat_ref, s_ctx_ref, p_lat_ref, p_ctx_ref, *, rows, group, n_batch, steps):
    n_units = n_batch * rows // group
    nk = WIN_H * GRID_W

    lane = lax.broadcasted_iota(jnp.int32, (GRID_W, LANES), 1)
    first_head = lane < HEAD_DIM

    def stack_heads(q2):
        zero = jnp.zeros_like(q2)
        return jnp.concatenate([jnp.where(first_head, q2, zero), jnp.where(first_head, zero, q2)], axis=0)

    def geometry(u, g):
        r = u * group + g
        if isinstance(r, int):
            bb, i = divmod(r, rows)
            rs = min(max(i - WIN_H // 2, 0), rows - WIN_H)
            return bb, i - rs, i * GRID_W, rs * GRID_W
        bb = r // rows
        i = r % rows
        rs = jnp.clip(i - WIN_H // 2, 0, rows - WIN_H)
        return bb, i - rs, pl.multiple_of(i * GRID_W, GRID_W), pl.multiple_of(rs * GRID_W, GRID_W)


    def scores_matmul(u):
        out = []
        for g in range(group):
            bb, off, tok0, key0 = geometry(u, g)
            qs = stack_heads(qrot_ref[bb, pl.ds(tok0, GRID_W), :])
            kband = k_ref[bb, pl.ds(key0, nk), :]
            qp = stack_heads(qpl_ref[bb, pl.ds(tok0, GRID_W), :])
            out.append((_dot_nt(qs, kband), off, _dot_nt(qp, kc_ref[bb])))
        return out


    def scores_store(slot, vals):
        for g, (s_lat, off, s_ctx) in enumerate(vals):
            first = (WIN_H - 1) - off
            bias = jnp.concatenate([bias_ref[0, first + 2 * t] for t in range(WIN_H // 2)], axis=1)
            s_lat_ref[slot, g] = s_lat + bias
            s_ctx_ref[slot, g] = s_ctx

    def softmax(slot):
        for g in range(group):
            s_lat = s_lat_ref[slot, g]
            s_ctx = s_ctx_ref[slot, g]
            m = jnp.maximum(jnp.max(s_lat, axis=-1, keepdims=True), jnp.max(s_ctx, axis=-1, keepdims=True))
            p_lat_ref[slot, g] = jnp.exp2(s_lat - m).astype(BF16)
            p_ctx_ref[slot, g] = jnp.exp2(s_ctx - m).astype(BF16)

    ones_lat = jnp.ones((nk, LANES), BF16)
    ones_ctx = jnp.ones((kc_ref.shape[1], LANES), BF16)

    def pv_matmul(u, slot):
        out = []
        for g in range(group):
            bb, _, _, key0 = geometry(u, g)
            v_lat = jnp.concatenate([v_ref[bb, pl.ds(key0, nk), :], ones_lat], axis=1)
            v_ctx = jnp.concatenate([vc_ref[bb], ones_ctx], axis=1)
            out.append(jnp.dot(p_lat_ref[slot, g], v_lat, preferred_element_type=F32)
                       + jnp.dot(p_ctx_ref[slot, g], v_ctx, preferred_element_type=F32))
        return out

    def pv_store(u, vals):
        for g, ol in enumerate(vals):
            bb, _, tok0, _ = geometry(u, g)
            o = ol[:, 0:LANES] * (1.0 / ol[:, LANES:])
            o2 = jnp.where(first_head, o[0:GRID_W], o[GRID_W:2 * GRID_W])
            gate = ga_ref[bb, pl.ds(tok0, GRID_W), :].astype(F32)
            o_ref[bb, pl.ds(tok0, GRID_W), :] = (o2 * gate).astype(BF16)

    scores_store(0, scores_matmul(0))
    sc = scores_matmul(1)
    softmax(0)
    scores_store(1, sc)

    def step(t, slot):
        sc = scores_matmul(t)
        pv = pv_matmul(t - 2, slot)
        softmax(1 - slot)
        scores_store(slot, sc)
        pv_store(t - 2, pv)

    n_steady, n_left = divmod(n_units - 2, steps)

    def steady(n, carry):
        for j in range(steps):
            step(steps * n + 2 + j, j % 2)
        return carry

    lax.fori_loop(0, n_steady, steady, 0)
    for j in range(n_left):
        step(steps * n_steady + 2 + j, j % 2)

    pv = pv_matmul(n_units - 2, 0)
    softmax(1)
    pv_store(n_units - 2, pv)
    pv_store(n_units - 1, pv_matmul(n_units - 1, 1))


def _attention(qrot, qpl, k, v, kc, vc, ga, bias, *, group, n_batch, steps):
    b, s, da = qrot.shape
    l = vc.shape[1]
    rows = s // GRID_W
    n_hp = da // LANES
    nq = 2 * GRID_W
    nk = WIN_H * GRID_W
    assert rows % (2 * group) == 0 and b % n_batch == 0 and steps % 2 == 0
    tok = pl.BlockSpec((n_batch, s, LANES), lambda hp, i: (i, 0, hp))
    ctx_tok = pl.BlockSpec((n_batch, l, LANES), lambda hp, i: (i, 0, hp))
    return pl.pallas_call(
        functools.partial(_attn_kernel, rows=rows, group=group, n_batch=n_batch, steps=steps),
        grid=(n_hp, b // n_batch),
        in_specs=[tok, tok, tok, tok, ctx_tok, ctx_tok, tok,
                  pl.BlockSpec((1,) + bias.shape[1:], lambda hp, i: (hp, 0, 0, 0))],
        out_specs=tok,
        out_shape=jax.ShapeDtypeStruct((b, s, da), BF16),
        scratch_shapes=[pltpu.VMEM((2, group, nq, nk), F32),
                        pltpu.VMEM((2, group, nq, l), F32),
                        pltpu.VMEM((2, group, nq, nk), BF16),
                        pltpu.VMEM((2, group, nq, l), BF16)],
        compiler_params=pltpu.CompilerParams(vmem_limit_bytes=VMEM_LIMIT),
        name="attn",
    )(qrot, qpl, k, v, kc, vc, ga, bias)


def _out_proj_kernel(x_ref, a_ref, c_ref, ada_ref, w_ref, o_ref, *, da):
    _, _, gate = _ada_row(ada_ref, pl.program_id(0))
    upd = (jnp.dot(a_ref[0], w_ref[0:da, :].astype(BF16), preferred_element_type=F32)
           + jnp.dot(c_ref[0], w_ref[da:, :].astype(BF16), preferred_element_type=F32))
    o_ref[0] = x_ref[0] + gate * upd


def _out_proj(x, attn, conv, ada, w_out, *, tm):
    b, s, d = x.shape
    da = attn.shape[2]
    dc = conv.shape[2]
    return pl.pallas_call(
        functools.partial(_out_proj_kernel, da=da),
        grid=(b, s // tm),
        in_specs=[pl.BlockSpec((1, tm, d), lambda i, j: (i, j, 0)),
                  pl.BlockSpec((1, tm, da), lambda i, j: (i, j, 0)),
                  pl.BlockSpec((1, tm, dc), lambda i, j: (i, j, 0)),
                  pl.BlockSpec(ada.shape, lambda i, j: (0, 0)),
                  pl.BlockSpec(w_out.shape, lambda i, j: (0, 0))],
        out_specs=pl.BlockSpec((1, tm, d), lambda i, j: (i, j, 0)),
        out_shape=jax.ShapeDtypeStruct((b, s, d), F32),
        compiler_params=pltpu.CompilerParams(vmem_limit_bytes=VMEM_LIMIT),
        name="out_proj",
    )(x, attn, conv, ada, w_out)


def _rope_tables(s):
    nf = HEAD_DIM // 4
    inv = (ROPE_THETA ** (-np.arange(nf, dtype=np.float32) / nf)).astype(np.float32)
    pos = np.arange(s)
    lane = np.arange(LANES)
    d = lane % HEAD_DIM
    axis = d // (2 * nf)
    half = (d % (2 * nf)) // nf
    coord = np.where(axis[None, :] == 0, (pos // GRID_W)[:, None], (pos % GRID_W)[:, None]).astype(np.float32)
    ang = (coord * inv[d % nf][None, :]).astype(np.float32)
    cos = np.cos(ang).astype(np.float32)
    sin = np.sin(ang).astype(np.float32)
    sin_lo = np.where(half[None, :] == 0, -sin, 0.0).astype(np.float32)
    sin_hi = np.where(half[None, :] == 1, sin, 0.0).astype(np.float32)
    return jnp.asarray(cos), jnp.asarray(sin_lo), jnp.asarray(sin_hi)


def _bias_tiles(rpb_ref, o_ref, first_tile):
    n_pairs = rpb_ref.shape[1] - 1
    cq = lax.broadcasted_iota(jnp.int32, (GRID_W, LANES), 0)
    ck = lax.broadcasted_iota(jnp.int32, (GRID_W, LANES), 1) % GRID_W
    col_start = jnp.clip(cq - WIN_W // 2, 0, GRID_W - WIN_W)
    valid = (ck >= col_start) & (ck < col_start + WIN_W)
    for e in range(o_ref.shape[0]):
        t = first_tile + e
        hp = t // n_pairs
        dr = t % n_pairs
        for a in range(2):
            two = rpb_ref[2 * hp + a, pl.ds(dr, 2), :]
            lanes = jnp.broadcast_to(jnp.concatenate([two[0:1], two[1:2]], axis=1), (GRID_W, LANES))
            toeplitz = pltpu.roll(lanes, LANES - (WIN_W - 1), 1, stride=1, stride_axis=0)
            o_ref[e, a * GRID_W:(a + 1) * GRID_W, :] = jnp.where(valid, toeplitz * LOG2_E, MASK_VALUE)


def kernel(x, c, ctx, c_ctx, w_ada, b_ada, norm_g, w_in, q_norm_g, k_norm_g, rpb, conv_w, conv_b, w_out):
    depth = w_ada.shape[0]
    b, s, d = x.shape
    dc = conv_w.shape[2]
    da = (w_in.shape[2] - 4 * dc) // 4
    rows = s // GRID_W
    assert depth == 1 and s % GRID_W == 0 and rows >= WIN_H and da % LANES == 0

    cos, slo, shi = _rope_tables(s)
    assert da % MXU_DIM == 0 and MXU_DIM % HEAD_DIM == 0
    seg = np.arange(MXU_DIM) // HEAD_DIM
    bd = jnp.asarray((seg[:, None] == seg[None, :]).astype(np.float32), dtype=BF16)

    ada, gains = _prep(c, c_ctx, w_ada[0], b_ada, q_norm_g, k_norm_g, da, steps=PREP_STEPS)

    qrot, qpl, k, v, ga, conv, kc, vc, bias = _in_proj(x, ctx, ada, b, norm_g, w_in[0], gains, cos, slo, shi, bd,
                                                       conv_w, conv_b, rpb[0], tm=IN_PROJ_ROWS, sub=IN_PROJ_SUB)
    attn = _attention(qrot, qpl, k, v, kc, vc, ga, bias, group=ATTN_GROUP, n_batch=ATTN_BATCH, steps=ATTN_STEPS)
    return _out_proj(x, attn, conv, ada, w_out[0], tm=OUT_PROJ_ROWS)
```

```python
import functools

import numpy as np
import jax
import jax.numpy as jnp
from jax import lax
from jax.experimental import pallas as pl
from jax.experimental.pallas import tpu as pltpu

F32 = jnp.float32
BF16 = jnp.bfloat16

HEAD_DIM = 64
GRID_W = 64
WIN_H = 8
WIN_W = 16
CONV_K = 3
ROPE_THETA = 10000.0
RMS_EPS = 1e-6
MASK_VALUE = -1e30
LOG2_E = 1.4426950408889634

LANES = 128
MXU_DIM = 256
F32_ROWS = 8
BF16_ROWS = 16
HALO = BF16_ROWS
VMEM_CAPACITY = 64 * 1024 * 1024
VMEM_LIMIT = VMEM_CAPACITY * 7 // 8

PREP_STEPS = 8
IN_PROJ_ROWS, IN_PROJ_SUB = 1024, 512
OUT_PROJ_ROWS = 2048
ATTN_GROUP, ATTN_BATCH, ATTN_STEPS = 2, 4, 30


def _silu(z):
    return z * jax.nn.sigmoid(z)


def _prep_kernel(c_ref, cctx_ref, w_ref, b_ref, qg_ref, kg_ref, ada_ref, gains_ref):
    nb, d = c_ref.shape
    cctx = cctx_ref[...].reshape(1, d)
    cond = jnp.concatenate([c_ref[...], jnp.broadcast_to(cctx, (ada_ref.shape[0] - nb, d))], axis=0)
    a = _silu(cond).astype(BF16)
    ada_ref[...] = jnp.dot(a, w_ref[...].astype(BF16), preferred_element_type=F32) + b_ref[...]
    g = jnp.concatenate([qg_ref[...] * (HEAD_DIM ** -0.5 * LOG2_E), kg_ref[...]], axis=0)
    gains_ref[...] = jnp.concatenate([g] * (gains_ref.shape[1] // HEAD_DIM), axis=1)


def _prep(c, c_ctx, w_ada, b_ada, q_gain, k_gain, d_attn, *, steps):
    nb, d = c.shape
    assert nb % F32_ROWS == 0
    rows = nb + F32_ROWS
    n = w_ada.shape[1]
    tn = n // steps
    assert tn % LANES == 0
    return pl.pallas_call(
        _prep_kernel,
        grid=(steps,),
        in_specs=[pl.BlockSpec((nb, d), lambda i: (0, 0)),
                  pl.BlockSpec((d,), lambda i: (0,)),
                  pl.BlockSpec((d, tn), lambda i: (0, i)),
                  pl.BlockSpec((1, tn), lambda i: (0, i)),
                  pl.BlockSpec((1, HEAD_DIM), lambda i: (0, 0)),
                  pl.BlockSpec((1, HEAD_DIM), lambda i: (0, 0))],
        out_specs=[pl.BlockSpec((rows, tn), lambda i: (0, i)),
                   pl.BlockSpec((2, d_attn), lambda i: (0, 0))],
        out_shape=[jax.ShapeDtypeStruct((rows, n), F32),
                   jax.ShapeDtypeStruct((2, d_attn), F32)],
        compiler_params=pltpu.CompilerParams(vmem_limit_bytes=VMEM_LIMIT),
        name="prep",
    )(c, c_ctx, w_ada, b_ada, q_gain, k_gain)


def _ada_row(ada_ref, row):
    r = ada_ref[pl.ds(row, 1), :]
    d = r.shape[1] // 3
    return r[:, 0:d], r[:, d:2 * d], r[:, 2 * d:]


def _modulated_norm(xt, mult, shift):
    ms = jnp.mean(xt * xt, axis=-1, keepdims=True)
    return (xt * lax.rsqrt(ms + RMS_EPS) * mult + shift).astype(BF16)


def _head_norm(t, gain, bd):
    t2 = (t * t).astype(BF16)
    w = bd.shape[0]
    ss = jnp.concatenate([jnp.dot(t2[:, c:c + w], bd, preferred_element_type=F32)
                          for c in range(0, t.shape[1], w)], axis=1)
    return t * lax.rsqrt(ss * (1.0 / HEAD_DIM) + RMS_EPS) * gain


def _rope(t, cos, sin_lo, sin_hi):
    outs = []
    for c in range(t.shape[1] // LANES):
        tc = t[:, c * LANES:(c + 1) * LANES]
        up = pltpu.roll(tc, LANES - HEAD_DIM // 4, 1)
        dn = pltpu.roll(tc, HEAD_DIM // 4, 1)
        outs.append(tc * cos + up * sin_lo + dn * sin_hi)
    return jnp.concatenate(outs, axis=1)


def _in_proj_kernel(x_ref, xp_ref, xn_ref, ctx_ref, ada_ref, ng_ref, w_ref, g_ref,
                    cos_ref, slo_ref, shi_ref, bd_ref, cw_ref, cb_ref, rpb_ref,
                    qrot_ref, qpl_ref, k_ref, v_ref, ga_ref, conv_ref, kc_ref, vc_ref, bias_ref,
                    hext_ref, cu_ref, wb_ref, *, tm, sub, da, dc, ctx_row, bias_blocks):
    j = pl.program_id(1)
    nj = pl.num_programs(1)

    @pl.when((pl.program_id(0) == 0) & (j == 0))
    def _():
        for c in range(0, w_ref.shape[1], MXU_DIM):
            wb_ref[:, c:c + MXU_DIM] = w_ref[:, c:c + MXU_DIM].astype(BF16)

    step = pl.program_id(0) * nj + j
    _bias_tiles(rpb_ref, bias_ref, jnp.minimum(step, bias_blocks - 1) * bias_ref.shape[0])
    shift, scale, _ = _ada_row(ada_ref, pl.program_id(0))
    mult = ng_ref[...] * (1.0 + scale)
    n_sub = tm // sub
    cr = ctx_ref.shape[1]

    def proj(src, lo, width):
        return jnp.dot(hext_ref[src, :], wb_ref[:, lo:lo + width], preferred_element_type=F32)

    bd = bd_ref[...]
    hext_ref[0:HALO, :] = _modulated_norm(xp_ref[0], mult, shift)
    hext_ref[HALO + tm:2 * HALO + tm, :] = _modulated_norm(xn_ref[0], mult, shift)
    shift_c, scale_c, _ = _ada_row(ada_ref, ctx_row)
    hext_ref[2 * HALO + tm:, :] = _modulated_norm(ctx_ref[0], ng_ref[...] * (1.0 + scale_c), shift_c)

    for n in range(n_sub):
        r0 = n * sub
        hext_ref[HALO + r0:HALO + r0 + sub, :] = _modulated_norm(x_ref[0, r0:r0 + sub, :], mult, shift)
    for n in range(n_sub):
        r0 = n * sub
        rows = slice(r0, r0 + sub)
        h = slice(HALO + r0, HALO + r0 + sub)
        with_ctx = n == n_sub - 1
        h_kv = slice(HALO + r0, 2 * HALO + tm + cr) if with_ctx else h
        cos, slo, shi = cos_ref[rows, :], slo_ref[rows, :], shi_ref[rows, :]


        hx = slice(r0, r0 + sub + 2 * HALO)
        cu = proj(hx, 4 * da, dc) * proj(hx, 4 * da + 2 * dc, dc)
        row = lax.broadcasted_iota(jnp.int32, (sub + 2 * HALO, 1), 0) + r0
        inside = ((row >= HALO) | (j > 0)) & ((row < HALO + tm) | (j < nj - 1))
        cu_ref[n] = jnp.where(inside, cu, 0.0)
        y = (cb_ref[...]
             + cw_ref[0, 0:1, :] * cu_ref[n, HALO - 1:HALO - 1 + sub, :]
             + cw_ref[0, 1:2, :] * cu_ref[n, HALO:HALO + sub, :]
             + cw_ref[0, 2:3, :] * cu_ref[n, HALO + 1:HALO + 1 + sub, :])
        bg = proj(h, 4 * da + dc, dc)
        zc = proj(h, 4 * da + 3 * dc, dc)
        conv_ref[0, rows, :] = (bg * y * _silu(zc)).astype(BF16)

        qn = _head_norm(proj(h, 0, da), g_ref[0:1, :], bd)
        qpl_ref[0, rows, :] = qn.astype(BF16)
        qrot_ref[0, rows, :] = _rope(qn, cos, slo, shi).astype(BF16)

        kf = proj(h_kv, da, da)
        kn = _head_norm(kf[0:sub], g_ref[1:2, :], bd)
        k_ref[0, rows, :] = _rope(kn, cos, slo, shi).astype(BF16)
        if with_ctx:
            kc_ref[0] = _head_norm(kf[sub + HALO:], g_ref[1:2, :], bd).astype(BF16)

        ga_ref[0, rows, :] = _silu(proj(h, 3 * da, da)).astype(BF16)
        vf = proj(h_kv, 2 * da, da)
        v_ref[0, rows, :] = vf[0:sub].astype(BF16)
        if with_ctx:
            vc_ref[0] = vf[sub + HALO:].astype(BF16)


def _in_proj(x, ctx, ada, ctx_row, norm_g, w_in, gains, cos, slo, shi, bd, conv_w, conv_b, rpb, *, tm, sub):
    b, s, d = x.shape
    n_heads, ndr, ndc = rpb.shape
    assert ndr == 2 * WIN_H - 1 and ndc == 2 * WIN_W - 1 and ndc <= GRID_W
    rpb_pad = jnp.pad(rpb, ((0, 0), (0, 0), (0, GRID_W - ndc)))
    n_tiles = (n_heads // 2) * (ndr - 1)
    bias_blocks = max(k for k in range(1, b * (s // tm) + 1) if n_tiles % k == 0)
    tiles_per_step = n_tiles // bias_blocks
    l = ctx.shape[1]
    assert s % tm == 0 and tm % sub == 0 and sub % HALO == 0
    cr = l // (s // tm)
    assert l % (s // tm) == 0 and cr % BF16_ROWS == 0
    dc = conv_w.shape[2]
    da = (w_in.shape[1] - 4 * dc) // 4
    nh = tm // HALO
    last_halo = s // HALO - 1
    tok = lambda width: pl.BlockSpec((1, tm, width), lambda i, j: (i, j, 0))
    const = lambda shape: pl.BlockSpec(shape, lambda i, j: (0,) * len(shape))
    tab = pl.BlockSpec((tm, LANES), lambda i, j: (j, 0))
    out_tok = jax.ShapeDtypeStruct((b, s, da), BF16)
    out_ctx = jax.ShapeDtypeStruct((b, l, da), BF16)
    ctx_spec = lambda width: pl.BlockSpec((1, cr, width), lambda i, j: (i, j, 0))
    nj = s // tm
    bias_spec = pl.BlockSpec((tiles_per_step, 2 * GRID_W, 2 * GRID_W),
                             lambda i, j: (jnp.minimum(i * nj + j, bias_blocks - 1), 0, 0))
    outs = pl.pallas_call(
        functools.partial(_in_proj_kernel, tm=tm, sub=sub, da=da, dc=dc, ctx_row=ctx_row,
                          bias_blocks=bias_blocks),
        grid=(b, s // tm),
        in_specs=[tok(d),
                  pl.BlockSpec((1, HALO, d), lambda i, j: (i, jnp.maximum(j * nh - 1, 0), 0)),
                  pl.BlockSpec((1, HALO, d), lambda i, j: (i, jnp.minimum((j + 1) * nh, last_halo), 0)),
                  ctx_spec(d),
                  const(ada.shape),
                  const((1, d)), const(w_in.shape), const(gains.shape),
                  tab, tab, tab, const(bd.shape), const((1, CONV_K, dc)), const((1, dc)),
                  const(rpb_pad.shape)],
        out_specs=[tok(da), tok(da), tok(da), tok(da), tok(da), tok(dc), ctx_spec(da), ctx_spec(da), bias_spec],
        out_shape=[out_tok, out_tok, out_tok, out_tok, out_tok,
                   jax.ShapeDtypeStruct((b, s, dc), BF16), out_ctx, out_ctx,
                   jax.ShapeDtypeStruct((n_tiles, 2 * GRID_W, 2 * GRID_W), F32)],
        scratch_shapes=[pltpu.VMEM((tm + 2 * HALO + cr, d), BF16),
                        pltpu.VMEM((tm // sub, sub + 2 * HALO, dc), F32),
                        pltpu.VMEM(w_in.shape, BF16)],
        compiler_params=pltpu.CompilerParams(vmem_limit_bytes=VMEM_LIMIT,
                                             dimension_semantics=("arbitrary", "arbitrary")),
        name="in_proj",
    )(x, x, x, ctx, ada, norm_g, w_in, gains, cos, slo, shi, bd, conv_w, conv_b, rpb_pad)
    bias = outs[-1].reshape(n_heads // 2, ndr - 1, 2 * GRID_W, 2 * GRID_W)
    return (*outs[:-1], bias)


def _dot_nt(a, b):
    return lax.dot_general(a, b, (((1,), (1,)), ((), ())), preferred_element_type=F32)


def _attn_kernel(qrot_ref, qpl_ref, k_ref, v_ref, kc_ref, vc_ref, ga_ref, bias_ref, o_ref,
                 s_lat_ref, s_ctx_ref, p_lat_ref, p_ctx_ref, *, rows, group, n_batch, steps):
    n_units = n_batch * rows // group
    nk = WIN_H * GRID_W

    lane = lax.broadcasted_iota(jnp.int32, (GRID_W, LANES), 1)
    first_head = lane < HEAD_DIM

    def stack_heads(q2):
        zero = jnp.zeros_like(q2)
        return jnp.concatenate([jnp.where(first_head, q2, zero), jnp.where(first_head, zero, q2)], axis=0)

    def geometry(u, g):
        r = u * group + g
        if isinstance(r, int):
            bb, i = divmod(r, rows)
            rs = min(max(i - WIN_H // 2, 0), rows - WIN_H)
            return bb, i - rs, i * GRID_W, rs * GRID_W
        bb = r // rows
        i = r % rows
        rs = jnp.clip(i - WIN_H // 2, 0, rows - WIN_H)
        return bb, i - rs, pl.multiple_of(i * GRID_W, GRID_W), pl.multiple_of(rs * GRID_W, GRID_W)


    def scores_matmul(u):
        out = []
        for g in range(group):
            bb, off, tok0, key0 = geometry(u, g)
            qs = stack_heads(qrot_ref[bb, pl.ds(tok0, GRID_W), :])
            kband = k_ref[bb, pl.ds(key0, nk), :]
            qp = stack_heads(qpl_ref[bb, pl.ds(tok0, GRID_W), :])
            out.append((_dot_nt(qs, kband), off, _dot_nt(qp, kc_ref[bb])))
        return out


    def scores_store(slot, vals):
        for g, (s_lat, off, s_ctx) in enumerate(vals):
            first = (WIN_H - 1) - off
            bias = jnp.concatenate([bias_ref[0, first + 2 * t] for t in range(WIN_H // 2)], axis=1)
            s_lat_ref[slot, g] = s_lat + bias
            s_ctx_ref[slot, g] = s_ctx

    def softmax(slot):
        for g in range(group):
            s_lat = s_lat_ref[slot, g]
            s_ctx = s_ctx_ref[slot, g]
            m = jnp.maximum(jnp.max(s_lat, axis=-1, keepdims=True), jnp.max(s_ctx, axis=-1, keepdims=True))
            p_lat_ref[slot, g] = jnp.exp2(s_lat - m).astype(BF16)
            p_ctx_ref[slot, g] = jnp.exp2(s_ctx - m).astype(BF16)

    ones_lat = jnp.ones((nk, LANES), BF16)
    ones_ctx = jnp.ones((kc_ref.shape[1], LANES), BF16)

    def pv_matmul(u, slot):
        out = []
        for g in range(group):
            bb, _, _, key0 = geometry(u, g)
            v_lat = jnp.concatenate([v_ref[bb, pl.ds(key0, nk), :], ones_lat], axis=1)
            v_ctx = jnp.concatenate([vc_ref[bb], ones_ctx], axis=1)
            out.append(jnp.dot(p_lat_ref[slot, g], v_lat, preferred_element_type=F32)
                       + jnp.dot(p_ctx_ref[slot, g], v_ctx, preferred_element_type=F32))
        return out

    def pv_store(u, vals):
        for g, ol in enumerate(vals):
            bb, _, tok0, _ = geometry(u, g)
            o = ol[:, 0:LANES] * (1.0 / ol[:, LANES:])
            o2 = jnp.where(first_head, o[0:GRID_W], o[GRID_W:2 * GRID_W])
            gate = ga_ref[bb, pl.ds(tok0, GRID_W), :].astype(F32)
            o_ref[bb, pl.ds(tok0, GRID_W), :] = (o2 * gate).astype(BF16)

    scores_store(0, scores_matmul(0))
    sc = scores_matmul(1)
    softmax(0)
    scores_store(1, sc)

    def step(t, slot):
        sc = scores_matmul(t)
        pv = pv_matmul(t - 2, slot)
        softmax(1 - slot)
        scores_store(slot, sc)
        pv_store(t - 2, pv)

    n_steady, n_left = divmod(n_units - 2, steps)

    def steady(n, carry):
        for j in range(steps):
            step(steps * n + 2 + j, j % 2)
        return carry

    lax.fori_loop(0, n_steady, steady, 0)
    for j in range(n_left):
        step(steps * n_steady + 2 + j, j % 2)

    pv = pv_matmul(n_units - 2, 0)
    softmax(1)
    pv_store(n_units - 2, pv)
    pv_store(n_units - 1, pv_matmul(n_units - 1, 1))


def _attention(qrot, qpl, k, v, kc, vc, ga, bias, *, group, n_batch, steps):
    b, s, da = qrot.shape
    l = vc.shape[1]
    rows = s // GRID_W
    n_hp = da // LANES
    nq = 2 * GRID_W
    nk = WIN_H * GRID_W
    assert rows % (2 * group) == 0 and b % n_batch == 0 and steps % 2 == 0
    tok = pl.BlockSpec((n_batch, s, LANES), lambda hp, i: (i, 0, hp))
    ctx_tok = pl.BlockSpec((n_batch, l, LANES), lambda hp, i: (i, 0, hp))
    return pl.pallas_call(
        functools.partial(_attn_kernel, rows=rows, group=group, n_batch=n_batch, steps=steps),
        grid=(n_hp, b // n_batch),
        in_specs=[tok, tok, tok, tok, ctx_tok, ctx_tok, tok,
                  pl.BlockSpec((1,) + bias.shape[1:], lambda hp, i: (hp, 0, 0, 0))],
        out_specs=tok,
        out_shape=jax.ShapeDtypeStruct((b, s, da), BF16),
        scratch_shapes=[pltpu.VMEM((2, group, nq, nk), F32),
                        pltpu.VMEM((2, group, nq, l), F32),
                        pltpu.VMEM((2, group, nq, nk), BF16),
                        pltpu.VMEM((2, group, nq, l), BF16)],
        compiler_params=pltpu.CompilerParams(vmem_limit_bytes=VMEM_LIMIT),
        name="attn",
    )(qrot, qpl, k, v, kc, vc, ga, bias)


def _out_proj_kernel(x_ref, a_ref, c_ref, ada_ref, w_ref, o_ref, *, da):
    _, _, gate = _ada_row(ada_ref, pl.program_id(0))
    upd = (jnp.dot(a_ref[0], w_ref[0:da, :].astype(BF16), preferred_element_type=F32)
           + jnp.dot(c_ref[0], w_ref[da:, :].astype(BF16), preferred_element_type=F32))
    o_ref[0] = x_ref[0] + gate * upd


def _out_proj(x, attn, conv, ada, w_out, *, tm):
    b, s, d = x.shape
    da = attn.shape[2]
    dc = conv.shape[2]
    return pl.pallas_call(
        functools.partial(_out_proj_kernel, da=da),
        grid=(b, s // tm),
        in_specs=[pl.BlockSpec((1, tm, d), lambda i, j: (i, j, 0)),
                  pl.BlockSpec((1, tm, da), lambda i, j: (i, j, 0)),
                  pl.BlockSpec((1, tm, dc), lambda i, j: (i, j, 0)),
                  pl.BlockSpec(ada.shape, lambda i, j: (0, 0)),
                  pl.BlockSpec(w_out.shape, lambda i, j: (0, 0))],
        out_specs=pl.BlockSpec((1, tm, d), lambda i, j: (i, j, 0)),
        out_shape=jax.ShapeDtypeStruct((b, s, d), F32),
        compiler_params=pltpu.CompilerParams(vmem_limit_bytes=VMEM_LIMIT),
        name="out_proj",
    )(x, attn, conv, ada, w_out)


def _rope_tables(s):
    nf = HEAD_DIM // 4
    inv = (ROPE_THETA ** (-np.arange(nf, dtype=np.float32) / nf)).astype(np.float32)
    pos = np.arange(s)
    lane = np.arange(LANES)
    d = lane % HEAD_DIM
    axis = d // (2 * nf)
    half = (d % (2 * nf)) // nf
    coord = np.where(axis[None, :] == 0, (pos // GRID_W)[:, None], (pos % GRID_W)[:, None]).astype(np.float32)
    ang = (coord * inv[d % nf][None, :]).astype(np.float32)
    cos = np.cos(ang).astype(np.float32)
    sin = np.sin(ang).astype(np.float32)
    sin_lo = np.where(half[None, :] == 0, -sin, 0.0).astype(np.float32)
    sin_hi = np.where(half[None, :] == 1, sin, 0.0).astype(np.float32)
    return jnp.asarray(cos), jnp.asarray(sin_lo), jnp.asarray(sin_hi)


def _bias_tiles(rpb_ref, o_ref, first_tile):
    n_pairs = rpb_ref.shape[1] - 1
    cq = lax.broadcasted_iota(jnp.int32, (GRID_W, LANES), 0)
    ck = lax.broadcasted_iota(jnp.int32, (GRID_W, LANES), 1) % GRID_W
    col_start = jnp.clip(cq - WIN_W // 2, 0, GRID_W - WIN_W)
    valid = (ck >= col_start) & (ck < col_start + WIN_W)
    for e in range(o_ref.shape[0]):
        t = first_tile + e
        hp = t // n_pairs
        dr = t % n_pairs
        for a in range(2):
            two = rpb_ref[2 * hp + a, pl.ds(dr, 2), :]
            lanes = jnp.broadcast_to(jnp.concatenate([two[0:1], two[1:2]], axis=1), (GRID_W, LANES))
            toeplitz = pltpu.roll(lanes, LANES - (WIN_W - 1), 1, stride=1, stride_axis=0)
            o_ref[e, a * GRID_W:(a + 1) * GRID_W, :] = jnp.where(valid, toeplitz * LOG2_E, MASK_VALUE)


def kernel(x, c, ctx, c_ctx, w_ada, b_ada, norm_g, w_in, q_norm_g, k_norm_g, rpb, conv_w, conv_b, w_out):
    depth = w_ada.shape[0]
    b, s, d = x.shape
    dc = conv_w.shape[2]
    da = (w_in.shape[2] - 4 * dc) // 4
    rows = s // GRID_W
    assert depth == 1 and s % GRID_W == 0 and rows >= WIN_H and da % LANES == 0

    cos, slo, shi = _rope_tables(s)
    assert da % MXU_DIM == 0 and MXU_DIM % HEAD_DIM == 0
    seg = np.arange(MXU_DIM) // HEAD_DIM
    bd = jnp.asarray((seg[:, None] == seg[None, :]).astype(np.float32), dtype=BF16)

    ada, gains = _prep(c, c_ctx, w_ada[0], b_ada, q_norm_g, k_norm_g, da, steps=PREP_STEPS)

    qrot, qpl, k, v, ga, conv, kc, vc, bias = _in_proj(x, ctx, ada, b, norm_g, w_in[0], gains, cos, slo, shi, bd,
                                                       conv_w, conv_b, rpb[0], tm=IN_PROJ_ROWS, sub=IN_PROJ_SUB)
    attn = _attention(qrot, qpl, k, v, kc, vc, ga, bias, group=ATTN_GROUP, n_batch=ATTN_BATCH, steps=ATTN_STEPS)
    return _out_proj(x, attn, conv, ada, w_out[0], tm=OUT_PROJ_ROWS)
```

```python
import functools

import numpy as np
import jax
import jax.numpy as jnp
from jax import lax
from jax.experimental import pallas as pl
from jax.experimental.pallas import tpu as pltpu

F32 = jnp.float32
BF16 = jnp.bfloat16

HEAD_DIM = 64
GRID_W = 64
WIN_H = 8
WIN_W = 16
CONV_K = 3
ROPE_THETA = 10000.0
RMS_EPS = 1e-6
MASK_VALUE = -1e30
LOG2_E = 1.4426950408889634

LANES = 128
MXU_DIM = 256
F32_ROWS = 8
BF16_ROWS = 16
HALO = BF16_ROWS
VMEM_CAPACITY = 64 * 1024 * 1024
VMEM_LIMIT = VMEM_CAPACITY * 7 // 8

PREP_STEPS = 4
IN_PROJ_ROWS, IN_PROJ_SUB = 1024, 512
OUT_PROJ_ROWS = 2048
ATTN_GROUP, ATTN_BATCH, ATTN_STEPS = 2, 4, 30


def _silu(z):
    return z * jax.nn.sigmoid(z)


def _prep_kernel(c_ref, cctx_ref, w_ref, b_ref, qg_ref, kg_ref, ada_ref, gains_ref):
    nb, d = c_ref.shape
    cctx = cctx_ref[...].reshape(1, d)
    cond = jnp.concatenate([c_ref[...], jnp.broadcast_to(cctx, (ada_ref.shape[0] - nb, d))], axis=0)
    a = _silu(cond).astype(BF16)
    ada_ref[...] = jnp.dot(a, w_ref[...].astype(BF16), preferred_element_type=F32) + b_ref[...]
    g = jnp.concatenate([qg_ref[...] * (HEAD_DIM ** -0.5 * LOG2_E), kg_ref[...]], axis=0)
    gains_ref[...] = jnp.concatenate([g] * (gains_ref.shape[1] // HEAD_DIM), axis=1)


def _prep(c, c_ctx, w_ada, b_ada, q_gain, k_gain, d_attn, *, steps):
    nb, d = c.shape
    assert nb % F32_ROWS == 0
    rows = nb + F32_ROWS
    n = w_ada.shape[1]
    tn = n // steps
    assert tn % LANES == 0
    return pl.pallas_call(
        _prep_kernel,
        grid=(steps,),
        in_specs=[pl.BlockSpec((nb, d), lambda i: (0, 0)),
                  pl.BlockSpec((d,), lambda i: (0,)),
                  pl.BlockSpec((d, tn), lambda i: (0, i)),
                  pl.BlockSpec((1, tn), lambda i: (0, i)),
                  pl.BlockSpec((1, HEAD_DIM), lambda i: (0, 0)),
                  pl.BlockSpec((1, HEAD_DIM), lambda i: (0, 0))],
        out_specs=[pl.BlockSpec((rows, tn), lambda i: (0, i)),
                   pl.BlockSpec((2, d_attn), lambda i: (0, 0))],
        out_shape=[jax.ShapeDtypeStruct((rows, n), F32),
                   jax.ShapeDtypeStruct((2, d_attn), F32)],
        compiler_params=pltpu.CompilerParams(vmem_limit_bytes=VMEM_LIMIT),
        name="prep",
    )(c, c_ctx, w_ada, b_ada, q_gain, k_gain)


def _ada_row(ada_ref, row):
    r = ada_ref[pl.ds(row, 1), :]
    d = r.shape[1] // 3
    return r[:, 0:d], r[:, d:2 * d], r[:, 2 * d:]


def _modulated_norm(xt, mult, shift):
    ms = jnp.mean(xt * xt, axis=-1, keepdims=True)
    return (xt * lax.rsqrt(ms + RMS_EPS) * mult + shift).astype(BF16)


def _head_norm(t, gain, bd):
    t2 = (t * t).astype(BF16)
    w = bd.shape[0]
    ss = jnp.concatenate([jnp.dot(t2[:, c:c + w], bd, preferred_element_type=F32)
                          for c in range(0, t.shape[1], w)], axis=1)
    return t * lax.rsqrt(ss * (1.0 / HEAD_DIM) + RMS_EPS) * gain


def _rope(t, cos, sin_lo, sin_hi):
    outs = []
    for c in range(t.shape[1] // LANES):
        tc = t[:, c * LANES:(c + 1) * LANES]
        up = pltpu.roll(tc, LANES - HEAD_DIM // 4, 1)
        dn = pltpu.roll(tc, HEAD_DIM // 4, 1)
        outs.append(tc * cos + up * sin_lo + dn * sin_hi)
    return jnp.concatenate(outs, axis=1)


def _in_proj_kernel(x_ref, xp_ref, xn_ref, ctx_ref, ada_ref, ng_ref, w_ref, g_ref,
                    cos_ref, slo_ref, shi_ref, bd_ref, cw_ref, cb_ref, rpb_ref,
                    qrot_ref, qpl_ref, k_ref, v_ref, ga_ref, conv_ref, kc_ref, vc_ref, bias_ref,
                    hext_ref, cu_ref, wb_ref, *, tm, sub, da, dc, ctx_row, bias_blocks):
    j = pl.program_id(1)
    nj = pl.num_programs(1)

    @pl.when((pl.program_id(0) == 0) & (j == 0))
    def _():
        for c in range(0, w_ref.shape[1], MXU_DIM):
            wb_ref[:, c:c + MXU_DIM] = w_ref[:, c:c + MXU_DIM].astype(BF16)

    step = pl.program_id(0) * nj + j
    _bias_tiles(rpb_ref, bias_ref, jnp.minimum(step, bias_blocks - 1) * bias_ref.shape[0])
    shift, scale, _ = _ada_row(ada_ref, pl.program_id(0))
    mult = ng_ref[...] * (1.0 + scale)
    n_sub = tm // sub
    cr = ctx_ref.shape[1]

    def proj(src, lo, width):
        return jnp.dot(hext_ref[src, :], wb_ref[:, lo:lo + width], preferred_element_type=F32)

    bd = bd_ref[...]
    hext_ref[0:HALO, :] = _modulated_norm(xp_ref[0], mult, shift)
    hext_ref[HALO + tm:2 * HALO + tm, :] = _modulated_norm(xn_ref[0], mult, shift)
    shift_c, scale_c, _ = _ada_row(ada_ref, ctx_row)
    hext_ref[2 * HALO + tm:, :] = _modulated_norm(ctx_ref[0], ng_ref[...] * (1.0 + scale_c), shift_c)

    for n in range(n_sub):
        r0 = n * sub
        hext_ref[HALO + r0:HALO + r0 + sub, :] = _modulated_norm(x_ref[0, r0:r0 + sub, :], mult, shift)
    for n in range(n_sub):
        r0 = n * sub
        rows = slice(r0, r0 + sub)
        h = slice(HALO + r0, HALO + r0 + sub)
        with_ctx = n == n_sub - 1
        h_kv = slice(HALO + r0, 2 * HALO + tm + cr) if with_ctx else h
        cos, slo, shi = cos_ref[rows, :], slo_ref[rows, :], shi_ref[rows, :]


        hx = slice(r0, r0 + sub + 2 * HALO)
        cu = proj(hx, 4 * da, dc) * proj(hx, 4 * da + 2 * dc, dc)
        row = lax.broadcasted_iota(jnp.int32, (sub + 2 * HALO, 1), 0) + r0
        inside = ((row >= HALO) | (j > 0)) & ((row < HALO + tm) | (j < nj - 1))
        cu_ref[n] = jnp.where(inside, cu, 0.0)
        y = (cb_ref[...]
             + cw_ref[0, 0:1, :] * cu_ref[n, HALO - 1:HALO - 1 + sub, :]
             + cw_ref[0, 1:2, :] * cu_ref[n, HALO:HALO + sub, :]
             + cw_ref[0, 2:3, :] * cu_ref[n, HALO + 1:HALO + 1 + sub, :])
        bg = proj(h, 4 * da + dc, dc)
        zc = proj(h, 4 * da + 3 * dc, dc)
        conv_ref[0, rows, :] = (bg * y * _silu(zc)).astype(BF16)

        qn = _head_norm(proj(h, 0, da), g_ref[0:1, :], bd)
        qpl_ref[0, rows, :] = qn.astype(BF16)
        qrot_ref[0, rows, :] = _rope(qn, cos, slo, shi).astype(BF16)

        kf = proj(h_kv, da, da)
        kn = _head_norm(kf[0:sub], g_ref[1:2, :], bd)
        k_ref[0, rows, :] = _rope(kn, cos, slo, shi).astype(BF16)
        if with_ctx:
            kc_ref[0] = _head_norm(kf[sub + HALO:], g_ref[1:2, :], bd).astype(BF16)

        ga_ref[0, rows, :] = _silu(proj(h, 3 * da, da)).astype(BF16)
        vf = proj(h_kv, 2 * da, da)
        v_ref[0, rows, :] = vf[0:sub].astype(BF16)
        if with_ctx:
            vc_ref[0] = vf[sub + HALO:].astype(BF16)


def _in_proj(x, ctx, ada, ctx_row, norm_g, w_in, gains, cos, slo, shi, bd, conv_w, conv_b, rpb, *, tm, sub):
    b, s, d = x.shape
    n_heads, ndr, ndc = rpb.shape
    assert ndr == 2 * WIN_H - 1 and ndc == 2 * WIN_W - 1 and ndc <= GRID_W
    rpb_pad = jnp.pad(rpb, ((0, 0), (0, 0), (0, GRID_W - ndc)))
    n_tiles = (n_heads // 2) * (ndr - 1)
    bias_blocks = max(k for k in range(1, b * (s // tm) + 1) if n_tiles % k == 0)
    tiles_per_step = n_tiles // bias_blocks
    l = ctx.shape[1]
    assert s % tm == 0 and tm % sub == 0 and sub % HALO == 0
    cr = l // (s // tm)
    assert l % (s // tm) == 0 and cr % BF16_ROWS == 0
    dc = conv_w.shape[2]
    da = (w_in.shape[1] - 4 * dc) // 4
    nh = tm // HALO
    last_halo = s // HALO - 1
    tok = lambda width: pl.BlockSpec((1, tm, width), lambda i, j: (i, j, 0))
    const = lambda shape: pl.BlockSpec(shape, lambda i, j: (0,) * len(shape))
    tab = pl.BlockSpec((tm, LANES), lambda i, j: (j, 0))
    out_tok = jax.ShapeDtypeStruct((b, s, da), BF16)
    out_ctx = jax.ShapeDtypeStruct((b, l, da), BF16)
    ctx_spec = lambda width: pl.BlockSpec((1, cr, width), lambda i, j: (i, j, 0))
    nj = s // tm
    bias_spec = pl.BlockSpec((tiles_per_step, 2 * GRID_W, 2 * GRID_W),
                             lambda i, j: (jnp.minimum(i * nj + j, bias_blocks - 1), 0, 0))
    outs = pl.pallas_call(
        functools.partial(_in_proj_kernel, tm=tm, sub=sub, da=da, dc=dc, ctx_row=ctx_row,
                          bias_blocks=bias_blocks),
        grid=(b, s // tm),
        in_specs=[tok(d),
                  pl.BlockSpec((1, HALO, d), lambda i, j: (i, jnp.maximum(j * nh - 1, 0), 0)),
                  pl.BlockSpec((1, HALO, d), lambda i, j: (i, jnp.minimum((j + 1) * nh, last_halo), 0)),
                  ctx_spec(d),
                  const(ada.shape),
                  const((1, d)), const(w_in.shape), const(gains.shape),
                  tab, tab, tab, const(bd.shape), const((1, CONV_K, dc)), const((1, dc)),
                  const(rpb_pad.shape)],
        out_specs=[tok(da), tok(da), tok(da), tok(da), tok(da), tok(dc), ctx_spec(da), ctx_spec(da), bias_spec],
        out_shape=[out_tok, out_tok, out_tok, out_tok, out_tok,
                   jax.ShapeDtypeStruct((b, s, dc), BF16), out_ctx, out_ctx,
                   jax.ShapeDtypeStruct((n_tiles, 2 * GRID_W, 2 * GRID_W), F32)],
        scratch_shapes=[pltpu.VMEM((tm + 2 * HALO + cr, d), BF16),
                        pltpu.VMEM((tm // sub, sub + 2 * HALO, dc), F32),
                        pltpu.VMEM(w_in.shape, BF16)],
        compiler_params=pltpu.CompilerParams(vmem_limit_bytes=VMEM_LIMIT,
                                             dimension_semantics=("arbitrary", "arbitrary")),
        name="in_proj",
    )(x, x, x, ctx, ada, norm_g, w_in, gains, cos, slo, shi, bd, conv_w, conv_b, rpb_pad)
    bias = outs[-1].reshape(n_heads // 2, ndr - 1, 2 * GRID_W, 2 * GRID_W)
    return (*outs[:-1], bias)


def _dot_nt(a, b):
    return lax.dot_general(a, b, (((1,), (1,)), ((), ())), preferred_element_type=F32)


def _attn_kernel(qrot_ref, qpl_ref, k_ref, v_ref, kc_ref, vc_ref, ga_ref, bias_ref, o_ref,
                 s_lat_ref, s_ctx_ref, p_lat_ref, p_ctx_ref, *, rows, group, n_batch, steps):
    n_units = n_batch * rows // group
    nk = WIN_H * GRID_W

    lane = lax.broadcasted_iota(jnp.int32, (GRID_W, LANES), 1)
    first_head = lane < HEAD_DIM

    def stack_heads(q2):
        zero = jnp.zeros_like(q2)
        return jnp.concatenate([jnp.where(first_head, q2, zero), jnp.where(first_head, zero, q2)], axis=0)

    def geometry(u, g):
        r = u * group + g
        if isinstance(r, int):
            bb, i = divmod(r, rows)
            rs = min(max(i - WIN_H // 2, 0), rows - WIN_H)
            return bb, i - rs, i * GRID_W, rs * GRID_W
        bb = r // rows
        i = r % rows
        rs = jnp.clip(i - WIN_H // 2, 0, rows - WIN_H)
        return bb, i - rs, pl.multiple_of(i * GRID_W, GRID_W), pl.multiple_of(rs * GRID_W, GRID_W)


    def scores_matmul(u):
        out = []
        for g in range(group):
            bb, off, tok0, key0 = geometry(u, g)
            qs = stack_heads(qrot_ref[bb, pl.ds(tok0, GRID_W), :])
            kband = k_ref[bb, pl.ds(key0, nk), :]
            qp = stack_heads(qpl_ref[bb, pl.ds(tok0, GRID_W), :])
            out.append((_dot_nt(qs, kband), off, _dot_nt(qp, kc_ref[bb])))
        return out


    def scores_store(slot, vals):
        for g, (s_lat, off, s_ctx) in enumerate(vals):
            first = (WIN_H - 1) - off
            bias = jnp.concatenate([bias_ref[0, first + 2 * t] for t in range(WIN_H // 2)], axis=1)
            s_lat_ref[slot, g] = s_lat + bias
            s_ctx_ref[slot, g] = s_ctx

    def softmax(slot):
        for g in range(group):
            s_lat = s_lat_ref[slot, g]
            s_ctx = s_ctx_ref[slot, g]
            m = jnp.maximum(jnp.max(s_lat, axis=-1, keepdims=True), jnp.max(s_ctx, axis=-1, keepdims=True))
            p_lat_ref[slot, g] = jnp.exp2(s_lat - m).astype(BF16)
            p_ctx_ref[slot, g] = jnp.exp2(s_ctx - m).astype(BF16)

    ones_lat = jnp.ones((nk, LANES), BF16)
    ones_ctx = jnp.ones((kc_ref.shape[1], LANES), BF16)

    def pv_matmul(u, slot):
        out = []
        for g in range(group):
            bb, _, _, key0 = geometry(u, g)
            v_lat = jnp.concatenate([v_ref[bb, pl.ds(key0, nk), :], ones_lat], axis=1)
            v_ctx = jnp.concatenate([vc_ref[bb], ones_ctx], axis=1)
            out.append(jnp.dot(p_lat_ref[slot, g], v_lat, preferred_element_type=F32)
                       + jnp.dot(p_ctx_ref[slot, g], v_ctx, preferred_element_type=F32))
        return out

    def pv_store(u, vals):
        for g, ol in enumerate(vals):
            bb, _, tok0, _ = geometry(u, g)
            o = ol[:, 0:LANES] * (1.0 / ol[:, LANES:])
            o2 = jnp.where(first_head, o[0:GRID_W], o[GRID_W:2 * GRID_W])
            gate = ga_ref[bb, pl.ds(tok0, GRID_W), :].astype(F32)
            o_ref[bb, pl.ds(tok0, GRID_W), :] = (o2 * gate).astype(BF16)

    scores_store(0, scores_matmul(0))
    sc = scores_matmul(1)
    softmax(0)
    scores_store(1, sc)

    def step(t, slot):
        sc = scores_matmul(t)
        pv = pv_matmul(t - 2, slot)
        softmax(1 - slot)
        scores_store(slot, sc)
        pv_store(t - 2, pv)

    n_steady, n_left = divmod(n_units - 2, steps)

    def steady(n, carry):
        for j in range(steps):
            step(steps * n + 2 + j, j % 2)
        return carry

    lax.fori_loop(0, n_steady, steady, 0)
    for j in range(n_left):
        step(steps * n_steady + 2 + j, j % 2)

    pv = pv_matmul(n_units - 2, 0)
    softmax(1)
    pv_store(n_units - 2, pv)
    pv_store(n_units - 1, pv_matmul(n_units - 1, 1))


def _attention(qrot, qpl, k, v, kc, vc, ga, bias, *, group, n_batch, steps):
    b, s, da = qrot.shape
    l = vc.shape[1]
    rows = s // GRID_W
    n_hp = da // LANES
    nq = 2 * GRID_W
    nk = WIN_H * GRID_W
    assert rows % (2 * group) == 0 and b % n_batch == 0 and steps % 2 == 0
    tok = pl.BlockSpec((n_batch, s, LANES), lambda hp, i: (i, 0, hp))
    ctx_tok = pl.BlockSpec((n_batch, l, LANES), lambda hp, i: (i, 0, hp))
    return pl.pallas_call(
        functools.partial(_attn_kernel, rows=rows, group=group, n_batch=n_batch, steps=steps),
        grid=(n_hp, b // n_batch),
        in_specs=[tok, tok, tok, tok, ctx_tok, ctx_tok, tok,
                  pl.BlockSpec((1,) + bias.shape[1:], lambda hp, i: (hp, 0, 0, 0))],
        out_specs=tok,
        out_shape=jax.ShapeDtypeStruct((b, s, da), BF16),
        scratch_shapes=[pltpu.VMEM((2, group, nq, nk), F32),
                        pltpu.VMEM((2, group, nq, l), F32),
                        pltpu.VMEM((2, group, nq, nk), BF16),
                        pltpu.VMEM((2, group, nq, l), BF16)],
        compiler_params=pltpu.CompilerParams(vmem_limit_bytes=VMEM_LIMIT),
        name="attn",
    )(qrot, qpl, k, v, kc, vc, ga, bias)


def _out_proj_kernel(x_ref, a_ref, c_ref, ada_ref, w_ref, o_ref, *, da):
    _, _, gate = _ada_row(ada_ref, pl.program_id(0))
    upd = (jnp.dot(a_ref[0], w_ref[0:da, :].astype(BF16), preferred_element_type=F32)
           + jnp.dot(c_ref[0], w_ref[da:, :].astype(BF16), preferred_element_type=F32))
    o_ref[0] = x_ref[0] + gate * upd


def _out_proj(x, attn, conv, ada, w_out, *, tm):
    b, s, d = x.shape
    da = attn.shape[2]
    dc = conv.shape[2]
    return pl.pallas_call(
        functools.partial(_out_proj_kernel, da=da),
        grid=(b, s // tm),
        in_specs=[pl.BlockSpec((1, tm, d), lambda i, j: (i, j, 0)),
                  pl.BlockSpec((1, tm, da), lambda i, j: (i, j, 0)),
                  pl.BlockSpec((1, tm, dc), lambda i, j: (i, j, 0)),
                  pl.BlockSpec(ada.shape, lambda i, j: (0, 0)),
                  pl.BlockSpec(w_out.shape, lambda i, j: (0, 0))],
        out_specs=pl.BlockSpec((1, tm, d), lambda i, j: (i, j, 0)),
        out_shape=jax.ShapeDtypeStruct((b, s, d), F32),
        compiler_params=pltpu.CompilerParams(vmem_limit_bytes=VMEM_LIMIT),
        name="out_proj",
    )(x, attn, conv, ada, w_out)


def _rope_tables(s):
    nf = HEAD_DIM // 4
    inv = (ROPE_THETA ** (-np.arange(nf, dtype=np.float32) / nf)).astype(np.float32)
    pos = np.arange(s)
    lane = np.arange(LANES)
    d = lane % HEAD_DIM
    axis = d // (2 * nf)
    half = (d % (2 * nf)) // nf
    coord = np.where(axis[None, :] == 0, (pos // GRID_W)[:, None], (pos % GRID_W)[:, None]).astype(np.float32)
    ang = (coord * inv[d % nf][None, :]).astype(np.float32)
    cos = np.cos(ang).astype(np.float32)
    sin = np.sin(ang).astype(np.float32)
    sin_lo = np.where(half[None, :] == 0, -sin, 0.0).astype(np.float32)
    sin_hi = np.where(half[None, :] == 1, sin, 0.0).astype(np.float32)
    return jnp.asarray(cos), jnp.asarray(sin_lo), jnp.asarray(sin_hi)


def _bias_tiles(rpb_ref, o_ref, first_tile):
    n_pairs = rpb_ref.shape[1] - 1
    cq = lax.broadcasted_iota(jnp.int32, (GRID_W, LANES), 0)
    ck = lax.broadcasted_iota(jnp.int32, (GRID_W, LANES), 1) % GRID_W
    col_start = jnp.clip(cq - WIN_W // 2, 0, GRID_W - WIN_W)
    valid = (ck >= col_start) & (ck < col_start + WIN_W)
    for e in range(o_ref.shape[0]):
        t = first_tile + e
        hp = t // n_pairs
        dr = t % n_pairs
        for a in range(2):
            two = rpb_ref[2 * hp + a, pl.ds(dr, 2), :]
            lanes = jnp.broadcast_to(jnp.concatenate([two[0:1], two[1:2]], axis=1), (GRID_W, LANES))
            toeplitz = pltpu.roll(lanes, LANES - (WIN_W - 1), 1, stride=1, stride_axis=0)
            o_ref[e, a * GRID_W:(a + 1) * GRID_W, :] = jnp.where(valid, toeplitz * LOG2_E, MASK_VALUE)


def kernel(x, c, ctx, c_ctx, w_ada, b_ada, norm_g, w_in, q_norm_g, k_norm_g, rpb, conv_w, conv_b, w_out):
    depth = w_ada.shape[0]
    b, s, d = x.shape
    dc = conv_w.shape[2]
    da = (w_in.shape[2] - 4 * dc) // 4
    rows = s // GRID_W
    assert depth == 1 and s % GRID_W == 0 and rows >= WIN_H and da % LANES == 0

    cos, slo, shi = _rope_tables(s)
    assert da % MXU_DIM == 0 and MXU_DIM % HEAD_DIM == 0
    seg = np.arange(MXU_DIM) // HEAD_DIM
    bd = jnp.asarray((seg[:, None] == seg[None, :]).astype(np.float32), dtype=BF16)

    ada, gains = _prep(c, c_ctx, w_ada[0], b_ada, q_norm_g, k_norm_g, da, steps=PREP_STEPS)

    qrot, qpl, k, v, ga, conv, kc, vc, bias = _in_proj(x, ctx, ada, b, norm_g, w_in[0], gains, cos, slo, shi, bd,
                                                       conv_w, conv_b, rpb[0], tm=IN_PROJ_ROWS, sub=IN_PROJ_SUB)
    attn = _attention(qrot, qpl, k, v, kc, vc, ga, bias, group=ATTN_GROUP, n_batch=ATTN_BATCH, steps=ATTN_STEPS)
    return _out_proj(x, attn, conv, ada, w_out[0], tm=OUT_PROJ_ROWS)
```

```python
import functools

import numpy as np
import jax
import jax.numpy as jnp
from jax import lax
from jax.experimental import pallas as pl
from jax.experimental.pallas import tpu as pltpu

F32 = jnp.float32
BF16 = jnp.bfloat16

HEAD_DIM = 64
GRID_W = 64
WIN_H = 8
WIN_W = 16
CONV_K = 3
ROPE_THETA = 10000.0
RMS_EPS = 1e-6
MASK_VALUE = -1e30
LOG2_E = 1.4426950408889634

LANES = 128
MXU_DIM = 256
F32_ROWS = 8
BF16_ROWS = 16
HALO = BF16_ROWS
VMEM_CAPACITY = 64 * 1024 * 1024
VMEM_LIMIT = VMEM_CAPACITY * 7 // 8

PREP_STEPS = 4
IN_PROJ_ROWS, IN_PROJ_SUB = 1024, 512
OUT_PROJ_ROWS = 2048
ATTN_GROUP, ATTN_BATCH, ATTN_STEPS = 2, 4, 30


def _silu(z):
    return z * jax.nn.sigmoid(z)


def _prep_kernel(c_ref, cctx_ref, w_ref, b_ref, qg_ref, kg_ref, ada_ref, gains_ref):
    nb, d = c_ref.shape
    cctx = cctx_ref[...].reshape(1, d)
    cond = jnp.concatenate([c_ref[...], jnp.broadcast_to(cctx, (ada_ref.shape[0] - nb, d))], axis=0)
    a = _silu(cond).astype(BF16)
    ada_ref[...] = jnp.dot(a, w_ref[...].astype(BF16), preferred_element_type=F32) + b_ref[...]
    g = jnp.concatenate([qg_ref[...] * (HEAD_DIM ** -0.5 * LOG2_E), kg_ref[...]], axis=0)
    gains_ref[...] = jnp.concatenate([g] * (gains_ref.shape[1] // HEAD_DIM), axis=1)


def _prep(c, c_ctx, w_ada, b_ada, q_gain, k_gain, d_attn, *, steps):
    nb, d = c.shape
    assert nb % F32_ROWS == 0
    rows = nb + F32_ROWS
    n = w_ada.shape[1]
    tn = n // steps
    assert tn % LANES == 0
    return pl.pallas_call(
        _prep_kernel,
        grid=(steps,),
        in_specs=[pl.BlockSpec((nb, d), lambda i: (0, 0)),
                  pl.BlockSpec((d,), lambda i: (0,)),
                  pl.BlockSpec((d, tn), lambda i: (0, i)),
                  pl.BlockSpec((1, tn), lambda i: (0, i)),
                  pl.BlockSpec((1, HEAD_DIM), lambda i: (0, 0)),
                  pl.BlockSpec((1, HEAD_DIM), lambda i: (0, 0))],
        out_specs=[pl.BlockSpec((rows, tn), lambda i: (0, i)),
                   pl.BlockSpec((2, d_attn), lambda i: (0, 0))],
        out_shape=[jax.ShapeDtypeStruct((rows, n), F32),
                   jax.ShapeDtypeStruct((2, d_attn), F32)],
        compiler_params=pltpu.CompilerParams(vmem_limit_bytes=VMEM_LIMIT),
        name="prep",
    )(c, c_ctx, w_ada, b_ada, q_gain, k_gain)


def _ada_row(ada_ref, row):
    r = ada_ref[pl.ds(row, 1), :]
    d = r.shape[1] // 3
    return r[:, 0:d], r[:, d:2 * d], r[:, 2 * d:]


def _modulated_norm(xt, mult, shift):
    ms = jnp.mean(xt * xt, axis=-1, keepdims=True)
    return (xt * lax.rsqrt(ms + RMS_EPS) * mult + shift).astype(BF16)


def _head_norm(t, gain, bd):
    t2 = (t * t).astype(BF16)
    w = bd.shape[0]
    ss = jnp.concatenate([jnp.dot(t2[:, c:c + w], bd, preferred_element_type=F32)
                          for c in range(0, t.shape[1], w)], axis=1)
    return t * lax.rsqrt(ss * (1.0 / HEAD_DIM) + RMS_EPS) * gain


def _rope(t, cos, sin_lo, sin_hi):
    outs = []
    for c in range(t.shape[1] // LANES):
        tc = t[:, c * LANES:(c + 1) * LANES]
        up = pltpu.roll(tc, LANES - HEAD_DIM // 4, 1)
        dn = pltpu.roll(tc, HEAD_DIM // 4, 1)
        outs.append(tc * cos + up * sin_lo + dn * sin_hi)
    return jnp.concatenate(outs, axis=1)


def _in_proj_kernel(x_ref, xp_ref, xn_ref, ctx_ref, ada_ref, ng_ref, w_ref, g_ref,
                    cos_ref, slo_ref, shi_ref, bd_ref, cw_ref, cb_ref, rpb_ref,
                    qrot_ref, qpl_ref, k_ref, v_ref, ga_ref, conv_ref, kc_ref, vc_ref, bias_ref,
                    hext_ref, cu_ref, wb_ref, *, tm, sub, da, dc, ctx_row, bias_blocks):
    j = pl.program_id(1)
    nj = pl.num_programs(1)

    @pl.when((pl.program_id(0) == 0) & (j == 0))
    def _():
        for c in range(0, w_ref.shape[1], MXU_DIM):
            wb_ref[:, c:c + MXU_DIM] = w_ref[:, c:c + MXU_DIM].astype(BF16)

    step = pl.program_id(0) * nj + j
    _bias_tiles(rpb_ref, bias_ref, jnp.minimum(step, bias_blocks - 1) * bias_ref.shape[0])
    shift, scale, _ = _ada_row(ada_ref, pl.program_id(0))
    mult = ng_ref[...] * (1.0 + scale)
    n_sub = tm // sub
    cr = ctx_ref.shape[1]

    def proj(src, lo, width):
        return jnp.dot(hext_ref[src, :], wb_ref[:, lo:lo + width], preferred_element_type=F32)

    bd = bd_ref[...]
    hext_ref[0:HALO, :] = _modulated_norm(xp_ref[0], mult, shift)
    hext_ref[HALO + tm:2 * HALO + tm, :] = _modulated_norm(xn_ref[0], mult, shift)
    shift_c, scale_c, _ = _ada_row(ada_ref, ctx_row)
    hext_ref[2 * HALO + tm:, :] = _modulated_norm(ctx_ref[0], ng_ref[...] * (1.0 + scale_c), shift_c)

    for n in range(n_sub):
        r0 = n * sub
        hext_ref[HALO + r0:HALO + r0 + sub, :] = _modulated_norm(x_ref[0, r0:r0 + sub, :], mult, shift)
    for n in range(n_sub):
        r0 = n * sub
        rows = slice(r0, r0 + sub)
        h = slice(HALO + r0, HALO + r0 + sub)
        with_ctx = n == n_sub - 1
        h_kv = slice(HALO + r0, 2 * HALO + tm + cr) if with_ctx else h
        cos, slo, shi = cos_ref[rows, :], slo_ref[rows, :], shi_ref[rows, :]


        hx = slice(r0, r0 + sub + 2 * HALO)
        cu = proj(hx, 4 * da, dc) * proj(hx, 4 * da + 2 * dc, dc)
        row = lax.broadcasted_iota(jnp.int32, (sub + 2 * HALO, 1), 0) + r0
        inside = ((row >= HALO) | (j > 0)) & ((row < HALO + tm) | (j < nj - 1))
        cu_ref[n] = jnp.where(inside, cu, 0.0)
        y = (cb_ref[...]
             + cw_ref[0, 0:1, :] * cu_ref[n, HALO - 1:HALO - 1 + sub, :]
             + cw_ref[0, 1:2, :] * cu_ref[n, HALO:HALO + sub, :]
             + cw_ref[0, 2:3, :] * cu_ref[n, HALO + 1:HALO + 1 + sub, :])
        bg = proj(h, 4 * da + dc, dc)
        zc = proj(h, 4 * da + 3 * dc, dc)
        conv_ref[0, rows, :] = (bg * y * _silu(zc)).astype(BF16)

        qn = _head_norm(proj(h, 0, da), g_ref[0:1, :], bd)
        qpl_ref[0, rows, :] = qn.astype(BF16)
        qrot_ref[0, rows, :] = _rope(qn, cos, slo, shi).astype(BF16)

        kf = proj(h_kv, da, da)
        kn = _head_norm(kf[0:sub], g_ref[1:2, :], bd)
        k_ref[0, rows, :] = _rope(kn, cos, slo, shi).astype(BF16)
        if with_ctx:
            kc_ref[0] = _head_norm(kf[sub + HALO:], g_ref[1:2, :], bd).astype(BF16)

        ga_ref[0, rows, :] = _silu(proj(h, 3 * da, da)).astype(BF16)
        vf = proj(h_kv, 2 * da, da)
        v_ref[0, rows, :] = vf[0:sub].astype(BF16)
        if with_ctx:
            vc_ref[0] = vf[sub + HALO:].astype(BF16)


def _in_proj(x, ctx, ada, ctx_row, norm_g, w_in, gains, cos, slo, shi, bd, conv_w, conv_b, rpb, *, tm, sub):
    b, s, d = x.shape
    _, n_heads, ndr, ndc = rpb.shape
    assert ndr == 2 * WIN_H - 1 and ndc == 2 * WIN_W - 1 and ndc <= GRID_W
    n_tiles = (n_heads // 2) * (ndr - 1)
    bias_blocks = max(k for k in range(1, b * (s // tm) + 1) if n_tiles % k == 0)
    tiles_per_step = n_tiles // bias_blocks
    l = ctx.shape[1]
    assert s % tm == 0 and tm % sub == 0 and sub % HALO == 0
    cr = l // (s // tm)
    assert l % (s // tm) == 0 and cr % BF16_ROWS == 0
    dc = conv_w.shape[2]
    da = (w_in.shape[1] - 4 * dc) // 4
    nh = tm // HALO
    last_halo = s // HALO - 1
    tok = lambda width: pl.BlockSpec((1, tm, width), lambda i, j: (i, j, 0))
    const = lambda shape: pl.BlockSpec(shape, lambda i, j: (0,) * len(shape))
    tab = pl.BlockSpec((tm, LANES), lambda i, j: (j, 0))
    out_tok = jax.ShapeDtypeStruct((b, s, da), BF16)
    out_ctx = jax.ShapeDtypeStruct((b, l, da), BF16)
    ctx_spec = lambda width: pl.BlockSpec((1, cr, width), lambda i, j: (i, j, 0))
    nj = s // tm
    bias_spec = pl.BlockSpec((tiles_per_step, 2 * GRID_W, 2 * GRID_W),
                             lambda i, j: (jnp.minimum(i * nj + j, bias_blocks - 1), 0, 0))
    outs = pl.pallas_call(
        functools.partial(_in_proj_kernel, tm=tm, sub=sub, da=da, dc=dc, ctx_row=ctx_row,
                          bias_blocks=bias_blocks),
        grid=(b, s // tm),
        in_specs=[tok(d),
                  pl.BlockSpec((1, HALO, d), lambda i, j: (i, jnp.maximum(j * nh - 1, 0), 0)),
                  pl.BlockSpec((1, HALO, d), lambda i, j: (i, jnp.minimum((j + 1) * nh, last_halo), 0)),
                  ctx_spec(d),
                  const(ada.shape),
                  const((1, d)), const(w_in.shape), const(gains.shape),
                  tab, tab, tab, const(bd.shape), const((1, CONV_K, dc)), const((1, dc)),
                  const(rpb.shape)],
        out_specs=[tok(da), tok(da), tok(da), tok(da), tok(da), tok(dc), ctx_spec(da), ctx_spec(da), bias_spec],
        out_shape=[out_tok, out_tok, out_tok, out_tok, out_tok,
                   jax.ShapeDtypeStruct((b, s, dc), BF16), out_ctx, out_ctx,
                   jax.ShapeDtypeStruct((n_tiles, 2 * GRID_W, 2 * GRID_W), F32)],
        scratch_shapes=[pltpu.VMEM((tm + 2 * HALO + cr, d), BF16),
                        pltpu.VMEM((tm // sub, sub + 2 * HALO, dc), F32),
                        pltpu.VMEM(w_in.shape, BF16)],
        compiler_params=pltpu.CompilerParams(vmem_limit_bytes=VMEM_LIMIT,
                                             dimension_semantics=("arbitrary", "arbitrary")),
        name="in_proj",
    )(x, x, x, ctx, ada, norm_g, w_in, gains, cos, slo, shi, bd, conv_w, conv_b, rpb)
    bias = outs[-1].reshape(n_heads // 2, ndr - 1, 2 * GRID_W, 2 * GRID_W)
    return (*outs[:-1], bias)


def _dot_nt(a, b):
    return lax.dot_general(a, b, (((1,), (1,)), ((), ())), preferred_element_type=F32)


def _attn_kernel(qrot_ref, qpl_ref, k_ref, v_ref, kc_ref, vc_ref, ga_ref, bias_ref, o_ref,
                 s_lat_ref, s_ctx_ref, p_lat_ref, p_ctx_ref, *, rows, group, n_batch, steps):
    n_units = n_batch * rows // group
    nk = WIN_H * GRID_W

    lane = lax.broadcasted_iota(jnp.int32, (GRID_W, LANES), 1)
    first_head = lane < HEAD_DIM

    def stack_heads(q2):
        zero = jnp.zeros_like(q2)
        return jnp.concatenate([jnp.where(first_head, q2, zero), jnp.where(first_head, zero, q2)], axis=0)

    def geometry(u, g):
        r = u * group + g
        if isinstance(r, int):
            bb, i = divmod(r, rows)
            rs = min(max(i - WIN_H // 2, 0), rows - WIN_H)
            return bb, i - rs, i * GRID_W, rs * GRID_W
        bb = r // rows
        i = r % rows
        rs = jnp.clip(i - WIN_H // 2, 0, rows - WIN_H)
        return bb, i - rs, pl.multiple_of(i * GRID_W, GRID_W), pl.multiple_of(rs * GRID_W, GRID_W)


    def scores_matmul(u):
        out = []
        for g in range(group):
            bb, off, tok0, key0 = geometry(u, g)
            qs = stack_heads(qrot_ref[bb, pl.ds(tok0, GRID_W), :])
            kband = k_ref[bb, pl.ds(key0, nk), :]
            qp = stack_heads(qpl_ref[bb, pl.ds(tok0, GRID_W), :])
            out.append((_dot_nt(qs, kband), off, _dot_nt(qp, kc_ref[bb])))
        return out


    def scores_store(slot, vals):
        for g, (s_lat, off, s_ctx) in enumerate(vals):
            first = (WIN_H - 1) - off
            bias = jnp.concatenate([bias_ref[0, first + 2 * t] for t in range(WIN_H // 2)], axis=1)
            s_lat_ref[slot, g] = s_lat + bias
            s_ctx_ref[slot, g] = s_ctx

    def softmax(slot):
        for g in range(group):
            s_lat = s_lat_ref[slot, g]
            s_ctx = s_ctx_ref[slot, g]
            m = jnp.maximum(jnp.max(s_lat, axis=-1, keepdims=True), jnp.max(s_ctx, axis=-1, keepdims=True))
            p_lat_ref[slot, g] = jnp.exp2(s_lat - m).astype(BF16)
            p_ctx_ref[slot, g] = jnp.exp2(s_ctx - m).astype(BF16)

    ones_lat = jnp.ones((nk, LANES), BF16)
    ones_ctx = jnp.ones((kc_ref.shape[1], LANES), BF16)

    def pv_matmul(u, slot):
        out = []
        for g in range(group):
            bb, _, _, key0 = geometry(u, g)
            v_lat = jnp.concatenate([v_ref[bb, pl.ds(key0, nk), :], ones_lat], axis=1)
            v_ctx = jnp.concatenate([vc_ref[bb], ones_ctx], axis=1)
            out.append(jnp.dot(p_lat_ref[slot, g], v_lat, preferred_element_type=F32)
                       + jnp.dot(p_ctx_ref[slot, g], v_ctx, preferred_element_type=F32))
        return out

    def pv_store(u, vals):
        for g, ol in enumerate(vals):
            bb, _, tok0, _ = geometry(u, g)
            o = ol[:, 0:LANES] * (1.0 / ol[:, LANES:])
            o2 = jnp.where(first_head, o[0:GRID_W], o[GRID_W:2 * GRID_W])
            gate = ga_ref[bb, pl.ds(tok0, GRID_W), :].astype(F32)
            o_ref[bb, pl.ds(tok0, GRID_W), :] = (o2 * gate).astype(BF16)

    scores_store(0, scores_matmul(0))
    sc = scores_matmul(1)
    softmax(0)
    scores_store(1, sc)

    def step(t, slot):
        sc = scores_matmul(t)
        pv = pv_matmul(t - 2, slot)
        softmax(1 - slot)
        scores_store(slot, sc)
        pv_store(t - 2, pv)

    n_steady, n_left = divmod(n_units - 2, steps)

    def steady(n, carry):
        for j in range(steps):
            step(steps * n + 2 + j, j % 2)
        return carry

    lax.fori_loop(0, n_steady, steady, 0)
    for j in range(n_left):
        step(steps * n_steady + 2 + j, j % 2)

    pv = pv_matmul(n_units - 2, 0)
    softmax(1)
    pv_store(n_units - 2, pv)
    pv_store(n_units - 1, pv_matmul(n_units - 1, 1))


def _attention(qrot, qpl, k, v, kc, vc, ga, bias, *, group, n_batch, steps):
    b, s, da = qrot.shape
    l = vc.shape[1]
    rows = s // GRID_W
    n_hp = da // LANES
    nq = 2 * GRID_W
    nk = WIN_H * GRID_W
    assert rows % (2 * group) == 0 and b % n_batch == 0 and steps % 2 == 0
    tok = pl.BlockSpec((n_batch, s, LANES), lambda hp, i: (i, 0, hp))
    ctx_tok = pl.BlockSpec((n_batch, l, LANES), lambda hp, i: (i, 0, hp))
    return pl.pallas_call(
        functools.partial(_attn_kernel, rows=rows, group=group, n_batch=n_batch, steps=steps),
        grid=(n_hp, b // n_batch),
        in_specs=[tok, tok, tok, tok, ctx_tok, ctx_tok, tok,
                  pl.BlockSpec((1,) + bias.shape[1:], lambda hp, i: (hp, 0, 0, 0))],
        out_specs=tok,
        out_shape=jax.ShapeDtypeStruct((b, s, da), BF16),
        scratch_shapes=[pltpu.VMEM((2, group, nq, nk), F32),
                        pltpu.VMEM((2, group, nq, l), F32),
                        pltpu.VMEM((2, group, nq, nk), BF16),
                        pltpu.VMEM((2, group, nq, l), BF16)],
        compiler_params=pltpu.CompilerParams(vmem_limit_bytes=VMEM_LIMIT),
        name="attn",
    )(qrot, qpl, k, v, kc, vc, ga, bias)


def _out_proj_kernel(x_ref, a_ref, c_ref, ada_ref, w_ref, o_ref, *, da):
    _, _, gate = _ada_row(ada_ref, pl.program_id(0))
    upd = (jnp.dot(a_ref[0], w_ref[0:da, :].astype(BF16), preferred_element_type=F32)
           + jnp.dot(c_ref[0], w_ref[da:, :].astype(BF16), preferred_element_type=F32))
    o_ref[0] = x_ref[0] + gate * upd


def _out_proj(x, attn, conv, ada, w_out, *, tm):
    b, s, d = x.shape
    da = attn.shape[2]
    dc = conv.shape[2]
    return pl.pallas_call(
        functools.partial(_out_proj_kernel, da=da),
        grid=(b, s // tm),
        in_specs=[pl.BlockSpec((1, tm, d), lambda i, j: (i, j, 0)),
                  pl.BlockSpec((1, tm, da), lambda i, j: (i, j, 0)),
                  pl.BlockSpec((1, tm, dc), lambda i, j: (i, j, 0)),
                  pl.BlockSpec(ada.shape, lambda i, j: (0, 0)),
                  pl.BlockSpec(w_out.shape, lambda i, j: (0, 0))],
        out_specs=pl.BlockSpec((1, tm, d), lambda i, j: (i, j, 0)),
        out_shape=jax.ShapeDtypeStruct((b, s, d), F32),
        compiler_params=pltpu.CompilerParams(vmem_limit_bytes=VMEM_LIMIT),
        name="out_proj",
    )(x, attn, conv, ada, w_out)


def _rope_tables(s):
    nf = HEAD_DIM // 4
    inv = (ROPE_THETA ** (-np.arange(nf, dtype=np.float32) / nf)).astype(np.float32)
    pos = np.arange(s)
    lane = np.arange(LANES)
    d = lane % HEAD_DIM
    axis = d // (2 * nf)
    half = (d % (2 * nf)) // nf
    coord = np.where(axis[None, :] == 0, (pos // GRID_W)[:, None], (pos % GRID_W)[:, None]).astype(np.float32)
    ang = (coord * inv[d % nf][None, :]).astype(np.float32)
    cos = np.cos(ang).astype(np.float32)
    sin = np.sin(ang).astype(np.float32)
    sin_lo = np.where(half[None, :] == 0, -sin, 0.0).astype(np.float32)
    sin_hi = np.where(half[None, :] == 1, sin, 0.0).astype(np.float32)
    return jnp.asarray(cos), jnp.asarray(sin_lo), jnp.asarray(sin_hi)


def _bias_tiles(rpb_ref, o_ref, first_tile):
    n_pairs = rpb_ref.shape[2] - 1
    cq = lax.broadcasted_iota(jnp.int32, (GRID_W, LANES), 0)
    ck = lax.broadcasted_iota(jnp.int32, (GRID_W, LANES), 1) % GRID_W
    col_start = jnp.clip(cq - WIN_W // 2, 0, GRID_W - WIN_W)
    valid = (ck >= col_start) & (ck < col_start + WIN_W)
    for e in range(o_ref.shape[0]):
        t = first_tile + e
        hp = t // n_pairs
        dr = t % n_pairs
        for a in range(2):
            two = rpb_ref[0, 2 * hp + a, pl.ds(dr, 2), :]
            fill = jnp.zeros((1, GRID_W - two.shape[1]), F32)
            lanes = jnp.broadcast_to(jnp.concatenate([two[0:1], fill, two[1:2], fill], axis=1), (GRID_W, LANES))
            toeplitz = pltpu.roll(lanes, LANES - (WIN_W - 1), 1, stride=1, stride_axis=0)
            o_ref[e, a * GRID_W:(a + 1) * GRID_W, :] = jnp.where(valid, toeplitz * LOG2_E, MASK_VALUE)


def kernel(x, c, ctx, c_ctx, w_ada, b_ada, norm_g, w_in, q_norm_g, k_norm_g, rpb, conv_w, conv_b, w_out):
    depth = w_ada.shape[0]
    b, s, d = x.shape
    dc = conv_w.shape[2]
    da = (w_in.shape[2] - 4 * dc) // 4
    rows = s // GRID_W
    assert depth == 1 and s % GRID_W == 0 and rows >= WIN_H and da % LANES == 0

    cos, slo, shi = _rope_tables(s)
    assert da % MXU_DIM == 0 and MXU_DIM % HEAD_DIM == 0
    seg = np.arange(MXU_DIM) // HEAD_DIM
    bd = jnp.asarray((seg[:, None] == seg[None, :]).astype(np.float32), dtype=BF16)

    ada, gains = _prep(c, c_ctx, w_ada[0], b_ada, q_norm_g, k_norm_g, da, steps=PREP_STEPS)

    qrot, qpl, k, v, ga, conv, kc, vc, bias = _in_proj(x, ctx, ada, b, norm_g, w_in[0], gains, cos, slo, shi, bd,
                                                       conv_w, conv_b, rpb, tm=IN_PROJ_ROWS, sub=IN_PROJ_SUB)
    attn = _attention(qrot, qpl, k, v, kc, vc, ga, bias, group=ATTN_GROUP, n_batch=ATTN_BATCH, steps=ATTN_STEPS)
    return _out_proj(x, attn, conv, ada, w_out[0], tm=OUT_PROJ_ROWS)
```

```python
import functools

import numpy as np
import jax
import jax.numpy as jnp
from jax import lax
from jax.experimental import pallas as pl
from jax.experimental.pallas import tpu as pltpu

F32 = jnp.float32
BF16 = jnp.bfloat16

HEAD_DIM = 64
GRID_W = 64
WIN_H = 8
WIN_W = 16
CONV_K = 3
ROPE_THETA = 10000.0
RMS_EPS = 1e-6
MASK_VALUE = -1e30
LOG2_E = 1.4426950408889634

LANES = 128
MXU_DIM = 256
F32_ROWS = 8
BF16_ROWS = 16
HALO = BF16_ROWS
VMEM_CAPACITY = 64 * 1024 * 1024
VMEM_LIMIT = VMEM_CAPACITY * 7 // 8

PREP_STEPS = 4
IN_PROJ_ROWS, IN_PROJ_SUB = 1024, 512
OUT_PROJ_ROWS = 1024
ATTN_GROUP, ATTN_BATCH, ATTN_STEPS = 2, 4, 30


def _silu(z):
    return z * jax.nn.sigmoid(z)


def _prep_kernel(c_ref, cctx_ref, w_ref, b_ref, qg_ref, kg_ref, ada_ref, gains_ref):
    nb, d = c_ref.shape
    cctx = cctx_ref[...].reshape(1, d)
    cond = jnp.concatenate([c_ref[...], jnp.broadcast_to(cctx, (ada_ref.shape[0] - nb, d))], axis=0)
    a = _silu(cond).astype(BF16)
    ada_ref[...] = jnp.dot(a, w_ref[...].astype(BF16), preferred_element_type=F32) + b_ref[...]
    g = jnp.concatenate([qg_ref[...] * (HEAD_DIM ** -0.5 * LOG2_E), kg_ref[...]], axis=0)
    gains_ref[...] = jnp.concatenate([g] * (gains_ref.shape[1] // HEAD_DIM), axis=1)


def _prep(c, c_ctx, w_ada, b_ada, q_gain, k_gain, d_attn, *, steps):
    nb, d = c.shape
    assert nb % F32_ROWS == 0
    rows = nb + F32_ROWS
    n = w_ada.shape[1]
    tn = n // steps
    assert tn % LANES == 0
    return pl.pallas_call(
        _prep_kernel,
        grid=(steps,),
        in_specs=[pl.BlockSpec((nb, d), lambda i: (0, 0)),
                  pl.BlockSpec((d,), lambda i: (0,)),
                  pl.BlockSpec((d, tn), lambda i: (0, i)),
                  pl.BlockSpec((1, tn), lambda i: (0, i)),
                  pl.BlockSpec((1, HEAD_DIM), lambda i: (0, 0)),
                  pl.BlockSpec((1, HEAD_DIM), lambda i: (0, 0))],
        out_specs=[pl.BlockSpec((rows, tn), lambda i: (0, i)),
                   pl.BlockSpec((2, d_attn), lambda i: (0, 0))],
        out_shape=[jax.ShapeDtypeStruct((rows, n), F32),
                   jax.ShapeDtypeStruct((2, d_attn), F32)],
        compiler_params=pltpu.CompilerParams(vmem_limit_bytes=VMEM_LIMIT),
        name="prep",
    )(c, c_ctx, w_ada, b_ada, q_gain, k_gain)


def _ada_row(ada_ref, row):
    r = ada_ref[pl.ds(row, 1), :]
    d = r.shape[1] // 3
    return r[:, 0:d], r[:, d:2 * d], r[:, 2 * d:]


def _modulated_norm(xt, mult, shift):
    ms = jnp.mean(xt * xt, axis=-1, keepdims=True)
    return (xt * lax.rsqrt(ms + RMS_EPS) * mult + shift).astype(BF16)


def _head_norm(t, gain, bd):
    t2 = (t * t).astype(BF16)
    w = bd.shape[0]
    ss = jnp.concatenate([jnp.dot(t2[:, c:c + w], bd, preferred_element_type=F32)
                          for c in range(0, t.shape[1], w)], axis=1)
    return t * lax.rsqrt(ss * (1.0 / HEAD_DIM) + RMS_EPS) * gain


def _rope(t, cos, sin_lo, sin_hi):
    outs = []
    for c in range(t.shape[1] // LANES):
        tc = t[:, c * LANES:(c + 1) * LANES]
        up = pltpu.roll(tc, LANES - HEAD_DIM // 4, 1)
        dn = pltpu.roll(tc, HEAD_DIM // 4, 1)
        outs.append(tc * cos + up * sin_lo + dn * sin_hi)
    return jnp.concatenate(outs, axis=1)


def _in_proj_kernel(x_ref, xp_ref, xn_ref, ctx_ref, ada_ref, ng_ref, w_ref, g_ref,
                    cos_ref, slo_ref, shi_ref, bd_ref, cw_ref, cb_ref, rpb_ref,
                    qrot_ref, qpl_ref, k_ref, v_ref, ga_ref, conv_ref, kc_ref, vc_ref, bias_ref,
                    hext_ref, cu_ref, wb_ref, *, tm, sub, da, dc, ctx_row, bias_blocks):
    j = pl.program_id(1)
    nj = pl.num_programs(1)

    @pl.when((pl.program_id(0) == 0) & (j == 0))
    def _():
        for c in range(0, w_ref.shape[1], MXU_DIM):
            wb_ref[:, c:c + MXU_DIM] = w_ref[:, c:c + MXU_DIM].astype(BF16)

    step = pl.program_id(0) * nj + j
    _bias_tiles(rpb_ref, bias_ref, jnp.minimum(step, bias_blocks - 1) * bias_ref.shape[0])
    shift, scale, _ = _ada_row(ada_ref, pl.program_id(0))
    mult = ng_ref[...] * (1.0 + scale)
    n_sub = tm // sub
    cr = ctx_ref.shape[1]

    def proj(src, lo, width):
        return jnp.dot(hext_ref[src, :], wb_ref[:, lo:lo + width], preferred_element_type=F32)

    bd = bd_ref[...]
    hext_ref[0:HALO, :] = _modulated_norm(xp_ref[0], mult, shift)
    hext_ref[HALO + tm:2 * HALO + tm, :] = _modulated_norm(xn_ref[0], mult, shift)
    shift_c, scale_c, _ = _ada_row(ada_ref, ctx_row)
    hext_ref[2 * HALO + tm:, :] = _modulated_norm(ctx_ref[0], ng_ref[...] * (1.0 + scale_c), shift_c)

    for n in range(n_sub):
        r0 = n * sub
        hext_ref[HALO + r0:HALO + r0 + sub, :] = _modulated_norm(x_ref[0, r0:r0 + sub, :], mult, shift)
    for n in range(n_sub):
        r0 = n * sub
        rows = slice(r0, r0 + sub)
        h = slice(HALO + r0, HALO + r0 + sub)
        with_ctx = n == n_sub - 1
        h_kv = slice(HALO + r0, 2 * HALO + tm + cr) if with_ctx else h
        cos, slo, shi = cos_ref[rows, :], slo_ref[rows, :], shi_ref[rows, :]


        hx = slice(r0, r0 + sub + 2 * HALO)
        cu = proj(hx, 4 * da, dc) * proj(hx, 4 * da + 2 * dc, dc)
        row = lax.broadcasted_iota(jnp.int32, (sub + 2 * HALO, 1), 0) + r0
        inside = ((row >= HALO) | (j > 0)) & ((row < HALO + tm) | (j < nj - 1))
        cu_ref[n] = jnp.where(inside, cu, 0.0)
        y = (cb_ref[...]
             + cw_ref[0, 0:1, :] * cu_ref[n, HALO - 1:HALO - 1 + sub, :]
             + cw_ref[0, 1:2, :] * cu_ref[n, HALO:HALO + sub, :]
             + cw_ref[0, 2:3, :] * cu_ref[n, HALO + 1:HALO + 1 + sub, :])
        bg = proj(h, 4 * da + dc, dc)
        zc = proj(h, 4 * da + 3 * dc, dc)
        conv_ref[0, rows, :] = (bg * y * _silu(zc)).astype(BF16)

        qn = _head_norm(proj(h, 0, da), g_ref[0:1, :], bd)
        qpl_ref[0, rows, :] = qn.astype(BF16)
        qrot_ref[0, rows, :] = _rope(qn, cos, slo, shi).astype(BF16)

        kf = proj(h_kv, da, da)
        kn = _head_norm(kf[0:sub], g_ref[1:2, :], bd)
        k_ref[0, rows, :] = _rope(kn, cos, slo, shi).astype(BF16)
        if with_ctx:
            kc_ref[0] = _head_norm(kf[sub + HALO:], g_ref[1:2, :], bd).astype(BF16)

        ga_ref[0, rows, :] = _silu(proj(h, 3 * da, da)).astype(BF16)
        vf = proj(h_kv, 2 * da, da)
        v_ref[0, rows, :] = vf[0:sub].astype(BF16)
        if with_ctx:
            vc_ref[0] = vf[sub + HALO:].astype(BF16)


def _in_proj(x, ctx, ada, ctx_row, norm_g, w_in, gains, cos, slo, shi, bd, conv_w, conv_b, rpb, *, tm, sub):
    b, s, d = x.shape
    n_heads, ndr, ndc = rpb.shape
    assert ndr == 2 * WIN_H - 1 and ndc == 2 * WIN_W - 1 and ndc <= GRID_W
    rpb_pad = jnp.pad(rpb, ((0, 0), (0, 0), (0, GRID_W - ndc)))
    n_tiles = (n_heads // 2) * (ndr - 1)
    bias_blocks = max(k for k in range(1, b * (s // tm) + 1) if n_tiles % k == 0)
    tiles_per_step = n_tiles // bias_blocks
    l = ctx.shape[1]
    assert s % tm == 0 and tm % sub == 0 and sub % HALO == 0
    cr = l // (s // tm)
    assert l % (s // tm) == 0 and cr % BF16_ROWS == 0
    dc = conv_w.shape[2]
    da = (w_in.shape[1] - 4 * dc) // 4
    nh = tm // HALO
    last_halo = s // HALO - 1
    tok = lambda width: pl.BlockSpec((1, tm, width), lambda i, j: (i, j, 0))
    const = lambda shape: pl.BlockSpec(shape, lambda i, j: (0,) * len(shape))
    tab = pl.BlockSpec((tm, LANES), lambda i, j: (j, 0))
    out_tok = jax.ShapeDtypeStruct((b, s, da), BF16)
    out_ctx = jax.ShapeDtypeStruct((b, l, da), BF16)
    ctx_spec = lambda width: pl.BlockSpec((1, cr, width), lambda i, j: (i, j, 0))
    nj = s // tm
    bias_spec = pl.BlockSpec((tiles_per_step, 2 * GRID_W, 2 * GRID_W),
                             lambda i, j: (jnp.minimum(i * nj + j, bias_blocks - 1), 0, 0))
    outs = pl.pallas_call(
        functools.partial(_in_proj_kernel, tm=tm, sub=sub, da=da, dc=dc, ctx_row=ctx_row,
                          bias_blocks=bias_blocks),
        grid=(b, s // tm),
        in_specs=[tok(d),
                  pl.BlockSpec((1, HALO, d), lambda i, j: (i, jnp.maximum(j * nh - 1, 0), 0)),
                  pl.BlockSpec((1, HALO, d), lambda i, j: (i, jnp.minimum((j + 1) * nh, last_halo), 0)),
                  ctx_spec(d),
                  const(ada.shape),
                  const((1, d)), const(w_in.shape), const(gains.shape),
                  tab, tab, tab, const(bd.shape), const((1, CONV_K, dc)), const((1, dc)),
                  const(rpb_pad.shape)],
        out_specs=[tok(da), tok(da), tok(da), tok(da), tok(da), tok(dc), ctx_spec(da), ctx_spec(da), bias_spec],
        out_shape=[out_tok, out_tok, out_tok, out_tok, out_tok,
                   jax.ShapeDtypeStruct((b, s, dc), BF16), out_ctx, out_ctx,
                   jax.ShapeDtypeStruct((n_tiles, 2 * GRID_W, 2 * GRID_W), F32)],
        scratch_shapes=[pltpu.VMEM((tm + 2 * HALO + cr, d), BF16),
                        pltpu.VMEM((tm // sub, sub + 2 * HALO, dc), F32),
                        pltpu.VMEM(w_in.shape, BF16)],
        compiler_params=pltpu.CompilerParams(vmem_limit_bytes=VMEM_LIMIT,
                                             dimension_semantics=("arbitrary", "arbitrary")),
        name="in_proj",
    )(x, x, x, ctx, ada, norm_g, w_in, gains, cos, slo, shi, bd, conv_w, conv_b, rpb_pad)
    bias = outs[-1].reshape(n_heads // 2, ndr - 1, 2 * GRID_W, 2 * GRID_W)
    return (*outs[:-1], bias)


def _dot_nt(a, b):
    return lax.dot_general(a, b, (((1,), (1,)), ((), ())), preferred_element_type=F32)


def _attn_kernel(qrot_ref, qpl_ref, k_ref, v_ref, kc_ref, vc_ref, ga_ref, bias_ref, o_ref,
                 s_lat_ref, s_ctx_ref, p_lat_ref, p_ctx_ref, *, rows, group, n_batch, steps):
    n_units = n_batch * rows // group
    nk = WIN_H * GRID_W

    lane = lax.broadcasted_iota(jnp.int32, (GRID_W, LANES), 1)
    first_head = lane < HEAD_DIM

    def stack_heads(q2):
        zero = jnp.zeros_like(q2)
        return jnp.concatenate([jnp.where(first_head, q2, zero), jnp.where(first_head, zero, q2)], axis=0)

    def geometry(u, g):
        r = u * group + g
        if isinstance(r, int):
            bb, i = divmod(r, rows)
            rs = min(max(i - WIN_H // 2, 0), rows - WIN_H)
            return bb, i - rs, i * GRID_W, rs * GRID_W
        bb = r // rows
        i = r % rows
        rs = jnp.clip(i - WIN_H // 2, 0, rows - WIN_H)
        return bb, i - rs, pl.multiple_of(i * GRID_W, GRID_W), pl.multiple_of(rs * GRID_W, GRID_W)


    def scores_matmul(u):
        out = []
        for g in range(group):
            bb, off, tok0, key0 = geometry(u, g)
            qs = stack_heads(qrot_ref[bb, pl.ds(tok0, GRID_W), :])
            kband = k_ref[bb, pl.ds(key0, nk), :]
            qp = stack_heads(qpl_ref[bb, pl.ds(tok0, GRID_W), :])
            out.append((_dot_nt(qs, kband), off, _dot_nt(qp, kc_ref[bb])))
        return out


    def scores_store(slot, vals):
        for g, (s_lat, off, s_ctx) in enumerate(vals):
            first = (WIN_H - 1) - off
            bias = jnp.concatenate([bias_ref[0, first + 2 * t] for t in range(WIN_H // 2)], axis=1)
            s_lat_ref[slot, g] = s_lat + bias
            s_ctx_ref[slot, g] = s_ctx

    def softmax(slot):
        for g in range(group):
            s_lat = s_lat_ref[slot, g]
            s_ctx = s_ctx_ref[slot, g]
            m = jnp.maximum(jnp.max(s_lat, axis=-1, keepdims=True), jnp.max(s_ctx, axis=-1, keepdims=True))
            p_lat_ref[slot, g] = jnp.exp2(s_lat - m).astype(BF16)
            p_ctx_ref[slot, g] = jnp.exp2(s_ctx - m).astype(BF16)

    ones_lat = jnp.ones((nk, LANES), BF16)
    ones_ctx = jnp.ones((kc_ref.shape[1], LANES), BF16)

    def pv_matmul(u, slot):
        out = []
        for g in range(group):
            bb, _, _, key0 = geometry(u, g)
            v_lat = jnp.concatenate([v_ref[bb, pl.ds(key0, nk), :], ones_lat], axis=1)
            v_ctx = jnp.concatenate([vc_ref[bb], ones_ctx], axis=1)
            out.append(jnp.dot(p_lat_ref[slot, g], v_lat, preferred_element_type=F32)
                       + jnp.dot(p_ctx_ref[slot, g], v_ctx, preferred_element_type=F32))
        return out

    def pv_store(u, vals):
        for g, ol in enumerate(vals):
            bb, _, tok0, _ = geometry(u, g)
            o = ol[:, 0:LANES] * (1.0 / ol[:, LANES:])
            o2 = jnp.where(first_head, o[0:GRID_W], o[GRID_W:2 * GRID_W])
            gate = ga_ref[bb, pl.ds(tok0, GRID_W), :].astype(F32)
            o_ref[bb, pl.ds(tok0, GRID_W), :] = (o2 * gate).astype(BF16)

    scores_store(0, scores_matmul(0))
    sc = scores_matmul(1)
    softmax(0)
    scores_store(1, sc)

    def step(t, slot):
        sc = scores_matmul(t)
        pv = pv_matmul(t - 2, slot)
        softmax(1 - slot)
        scores_store(slot, sc)
        pv_store(t - 2, pv)

    n_steady, n_left = divmod(n_units - 2, steps)

    def steady(n, carry):
        for j in range(steps):
            step(steps * n + 2 + j, j % 2)
        return carry

    lax.fori_loop(0, n_steady, steady, 0)
    for j in range(n_left):
        step(steps * n_steady + 2 + j, j % 2)

    pv = pv_matmul(n_units - 2, 0)
    softmax(1)
    pv_store(n_units - 2, pv)
    pv_store(n_units - 1, pv_matmul(n_units - 1, 1))


def _attention(qrot, qpl, k, v, kc, vc, ga, bias, *, group, n_batch, steps):
    b, s, da = qrot.shape
    l = vc.shape[1]
    rows = s // GRID_W
    n_hp = da // LANES
    nq = 2 * GRID_W
    nk = WIN_H * GRID_W
    assert rows % (2 * group) == 0 and b % n_batch == 0 and steps % 2 == 0
    tok = pl.BlockSpec((n_batch, s, LANES), lambda hp, i: (i, 0, hp))
    ctx_tok = pl.BlockSpec((n_batch, l, LANES), lambda hp, i: (i, 0, hp))
    return pl.pallas_call(
        functools.partial(_attn_kernel, rows=rows, group=group, n_batch=n_batch, steps=steps),
        grid=(n_hp, b // n_batch),
        in_specs=[tok, tok, tok, tok, ctx_tok, ctx_tok, tok,
                  pl.BlockSpec((1,) + bias.shape[1:], lambda hp, i: (hp, 0, 0, 0))],
        out_specs=tok,
        out_shape=jax.ShapeDtypeStruct((b, s, da), BF16),
        scratch_shapes=[pltpu.VMEM((2, group, nq, nk), F32),
                        pltpu.VMEM((2, group, nq, l), F32),
                        pltpu.VMEM((2, group, nq, nk), BF16),
                        pltpu.VMEM((2, group, nq, l), BF16)],
        compiler_params=pltpu.CompilerParams(vmem_limit_bytes=VMEM_LIMIT),
        name="attn",
    )(qrot, qpl, k, v, kc, vc, ga, bias)


X_RING = 3


def _out_proj_kernel(x_hbm, a_ref, c_ref, ada_ref, w_ref, o_ref, xbuf_ref, sem_ref, *, da, tm, nj):
    t = pl.program_id(0)
    n = pl.num_programs(0)

    def x_copy(tile, slot):
        return pltpu.make_async_copy(
            x_hbm.at[tile // nj, pl.ds(pl.multiple_of((tile % nj) * tm, tm), tm), :],
            xbuf_ref.at[slot], sem_ref.at[slot])

    @pl.when(t == 0)
    def _():
        for k in range(X_RING - 1):
            x_copy(k, k).start()

    @pl.when(t + (X_RING - 1) < n)
    def _():
        x_copy(t + (X_RING - 1), (t + (X_RING - 1)) % X_RING).start()

    slot = t % X_RING
    x_copy(t, slot).wait()
    _, _, gate = _ada_row(ada_ref, t // nj)
    upd = (jnp.dot(a_ref[0], w_ref[0:da, :].astype(BF16), preferred_element_type=F32)
           + jnp.dot(c_ref[0], w_ref[da:, :].astype(BF16), preferred_element_type=F32))
    o_ref[0] = xbuf_ref[slot] + gate * upd


def _out_proj(x, attn, conv, ada, w_out, *, tm):
    b, s, d = x.shape
    da = attn.shape[2]
    dc = conv.shape[2]
    nj = s // tm
    assert s % tm == 0 and b * nj >= X_RING
    tok = lambda width: pl.BlockSpec((1, tm, width), lambda t: (t // nj, t % nj, 0))
    return pl.pallas_call(
        functools.partial(_out_proj_kernel, da=da, tm=tm, nj=nj),
        grid=(b * nj,),
        in_specs=[pl.BlockSpec(memory_space=pl.ANY),
                  tok(da), tok(dc),
                  pl.BlockSpec(ada.shape, lambda t: (0, 0)),
                  pl.BlockSpec(w_out.shape, lambda t: (0, 0))],
        out_specs=tok(d),
        out_shape=jax.ShapeDtypeStruct((b, s, d), F32),
        scratch_shapes=[pltpu.VMEM((X_RING, tm, d), F32),
                        pltpu.SemaphoreType.DMA((X_RING,))],
        compiler_params=pltpu.CompilerParams(vmem_limit_bytes=VMEM_LIMIT,
                                             dimension_semantics=("arbitrary",)),
        name="out_proj",
    )(x, attn, conv, ada, w_out)


def _rope_tables(s):
    nf = HEAD_DIM // 4
    inv = (ROPE_THETA ** (-np.arange(nf, dtype=np.float32) / nf)).astype(np.float32)
    pos = np.arange(s)
    lane = np.arange(LANES)
    d = lane % HEAD_DIM
    axis = d // (2 * nf)
    half = (d % (2 * nf)) // nf
    coord = np.where(axis[None, :] == 0, (pos // GRID_W)[:, None], (pos % GRID_W)[:, None]).astype(np.float32)
    ang = (coord * inv[d % nf][None, :]).astype(np.float32)
    cos = np.cos(ang).astype(np.float32)
    sin = np.sin(ang).astype(np.float32)
    sin_lo = np.where(half[None, :] == 0, -sin, 0.0).astype(np.float32)
    sin_hi = np.where(half[None, :] == 1, sin, 0.0).astype(np.float32)
    return jnp.asarray(cos), jnp.asarray(sin_lo), jnp.asarray(sin_hi)


def _bias_tiles(rpb_ref, o_ref, first_tile):
    n_pairs = rpb_ref.shape[1] - 1
    cq = lax.broadcasted_iota(jnp.int32, (GRID_W, LANES), 0)
    ck = lax.broadcasted_iota(jnp.int32, (GRID_W, LANES), 1) % GRID_W
    col_start = jnp.clip(cq - WIN_W // 2, 0, GRID_W - WIN_W)
    valid = (ck >= col_start) & (ck < col_start + WIN_W)
    for e in range(o_ref.shape[0]):
        t = first_tile + e
        hp = t // n_pairs
        dr = t % n_pairs
        for a in range(2):
            two = rpb_ref[2 * hp + a, pl.ds(dr, 2), :]
            lanes = jnp.broadcast_to(jnp.concatenate([two[0:1], two[1:2]], axis=1), (GRID_W, LANES))
            toeplitz = pltpu.roll(lanes, LANES - (WIN_W - 1), 1, stride=1, stride_axis=0)
            o_ref[e, a * GRID_W:(a + 1) * GRID_W, :] = jnp.where(valid, toeplitz * LOG2_E, MASK_VALUE)


def kernel(x, c, ctx, c_ctx, w_ada, b_ada, norm_g, w_in, q_norm_g, k_norm_g, rpb, conv_w, conv_b, w_out):
    depth = w_ada.shape[0]
    b, s, d = x.shape
    dc = conv_w.shape[2]
    da = (w_in.shape[2] - 4 * dc) // 4
    rows = s // GRID_W
    assert depth == 1 and s % GRID_W == 0 and rows >= WIN_H and da % LANES == 0

    cos, slo, shi = _rope_tables(s)
    assert da % MXU_DIM == 0 and MXU_DIM % HEAD_DIM == 0
    seg = np.arange(MXU_DIM) // HEAD_DIM
    bd = jnp.asarray((seg[:, None] == seg[None, :]).astype(np.float32), dtype=BF16)

    ada, gains = _prep(c, c_ctx, w_ada[0], b_ada, q_norm_g, k_norm_g, da, steps=PREP_STEPS)

    qrot, qpl, k, v, ga, conv, kc, vc, bias = _in_proj(x, ctx, ada, b, norm_g, w_in[0], gains, cos, slo, shi, bd,
                                                       conv_w, conv_b, rpb[0], tm=IN_PROJ_ROWS, sub=IN_PROJ_SUB)
    attn = _attention(qrot, qpl, k, v, kc, vc, ga, bias, group=ATTN_GROUP, n_batch=ATTN_BATCH, steps=ATTN_STEPS)
    return _out_proj(x, attn, conv, ada, w_out[0], tm=OUT_PROJ_ROWS)
```

```python
import functools

import numpy as np
import jax
import jax.numpy as jnp
from jax import lax
from jax.experimental import pallas as pl
from jax.experimental.pallas import tpu as pltpu

F32 = jnp.float32
BF16 = jnp.bfloat16

HEAD_DIM = 64
GRID_W = 64
WIN_H = 8
WIN_W = 16
CONV_K = 3
ROPE_THETA = 10000.0
RMS_EPS = 1e-6
MASK_VALUE = -1e30
LOG2_E = 1.4426950408889634

LANES = 128
MXU_DIM = 256
F32_ROWS = 8
BF16_ROWS = 16
HALO = BF16_ROWS
VMEM_CAPACITY = 64 * 1024 * 1024
VMEM_LIMIT = VMEM_CAPACITY * 7 // 8

PREP_STEPS = 4
IN_PROJ_ROWS, IN_PROJ_SUB = 1024, 512
OUT_PROJ_ROWS = 1024
ATTN_GROUP, ATTN_BATCH, ATTN_STEPS = 2, 4, 30


def _silu(z):
    return z * jax.nn.sigmoid(z)


def _prep_kernel(c_ref, cctx_ref, w_ref, b_ref, qg_ref, kg_ref, ada_ref, gains_ref):
    nb, d = c_ref.shape
    cctx = cctx_ref[...].reshape(1, d)
    cond = jnp.concatenate([c_ref[...], jnp.broadcast_to(cctx, (ada_ref.shape[0] - nb, d))], axis=0)
    a = _silu(cond).astype(BF16)
    ada_ref[...] = jnp.dot(a, w_ref[...].astype(BF16), preferred_element_type=F32) + b_ref[...]
    g = jnp.concatenate([qg_ref[...] * (HEAD_DIM ** -0.5 * LOG2_E), kg_ref[...]], axis=0)
    gains_ref[...] = jnp.concatenate([g] * (gains_ref.shape[1] // HEAD_DIM), axis=1)


def _prep(c, c_ctx, w_ada, b_ada, q_gain, k_gain, d_attn, *, steps):
    nb, d = c.shape
    assert nb % F32_ROWS == 0
    rows = nb + F32_ROWS
    n = w_ada.shape[1]
    tn = n // steps
    assert tn % LANES == 0
    return pl.pallas_call(
        _prep_kernel,
        grid=(steps,),
        in_specs=[pl.BlockSpec((nb, d), lambda i: (0, 0)),
                  pl.BlockSpec((d,), lambda i: (0,)),
                  pl.BlockSpec((d, tn), lambda i: (0, i)),
                  pl.BlockSpec((1, tn), lambda i: (0, i)),
                  pl.BlockSpec((1, HEAD_DIM), lambda i: (0, 0)),
                  pl.BlockSpec((1, HEAD_DIM), lambda i: (0, 0))],
        out_specs=[pl.BlockSpec((rows, tn), lambda i: (0, i)),
                   pl.BlockSpec((2, d_attn), lambda i: (0, 0))],
        out_shape=[jax.ShapeDtypeStruct((rows, n), F32),
                   jax.ShapeDtypeStruct((2, d_attn), F32)],
        compiler_params=pltpu.CompilerParams(vmem_limit_bytes=VMEM_LIMIT),
        name="prep",
    )(c, c_ctx, w_ada, b_ada, q_gain, k_gain)


def _ada_row(ada_ref, row):
    r = ada_ref[pl.ds(row, 1), :]
    d = r.shape[1] // 3
    return r[:, 0:d], r[:, d:2 * d], r[:, 2 * d:]


def _modulated_norm(xt, mult, shift):
    ms = jnp.mean(xt * xt, axis=-1, keepdims=True)
    return (xt * lax.rsqrt(ms + RMS_EPS) * mult + shift).astype(BF16)


def _head_norm(t, gain, bd):
    t2 = (t * t).astype(BF16)
    w = bd.shape[0]
    ss = jnp.concatenate([jnp.dot(t2[:, c:c + w], bd, preferred_element_type=F32)
                          for c in range(0, t.shape[1], w)], axis=1)
    return t * lax.rsqrt(ss * (1.0 / HEAD_DIM) + RMS_EPS) * gain


def _rope(t, cos, sin_lo, sin_hi):
    outs = []
    for c in range(t.shape[1] // LANES):
        tc = t[:, c * LANES:(c + 1) * LANES]
        up = pltpu.roll(tc, LANES - HEAD_DIM // 4, 1)
        dn = pltpu.roll(tc, HEAD_DIM // 4, 1)
        outs.append(tc * cos + up * sin_lo + dn * sin_hi)
    return jnp.concatenate(outs, axis=1)


def _in_proj_kernel(x_ref, xp_ref, xn_ref, ctx_ref, ada_ref, ng_ref, w_ref, g_ref,
                    cos_ref, slo_ref, shi_ref, bd_ref, cw_ref, cb_ref, rpb_ref,
                    qrot_ref, qpl_ref, k_ref, v_ref, ga_ref, conv_ref, kc_ref, vc_ref, bias_ref,
                    hext_ref, cu_ref, wb_ref, *, tm, sub, da, dc, ctx_row, bias_blocks):
    j = pl.program_id(1)
    nj = pl.num_programs(1)

    @pl.when((pl.program_id(0) == 0) & (j == 0))
    def _():
        for c in range(0, w_ref.shape[1], MXU_DIM):
            wb_ref[:, c:c + MXU_DIM] = w_ref[:, c:c + MXU_DIM].astype(BF16)

    step = pl.program_id(0) * nj + j
    _bias_tiles(rpb_ref, bias_ref, jnp.minimum(step, bias_blocks - 1) * bias_ref.shape[0])
    shift, scale, _ = _ada_row(ada_ref, pl.program_id(0))
    mult = ng_ref[...] * (1.0 + scale)
    n_sub = tm // sub
    cr = ctx_ref.shape[1]

    def proj(src, lo, width):
        return jnp.dot(hext_ref[src, :], wb_ref[:, lo:lo + width], preferred_element_type=F32)

    bd = bd_ref[...]
    hext_ref[0:HALO, :] = _modulated_norm(xp_ref[0], mult, shift)
    hext_ref[HALO + tm:2 * HALO + tm, :] = _modulated_norm(xn_ref[0], mult, shift)
    shift_c, scale_c, _ = _ada_row(ada_ref, ctx_row)
    hext_ref[2 * HALO + tm:, :] = _modulated_norm(ctx_ref[0], ng_ref[...] * (1.0 + scale_c), shift_c)

    for n in range(n_sub):
        r0 = n * sub
        hext_ref[HALO + r0:HALO + r0 + sub, :] = _modulated_norm(x_ref[0, r0:r0 + sub, :], mult, shift)
    for n in range(n_sub):
        r0 = n * sub
        rows = slice(r0, r0 + sub)
        h = slice(HALO + r0, HALO + r0 + sub)
        with_ctx = n == n_sub - 1
        h_kv = slice(HALO + r0, 2 * HALO + tm + cr) if with_ctx else h
        cos, slo, shi = cos_ref[rows, :], slo_ref[rows, :], shi_ref[rows, :]


        hx = slice(r0, r0 + sub + 2 * HALO)
        cu = proj(hx, 4 * da, dc) * proj(hx, 4 * da + 2 * dc, dc)
        row = lax.broadcasted_iota(jnp.int32, (sub + 2 * HALO, 1), 0) + r0
        inside = ((row >= HALO) | (j > 0)) & ((row < HALO + tm) | (j < nj - 1))
        cu_ref[n] = jnp.where(inside, cu, 0.0)
        y = (cb_ref[...]
             + cw_ref[0, 0:1, :] * cu_ref[n, HALO - 1:HALO - 1 + sub, :]
             + cw_ref[0, 1:2, :] * cu_ref[n, HALO:HALO + sub, :]
             + cw_ref[0, 2:3, :] * cu_ref[n, HALO + 1:HALO + 1 + sub, :])
        bg = proj(h, 4 * da + dc, dc)
        zc = proj(h, 4 * da + 3 * dc, dc)
        conv_ref[0, rows, :] = (bg * y * _silu(zc)).astype(BF16)

        qn = _head_norm(proj(h, 0, da), g_ref[0:1, :], bd)
        qpl_ref[0, rows, :] = qn.astype(BF16)
        qrot_ref[0, rows, :] = _rope(qn, cos, slo, shi).astype(BF16)

        kf = proj(h_kv, da, da)
        kn = _head_norm(kf[0:sub], g_ref[1:2, :], bd)
        k_ref[0, rows, :] = _rope(kn, cos, slo, shi).astype(BF16)
        if with_ctx:
            kc_ref[0] = _head_norm(kf[sub + HALO:], g_ref[1:2, :], bd).astype(BF16)

        ga_ref[0, rows, :] = _silu(proj(h, 3 * da, da)).astype(BF16)
        vf = proj(h_kv, 2 * da, da)
        v_ref[0, rows, :] = vf[0:sub].astype(BF16)
        if with_ctx:
            vc_ref[0] = vf[sub + HALO:].astype(BF16)


def _in_proj(x, ctx, ada, ctx_row, norm_g, w_in, gains, cos, slo, shi, bd, conv_w, conv_b, rpb, *, tm, sub):
    b, s, d = x.shape
    n_heads, ndr, ndc = rpb.shape
    assert ndr == 2 * WIN_H - 1 and ndc == 2 * WIN_W - 1 and ndc <= GRID_W
    rpb_pad = jnp.pad(rpb, ((0, 0), (0, 0), (0, GRID_W - ndc)))
    n_tiles = (n_heads // 2) * (ndr - 1)
    bias_blocks = max(k for k in range(1, b * (s // tm) + 1) if n_tiles % k == 0)
    tiles_per_step = n_tiles // bias_blocks
    l = ctx.shape[1]
    assert s % tm == 0 and tm % sub == 0 and sub % HALO == 0
    cr = l // (s // tm)
    assert l % (s // tm) == 0 and cr % BF16_ROWS == 0
    dc = conv_w.shape[2]
    da = (w_in.shape[1] - 4 * dc) // 4
    nh = tm // HALO
    last_halo = s // HALO - 1
    tok = lambda width: pl.BlockSpec((1, tm, width), lambda i, j: (i, j, 0))
    const = lambda shape: pl.BlockSpec(shape, lambda i, j: (0,) * len(shape))
    tab = pl.BlockSpec((tm, LANES), lambda i, j: (j, 0))
    out_tok = jax.ShapeDtypeStruct((b, s, da), BF16)
    out_ctx = jax.ShapeDtypeStruct((b, l, da), BF16)
    ctx_spec = lambda width: pl.BlockSpec((1, cr, width), lambda i, j: (i, j, 0))
    nj = s // tm
    bias_spec = pl.BlockSpec((tiles_per_step, 2 * GRID_W, 2 * GRID_W),
                             lambda i, j: (jnp.minimum(i * nj + j, bias_blocks - 1), 0, 0))
    outs = pl.pallas_call(
        functools.partial(_in_proj_kernel, tm=tm, sub=sub, da=da, dc=dc, ctx_row=ctx_row,
                          bias_blocks=bias_blocks),
        grid=(b, s // tm),
        in_specs=[tok(d),
                  pl.BlockSpec((1, HALO, d), lambda i, j: (i, jnp.maximum(j * nh - 1, 0), 0)),
                  pl.BlockSpec((1, HALO, d), lambda i, j: (i, jnp.minimum((j + 1) * nh, last_halo), 0)),
                  ctx_spec(d),
                  const(ada.shape),
                  const((1, d)), const(w_in.shape), const(gains.shape),
                  tab, tab, tab, const(bd.shape), const((1, CONV_K, dc)), const((1, dc)),
                  const(rpb_pad.shape)],
        out_specs=[tok(da), tok(da), tok(da), tok(da), tok(da), tok(dc), ctx_spec(da), ctx_spec(da), bias_spec],
        out_shape=[out_tok, out_tok, out_tok, out_tok, out_tok,
                   jax.ShapeDtypeStruct((b, s, dc), BF16), out_ctx, out_ctx,
                   jax.ShapeDtypeStruct((n_tiles, 2 * GRID_W, 2 * GRID_W), F32)],
        scratch_shapes=[pltpu.VMEM((tm + 2 * HALO + cr, d), BF16),
                        pltpu.VMEM((tm // sub, sub + 2 * HALO, dc), F32),
                        pltpu.VMEM(w_in.shape, BF16)],
        compiler_params=pltpu.CompilerParams(vmem_limit_bytes=VMEM_LIMIT,
                                             dimension_semantics=("arbitrary", "arbitrary")),
        name="in_proj",
    )(x, x, x, ctx, ada, norm_g, w_in, gains, cos, slo, shi, bd, conv_w, conv_b, rpb_pad)
    bias = outs[-1].reshape(n_heads // 2, ndr - 1, 2 * GRID_W, 2 * GRID_W)
    return (*outs[:-1], bias)


def _dot_nt(a, b):
    return lax.dot_general(a, b, (((1,), (1,)), ((), ())), preferred_element_type=F32)


def _attn_kernel(qrot_ref, qpl_ref, k_ref, v_ref, kc_ref, vc_ref, ga_ref, bias_ref, o_ref,
                 s_lat_ref, s_ctx_ref, p_lat_ref, p_ctx_ref, *, rows, group, n_batch, steps):
    n_units = n_batch * rows // group
    nk = WIN_H * GRID_W

    lane = lax.broadcasted_iota(jnp.int32, (GRID_W, LANES), 1)
    first_head = lane < HEAD_DIM

    def stack_heads(q2):
        zero = jnp.zeros_like(q2)
        return jnp.concatenate([jnp.where(first_head, q2, zero), jnp.where(first_head, zero, q2)], axis=0)

    def geometry(u, g):
        r = u * group + g
        if isinstance(r, int):
            bb, i = divmod(r, rows)
            rs = min(max(i - WIN_H // 2, 0), rows - WIN_H)
            return bb, i - rs, i * GRID_W, rs * GRID_W
        bb = r // rows
        i = r % rows
        rs = jnp.clip(i - WIN_H // 2, 0, rows - WIN_H)
        return bb, i - rs, pl.multiple_of(i * GRID_W, GRID_W), pl.multiple_of(rs * GRID_W, GRID_W)


    def scores_matmul(u):
        out = []
        for g in range(group):
            bb, off, tok0, key0 = geometry(u, g)
            qs = stack_heads(qrot_ref[bb, pl.ds(tok0, GRID_W), :])
            kband = k_ref[bb, pl.ds(key0, nk), :]
            qp = stack_heads(qpl_ref[bb, pl.ds(tok0, GRID_W), :])
            out.append((_dot_nt(qs, kband), off, _dot_nt(qp, kc_ref[bb])))
        return out


    def scores_store(slot, vals):
        for g, (s_lat, off, s_ctx) in enumerate(vals):
            first = (WIN_H - 1) - off
            bias = jnp.concatenate([bias_ref[0, first + 2 * t] for t in range(WIN_H // 2)], axis=1)
            s_lat_ref[slot, g] = s_lat + bias
            s_ctx_ref[slot, g] = s_ctx

    def softmax(slot):
        for g in range(group):
            s_lat = s_lat_ref[slot, g]
            s_ctx = s_ctx_ref[slot, g]
            m = jnp.maximum(jnp.max(s_lat, axis=-1, keepdims=True), jnp.max(s_ctx, axis=-1, keepdims=True))
            p_lat_ref[slot, g] = jnp.exp2(s_lat - m).astype(BF16)
            p_ctx_ref[slot, g] = jnp.exp2(s_ctx - m).astype(BF16)

    ones_lat = jnp.ones((nk, LANES), BF16)
    ones_ctx = jnp.ones((kc_ref.shape[1], LANES), BF16)

    def pv_matmul(u, slot):
        out = []
        for g in range(group):
            bb, _, _, key0 = geometry(u, g)
            v_lat = jnp.concatenate([v_ref[bb, pl.ds(key0, nk), :], ones_lat], axis=1)
            v_ctx = jnp.concatenate([vc_ref[bb], ones_ctx], axis=1)
            out.append(jnp.dot(p_lat_ref[slot, g], v_lat, preferred_element_type=F32)
                       + jnp.dot(p_ctx_ref[slot, g], v_ctx, preferred_element_type=F32))
        return out

    def pv_store(u, vals):
        for g, ol in enumerate(vals):
            bb, _, tok0, _ = geometry(u, g)
            o = ol[:, 0:LANES] * (1.0 / ol[:, LANES:])
            o2 = jnp.where(first_head, o[0:GRID_W], o[GRID_W:2 * GRID_W])
            gate = ga_ref[bb, pl.ds(tok0, GRID_W), :].astype(F32)
            o_ref[bb, pl.ds(tok0, GRID_W), :] = (o2 * gate).astype(BF16)

    scores_store(0, scores_matmul(0))
    sc = scores_matmul(1)
    softmax(0)
    scores_store(1, sc)

    def step(t, slot):
        sc = scores_matmul(t)
        pv = pv_matmul(t - 2, slot)
        softmax(1 - slot)
        scores_store(slot, sc)
        pv_store(t - 2, pv)

    n_steady, n_left = divmod(n_units - 2, steps)

    def steady(n, carry):
        for j in range(steps):
            step(steps * n + 2 + j, j % 2)
        return carry

    lax.fori_loop(0, n_steady, steady, 0)
    for j in range(n_left):
        step(steps * n_steady + 2 + j, j % 2)

    pv = pv_matmul(n_units - 2, 0)
    softmax(1)
    pv_store(n_units - 2, pv)
    pv_store(n_units - 1, pv_matmul(n_units - 1, 1))


def _attention(qrot, qpl, k, v, kc, vc, ga, bias, *, group, n_batch, steps):
    b, s, da = qrot.shape
    l = vc.shape[1]
    rows = s // GRID_W
    n_hp = da // LANES
    nq = 2 * GRID_W
    nk = WIN_H * GRID_W
    assert rows % (2 * group) == 0 and b % n_batch == 0 and steps % 2 == 0
    tok = pl.BlockSpec((n_batch, s, LANES), lambda hp, i: (i, 0, hp))
    ctx_tok = pl.BlockSpec((n_batch, l, LANES), lambda hp, i: (i, 0, hp))
    return pl.pallas_call(
        functools.partial(_attn_kernel, rows=rows, group=group, n_batch=n_batch, steps=steps),
        grid=(n_hp, b // n_batch),
        in_specs=[tok, tok, tok, tok, ctx_tok, ctx_tok, tok,
                  pl.BlockSpec((1,) + bias.shape[1:], lambda hp, i: (hp, 0, 0, 0))],
        out_specs=tok,
        out_shape=jax.ShapeDtypeStruct((b, s, da), BF16),
        scratch_shapes=[pltpu.VMEM((2, group, nq, nk), F32),
                        pltpu.VMEM((2, group, nq, l), F32),
                        pltpu.VMEM((2, group, nq, nk), BF16),
                        pltpu.VMEM((2, group, nq, l), BF16)],
        compiler_params=pltpu.CompilerParams(vmem_limit_bytes=VMEM_LIMIT),
        name="attn",
    )(qrot, qpl, k, v, kc, vc, ga, bias)


X_RING = 3


def _out_proj_kernel(x_hbm, a_hbm, c_hbm, ada_ref, w_ref, o_ref, xbuf_ref, abuf_ref, cbuf_ref, sem_ref,
                     *, da, tm, nj):
    t = pl.program_id(0)
    n = pl.num_programs(0)
    streams = ((x_hbm, xbuf_ref), (a_hbm, abuf_ref), (c_hbm, cbuf_ref))

    def copies(tile, slot):
        rows = pl.ds(pl.multiple_of((tile % nj) * tm, tm), tm)
        return [pltpu.make_async_copy(hbm.at[tile // nj, rows, :], buf.at[slot], sem_ref.at[k, slot])
                for k, (hbm, buf) in enumerate(streams)]

    @pl.when(t == 0)
    def _():
        for k in range(X_RING - 1):
            for cp in copies(k, k):
                cp.start()

    @pl.when(t + (X_RING - 1) < n)
    def _():
        for cp in copies(t + (X_RING - 1), (t + (X_RING - 1)) % X_RING):
            cp.start()

    slot = t % X_RING
    for cp in copies(t, slot):
        cp.wait()
    _, _, gate = _ada_row(ada_ref, t // nj)
    upd = (jnp.dot(abuf_ref[slot], w_ref[0:da, :].astype(BF16), preferred_element_type=F32)
           + jnp.dot(cbuf_ref[slot], w_ref[da:, :].astype(BF16), preferred_element_type=F32))
    o_ref[0] = xbuf_ref[slot] + gate * upd


def _out_proj(x, attn, conv, ada, w_out, *, tm):
    b, s, d = x.shape
    da = attn.shape[2]
    dc = conv.shape[2]
    nj = s // tm
    assert s % tm == 0 and b * nj >= X_RING
    tok = lambda width: pl.BlockSpec((1, tm, width), lambda t: (t // nj, t % nj, 0))
    return pl.pallas_call(
        functools.partial(_out_proj_kernel, da=da, tm=tm, nj=nj),
        grid=(b * nj,),
        in_specs=[pl.BlockSpec(memory_space=pl.ANY),
                  pl.BlockSpec(memory_space=pl.ANY),
                  pl.BlockSpec(memory_space=pl.ANY),
                  pl.BlockSpec(ada.shape, lambda t: (0, 0)),
                  pl.BlockSpec(w_out.shape, lambda t: (0, 0))],
        out_specs=tok(d),
        out_shape=jax.ShapeDtypeStruct((b, s, d), F32),
        scratch_shapes=[pltpu.VMEM((X_RING, tm, d), F32),
                        pltpu.VMEM((X_RING, tm, da), BF16),
                        pltpu.VMEM((X_RING, tm, dc), BF16),
                        pltpu.SemaphoreType.DMA((3, X_RING))],
        compiler_params=pltpu.CompilerParams(vmem_limit_bytes=VMEM_LIMIT,
                                             dimension_semantics=("arbitrary",)),
        name="out_proj",
    )(x, attn, conv, ada, w_out)


def _rope_tables(s):
    nf = HEAD_DIM // 4
    inv = (ROPE_THETA ** (-np.arange(nf, dtype=np.float32) / nf)).astype(np.float32)
    pos = np.arange(s)
    lane = np.arange(LANES)
    d = lane % HEAD_DIM
    axis = d // (2 * nf)
    half = (d % (2 * nf)) // nf
    coord = np.where(axis[None, :] == 0, (pos // GRID_W)[:, None], (pos % GRID_W)[:, None]).astype(np.float32)
    ang = (coord * inv[d % nf][None, :]).astype(np.float32)
    cos = np.cos(ang).astype(np.float32)
    sin = np.sin(ang).astype(np.float32)
    sin_lo = np.where(half[None, :] == 0, -sin, 0.0).astype(np.float32)
    sin_hi = np.where(half[None, :] == 1, sin, 0.0).astype(np.float32)
    return jnp.asarray(cos), jnp.asarray(sin_lo), jnp.asarray(sin_hi)


def _bias_tiles(rpb_ref, o_ref, first_tile):
    n_pairs = rpb_ref.shape[1] - 1
    cq = lax.broadcasted_iota(jnp.int32, (GRID_W, LANES), 0)
    ck = lax.broadcasted_iota(jnp.int32, (GRID_W, LANES), 1) % GRID_W
    col_start = jnp.clip(cq - WIN_W // 2, 0, GRID_W - WIN_W)
    valid = (ck >= col_start) & (ck < col_start + WIN_W)
    for e in range(o_ref.shape[0]):
        t = first_tile + e
        hp = t // n_pairs
        dr = t % n_pairs
        for a in range(2):
            two = rpb_ref[2 * hp + a, pl.ds(dr, 2), :]
            lanes = jnp.broadcast_to(jnp.concatenate([two[0:1], two[1:2]], axis=1), (GRID_W, LANES))
            toeplitz = pltpu.roll(lanes, LANES - (WIN_W - 1), 1, stride=1, stride_axis=0)
            o_ref[e, a * GRID_W:(a + 1) * GRID_W, :] = jnp.where(valid, toeplitz * LOG2_E, MASK_VALUE)


def kernel(x, c, ctx, c_ctx, w_ada, b_ada, norm_g, w_in, q_norm_g, k_norm_g, rpb, conv_w, conv_b, w_out):
    depth = w_ada.shape[0]
    b, s, d = x.shape
    dc = conv_w.shape[2]
    da = (w_in.shape[2] - 4 * dc) // 4
    rows = s // GRID_W
    assert depth == 1 and s % GRID_W == 0 and rows >= WIN_H and da % LANES == 0

    cos, slo, shi = _rope_tables(s)
    assert da % MXU_DIM == 0 and MXU_DIM % HEAD_DIM == 0
    seg = np.arange(MXU_DIM) // HEAD_DIM
    bd = jnp.asarray((seg[:, None] == seg[None, :]).astype(np.float32), dtype=BF16)

    ada, gains = _prep(c, c_ctx, w_ada[0], b_ada, q_norm_g, k_norm_g, da, steps=PREP_STEPS)

    qrot, qpl, k, v, ga, conv, kc, vc, bias = _in_proj(x, ctx, ada, b, norm_g, w_in[0], gains, cos, slo, shi, bd,
                                                       conv_w, conv_b, rpb[0], tm=IN_PROJ_ROWS, sub=IN_PROJ_SUB)
    attn = _attention(qrot, qpl, k, v, kc, vc, ga, bias, group=ATTN_GROUP, n_batch=ATTN_BATCH, steps=ATTN_STEPS)
    return _out_proj(x, attn, conv, ada, w_out[0], tm=OUT_PROJ_ROWS)
```

```python
import functools

import numpy as np
import jax
import jax.numpy as jnp
from jax import lax
from jax.experimental import pallas as pl
from jax.experimental.pallas import tpu as pltpu

F32 = jnp.float32
BF16 = jnp.bfloat16

HEAD_DIM = 64
GRID_W = 64
WIN_H = 8
WIN_W = 16
CONV_K = 3
ROPE_THETA = 10000.0
RMS_EPS = 1e-6
MASK_VALUE = -1e30
LOG2_E = 1.4426950408889634

LANES = 128
MXU_DIM = 256
F32_ROWS = 8
BF16_ROWS = 16
HALO = BF16_ROWS
VMEM_CAPACITY = 64 * 1024 * 1024
VMEM_LIMIT = VMEM_CAPACITY * 7 // 8

PREP_STEPS = 4
IN_PROJ_ROWS, IN_PROJ_SUB = 1024, 512
OUT_PROJ_ROWS = 512
ATTN_GROUP, ATTN_BATCH, ATTN_STEPS = 2, 4, 30


def _silu(z):
    return z * jax.nn.sigmoid(z)


def _prep_kernel(c_ref, cctx_ref, w_ref, b_ref, qg_ref, kg_ref, ada_ref, gains_ref):
    nb, d = c_ref.shape
    cctx = cctx_ref[...].reshape(1, d)
    cond = jnp.concatenate([c_ref[...], jnp.broadcast_to(cctx, (ada_ref.shape[0] - nb, d))], axis=0)
    a = _silu(cond).astype(BF16)
    ada_ref[...] = jnp.dot(a, w_ref[...].astype(BF16), preferred_element_type=F32) + b_ref[...]
    g = jnp.concatenate([qg_ref[...] * (HEAD_DIM ** -0.5 * LOG2_E), kg_ref[...]], axis=0)
    gains_ref[...] = jnp.concatenate([g] * (gains_ref.shape[1] // HEAD_DIM), axis=1)


def _prep(c, c_ctx, w_ada, b_ada, q_gain, k_gain, d_attn, *, steps):
    nb, d = c.shape
    assert nb % F32_ROWS == 0
    rows = nb + F32_ROWS
    n = w_ada.shape[1]
    tn = n // steps
    assert tn % LANES == 0
    return pl.pallas_call(
        _prep_kernel,
        grid=(steps,),
        in_specs=[pl.BlockSpec((nb, d), lambda i: (0, 0)),
                  pl.BlockSpec((d,), lambda i: (0,)),
                  pl.BlockSpec((d, tn), lambda i: (0, i)),
                  pl.BlockSpec((1, tn), lambda i: (0, i)),
                  pl.BlockSpec((1, HEAD_DIM), lambda i: (0, 0)),
                  pl.BlockSpec((1, HEAD_DIM), lambda i: (0, 0))],
        out_specs=[pl.BlockSpec((rows, tn), lambda i: (0, i)),
                   pl.BlockSpec((2, d_attn), lambda i: (0, 0))],
        out_shape=[jax.ShapeDtypeStruct((rows, n), F32),
                   jax.ShapeDtypeStruct((2, d_attn), F32)],
        compiler_params=pltpu.CompilerParams(vmem_limit_bytes=VMEM_LIMIT),
        name="prep",
    )(c, c_ctx, w_ada, b_ada, q_gain, k_gain)


def _ada_row(ada_ref, row):
    r = ada_ref[pl.ds(row, 1), :]
    d = r.shape[1] // 3
    return r[:, 0:d], r[:, d:2 * d], r[:, 2 * d:]


def _modulated_norm(xt, mult, shift):
    ms = jnp.mean(xt * xt, axis=-1, keepdims=True)
    return (xt * lax.rsqrt(ms + RMS_EPS) * mult + shift).astype(BF16)


def _head_norm(t, gain, bd):
    t2 = (t * t).astype(BF16)
    w = bd.shape[0]
    ss = jnp.concatenate([jnp.dot(t2[:, c:c + w], bd, preferred_element_type=F32)
                          for c in range(0, t.shape[1], w)], axis=1)
    return t * lax.rsqrt(ss * (1.0 / HEAD_DIM) + RMS_EPS) * gain


def _rope(t, cos, sin_lo, sin_hi):
    outs = []
    for c in range(t.shape[1] // LANES):
        tc = t[:, c * LANES:(c + 1) * LANES]
        up = pltpu.roll(tc, LANES - HEAD_DIM // 4, 1)
        dn = pltpu.roll(tc, HEAD_DIM // 4, 1)
        outs.append(tc * cos + up * sin_lo + dn * sin_hi)
    return jnp.concatenate(outs, axis=1)


def _in_proj_kernel(x_ref, xp_ref, xn_ref, ctx_ref, ada_ref, ng_ref, w_ref, g_ref,
                    cos_ref, slo_ref, shi_ref, bd_ref, cw_ref, cb_ref, rpb_ref,
                    qrot_ref, qpl_ref, k_ref, v_ref, ga_ref, conv_ref, kc_ref, vc_ref, bias_ref,
                    hext_ref, cu_ref, wb_ref, *, tm, sub, da, dc, ctx_row, bias_blocks):
    j = pl.program_id(1)
    nj = pl.num_programs(1)

    @pl.when((pl.program_id(0) == 0) & (j == 0))
    def _():
        for c in range(0, w_ref.shape[1], MXU_DIM):
            wb_ref[:, c:c + MXU_DIM] = w_ref[:, c:c + MXU_DIM].astype(BF16)

    step = pl.program_id(0) * nj + j
    _bias_tiles(rpb_ref, bias_ref, jnp.minimum(step, bias_blocks - 1) * bias_ref.shape[0])
    shift, scale, _ = _ada_row(ada_ref, pl.program_id(0))
    mult = ng_ref[...] * (1.0 + scale)
    n_sub = tm // sub
    cr = ctx_ref.shape[1]

    def proj(src, lo, width):
        return jnp.dot(hext_ref[src, :], wb_ref[:, lo:lo + width], preferred_element_type=F32)

    bd = bd_ref[...]
    hext_ref[0:HALO, :] = _modulated_norm(xp_ref[0], mult, shift)
    hext_ref[HALO + tm:2 * HALO + tm, :] = _modulated_norm(xn_ref[0], mult, shift)
    shift_c, scale_c, _ = _ada_row(ada_ref, ctx_row)
    hext_ref[2 * HALO + tm:, :] = _modulated_norm(ctx_ref[0], ng_ref[...] * (1.0 + scale_c), shift_c)

    for n in range(n_sub):
        r0 = n * sub
        hext_ref[HALO + r0:HALO + r0 + sub, :] = _modulated_norm(x_ref[0, r0:r0 + sub, :], mult, shift)
    for n in range(n_sub):
        r0 = n * sub
        rows = slice(r0, r0 + sub)
        h = slice(HALO + r0, HALO + r0 + sub)
        with_ctx = n == n_sub - 1
        h_kv = slice(HALO + r0, 2 * HALO + tm + cr) if with_ctx else h
        cos, slo, shi = cos_ref[rows, :], slo_ref[rows, :], shi_ref[rows, :]


        hx = slice(r0, r0 + sub + 2 * HALO)
        cu = proj(hx, 4 * da, dc) * proj(hx, 4 * da + 2 * dc, dc)
        row = lax.broadcasted_iota(jnp.int32, (sub + 2 * HALO, 1), 0) + r0
        inside = ((row >= HALO) | (j > 0)) & ((row < HALO + tm) | (j < nj - 1))
        cu_ref[n] = jnp.where(inside, cu, 0.0)
        y = (cb_ref[...]
             + cw_ref[0, 0:1, :] * cu_ref[n, HALO - 1:HALO - 1 + sub, :]
             + cw_ref[0, 1:2, :] * cu_ref[n, HALO:HALO + sub, :]
             + cw_ref[0, 2:3, :] * cu_ref[n, HALO + 1:HALO + 1 + sub, :])
        bg = proj(h, 4 * da + dc, dc)
        zc = proj(h, 4 * da + 3 * dc, dc)
        conv_ref[0, rows, :] = (bg * y * _silu(zc)).astype(BF16)

        qn = _head_norm(proj(h, 0, da), g_ref[0:1, :], bd)
        qpl_ref[0, rows, :] = qn.astype(BF16)
        qrot_ref[0, rows, :] = _rope(qn, cos, slo, shi).astype(BF16)

        kf = proj(h_kv, da, da)
        kn = _head_norm(kf[0:sub], g_ref[1:2, :], bd)
        k_ref[0, rows, :] = _rope(kn, cos, slo, shi).astype(BF16)
        if with_ctx:
            kc_ref[0] = _head_norm(kf[sub + HALO:], g_ref[1:2, :], bd).astype(BF16)

        ga_ref[0, rows, :] = _silu(proj(h, 3 * da, da)).astype(BF16)
        vf = proj(h_kv, 2 * da, da)
        v_ref[0, rows, :] = vf[0:sub].astype(BF16)
        if with_ctx:
            vc_ref[0] = vf[sub + HALO:].astype(BF16)


def _in_proj(x, ctx, ada, ctx_row, norm_g, w_in, gains, cos, slo, shi, bd, conv_w, conv_b, rpb, *, tm, sub):
    b, s, d = x.shape
    n_heads, ndr, ndc = rpb.shape
    assert ndr == 2 * WIN_H - 1 and ndc == 2 * WIN_W - 1 and ndc <= GRID_W
    rpb_pad = jnp.pad(rpb, ((0, 0), (0, 0), (0, GRID_W - ndc)))
    n_tiles = (n_heads // 2) * (ndr - 1)
    bias_blocks = max(k for k in range(1, b * (s // tm) + 1) if n_tiles % k == 0)
    tiles_per_step = n_tiles // bias_blocks
    l = ctx.shape[1]
    assert s % tm == 0 and tm % sub == 0 and sub % HALO == 0
    cr = l // (s // tm)
    assert l % (s // tm) == 0 and cr % BF16_ROWS == 0
    dc = conv_w.shape[2]
    da = (w_in.shape[1] - 4 * dc) // 4
    nh = tm // HALO
    last_halo = s // HALO - 1
    tok = lambda width: pl.BlockSpec((1, tm, width), lambda i, j: (i, j, 0))
    const = lambda shape: pl.BlockSpec(shape, lambda i, j: (0,) * len(shape))
    tab = pl.BlockSpec((tm, LANES), lambda i, j: (j, 0))
    out_tok = jax.ShapeDtypeStruct((b, s, da), BF16)
    out_ctx = jax.ShapeDtypeStruct((b, l, da), BF16)
    ctx_spec = lambda width: pl.BlockSpec((1, cr, width), lambda i, j: (i, j, 0))
    nj = s // tm
    bias_spec = pl.BlockSpec((tiles_per_step, 2 * GRID_W, 2 * GRID_W),
                             lambda i, j: (jnp.minimum(i * nj + j, bias_blocks - 1), 0, 0))
    outs = pl.pallas_call(
        functools.partial(_in_proj_kernel, tm=tm, sub=sub, da=da, dc=dc, ctx_row=ctx_row,
                          bias_blocks=bias_blocks),
        grid=(b, s // tm),
        in_specs=[tok(d),
                  pl.BlockSpec((1, HALO, d), lambda i, j: (i, jnp.maximum(j * nh - 1, 0), 0)),
                  pl.BlockSpec((1, HALO, d), lambda i, j: (i, jnp.minimum((j + 1) * nh, last_halo), 0)),
                  ctx_spec(d),
                  const(ada.shape),
                  const((1, d)), const(w_in.shape), const(gains.shape),
                  tab, tab, tab, const(bd.shape), const((1, CONV_K, dc)), const((1, dc)),
                  const(rpb_pad.shape)],
        out_specs=[tok(da), tok(da), tok(da), tok(da), tok(da), tok(dc), ctx_spec(da), ctx_spec(da), bias_spec],
        out_shape=[out_tok, out_tok, out_tok, out_tok, out_tok,
                   jax.ShapeDtypeStruct((b, s, dc), BF16), out_ctx, out_ctx,
                   jax.ShapeDtypeStruct((n_tiles, 2 * GRID_W, 2 * GRID_W), F32)],
        scratch_shapes=[pltpu.VMEM((tm + 2 * HALO + cr, d), BF16),
                        pltpu.VMEM((tm // sub, sub + 2 * HALO, dc), F32),
                        pltpu.VMEM(w_in.shape, BF16)],
        compiler_params=pltpu.CompilerParams(vmem_limit_bytes=VMEM_LIMIT,
                                             dimension_semantics=("arbitrary", "arbitrary")),
        name="in_proj",
    )(x, x, x, ctx, ada, norm_g, w_in, gains, cos, slo, shi, bd, conv_w, conv_b, rpb_pad)
    bias = outs[-1].reshape(n_heads // 2, ndr - 1, 2 * GRID_W, 2 * GRID_W)
    return (*outs[:-1], bias)


def _dot_nt(a, b):
    return lax.dot_general(a, b, (((1,), (1,)), ((), ())), preferred_element_type=F32)


def _attn_kernel(qrot_ref, qpl_ref, k_ref, v_ref, kc_ref, vc_ref, ga_ref, bias_ref, o_ref,
                 s_lat_ref, s_ctx_ref, p_lat_ref, p_ctx_ref, *, rows, group, n_batch, steps):
    n_units = n_batch * rows // group
    nk = WIN_H * GRID_W

    lane = lax.broadcasted_iota(jnp.int32, (GRID_W, LANES), 1)
    first_head = lane < HEAD_DIM

    def stack_heads(q2):
        zero = jnp.zeros_like(q2)
        return jnp.concatenate([jnp.where(first_head, q2, zero), jnp.where(first_head, zero, q2)], axis=0)

    def geometry(u, g):
        r = u * group + g
        if isinstance(r, int):
            bb, i = divmod(r, rows)
            rs = min(max(i - WIN_H // 2, 0), rows - WIN_H)
            return bb, i - rs, i * GRID_W, rs * GRID_W
        bb = r // rows
        i = r % rows
        rs = jnp.clip(i - WIN_H // 2, 0, rows - WIN_H)
        return bb, i - rs, pl.multiple_of(i * GRID_W, GRID_W), pl.multiple_of(rs * GRID_W, GRID_W)


    def scores_matmul(u):
        out = []
        for g in range(group):
            bb, off, tok0, key0 = geometry(u, g)
            qs = stack_heads(qrot_ref[bb, pl.ds(tok0, GRID_W), :])
            kband = k_ref[bb, pl.ds(key0, nk), :]
            qp = stack_heads(qpl_ref[bb, pl.ds(tok0, GRID_W), :])
            out.append((_dot_nt(qs, kband), off, _dot_nt(qp, kc_ref[bb])))
        return out


    def scores_store(slot, vals):
        for g, (s_lat, off, s_ctx) in enumerate(vals):
            first = (WIN_H - 1) - off
            bias = jnp.concatenate([bias_ref[0, first + 2 * t] for t in range(WIN_H // 2)], axis=1)
            s_lat_ref[slot, g] = s_lat + bias
            s_ctx_ref[slot, g] = s_ctx

    def softmax(slot):
        for g in range(group):
            s_lat = s_lat_ref[slot, g]
            s_ctx = s_ctx_ref[slot, g]
            m = jnp.maximum(jnp.max(s_lat, axis=-1, keepdims=True), jnp.max(s_ctx, axis=-1, keepdims=True))
            p_lat_ref[slot, g] = jnp.exp2(s_lat - m).astype(BF16)
            p_ctx_ref[slot, g] = jnp.exp2(s_ctx - m).astype(BF16)

    ones_lat = jnp.ones((nk, LANES), BF16)
    ones_ctx = jnp.ones((kc_ref.shape[1], LANES), BF16)

    def pv_matmul(u, slot):
        out = []
        for g in range(group):
            bb, _, _, key0 = geometry(u, g)
            v_lat = jnp.concatenate([v_ref[bb, pl.ds(key0, nk), :], ones_lat], axis=1)
            v_ctx = jnp.concatenate([vc_ref[bb], ones_ctx], axis=1)
            out.append(jnp.dot(p_lat_ref[slot, g], v_lat, preferred_element_type=F32)
                       + jnp.dot(p_ctx_ref[slot, g], v_ctx, preferred_element_type=F32))
        return out

    def pv_store(u, vals):
        for g, ol in enumerate(vals):
            bb, _, tok0, _ = geometry(u, g)
            o = ol[:, 0:LANES] * (1.0 / ol[:, LANES:])
            o2 = jnp.where(first_head, o[0:GRID_W], o[GRID_W:2 * GRID_W])
            gate = ga_ref[bb, pl.ds(tok0, GRID_W), :].astype(F32)
            o_ref[bb, pl.ds(tok0, GRID_W), :] = (o2 * gate).astype(BF16)

    scores_store(0, scores_matmul(0))
    sc = scores_matmul(1)
    softmax(0)
    scores_store(1, sc)

    def step(t, slot):
        sc = scores_matmul(t)
        pv = pv_matmul(t - 2, slot)
        softmax(1 - slot)
        scores_store(slot, sc)
        pv_store(t - 2, pv)

    n_steady, n_left = divmod(n_units - 2, steps)

    def steady(n, carry):
        for j in range(steps):
            step(steps * n + 2 + j, j % 2)
        return carry

    lax.fori_loop(0, n_steady, steady, 0)
    for j in range(n_left):
        step(steps * n_steady + 2 + j, j % 2)

    pv = pv_matmul(n_units - 2, 0)
    softmax(1)
    pv_store(n_units - 2, pv)
    pv_store(n_units - 1, pv_matmul(n_units - 1, 1))


def _attention(qrot, qpl, k, v, kc, vc, ga, bias, *, group, n_batch, steps):
    b, s, da = qrot.shape
    l = vc.shape[1]
    rows = s // GRID_W
    n_hp = da // LANES
    nq = 2 * GRID_W
    nk = WIN_H * GRID_W
    assert rows % (2 * group) == 0 and b % n_batch == 0 and steps % 2 == 0
    tok = pl.BlockSpec((n_batch, s, LANES), lambda hp, i: (i, 0, hp))
    ctx_tok = pl.BlockSpec((n_batch, l, LANES), lambda hp, i: (i, 0, hp))
    return pl.pallas_call(
        functools.partial(_attn_kernel, rows=rows, group=group, n_batch=n_batch, steps=steps),
        grid=(n_hp, b // n_batch),
        in_specs=[tok, tok, tok, tok, ctx_tok, ctx_tok, tok,
                  pl.BlockSpec((1,) + bias.shape[1:], lambda hp, i: (hp, 0, 0, 0))],
        out_specs=tok,
        out_shape=jax.ShapeDtypeStruct((b, s, da), BF16),
        scratch_shapes=[pltpu.VMEM((2, group, nq, nk), F32),
                        pltpu.VMEM((2, group, nq, l), F32),
                        pltpu.VMEM((2, group, nq, nk), BF16),
                        pltpu.VMEM((2, group, nq, l), BF16)],
        compiler_params=pltpu.CompilerParams(vmem_limit_bytes=VMEM_LIMIT),
        name="attn",
    )(qrot, qpl, k, v, kc, vc, ga, bias)


X_RING = 4


def _out_proj_kernel(x_hbm, a_hbm, c_hbm, ada_ref, w_ref, o_ref, xbuf_ref, abuf_ref, cbuf_ref, sem_ref,
                     *, da, tm, nj):
    t = pl.program_id(0)
    n = pl.num_programs(0)
    streams = ((x_hbm, xbuf_ref), (a_hbm, abuf_ref), (c_hbm, cbuf_ref))

    def copies(tile, slot):
        rows = pl.ds(pl.multiple_of((tile % nj) * tm, tm), tm)
        return [pltpu.make_async_copy(hbm.at[tile // nj, rows, :], buf.at[slot], sem_ref.at[k, slot])
                for k, (hbm, buf) in enumerate(streams)]

    @pl.when(t == 0)
    def _():
        for k in range(X_RING - 1):
            for cp in copies(k, k):
                cp.start()

    @pl.when(t + (X_RING - 1) < n)
    def _():
        for cp in copies(t + (X_RING - 1), (t + (X_RING - 1)) % X_RING):
            cp.start()

    slot = t % X_RING
    for cp in copies(t, slot):
        cp.wait()
    _, _, gate = _ada_row(ada_ref, t // nj)
    upd = (jnp.dot(abuf_ref[slot], w_ref[0:da, :].astype(BF16), preferred_element_type=F32)
           + jnp.dot(cbuf_ref[slot], w_ref[da:, :].astype(BF16), preferred_element_type=F32))
    o_ref[0] = xbuf_ref[slot] + gate * upd


def _out_proj(x, attn, conv, ada, w_out, *, tm):
    b, s, d = x.shape
    da = attn.shape[2]
    dc = conv.shape[2]
    nj = s // tm
    assert s % tm == 0 and b * nj >= X_RING
    tok = lambda width: pl.BlockSpec((1, tm, width), lambda t: (t // nj, t % nj, 0))
    return pl.pallas_call(
        functools.partial(_out_proj_kernel, da=da, tm=tm, nj=nj),
        grid=(b * nj,),
        in_specs=[pl.BlockSpec(memory_space=pl.ANY),
                  pl.BlockSpec(memory_space=pl.ANY),
                  pl.BlockSpec(memory_space=pl.ANY),
                  pl.BlockSpec(ada.shape, lambda t: (0, 0)),
                  pl.BlockSpec(w_out.shape, lambda t: (0, 0))],
        out_specs=tok(d),
        out_shape=jax.ShapeDtypeStruct((b, s, d), F32),
        scratch_shapes=[pltpu.VMEM((X_RING, tm, d), F32),
                        pltpu.VMEM((X_RING, tm, da), BF16),
                        pltpu.VMEM((X_RING, tm, dc), BF16),
                        pltpu.SemaphoreType.DMA((3, X_RING))],
        compiler_params=pltpu.CompilerParams(vmem_limit_bytes=VMEM_LIMIT,
                                             dimension_semantics=("arbitrary",)),
        name="out_proj",
    )(x, attn, conv, ada, w_out)


def _rope_tables(s):
    nf = HEAD_DIM // 4
    inv = (ROPE_THETA ** (-np.arange(nf, dtype=np.float32) / nf)).astype(np.float32)
    pos = np.arange(s)
    lane = np.arange(LANES)
    d = lane % HEAD_DIM
    axis = d // (2 * nf)
    half = (d % (2 * nf)) // nf
    coord = np.where(axis[None, :] == 0, (pos // GRID_W)[:, None], (pos % GRID_W)[:, None]).astype(np.float32)
    ang = (coord * inv[d % nf][None, :]).astype(np.float32)
    cos = np.cos(ang).astype(np.float32)
    sin = np.sin(ang).astype(np.float32)
    sin_lo = np.where(half[None, :] == 0, -sin, 0.0).astype(np.float32)
    sin_hi = np.where(half[None, :] == 1, sin, 0.0).astype(np.float32)
    return jnp.asarray(cos), jnp.asarray(sin_lo), jnp.asarray(sin_hi)


def _bias_tiles(rpb_ref, o_ref, first_tile):
    n_pairs = rpb_ref.shape[1] - 1
    cq = lax.broadcasted_iota(jnp.int32, (GRID_W, LANES), 0)
    ck = lax.broadcasted_iota(jnp.int32, (GRID_W, LANES), 1) % GRID_W
    col_start = jnp.clip(cq - WIN_W // 2, 0, GRID_W - WIN_W)
    valid = (ck >= col_start) & (ck < col_start + WIN_W)
    for e in range(o_ref.shape[0]):
        t = first_tile + e
        hp = t // n_pairs
        dr = t % n_pairs
        for a in range(2):
            two = rpb_ref[2 * hp + a, pl.ds(dr, 2), :]
            lanes = jnp.broadcast_to(jnp.concatenate([two[0:1], two[1:2]], axis=1), (GRID_W, LANES))
            toeplitz = pltpu.roll(lanes, LANES - (WIN_W - 1), 1, stride=1, stride_axis=0)
            o_ref[e, a * GRID_W:(a + 1) * GRID_W, :] = jnp.where(valid, toeplitz * LOG2_E, MASK_VALUE)


def kernel(x, c, ctx, c_ctx, w_ada, b_ada, norm_g, w_in, q_norm_g, k_norm_g, rpb, conv_w, conv_b, w_out):
    depth = w_ada.shape[0]
    b, s, d = x.shape
    dc = conv_w.shape[2]
    da = (w_in.shape[2] - 4 * dc) // 4
    rows = s // GRID_W
    assert depth == 1 and s % GRID_W == 0 and rows >= WIN_H and da % LANES == 0

    cos, slo, shi = _rope_tables(s)
    assert da % MXU_DIM == 0 and MXU_DIM % HEAD_DIM == 0
    seg = np.arange(MXU_DIM) // HEAD_DIM
    bd = jnp.asarray((seg[:, None] == seg[None, :]).astype(np.float32), dtype=BF16)

    ada, gains = _prep(c, c_ctx, w_ada[0], b_ada, q_norm_g, k_norm_g, da, steps=PREP_STEPS)

    qrot, qpl, k, v, ga, conv, kc, vc, bias = _in_proj(x, ctx, ada, b, norm_g, w_in[0], gains, cos, slo, shi, bd,
                                                       conv_w, conv_b, rpb[0], tm=IN_PROJ_ROWS, sub=IN_PROJ_SUB)
    attn = _attention(qrot, qpl, k, v, kc, vc, ga, bias, group=ATTN_GROUP, n_batch=ATTN_BATCH, steps=ATTN_STEPS)
    return _out_proj(x, attn, conv, ada, w_out[0], tm=OUT_PROJ_ROWS)
```

```python
import functools

import numpy as np
import jax
import jax.numpy as jnp
from jax import lax
from jax.experimental import pallas as pl
from jax.experimental.pallas import tpu as pltpu

F32 = jnp.float32
BF16 = jnp.bfloat16

HEAD_DIM = 64
GRID_W = 64
WIN_H = 8
WIN_W = 16
CONV_K = 3
ROPE_THETA = 10000.0
RMS_EPS = 1e-6
MASK_VALUE = -1e30
LOG2_E = 1.4426950408889634

LANES = 128
MXU_DIM = 256
F32_ROWS = 8
BF16_ROWS = 16
HALO = BF16_ROWS
VMEM_CAPACITY = 64 * 1024 * 1024
VMEM_LIMIT = VMEM_CAPACITY * 7 // 8

PREP_STEPS = 4
IN_PROJ_ROWS, IN_PROJ_SUB = 1024, 512
OUT_PROJ_ROWS = 512
ATTN_GROUP, ATTN_BATCH, ATTN_STEPS = 2, 4, 30


def _silu(z):
    return z * jax.nn.sigmoid(z)


def _prep_kernel(c_ref, cctx_ref, w_ref, b_ref, qg_ref, kg_ref, ada_ref, gains_ref):
    nb, d = c_ref.shape
    cctx = cctx_ref[...].reshape(1, d)
    cond = jnp.concatenate([c_ref[...], jnp.broadcast_to(cctx, (ada_ref.shape[0] - nb, d))], axis=0)
    a = _silu(cond).astype(BF16)
    ada_ref[...] = jnp.dot(a, w_ref[...].astype(BF16), preferred_element_type=F32) + b_ref[...]
    g = jnp.concatenate([qg_ref[...] * (HEAD_DIM ** -0.5 * LOG2_E), kg_ref[...]], axis=0)
    gains_ref[...] = jnp.concatenate([g] * (gains_ref.shape[1] // HEAD_DIM), axis=1)


def _prep(c, c_ctx, w_ada, b_ada, q_gain, k_gain, d_attn, *, steps):
    nb, d = c.shape
    assert nb % F32_ROWS == 0
    rows = nb + F32_ROWS
    n = w_ada.shape[1]
    tn = n // steps
    assert tn % LANES == 0
    return pl.pallas_call(
        _prep_kernel,
        grid=(steps,),
        in_specs=[pl.BlockSpec((nb, d), lambda i: (0, 0)),
                  pl.BlockSpec((d,), lambda i: (0,)),
                  pl.BlockSpec((d, tn), lambda i: (0, i)),
                  pl.BlockSpec((1, tn), lambda i: (0, i)),
                  pl.BlockSpec((1, HEAD_DIM), lambda i: (0, 0)),
                  pl.BlockSpec((1, HEAD_DIM), lambda i: (0, 0))],
        out_specs=[pl.BlockSpec((rows, tn), lambda i: (0, i)),
                   pl.BlockSpec((2, d_attn), lambda i: (0, 0))],
        out_shape=[jax.ShapeDtypeStruct((rows, n), F32),
                   jax.ShapeDtypeStruct((2, d_attn), F32)],
        compiler_params=pltpu.CompilerParams(vmem_limit_bytes=VMEM_LIMIT),
        name="prep",
    )(c, c_ctx, w_ada, b_ada, q_gain, k_gain)


def _ada_row(ada_ref, row):
    r = ada_ref[pl.ds(row, 1), :]
    d = r.shape[1] // 3
    return r[:, 0:d], r[:, d:2 * d], r[:, 2 * d:]


def _modulated_norm(xt, mult, shift):
    ms = jnp.mean(xt * xt, axis=-1, keepdims=True)
    return (xt * lax.rsqrt(ms + RMS_EPS) * mult + shift).astype(BF16)


def _head_norm(t, gain, bd):
    t2 = (t * t).astype(BF16)
    w = bd.shape[0]
    ss = jnp.concatenate([jnp.dot(t2[:, c:c + w], bd, preferred_element_type=F32)
                          for c in range(0, t.shape[1], w)], axis=1)
    return t * lax.rsqrt(ss * (1.0 / HEAD_DIM) + RMS_EPS) * gain


def _rope(t, cos, sin_lo, sin_hi):
    outs = []
    for c in range(t.shape[1] // LANES):
        tc = t[:, c * LANES:(c + 1) * LANES]
        up = pltpu.roll(tc, LANES - HEAD_DIM // 4, 1)
        dn = pltpu.roll(tc, HEAD_DIM // 4, 1)
        outs.append(tc * cos + up * sin_lo + dn * sin_hi)
    return jnp.concatenate(outs, axis=1)


def _in_proj_kernel(x_ref, xp_ref, xn_ref, ctx_ref, ada_ref, ng_ref, w_ref, g_ref,
                    cos_ref, slo_ref, shi_ref, bd_ref, cw_ref, cb_ref, rpb_ref,
                    qrot_ref, qpl_ref, k_ref, v_ref, ga_ref, conv_ref, kc_ref, vc_ref, bias_ref,
                    hext_ref, cu_ref, wb_ref, *, tm, sub, da, dc, ctx_row, bias_blocks):
    j = pl.program_id(1)
    nj = pl.num_programs(1)

    @pl.when((pl.program_id(0) == 0) & (j == 0))
    def _():
        for c in range(0, w_ref.shape[1], MXU_DIM):
            wb_ref[:, c:c + MXU_DIM] = w_ref[:, c:c + MXU_DIM].astype(BF16)

    step = pl.program_id(0) * nj + j
    _bias_tiles(rpb_ref, bias_ref, jnp.minimum(step, bias_blocks - 1) * bias_ref.shape[0])
    shift, scale, _ = _ada_row(ada_ref, pl.program_id(0))
    mult = ng_ref[...] * (1.0 + scale)
    n_sub = tm // sub
    cr = ctx_ref.shape[1]

    def proj(src, lo, width):
        return jnp.dot(hext_ref[src, :], wb_ref[:, lo:lo + width], preferred_element_type=F32)

    bd = bd_ref[...]
    hext_ref[0:HALO, :] = _modulated_norm(xp_ref[0], mult, shift)
    hext_ref[HALO + tm:2 * HALO + tm, :] = _modulated_norm(xn_ref[0], mult, shift)
    shift_c, scale_c, _ = _ada_row(ada_ref, ctx_row)
    hext_ref[2 * HALO + tm:, :] = _modulated_norm(ctx_ref[0], ng_ref[...] * (1.0 + scale_c), shift_c)

    for n in range(n_sub):
        r0 = n * sub
        hext_ref[HALO + r0:HALO + r0 + sub, :] = _modulated_norm(x_ref[0, r0:r0 + sub, :], mult, shift)
    for n in range(n_sub):
        r0 = n * sub
        rows = slice(r0, r0 + sub)
        h = slice(HALO + r0, HALO + r0 + sub)
        with_ctx = n == n_sub - 1
        h_kv = slice(HALO + r0, 2 * HALO + tm + cr) if with_ctx else h
        cos, slo, shi = cos_ref[rows, :], slo_ref[rows, :], shi_ref[rows, :]


        hx = slice(r0, r0 + sub + 2 * HALO)
        cu = proj(hx, 4 * da, dc) * proj(hx, 4 * da + 2 * dc, dc)
        row = lax.broadcasted_iota(jnp.int32, (sub + 2 * HALO, 1), 0) + r0
        inside = ((row >= HALO) | (j > 0)) & ((row < HALO + tm) | (j < nj - 1))
        cu_ref[n] = jnp.where(inside, cu, 0.0)
        y = (cb_ref[...]
             + cw_ref[0, 0:1, :] * cu_ref[n, HALO - 1:HALO - 1 + sub, :]
             + cw_ref[0, 1:2, :] * cu_ref[n, HALO:HALO + sub, :]
             + cw_ref[0, 2:3, :] * cu_ref[n, HALO + 1:HALO + 1 + sub, :])
        bg = proj(h, 4 * da + dc, dc)
        zc = proj(h, 4 * da + 3 * dc, dc)
        conv_ref[0, rows, :] = (bg * y * _silu(zc)).astype(BF16)

        qn = _head_norm(proj(h, 0, da), g_ref[0:1, :], bd)
        qpl_ref[0, rows, :] = qn.astype(BF16)
        qrot_ref[0, rows, :] = _rope(qn, cos, slo, shi).astype(BF16)

        kf = proj(h_kv, da, da)
        kn = _head_norm(kf[0:sub], g_ref[1:2, :], bd)
        k_ref[0, rows, :] = _rope(kn, cos, slo, shi).astype(BF16)
        if with_ctx:
            kc_ref[0] = _head_norm(kf[sub + HALO:], g_ref[1:2, :], bd).astype(BF16)

        ga_ref[0, rows, :] = _silu(proj(h, 3 * da, da)).astype(BF16)
        vf = proj(h_kv, 2 * da, da)
        v_ref[0, rows, :] = vf[0:sub].astype(BF16)
        if with_ctx:
            vc_ref[0] = vf[sub + HALO:].astype(BF16)


def _in_proj(x, ctx, ada, ctx_row, norm_g, w_in, gains, cos, slo, shi, bd, conv_w, conv_b, rpb, *, tm, sub):
    b, s, d = x.shape
    n_heads, ndr, ndc = rpb.shape
    assert ndr == 2 * WIN_H - 1 and ndc == 2 * WIN_W - 1 and ndc <= GRID_W
    rpb_pad = jnp.pad(rpb, ((0, 0), (0, 0), (0, GRID_W - ndc)))
    n_tiles = (n_heads // 2) * (ndr - 1)
    bias_blocks = max(k for k in range(1, b * (s // tm) + 1) if n_tiles % k == 0)
    tiles_per_step = n_tiles // bias_blocks
    l = ctx.shape[1]
    assert s % tm == 0 and tm % sub == 0 and sub % HALO == 0
    cr = l // (s // tm)
    assert l % (s // tm) == 0 and cr % BF16_ROWS == 0
    dc = conv_w.shape[2]
    da = (w_in.shape[1] - 4 * dc) // 4
    nh = tm // HALO
    last_halo = s // HALO - 1
    tok = lambda width: pl.BlockSpec((1, tm, width), lambda i, j: (i, j, 0))
    const = lambda shape: pl.BlockSpec(shape, lambda i, j: (0,) * len(shape))
    tab = pl.BlockSpec((tm, LANES), lambda i, j: (j, 0))
    out_tok = jax.ShapeDtypeStruct((b, s, da), BF16)
    out_ctx = jax.ShapeDtypeStruct((b, l, da), BF16)
    ctx_spec = lambda width: pl.BlockSpec((1, cr, width), lambda i, j: (i, j, 0))
    nj = s // tm
    bias_spec = pl.BlockSpec((tiles_per_step, 2 * GRID_W, 2 * GRID_W),
                             lambda i, j: (jnp.minimum(i * nj + j, bias_blocks - 1), 0, 0))
    outs = pl.pallas_call(
        functools.partial(_in_proj_kernel, tm=tm, sub=sub, da=da, dc=dc, ctx_row=ctx_row,
                          bias_blocks=bias_blocks),
        grid=(b, s // tm),
        in_specs=[tok(d),
                  pl.BlockSpec((1, HALO, d), lambda i, j: (i, jnp.maximum(j * nh - 1, 0), 0)),
                  pl.BlockSpec((1, HALO, d), lambda i, j: (i, jnp.minimum((j + 1) * nh, last_halo), 0)),
                  ctx_spec(d),
                  const(ada.shape),
                  const((1, d)), const(w_in.shape), const(gains.shape),
                  tab, tab, tab, const(bd.shape), const((1, CONV_K, dc)), const((1, dc)),
                  const(rpb_pad.shape)],
        out_specs=[tok(da), tok(da), tok(da), tok(da), tok(da), tok(dc), ctx_spec(da), ctx_spec(da), bias_spec],
        out_shape=[out_tok, out_tok, out_tok, out_tok, out_tok,
                   jax.ShapeDtypeStruct((b, s, dc), BF16), out_ctx, out_ctx,
                   jax.ShapeDtypeStruct((n_tiles, 2 * GRID_W, 2 * GRID_W), F32)],
        scratch_shapes=[pltpu.VMEM((tm + 2 * HALO + cr, d), BF16),
                        pltpu.VMEM((tm // sub, sub + 2 * HALO, dc), F32),
                        pltpu.VMEM(w_in.shape, BF16)],
        compiler_params=pltpu.CompilerParams(vmem_limit_bytes=VMEM_LIMIT,
                                             dimension_semantics=("arbitrary", "arbitrary")),
        name="in_proj",
    )(x, x, x, ctx, ada, norm_g, w_in, gains, cos, slo, shi, bd, conv_w, conv_b, rpb_pad)
    bias = outs[-1].reshape(n_heads // 2, ndr - 1, 2 * GRID_W, 2 * GRID_W)
    return (*outs[:-1], bias)


def _dot_nt(a, b):
    return lax.dot_general(a, b, (((1,), (1,)), ((), ())), preferred_element_type=F32)


def _attn_kernel(qrot_ref, qpl_ref, k_ref, v_ref, kc_ref, vc_ref, ga_ref, bias_ref, o_ref,
                 s_lat_ref, s_ctx_ref, p_lat_ref, p_ctx_ref, *, rows, group, n_batch, steps):
    n_units = n_batch * rows // group
    nk = WIN_H * GRID_W

    lane = lax.broadcasted_iota(jnp.int32, (GRID_W, LANES), 1)
    first_head = lane < HEAD_DIM

    def stack_heads(q2):
        zero = jnp.zeros_like(q2)
        return jnp.concatenate([jnp.where(first_head, q2, zero), jnp.where(first_head, zero, q2)], axis=0)

    def geometry(u, g):
        r = u * group + g
        if isinstance(r, int):
            bb, i = divmod(r, rows)
            rs = min(max(i - WIN_H // 2, 0), rows - WIN_H)
            return bb, i - rs, i * GRID_W, rs * GRID_W
        bb = r // rows
        i = r % rows
        rs = jnp.clip(i - WIN_H // 2, 0, rows - WIN_H)
        return bb, i - rs, pl.multiple_of(i * GRID_W, GRID_W), pl.multiple_of(rs * GRID_W, GRID_W)


    def scores_matmul(u):
        out = []
        for g in range(group):
            bb, off, tok0, key0 = geometry(u, g)
            qs = stack_heads(qrot_ref[bb, pl.ds(tok0, GRID_W), :])
            kband = k_ref[bb, pl.ds(key0, nk), :]
            qp = stack_heads(qpl_ref[bb, pl.ds(tok0, GRID_W), :])
            out.append((_dot_nt(qs, kband), off, _dot_nt(qp, kc_ref[bb])))
        return out


    def scores_store(slot, vals):
        for g, (s_lat, off, s_ctx) in enumerate(vals):
            first = (WIN_H - 1) - off
            bias = jnp.concatenate([bias_ref[0, first + 2 * t] for t in range(WIN_H // 2)], axis=1)
            s_lat_ref[slot, g] = s_lat + bias
            s_ctx_ref[slot, g] = s_ctx

    def softmax(slot):
        for g in range(group):
            s_lat = s_lat_ref[slot, g]
            s_ctx = s_ctx_ref[slot, g]
            m = jnp.maximum(jnp.max(s_lat, axis=-1, keepdims=True), jnp.max(s_ctx, axis=-1, keepdims=True))
            p_lat_ref[slot, g] = jnp.exp2(s_lat - m).astype(BF16)
            p_ctx_ref[slot, g] = jnp.exp2(s_ctx - m).astype(BF16)

    ones_lat = jnp.ones((nk, LANES), BF16)
    ones_ctx = jnp.ones((kc_ref.shape[1], LANES), BF16)

    def pv_matmul(u, slot):
        out = []
        for g in range(group):
            bb, _, _, key0 = geometry(u, g)
            v_lat = jnp.concatenate([v_ref[bb, pl.ds(key0, nk), :], ones_lat], axis=1)
            v_ctx = jnp.concatenate([vc_ref[bb], ones_ctx], axis=1)
            out.append(jnp.dot(p_lat_ref[slot, g], v_lat, preferred_element_type=F32)
                       + jnp.dot(p_ctx_ref[slot, g], v_ctx, preferred_element_type=F32))
        return out

    def pv_store(u, vals):
        for g, ol in enumerate(vals):
            bb, _, tok0, _ = geometry(u, g)
            o = ol[:, 0:LANES] * (1.0 / ol[:, LANES:])
            o2 = jnp.where(first_head, o[0:GRID_W], o[GRID_W:2 * GRID_W])
            gate = ga_ref[bb, pl.ds(tok0, GRID_W), :].astype(F32)
            o_ref[bb, pl.ds(tok0, GRID_W), :] = (o2 * gate).astype(BF16)

    scores_store(0, scores_matmul(0))
    sc = scores_matmul(1)
    softmax(0)
    scores_store(1, sc)

    def step(t, slot):
        sc = scores_matmul(t)
        pv = pv_matmul(t - 2, slot)
        softmax(1 - slot)
        scores_store(slot, sc)
        pv_store(t - 2, pv)

    n_steady, n_left = divmod(n_units - 2, steps)

    def steady(n, carry):
        for j in range(steps):
            step(steps * n + 2 + j, j % 2)
        return carry

    lax.fori_loop(0, n_steady, steady, 0)
    for j in range(n_left):
        step(steps * n_steady + 2 + j, j % 2)

    pv = pv_matmul(n_units - 2, 0)
    softmax(1)
    pv_store(n_units - 2, pv)
    pv_store(n_units - 1, pv_matmul(n_units - 1, 1))


def _attention(qrot, qpl, k, v, kc, vc, ga, bias, *, group, n_batch, steps):
    b, s, da = qrot.shape
    l = vc.shape[1]
    rows = s // GRID_W
    n_hp = da // LANES
    nq = 2 * GRID_W
    nk = WIN_H * GRID_W
    assert rows % (2 * group) == 0 and b % n_batch == 0 and steps % 2 == 0
    tok = pl.BlockSpec((n_batch, s, LANES), lambda hp, i: (i, 0, hp))
    ctx_tok = pl.BlockSpec((n_batch, l, LANES), lambda hp, i: (i, 0, hp))
    return pl.pallas_call(
        functools.partial(_attn_kernel, rows=rows, group=group, n_batch=n_batch, steps=steps),
        grid=(n_hp, b // n_batch),
        in_specs=[tok, tok, tok, tok, ctx_tok, ctx_tok, tok,
                  pl.BlockSpec((1,) + bias.shape[1:], lambda hp, i: (hp, 0, 0, 0))],
        out_specs=tok,
        out_shape=jax.ShapeDtypeStruct((b, s, da), BF16),
        scratch_shapes=[pltpu.VMEM((2, group, nq, nk), F32),
                        pltpu.VMEM((2, group, nq, l), F32),
                        pltpu.VMEM((2, group, nq, nk), BF16),
                        pltpu.VMEM((2, group, nq, l), BF16)],
        compiler_params=pltpu.CompilerParams(vmem_limit_bytes=VMEM_LIMIT),
        name="attn",
    )(qrot, qpl, k, v, kc, vc, ga, bias)


X_RING = 4


def _out_proj_kernel(x_hbm, a_hbm, c_hbm, ada_ref, w_hbm, o_ref, xbuf_ref, abuf_ref, cbuf_ref, sem_ref,
                     wf_ref, w_ref, wsem_ref, *, da, tm, nj):
    t = pl.program_id(0)
    n = pl.num_programs(0)
    streams = ((x_hbm, xbuf_ref), (a_hbm, abuf_ref), (c_hbm, cbuf_ref))

    def copies(tile, slot):
        rows = pl.ds(pl.multiple_of((tile % nj) * tm, tm), tm)
        return [pltpu.make_async_copy(hbm.at[tile // nj, rows, :], buf.at[slot], sem_ref.at[k, slot])
                for k, (hbm, buf) in enumerate(streams)]

    @pl.when(t == 0)
    def _():
        for k in range(X_RING - 1):
            for cp in copies(k, k):
                cp.start()
        w_copy = pltpu.make_async_copy(w_hbm, wf_ref, wsem_ref.at[0])
        w_copy.start()
        w_copy.wait()
        w_ref[...] = wf_ref[...].astype(BF16)

    @pl.when(t + (X_RING - 1) < n)
    def _():
        for cp in copies(t + (X_RING - 1), (t + (X_RING - 1)) % X_RING):
            cp.start()

    slot = t % X_RING
    for cp in copies(t, slot):
        cp.wait()
    _, _, gate = _ada_row(ada_ref, t // nj)
    upd = (jnp.dot(abuf_ref[slot], w_ref[0:da, :], preferred_element_type=F32)
           + jnp.dot(cbuf_ref[slot], w_ref[da:, :], preferred_element_type=F32))
    o_ref[0] = xbuf_ref[slot] + gate * upd


def _out_proj(x, attn, conv, ada, w_out, *, tm):
    b, s, d = x.shape
    da = attn.shape[2]
    dc = conv.shape[2]
    nj = s // tm
    assert s % tm == 0 and b * nj >= X_RING
    tok = lambda width: pl.BlockSpec((1, tm, width), lambda t: (t // nj, t % nj, 0))
    return pl.pallas_call(
        functools.partial(_out_proj_kernel, da=da, tm=tm, nj=nj),
        grid=(b * nj,),
        in_specs=[pl.BlockSpec(memory_space=pl.ANY),
                  pl.BlockSpec(memory_space=pl.ANY),
                  pl.BlockSpec(memory_space=pl.ANY),
                  pl.BlockSpec(ada.shape, lambda t: (0, 0)),
                  pl.BlockSpec(memory_space=pl.ANY)],
        out_specs=tok(d),
        out_shape=jax.ShapeDtypeStruct((b, s, d), F32),
        scratch_shapes=[pltpu.VMEM((X_RING, tm, d), F32),
                        pltpu.VMEM((X_RING, tm, da), BF16),
                        pltpu.VMEM((X_RING, tm, dc), BF16),
                        pltpu.SemaphoreType.DMA((3, X_RING)),
                        pltpu.VMEM(w_out.shape, F32),
                        pltpu.VMEM(w_out.shape, BF16),
                        pltpu.SemaphoreType.DMA((1,))],
        compiler_params=pltpu.CompilerParams(vmem_limit_bytes=VMEM_LIMIT,
                                             dimension_semantics=("arbitrary",)),
        name="out_proj",
    )(x, attn, conv, ada, w_out)


def _rope_tables(s):
    nf = HEAD_DIM // 4
    inv = (ROPE_THETA ** (-np.arange(nf, dtype=np.float32) / nf)).astype(np.float32)
    pos = np.arange(s)
    lane = np.arange(LANES)
    d = lane % HEAD_DIM
    axis = d // (2 * nf)
    half = (d % (2 * nf)) // nf
    coord = np.where(axis[None, :] == 0, (pos // GRID_W)[:, None], (pos % GRID_W)[:, None]).astype(np.float32)
    ang = (coord * inv[d % nf][None, :]).astype(np.float32)
    cos = np.cos(ang).astype(np.float32)
    sin = np.sin(ang).astype(np.float32)
    sin_lo = np.where(half[None, :] == 0, -sin, 0.0).astype(np.float32)
    sin_hi = np.where(half[None, :] == 1, sin, 0.0).astype(np.float32)
    return jnp.asarray(cos), jnp.asarray(sin_lo), jnp.asarray(sin_hi)


def _bias_tiles(rpb_ref, o_ref, first_tile):
    n_pairs = rpb_ref.shape[1] - 1
    cq = lax.broadcasted_iota(jnp.int32, (GRID_W, LANES), 0)
    ck = lax.broadcasted_iota(jnp.int32, (GRID_W, LANES), 1) % GRID_W
    col_start = jnp.clip(cq - WIN_W // 2, 0, GRID_W - WIN_W)
    valid = (ck >= col_start) & (ck < col_start + WIN_W)
    for e in range(o_ref.shape[0]):
        t = first_tile + e
        hp = t // n_pairs
        dr = t % n_pairs
        for a in range(2):
            two = rpb_ref[2 * hp + a, pl.ds(dr, 2), :]
            lanes = jnp.broadcast_to(jnp.concatenate([two[0:1], two[1:2]], axis=1), (GRID_W, LANES))
            toeplitz = pltpu.roll(lanes, LANES - (WIN_W - 1), 1, stride=1, stride_axis=0)
            o_ref[e, a * GRID_W:(a + 1) * GRID_W, :] = jnp.where(valid, toeplitz * LOG2_E, MASK_VALUE)


def kernel(x, c, ctx, c_ctx, w_ada, b_ada, norm_g, w_in, q_norm_g, k_norm_g, rpb, conv_w, conv_b, w_out):
    depth = w_ada.shape[0]
    b, s, d = x.shape
    dc = conv_w.shape[2]
    da = (w_in.shape[2] - 4 * dc) // 4
    rows = s // GRID_W
    assert depth == 1 and s % GRID_W == 0 and rows >= WIN_H and da % LANES == 0

    cos, slo, shi = _rope_tables(s)
    assert da % MXU_DIM == 0 and MXU_DIM % HEAD_DIM == 0
    seg = np.arange(MXU_DIM) // HEAD_DIM
    bd = jnp.asarray((seg[:, None] == seg[None, :]).astype(np.float32), dtype=BF16)

    ada, gains = _prep(c, c_ctx, w_ada[0], b_ada, q_norm_g, k_norm_g, da, steps=PREP_STEPS)

    qrot, qpl, k, v, ga, conv, kc, vc, bias = _in_proj(x, ctx, ada, b, norm_g, w_in[0], gains, cos, slo, shi, bd,
                                                       conv_w, conv_b, rpb[0], tm=IN_PROJ_ROWS, sub=IN_PROJ_SUB)
    attn = _attention(qrot, qpl, k, v, kc, vc, ga, bias, group=ATTN_GROUP, n_batch=ATTN_BATCH, steps=ATTN_STEPS)
    return _out_proj(x, attn, conv, ada, w_out[0], tm=OUT_PROJ_ROWS)
```

```python
import functools

import numpy as np
import jax
import jax.numpy as jnp
from jax import lax
from jax.experimental import pallas as pl
from jax.experimental.pallas import tpu as pltpu

F32 = jnp.float32
BF16 = jnp.bfloat16

HEAD_DIM = 64
GRID_W = 64
WIN_H = 8
WIN_W = 16
CONV_K = 3
ROPE_THETA = 10000.0
RMS_EPS = 1e-6
MASK_VALUE = -1e30
LOG2_E = 1.4426950408889634

LANES = 128
MXU_DIM = 256
F32_ROWS = 8
BF16_ROWS = 16
HALO = BF16_ROWS
VMEM_CAPACITY = 64 * 1024 * 1024
VMEM_LIMIT = VMEM_CAPACITY * 7 // 8

PREP_STEPS = 4
IN_PROJ_ROWS, IN_PROJ_SUB = 1024, 512
OUT_PROJ_ROWS = 512
ATTN_GROUP, ATTN_BATCH, ATTN_STEPS = 2, 4, 30


def _silu(z):
    return z * jax.nn.sigmoid(z)


def _prep_kernel(c_ref, cctx_ref, w_ref, b_ref, qg_ref, kg_ref, ada_ref, gains_ref):
    nb, d = c_ref.shape
    cctx = cctx_ref[...].reshape(1, d)
    cond = jnp.concatenate([c_ref[...], jnp.broadcast_to(cctx, (ada_ref.shape[0] - nb, d))], axis=0)
    a = _silu(cond).astype(BF16)
    ada_ref[...] = jnp.dot(a, w_ref[...].astype(BF16), preferred_element_type=F32) + b_ref[...]
    g = jnp.concatenate([qg_ref[...] * (HEAD_DIM ** -0.5 * LOG2_E), kg_ref[...]], axis=0)
    gains_ref[...] = jnp.concatenate([g] * (gains_ref.shape[1] // HEAD_DIM), axis=1)


def _prep(c, c_ctx, w_ada, b_ada, q_gain, k_gain, d_attn, *, steps):
    nb, d = c.shape
    assert nb % F32_ROWS == 0
    rows = nb + F32_ROWS
    n = w_ada.shape[1]
    tn = n // steps
    assert tn % LANES == 0
    return pl.pallas_call(
        _prep_kernel,
        grid=(steps,),
        in_specs=[pl.BlockSpec((nb, d), lambda i: (0, 0)),
                  pl.BlockSpec((d,), lambda i: (0,)),
                  pl.BlockSpec((d, tn), lambda i: (0, i)),
                  pl.BlockSpec((1, tn), lambda i: (0, i)),
                  pl.BlockSpec((1, HEAD_DIM), lambda i: (0, 0)),
                  pl.BlockSpec((1, HEAD_DIM), lambda i: (0, 0))],
        out_specs=[pl.BlockSpec((rows, tn), lambda i: (0, i)),
                   pl.BlockSpec((2, d_attn), lambda i: (0, 0))],
        out_shape=[jax.ShapeDtypeStruct((rows, n), F32),
                   jax.ShapeDtypeStruct((2, d_attn), F32)],
        compiler_params=pltpu.CompilerParams(vmem_limit_bytes=VMEM_LIMIT),
        name="prep",
    )(c, c_ctx, w_ada, b_ada, q_gain, k_gain)


def _ada_row(ada_ref, row):
    r = ada_ref[pl.ds(row, 1), :]
    d = r.shape[1] // 3
    return r[:, 0:d], r[:, d:2 * d], r[:, 2 * d:]


def _modulated_norm(xt, mult, shift):
    ms = jnp.mean(xt * xt, axis=-1, keepdims=True)
    return (xt * lax.rsqrt(ms + RMS_EPS) * mult + shift).astype(BF16)


def _head_norm(t, gain, bd):
    t2 = (t * t).astype(BF16)
    w = bd.shape[0]
    ss = jnp.concatenate([jnp.dot(t2[:, c:c + w], bd, preferred_element_type=F32)
                          for c in range(0, t.shape[1], w)], axis=1)
    return t * lax.rsqrt(ss * (1.0 / HEAD_DIM) + RMS_EPS) * gain


def _rope(t, cos, sin_lo, sin_hi):
    outs = []
    for c in range(t.shape[1] // LANES):
        tc = t[:, c * LANES:(c + 1) * LANES]
        up = pltpu.roll(tc, LANES - HEAD_DIM // 4, 1)
        dn = pltpu.roll(tc, HEAD_DIM // 4, 1)
        outs.append(tc * cos + up * sin_lo + dn * sin_hi)
    return jnp.concatenate(outs, axis=1)


def _in_proj_kernel(x_ref, xp_ref, xn_ref, ctx_ref, ada_ref, ng_ref, w_ref, g_ref,
                    cos_ref, slo_ref, shi_ref, bd_ref, cw_ref, cb_ref, rpb_ref,
                    qrot_ref, qpl_ref, k_ref, v_ref, ga_ref, conv_ref, kc_ref, vc_ref, bias_ref,
                    hext_ref, cu_ref, wb_ref, *, tm, sub, da, dc, ctx_row, bias_blocks):
    j = pl.program_id(1)
    nj = pl.num_programs(1)

    @pl.when((pl.program_id(0) == 0) & (j == 0))
    def _():
        for c in range(0, w_ref.shape[1], MXU_DIM):
            wb_ref[:, c:c + MXU_DIM] = w_ref[:, c:c + MXU_DIM].astype(BF16)

    step = pl.program_id(0) * nj + j
    _bias_tiles(rpb_ref, bias_ref, jnp.minimum(step, bias_blocks - 1) * bias_ref.shape[0])
    shift, scale, _ = _ada_row(ada_ref, pl.program_id(0))
    mult = ng_ref[...] * (1.0 + scale)
    n_sub = tm // sub
    cr = ctx_ref.shape[1]

    def proj(src, lo, width):
        return jnp.dot(hext_ref[src, :], wb_ref[:, lo:lo + width], preferred_element_type=F32)

    bd = bd_ref[...]
    hext_ref[0:HALO, :] = _modulated_norm(xp_ref[0], mult, shift)
    hext_ref[HALO + tm:2 * HALO + tm, :] = _modulated_norm(xn_ref[0], mult, shift)
    shift_c, scale_c, _ = _ada_row(ada_ref, ctx_row)
    hext_ref[2 * HALO + tm:, :] = _modulated_norm(ctx_ref[0], ng_ref[...] * (1.0 + scale_c), shift_c)

    for n in range(n_sub):
        r0 = n * sub
        hext_ref[HALO + r0:HALO + r0 + sub, :] = _modulated_norm(x_ref[0, r0:r0 + sub, :], mult, shift)
    for n in range(n_sub):
        r0 = n * sub
        rows = slice(r0, r0 + sub)
        h = slice(HALO + r0, HALO + r0 + sub)
        with_ctx = n == n_sub - 1
        h_kv = slice(HALO + r0, 2 * HALO + tm + cr) if with_ctx else h
        cos, slo, shi = cos_ref[rows, :], slo_ref[rows, :], shi_ref[rows, :]


        hx = slice(r0, r0 + sub + 2 * HALO)
        cu = proj(hx, 4 * da, dc) * proj(hx, 4 * da + 2 * dc, dc)
        row = lax.broadcasted_iota(jnp.int32, (sub + 2 * HALO, 1), 0) + r0
        inside = ((row >= HALO) | (j > 0)) & ((row < HALO + tm) | (j < nj - 1))
        cu_ref[n] = jnp.where(inside, cu, 0.0)
        y = (cb_ref[...]
             + cw_ref[0, 0:1, :] * cu_ref[n, HALO - 1:HALO - 1 + sub, :]
             + cw_ref[0, 1:2, :] * cu_ref[n, HALO:HALO + sub, :]
             + cw_ref[0, 2:3, :] * cu_ref[n, HALO + 1:HALO + 1 + sub, :])
        bg = proj(h, 4 * da + dc, dc)
        zc = proj(h, 4 * da + 3 * dc, dc)
        conv_ref[0, rows, :] = (bg * y * _silu(zc)).astype(BF16)

        qn = _head_norm(proj(h, 0, da), g_ref[0:1, :], bd)
        qpl_ref[0, rows, :] = qn.astype(BF16)
        qrot_ref[0, rows, :] = _rope(qn, cos, slo, shi).astype(BF16)

        kf = proj(h_kv, da, da)
        kn = _head_norm(kf[0:sub], g_ref[1:2, :], bd)
        k_ref[0, rows, :] = _rope(kn, cos, slo, shi).astype(BF16)
        if with_ctx:
            kc_ref[0] = _head_norm(kf[sub + HALO:], g_ref[1:2, :], bd).astype(BF16)

        ga_ref[0, rows, :] = _silu(proj(h, 3 * da, da)).astype(BF16)
        vf = proj(h_kv, 2 * da, da)
        v_ref[0, rows, :] = vf[0:sub].astype(BF16)
        if with_ctx:
            vc_ref[0] = vf[sub + HALO:].astype(BF16)


def _in_proj(x, ctx, ada, ctx_row, norm_g, w_in, gains, cos, slo, shi, bd, conv_w, conv_b, rpb, *, tm, sub):
    b, s, d = x.shape
    n_heads, ndr, ndc = rpb.shape
    assert ndr == 2 * WIN_H - 1 and ndc == 2 * WIN_W - 1 and ndc <= GRID_W
    rpb_pad = jnp.pad(rpb, ((0, 0), (0, 0), (0, GRID_W - ndc)))
    n_tiles = (n_heads // 2) * (ndr - 1)
    bias_blocks = max(k for k in range(1, b * (s // tm) + 1) if n_tiles % k == 0)
    tiles_per_step = n_tiles // bias_blocks
    l = ctx.shape[1]
    assert s % tm == 0 and tm % sub == 0 and sub % HALO == 0
    cr = l // (s // tm)
    assert l % (s // tm) == 0 and cr % BF16_ROWS == 0
    dc = conv_w.shape[2]
    da = (w_in.shape[1] - 4 * dc) // 4
    nh = tm // HALO
    last_halo = s // HALO - 1
    tok = lambda width: pl.BlockSpec((1, tm, width), lambda i, j: (i, j, 0))
    const = lambda shape: pl.BlockSpec(shape, lambda i, j: (0,) * len(shape))
    tab = pl.BlockSpec((tm, LANES), lambda i, j: (j, 0))
    out_tok = jax.ShapeDtypeStruct((b, s, da), BF16)
    out_ctx = jax.ShapeDtypeStruct((b, l, da), BF16)
    ctx_spec = lambda width: pl.BlockSpec((1, cr, width), lambda i, j: (i, j, 0))
    nj = s // tm
    bias_spec = pl.BlockSpec((tiles_per_step, 2 * GRID_W, 2 * GRID_W),
                             lambda i, j: (jnp.minimum(i * nj + j, bias_blocks - 1), 0, 0))
    outs = pl.pallas_call(
        functools.partial(_in_proj_kernel, tm=tm, sub=sub, da=da, dc=dc, ctx_row=ctx_row,
                          bias_blocks=bias_blocks),
        grid=(b, s // tm),
        in_specs=[tok(d),
                  pl.BlockSpec((1, HALO, d), lambda i, j: (i, jnp.maximum(j * nh - 1, 0), 0)),
                  pl.BlockSpec((1, HALO, d), lambda i, j: (i, jnp.minimum((j + 1) * nh, last_halo), 0)),
                  ctx_spec(d),
                  const(ada.shape),
                  const((1, d)), const(w_in.shape), const(gains.shape),
                  tab, tab, tab, const(bd.shape), const((1, CONV_K, dc)), const((1, dc)),
                  const(rpb_pad.shape)],
        out_specs=[tok(da), tok(da), tok(da), tok(da), tok(da), tok(dc), ctx_spec(da), ctx_spec(da), bias_spec],
        out_shape=[out_tok, out_tok, out_tok, out_tok, out_tok,
                   jax.ShapeDtypeStruct((b, s, dc), BF16), out_ctx, out_ctx,
                   jax.ShapeDtypeStruct((n_tiles, 2 * GRID_W, 2 * GRID_W), F32)],
        scratch_shapes=[pltpu.VMEM((tm + 2 * HALO + cr, d), BF16),
                        pltpu.VMEM((tm // sub, sub + 2 * HALO, dc), F32),
                        pltpu.VMEM(w_in.shape, BF16)],
        compiler_params=pltpu.CompilerParams(vmem_limit_bytes=VMEM_LIMIT,
                                             dimension_semantics=("arbitrary", "arbitrary")),
        name="in_proj",
    )(x, x, x, ctx, ada, norm_g, w_in, gains, cos, slo, shi, bd, conv_w, conv_b, rpb_pad)
    bias = outs[-1].reshape(n_heads // 2, ndr - 1, 2 * GRID_W, 2 * GRID_W)
    return (*outs[:-1], bias)


def _dot_nt(a, b):
    return lax.dot_general(a, b, (((1,), (1,)), ((), ())), preferred_element_type=F32)


def _attn_kernel(qrot_ref, qpl_ref, k_ref, v_ref, kc_ref, vc_ref, ga_ref, bias_ref, o_ref,
                 s_lat_ref, s_ctx_ref, p_lat_ref, p_ctx_ref, *, rows, group, n_batch, steps):
    n_units = n_batch * rows // group
    nk = WIN_H * GRID_W

    lane = lax.broadcasted_iota(jnp.int32, (GRID_W, LANES), 1)
    first_head = lane < HEAD_DIM

    def stack_heads(q2):
        zero = jnp.zeros_like(q2)
        return jnp.concatenate([jnp.where(first_head, q2, zero), jnp.where(first_head, zero, q2)], axis=0)

    def geometry(u, g):
        r = u * group + g
        if isinstance(r, int):
            bb, i = divmod(r, rows)
            rs = min(max(i - WIN_H // 2, 0), rows - WIN_H)
            return bb, i - rs, i * GRID_W, rs * GRID_W
        bb = r // rows
        i = r % rows
        rs = jnp.clip(i - WIN_H // 2, 0, rows - WIN_H)
        return bb, i - rs, pl.multiple_of(i * GRID_W, GRID_W), pl.multiple_of(rs * GRID_W, GRID_W)


    def scores_matmul(u):
        out = []
        for g in range(group):
            bb, off, tok0, key0 = geometry(u, g)
            qs = stack_heads(qrot_ref[bb, pl.ds(tok0, GRID_W), :])
            kband = k_ref[bb, pl.ds(key0, nk), :]
            qp = stack_heads(qpl_ref[bb, pl.ds(tok0, GRID_W), :])
            out.append((_dot_nt(qs, kband), off, _dot_nt(qp, kc_ref[bb])))
        return out


    def scores_store(slot, vals):
        for g, (s_lat, off, s_ctx) in enumerate(vals):
            first = (WIN_H - 1) - off
            bias = jnp.concatenate([bias_ref[0, first + 2 * t] for t in range(WIN_H // 2)], axis=1)
            s_lat_ref[slot, g] = s_lat + bias
            s_ctx_ref[slot, g] = s_ctx

    def softmax(slot):
        for g in range(group):
            s_lat = s_lat_ref[slot, g]
            s_ctx = s_ctx_ref[slot, g]
            m = jnp.maximum(jnp.max(s_lat, axis=-1, keepdims=True), jnp.max(s_ctx, axis=-1, keepdims=True))
            p_lat_ref[slot, g] = jnp.exp2(s_lat - m).astype(BF16)
            p_ctx_ref[slot, g] = jnp.exp2(s_ctx - m).astype(BF16)

    ones_lat = jnp.ones((nk, LANES), BF16)
    ones_ctx = jnp.ones((kc_ref.shape[1], LANES), BF16)

    def pv_matmul(u, slot):
        out = []
        for g in range(group):
            bb, _, _, key0 = geometry(u, g)
            v_lat = jnp.concatenate([v_ref[bb, pl.ds(key0, nk), :], ones_lat], axis=1)
            v_ctx = jnp.concatenate([vc_ref[bb], ones_ctx], axis=1)
            out.append(jnp.dot(p_lat_ref[slot, g], v_lat, preferred_element_type=F32)
                       + jnp.dot(p_ctx_ref[slot, g], v_ctx, preferred_element_type=F32))
        return out

    def pv_store(u, vals):
        for g, ol in enumerate(vals):
            bb, _, tok0, _ = geometry(u, g)
            o = ol[:, 0:LANES] * (1.0 / ol[:, LANES:])
            o2 = jnp.where(first_head, o[0:GRID_W], o[GRID_W:2 * GRID_W])
            gate = ga_ref[bb, pl.ds(tok0, GRID_W), :].astype(F32)
            o_ref[bb, pl.ds(tok0, GRID_W), :] = (o2 * gate).astype(BF16)

    scores_store(0, scores_matmul(0))
    sc = scores_matmul(1)
    softmax(0)
    scores_store(1, sc)

    def step(t, slot):
        sc = scores_matmul(t)
        pv = pv_matmul(t - 2, slot)
        softmax(1 - slot)
        scores_store(slot, sc)
        pv_store(t - 2, pv)

    n_steady, n_left = divmod(n_units - 2, steps)

    def steady(n, carry):
        for j in range(steps):
            step(steps * n + 2 + j, j % 2)
        return carry

    lax.fori_loop(0, n_steady, steady, 0)
    for j in range(n_left):
        step(steps * n_steady + 2 + j, j % 2)

    pv = pv_matmul(n_units - 2, 0)
    softmax(1)
    pv_store(n_units - 2, pv)
    pv_store(n_units - 1, pv_matmul(n_units - 1, 1))


def _attention(qrot, qpl, k, v, kc, vc, ga, bias, *, group, n_batch, steps):
    b, s, da = qrot.shape
    l = vc.shape[1]
    rows = s // GRID_W
    n_hp = da // LANES
    nq = 2 * GRID_W
    nk = WIN_H * GRID_W
    assert rows % (2 * group) == 0 and b % n_batch == 0 and steps % 2 == 0
    tok = pl.BlockSpec((n_batch, s, LANES), lambda hp, i: (i, 0, hp))
    ctx_tok = pl.BlockSpec((n_batch, l, LANES), lambda hp, i: (i, 0, hp))
    return pl.pallas_call(
        functools.partial(_attn_kernel, rows=rows, group=group, n_batch=n_batch, steps=steps),
        grid=(n_hp, b // n_batch),
        in_specs=[tok, tok, tok, tok, ctx_tok, ctx_tok, tok,
                  pl.BlockSpec((1,) + bias.shape[1:], lambda hp, i: (hp, 0, 0, 0))],
        out_specs=tok,
        out_shape=jax.ShapeDtypeStruct((b, s, da), BF16),
        scratch_shapes=[pltpu.VMEM((2, group, nq, nk), F32),
                        pltpu.VMEM((2, group, nq, l), F32),
                        pltpu.VMEM((2, group, nq, nk), BF16),
                        pltpu.VMEM((2, group, nq, l), BF16)],
        compiler_params=pltpu.CompilerParams(vmem_limit_bytes=VMEM_LIMIT),
        name="attn",
    )(qrot, qpl, k, v, kc, vc, ga, bias)


X_RING = 4


def _out_proj_kernel(x_hbm, a_hbm, c_hbm, ada_ref, w_ref, o_ref, xbuf_ref, abuf_ref, cbuf_ref, sem_ref,
                     *, da, tm, nj):
    t = pl.program_id(0)
    n = pl.num_programs(0)
    streams = ((x_hbm, xbuf_ref), (a_hbm, abuf_ref), (c_hbm, cbuf_ref))

    def copies(tile, slot):
        rows = pl.ds(pl.multiple_of((tile % nj) * tm, tm), tm)
        return [pltpu.make_async_copy(hbm.at[tile // nj, rows, :], buf.at[slot], sem_ref.at[k, slot])
                for k, (hbm, buf) in enumerate(streams)]

    def start_all(cps):
        for k, cp in enumerate(cps):
            cp.start(priority=min(k, 1))

    @pl.when(t == 0)
    def _():
        for k in range(X_RING - 1):
            start_all(copies(k, k))

    @pl.when(t + (X_RING - 1) < n)
    def _():
        start_all(copies(t + (X_RING - 1), (t + (X_RING - 1)) % X_RING))

    slot = t % X_RING
    for cp in copies(t, slot):
        cp.wait()
    _, _, gate = _ada_row(ada_ref, t // nj)
    upd = (jnp.dot(abuf_ref[slot], w_ref[0:da, :].astype(BF16), preferred_element_type=F32)
           + jnp.dot(cbuf_ref[slot], w_ref[da:, :].astype(BF16), preferred_element_type=F32))
    o_ref[0] = xbuf_ref[slot] + gate * upd


def _out_proj(x, attn, conv, ada, w_out, *, tm):
    b, s, d = x.shape
    da = attn.shape[2]
    dc = conv.shape[2]
    nj = s // tm
    assert s % tm == 0 and b * nj >= X_RING
    tok = lambda width: pl.BlockSpec((1, tm, width), lambda t: (t // nj, t % nj, 0))
    return pl.pallas_call(
        functools.partial(_out_proj_kernel, da=da, tm=tm, nj=nj),
        grid=(b * nj,),
        in_specs=[pl.BlockSpec(memory_space=pl.ANY),
                  pl.BlockSpec(memory_space=pl.ANY),
                  pl.BlockSpec(memory_space=pl.ANY),
                  pl.BlockSpec(ada.shape, lambda t: (0, 0)),
                  pl.BlockSpec(w_out.shape, lambda t: (0, 0))],
        out_specs=tok(d),
        out_shape=jax.ShapeDtypeStruct((b, s, d), F32),
        scratch_shapes=[pltpu.VMEM((X_RING, tm, d), F32),
                        pltpu.VMEM((X_RING, tm, da), BF16),
                        pltpu.VMEM((X_RING, tm, dc), BF16),
                        pltpu.SemaphoreType.DMA((3, X_RING))],
        compiler_params=pltpu.CompilerParams(vmem_limit_bytes=VMEM_LIMIT,
                                             dimension_semantics=("arbitrary",)),
        name="out_proj",
    )(x, attn, conv, ada, w_out)


def _rope_tables(s):
    nf = HEAD_DIM // 4
    inv = (ROPE_THETA ** (-np.arange(nf, dtype=np.float32) / nf)).astype(np.float32)
    pos = np.arange(s)
    lane = np.arange(LANES)
    d = lane % HEAD_DIM
    axis = d // (2 * nf)
    half = (d % (2 * nf)) // nf
    coord = np.where(axis[None, :] == 0, (pos // GRID_W)[:, None], (pos % GRID_W)[:, None]).astype(np.float32)
    ang = (coord * inv[d % nf][None, :]).astype(np.float32)
    cos = np.cos(ang).astype(np.float32)
    sin = np.sin(ang).astype(np.float32)
    sin_lo = np.where(half[None, :] == 0, -sin, 0.0).astype(np.float32)
    sin_hi = np.where(half[None, :] == 1, sin, 0.0).astype(np.float32)
    return jnp.asarray(cos), jnp.asarray(sin_lo), jnp.asarray(sin_hi)


def _bias_tiles(rpb_ref, o_ref, first_tile):
    n_pairs = rpb_ref.shape[1] - 1
    cq = lax.broadcasted_iota(jnp.int32, (GRID_W, LANES), 0)
    ck = lax.broadcasted_iota(jnp.int32, (GRID_W, LANES), 1) % GRID_W
    col_start = jnp.clip(cq - WIN_W // 2, 0, GRID_W - WIN_W)
    valid = (ck >= col_start) & (ck < col_start + WIN_W)
    for e in range(o_ref.shape[0]):
        t = first_tile + e
        hp = t // n_pairs
        dr = t % n_pairs
        for a in range(2):
            two = rpb_ref[2 * hp + a, pl.ds(dr, 2), :]
            lanes = jnp.broadcast_to(jnp.concatenate([two[0:1], two[1:2]], axis=1), (GRID_W, LANES))
            toeplitz = pltpu.roll(lanes, LANES - (WIN_W - 1), 1, stride=1, stride_axis=0)
            o_ref[e, a * GRID_W:(a + 1) * GRID_W, :] = jnp.where(valid, toeplitz * LOG2_E, MASK_VALUE)


def kernel(x, c, ctx, c_ctx, w_ada, b_ada, norm_g, w_in, q_norm_g, k_norm_g, rpb, conv_w, conv_b, w_out):
    depth = w_ada.shape[0]
    b, s, d = x.shape
    dc = conv_w.shape[2]
    da = (w_in.shape[2] - 4 * dc) // 4
    rows = s // GRID_W
    assert depth == 1 and s % GRID_W == 0 and rows >= WIN_H and da % LANES == 0

    cos, slo, shi = _rope_tables(s)
    assert da % MXU_DIM == 0 and MXU_DIM % HEAD_DIM == 0
    seg = np.arange(MXU_DIM) // HEAD_DIM
    bd = jnp.asarray((seg[:, None] == seg[None, :]).astype(np.float32), dtype=BF16)

    ada, gains = _prep(c, c_ctx, w_ada[0], b_ada, q_norm_g, k_norm_g, da, steps=PREP_STEPS)

    qrot, qpl, k, v, ga, conv, kc, vc, bias = _in_proj(x, ctx, ada, b, norm_g, w_in[0], gains, cos, slo, shi, bd,
                                                       conv_w, conv_b, rpb[0], tm=IN_PROJ_ROWS, sub=IN_PROJ_SUB)
    attn = _attention(qrot, qpl, k, v, kc, vc, ga, bias, group=ATTN_GROUP, n_batch=ATTN_BATCH, steps=ATTN_STEPS)
    return _out_proj(x, attn, conv, ada, w_out[0], tm=OUT_PROJ_ROWS)
```

```python
import functools

import numpy as np
import jax
import jax.numpy as jnp
from jax import lax
from jax.experimental import pallas as pl
from jax.experimental.pallas import tpu as pltpu

F32 = jnp.float32
BF16 = jnp.bfloat16

HEAD_DIM = 64
GRID_W = 64
WIN_H = 8
WIN_W = 16
CONV_K = 3
ROPE_THETA = 10000.0
RMS_EPS = 1e-6
MASK_VALUE = -1e30
LOG2_E = 1.4426950408889634

LANES = 128
MXU_DIM = 256
F32_ROWS = 8
BF16_ROWS = 16
HALO = BF16_ROWS
VMEM_CAPACITY = 64 * 1024 * 1024
VMEM_LIMIT = VMEM_CAPACITY * 7 // 8

PREP_STEPS = 4
IN_PROJ_ROWS, IN_PROJ_SUB = 1024, 512
OUT_PROJ_ROWS = 512
ATTN_GROUP, ATTN_BATCH, ATTN_STEPS = 2, 4, 30


def _silu(z):
    return z * jax.nn.sigmoid(z)


def _prep_kernel(c_ref, cctx_ref, w_ref, b_ref, qg_ref, kg_ref, ada_ref, gains_ref):
    nb, d = c_ref.shape
    cctx = cctx_ref[...].reshape(1, d)
    cond = jnp.concatenate([c_ref[...], jnp.broadcast_to(cctx, (ada_ref.shape[0] - nb, d))], axis=0)
    a = _silu(cond).astype(BF16)
    ada_ref[...] = jnp.dot(a, w_ref[...].astype(BF16), preferred_element_type=F32) + b_ref[...]
    g = jnp.concatenate([qg_ref[...] * (HEAD_DIM ** -0.5 * LOG2_E), kg_ref[...]], axis=0)
    gains_ref[...] = jnp.concatenate([g] * (gains_ref.shape[1] // HEAD_DIM), axis=1)


def _prep(c, c_ctx, w_ada, b_ada, q_gain, k_gain, d_attn, *, steps):
    nb, d = c.shape
    assert nb % F32_ROWS == 0
    rows = nb + F32_ROWS
    n = w_ada.shape[1]
    tn = n // steps
    assert tn % LANES == 0
    return pl.pallas_call(
        _prep_kernel,
        grid=(steps,),
        in_specs=[pl.BlockSpec((nb, d), lambda i: (0, 0)),
                  pl.BlockSpec((d,), lambda i: (0,)),
                  pl.BlockSpec((d, tn), lambda i: (0, i)),
                  pl.BlockSpec((1, tn), lambda i: (0, i)),
                  pl.BlockSpec((1, HEAD_DIM), lambda i: (0, 0)),
                  pl.BlockSpec((1, HEAD_DIM), lambda i: (0, 0))],
        out_specs=[pl.BlockSpec((rows, tn), lambda i: (0, i)),
                   pl.BlockSpec((2, d_attn), lambda i: (0, 0))],
        out_shape=[jax.ShapeDtypeStruct((rows, n), F32),
                   jax.ShapeDtypeStruct((2, d_attn), F32)],
        compiler_params=pltpu.CompilerParams(vmem_limit_bytes=VMEM_LIMIT),
        name="prep",
    )(c, c_ctx, w_ada, b_ada, q_gain, k_gain)


def _ada_row(ada_ref, row):
    r = ada_ref[pl.ds(row, 1), :]
    d = r.shape[1] // 3
    return r[:, 0:d], r[:, d:2 * d], r[:, 2 * d:]


def _modulated_norm(xt, mult, shift):
    ms = jnp.mean(xt * xt, axis=-1, keepdims=True)
    return (xt * lax.rsqrt(ms + RMS_EPS) * mult + shift).astype(BF16)


def _head_norm(t, gain, bd):
    t2 = (t * t).astype(BF16)
    w = bd.shape[0]
    ss = jnp.concatenate([jnp.dot(t2[:, c:c + w], bd, preferred_element_type=F32)
                          for c in range(0, t.shape[1], w)], axis=1)
    return t * lax.rsqrt(ss * (1.0 / HEAD_DIM) + RMS_EPS) * gain


def _rope(t, cos, sin_lo, sin_hi):
    outs = []
    for c in range(t.shape[1] // LANES):
        tc = t[:, c * LANES:(c + 1) * LANES]
        up = pltpu.roll(tc, LANES - HEAD_DIM // 4, 1)
        dn = pltpu.roll(tc, HEAD_DIM // 4, 1)
        outs.append(tc * cos + up * sin_lo + dn * sin_hi)
    return jnp.concatenate(outs, axis=1)


def _in_proj_kernel(x_ref, xp_ref, xn_ref, ctx_ref, ada_ref, ng_ref, w_ref, g_ref,
                    cos_ref, slo_ref, shi_ref, bd_ref, cw_ref, cb_ref, rpb_ref,
                    qrot_ref, qpl_ref, k_ref, v_ref, ga_ref, conv_ref, kc_ref, vc_ref, bias_ref,
                    hext_ref, cu_ref, wb_ref, *, tm, sub, da, dc, ctx_row, bias_blocks):
    j = pl.program_id(1)
    nj = pl.num_programs(1)

    @pl.when((pl.program_id(0) == 0) & (j == 0))
    def _():
        for c in range(0, w_ref.shape[1], MXU_DIM):
            wb_ref[:, c:c + MXU_DIM] = w_ref[:, c:c + MXU_DIM].astype(BF16)

    step = pl.program_id(0) * nj + j
    _bias_tiles(rpb_ref, bias_ref, jnp.minimum(step, bias_blocks - 1) * bias_ref.shape[0])
    shift, scale, _ = _ada_row(ada_ref, pl.program_id(0))
    mult = ng_ref[...] * (1.0 + scale)
    n_sub = tm // sub
    cr = ctx_ref.shape[1]

    def proj(src, lo, width):
        return jnp.dot(hext_ref[src, :], wb_ref[:, lo:lo + width], preferred_element_type=F32)

    bd = bd_ref[...]
    hext_ref[0:HALO, :] = _modulated_norm(xp_ref[0], mult, shift)
    hext_ref[HALO + tm:2 * HALO + tm, :] = _modulated_norm(xn_ref[0], mult, shift)
    shift_c, scale_c, _ = _ada_row(ada_ref, ctx_row)
    hext_ref[2 * HALO + tm:, :] = _modulated_norm(ctx_ref[0], ng_ref[...] * (1.0 + scale_c), shift_c)

    for n in range(n_sub):
        r0 = n * sub
        hext_ref[HALO + r0:HALO + r0 + sub, :] = _modulated_norm(x_ref[0, r0:r0 + sub, :], mult, shift)
    for n in range(n_sub):
        r0 = n * sub
        rows = slice(r0, r0 + sub)
        h = slice(HALO + r0, HALO + r0 + sub)
        with_ctx = n == n_sub - 1
        h_kv = slice(HALO + r0, 2 * HALO + tm + cr) if with_ctx else h
        cos, slo, shi = cos_ref[rows, :], slo_ref[rows, :], shi_ref[rows, :]


        hx = slice(r0, r0 + sub + 2 * HALO)
        cu = proj(hx, 4 * da, dc) * proj(hx, 4 * da + 2 * dc, dc)
        row = lax.broadcasted_iota(jnp.int32, (sub + 2 * HALO, 1), 0) + r0
        inside = ((row >= HALO) | (j > 0)) & ((row < HALO + tm) | (j < nj - 1))
        cu_ref[n] = jnp.where(inside, cu, 0.0)
        y = (cb_ref[...]
             + cw_ref[0, 0:1, :] * cu_ref[n, HALO - 1:HALO - 1 + sub, :]
             + cw_ref[0, 1:2, :] * cu_ref[n, HALO:HALO + sub, :]
             + cw_ref[0, 2:3, :] * cu_ref[n, HALO + 1:HALO + 1 + sub, :])
        bg = proj(h, 4 * da + dc, dc)
        zc = proj(h, 4 * da + 3 * dc, dc)
        conv_ref[0, rows, :] = (bg * y * _silu(zc)).astype(BF16)

        qn = _head_norm(proj(h, 0, da), g_ref[0:1, :], bd)
        qpl_ref[0, rows, :] = qn.astype(BF16)
        qrot_ref[0, rows, :] = _rope(qn, cos, slo, shi).astype(BF16)

        kf = proj(h_kv, da, da)
        kn = _head_norm(kf[0:sub], g_ref[1:2, :], bd)
        k_ref[0, rows, :] = _rope(kn, cos, slo, shi).astype(BF16)
        if with_ctx:
            kc_ref[0] = _head_norm(kf[sub + HALO:], g_ref[1:2, :], bd).astype(BF16)

        ga_ref[0, rows, :] = _silu(proj(h, 3 * da, da)).astype(BF16)
        vf = proj(h_kv, 2 * da, da)
        v_ref[0, rows, :] = vf[0:sub].astype(BF16)
        if with_ctx:
            vc_ref[0] = vf[sub + HALO:].astype(BF16)


def _in_proj(x, ctx, ada, ctx_row, norm_g, w_in, gains, cos, slo, shi, bd, conv_w, conv_b, rpb, *, tm, sub):
    b, s, d = x.shape
    n_heads, ndr, ndc = rpb.shape
    assert ndr == 2 * WIN_H - 1 and ndc == 2 * WIN_W - 1 and ndc <= GRID_W
    rpb_pad = jnp.pad(rpb, ((0, 0), (0, 0), (0, GRID_W - ndc)))
    n_tiles = (n_heads // 2) * (ndr - 1)
    bias_blocks = max(k for k in range(1, b * (s // tm) + 1) if n_tiles % k == 0)
    tiles_per_step = n_tiles // bias_blocks
    l = ctx.shape[1]
    assert s % tm == 0 and tm % sub == 0 and sub % HALO == 0
    cr = l // (s // tm)
    assert l % (s // tm) == 0 and cr % BF16_ROWS == 0
    dc = conv_w.shape[2]
    da = (w_in.shape[1] - 4 * dc) // 4
    nh = tm // HALO
    last_halo = s // HALO - 1
    tok = lambda width: pl.BlockSpec((1, tm, width), lambda i, j: (i, j, 0))
    const = lambda shape: pl.BlockSpec(shape, lambda i, j: (0,) * len(shape))
    tab = pl.BlockSpec((tm, LANES), lambda i, j: (j, 0))
    out_tok = jax.ShapeDtypeStruct((b, s, da), BF16)
    out_ctx = jax.ShapeDtypeStruct((b, l, da), BF16)
    ctx_spec = lambda width: pl.BlockSpec((1, cr, width), lambda i, j: (i, j, 0))
    nj = s // tm
    bias_spec = pl.BlockSpec((tiles_per_step, 2 * GRID_W, 2 * GRID_W),
                             lambda i, j: (jnp.minimum(i * nj + j, bias_blocks - 1), 0, 0))
    outs = pl.pallas_call(
        functools.partial(_in_proj_kernel, tm=tm, sub=sub, da=da, dc=dc, ctx_row=ctx_row,
                          bias_blocks=bias_blocks),
        grid=(b, s // tm),
        in_specs=[tok(d),
                  pl.BlockSpec((1, HALO, d), lambda i, j: (i, jnp.maximum(j * nh - 1, 0), 0)),
                  pl.BlockSpec((1, HALO, d), lambda i, j: (i, jnp.minimum((j + 1) * nh, last_halo), 0)),
                  ctx_spec(d),
                  const(ada.shape),
                  const((1, d)), const(w_in.shape), const(gains.shape),
                  tab, tab, tab, const(bd.shape), const((1, CONV_K, dc)), const((1, dc)),
                  const(rpb_pad.shape)],
        out_specs=[tok(da), tok(da), tok(da), tok(da), tok(da), tok(dc), ctx_spec(da), ctx_spec(da), bias_spec],
        out_shape=[out_tok, out_tok, out_tok, out_tok, out_tok,
                   jax.ShapeDtypeStruct((b, s, dc), BF16), out_ctx, out_ctx,
                   jax.ShapeDtypeStruct((n_tiles, 2 * GRID_W, 2 * GRID_W), F32)],
        scratch_shapes=[pltpu.VMEM((tm + 2 * HALO + cr, d), BF16),
                        pltpu.VMEM((tm // sub, sub + 2 * HALO, dc), F32),
                        pltpu.VMEM(w_in.shape, BF16)],
        compiler_params=pltpu.CompilerParams(vmem_limit_bytes=VMEM_LIMIT,
                                             dimension_semantics=("arbitrary", "arbitrary")),
        name="in_proj",
    )(x, x, x, ctx, ada, norm_g, w_in, gains, cos, slo, shi, bd, conv_w, conv_b, rpb_pad)
    bias = outs[-1].reshape(n_heads // 2, ndr - 1, 2 * GRID_W, 2 * GRID_W)
    return (*outs[:-1], bias)


def _dot_nt(a, b):
    return lax.dot_general(a, b, (((1,), (1,)), ((), ())), preferred_element_type=F32)


def _attn_kernel(qrot_ref, qpl_ref, k_ref, v_ref, kc_ref, vc_ref, ga_ref, bias_ref, o_ref,
                 s_lat_ref, s_ctx_ref, p_lat_ref, p_ctx_ref, *, rows, group, n_batch, steps):
    n_units = n_batch * rows // group
    nk = WIN_H * GRID_W

    lane = lax.broadcasted_iota(jnp.int32, (GRID_W, LANES), 1)
    first_head = lane < HEAD_DIM

    def stack_heads(q2):
        zero = jnp.zeros_like(q2)
        return jnp.concatenate([jnp.where(first_head, q2, zero), jnp.where(first_head, zero, q2)], axis=0)

    def geometry(u, g):
        r = u * group + g
        if isinstance(r, int):
            bb, i = divmod(r, rows)
            rs = min(max(i - WIN_H // 2, 0), rows - WIN_H)
            return bb, i - rs, i * GRID_W, rs * GRID_W
        bb = r // rows
        i = r % rows
        rs = jnp.clip(i - WIN_H // 2, 0, rows - WIN_H)
        return bb, i - rs, pl.multiple_of(i * GRID_W, GRID_W), pl.multiple_of(rs * GRID_W, GRID_W)


    def scores_matmul(u):
        out = []
        for g in range(group):
            bb, off, tok0, key0 = geometry(u, g)
            qs = stack_heads(qrot_ref[bb, pl.ds(tok0, GRID_W), :])
            kband = k_ref[bb, pl.ds(key0, nk), :]
            qp = stack_heads(qpl_ref[bb, pl.ds(tok0, GRID_W), :])
            out.append((_dot_nt(qs, kband), off, _dot_nt(qp, kc_ref[bb])))
        return out


    def scores_store(slot, vals):
        for g, (s_lat, off, s_ctx) in enumerate(vals):
            first = (WIN_H - 1) - off
            bias = jnp.concatenate([bias_ref[0, first + 2 * t] for t in range(WIN_H // 2)], axis=1)
            s_lat_ref[slot, g] = s_lat + bias
            s_ctx_ref[slot, g] = s_ctx

    def softmax(slot):
        for g in range(group):
            s_lat = s_lat_ref[slot, g]
            s_ctx = s_ctx_ref[slot, g]
            m = jnp.maximum(jnp.max(s_lat, axis=-1, keepdims=True), jnp.max(s_ctx, axis=-1, keepdims=True))
            p_lat_ref[slot, g] = jnp.exp2(s_lat - m).astype(BF16)
            p_ctx_ref[slot, g] = jnp.exp2(s_ctx - m).astype(BF16)

    ones_lat = jnp.ones((nk, LANES), BF16)
    ones_ctx = jnp.ones((kc_ref.shape[1], LANES), BF16)

    def pv_matmul(u, slot):
        out = []
        for g in range(group):
            bb, _, _, key0 = geometry(u, g)
            v_lat = jnp.concatenate([v_ref[bb, pl.ds(key0, nk), :], ones_lat], axis=1)
            v_ctx = jnp.concatenate([vc_ref[bb], ones_ctx], axis=1)
            out.append(jnp.dot(p_lat_ref[slot, g], v_lat, preferred_element_type=F32)
                       + jnp.dot(p_ctx_ref[slot, g], v_ctx, preferred_element_type=F32))
        return out

    def pv_store(u, vals):
        for g, ol in enumerate(vals):
            bb, _, tok0, _ = geometry(u, g)
            o = ol[:, 0:LANES] * (1.0 / ol[:, LANES:])
            o2 = jnp.where(first_head, o[0:GRID_W], o[GRID_W:2 * GRID_W])
            gate = ga_ref[bb, pl.ds(tok0, GRID_W), :].astype(F32)
            o_ref[bb, pl.ds(tok0, GRID_W), :] = (o2 * gate).astype(BF16)

    scores_store(0, scores_matmul(0))
    sc = scores_matmul(1)
    softmax(0)
    scores_store(1, sc)

    def step(t, slot):
        sc = scores_matmul(t)
        pv = pv_matmul(t - 2, slot)
        softmax(1 - slot)
        scores_store(slot, sc)
        pv_store(t - 2, pv)

    n_steady, n_left = divmod(n_units - 2, steps)

    def steady(n, carry):
        for j in range(steps):
            step(steps * n + 2 + j, j % 2)
        return carry

    lax.fori_loop(0, n_steady, steady, 0)
    for j in range(n_left):
        step(steps * n_steady + 2 + j, j % 2)

    pv = pv_matmul(n_units - 2, 0)
    softmax(1)
    pv_store(n_units - 2, pv)
    pv_store(n_units - 1, pv_matmul(n_units - 1, 1))


def _attention(qrot, qpl, k, v, kc, vc, ga, bias, *, group, n_batch, steps):
    b, s, da = qrot.shape
    l = vc.shape[1]
    rows = s // GRID_W
    n_hp = da // LANES
    nq = 2 * GRID_W
    nk = WIN_H * GRID_W
    assert rows % (2 * group) == 0 and b % n_batch == 0 and steps % 2 == 0
    tok = pl.BlockSpec((n_batch, s, LANES), lambda hp, i: (i, 0, hp))
    ctx_tok = pl.BlockSpec((n_batch, l, LANES), lambda hp, i: (i, 0, hp))
    return pl.pallas_call(
        functools.partial(_attn_kernel, rows=rows, group=group, n_batch=n_batch, steps=steps),
        grid=(n_hp, b // n_batch),
        in_specs=[tok, tok, tok, tok, ctx_tok, ctx_tok, tok,
                  pl.BlockSpec((1,) + bias.shape[1:], lambda hp, i: (hp, 0, 0, 0))],
        out_specs=tok,
        out_shape=jax.ShapeDtypeStruct((b, s, da), BF16),
        scratch_shapes=[pltpu.VMEM((2, group, nq, nk), F32),
                        pltpu.VMEM((2, group, nq, l), F32),
                        pltpu.VMEM((2, group, nq, nk), BF16),
                        pltpu.VMEM((2, group, nq, l), BF16)],
        compiler_params=pltpu.CompilerParams(vmem_limit_bytes=VMEM_LIMIT),
        name="attn",
    )(qrot, qpl, k, v, kc, vc, ga, bias)


X_RING = 4


def _out_proj_kernel(x_hbm, a_hbm, c_hbm, ada_ref, w_ref, o_ref, xbuf_ref, abuf_ref, cbuf_ref, sem_ref,
                     wb_ref, *, da, tm, nj):
    t = pl.program_id(0)
    n = pl.num_programs(0)
    streams = ((x_hbm, xbuf_ref), (a_hbm, abuf_ref), (c_hbm, cbuf_ref))

    def copies(tile, slot):
        rows = pl.ds(pl.multiple_of((tile % nj) * tm, tm), tm)
        return [pltpu.make_async_copy(hbm.at[tile // nj, rows, :], buf.at[slot], sem_ref.at[k, slot])
                for k, (hbm, buf) in enumerate(streams)]

    @pl.when(t == 0)
    def _():
        for k in range(X_RING - 1):
            for cp in copies(k, k):
                cp.start()
        wb_ref[...] = w_ref[...].astype(BF16)

    @pl.when(t + (X_RING - 1) < n)
    def _():
        for cp in copies(t + (X_RING - 1), (t + (X_RING - 1)) % X_RING):
            cp.start()

    slot = t % X_RING
    for cp in copies(t, slot):
        cp.wait()
    _, _, gate = _ada_row(ada_ref, t // nj)
    upd = (jnp.dot(abuf_ref[slot], wb_ref[0:da, :], preferred_element_type=F32)
           + jnp.dot(cbuf_ref[slot], wb_ref[da:, :], preferred_element_type=F32))
    o_ref[0] = xbuf_ref[slot] + gate * upd


def _out_proj(x, attn, conv, ada, w_out, *, tm):
    b, s, d = x.shape
    da = attn.shape[2]
    dc = conv.shape[2]
    nj = s // tm
    assert s % tm == 0 and b * nj >= X_RING
    tok = lambda width: pl.BlockSpec((1, tm, width), lambda t: (t // nj, t % nj, 0))
    return pl.pallas_call(
        functools.partial(_out_proj_kernel, da=da, tm=tm, nj=nj),
        grid=(b * nj,),
        in_specs=[pl.BlockSpec(memory_space=pl.ANY),
                  pl.BlockSpec(memory_space=pl.ANY),
                  pl.BlockSpec(memory_space=pl.ANY),
                  pl.BlockSpec(ada.shape, lambda t: (0, 0)),
                  pl.BlockSpec(w_out.shape, lambda t: (0, 0))],
        out_specs=tok(d),
        out_shape=jax.ShapeDtypeStruct((b, s, d), F32),
        scratch_shapes=[pltpu.VMEM((X_RING, tm, d), F32),
                        pltpu.VMEM((X_RING, tm, da), BF16),
                        pltpu.VMEM((X_RING, tm, dc), BF16),
                        pltpu.SemaphoreType.DMA((3, X_RING)),
                        pltpu.VMEM(w_out.shape, BF16)],
        compiler_params=pltpu.CompilerParams(vmem_limit_bytes=VMEM_LIMIT,
                                             dimension_semantics=("arbitrary",)),
        name="out_proj",
    )(x, attn, conv, ada, w_out)


def _rope_tables(s):
    nf = HEAD_DIM // 4
    inv = (ROPE_THETA ** (-np.arange(nf, dtype=np.float32) / nf)).astype(np.float32)
    pos = np.arange(s)
    lane = np.arange(LANES)
    d = lane % HEAD_DIM
    axis = d // (2 * nf)
    half = (d % (2 * nf)) // nf
    coord = np.where(axis[None, :] == 0, (pos // GRID_W)[:, None], (pos % GRID_W)[:, None]).astype(np.float32)
    ang = (coord * inv[d % nf][None, :]).astype(np.float32)
    cos = np.cos(ang).astype(np.float32)
    sin = np.sin(ang).astype(np.float32)
    sin_lo = np.where(half[None, :] == 0, -sin, 0.0).astype(np.float32)
    sin_hi = np.where(half[None, :] == 1, sin, 0.0).astype(np.float32)
    return jnp.asarray(cos), jnp.asarray(sin_lo), jnp.asarray(sin_hi)


def _bias_tiles(rpb_ref, o_ref, first_tile):
    n_pairs = rpb_ref.shape[1] - 1
    cq = lax.broadcasted_iota(jnp.int32, (GRID_W, LANES), 0)
    ck = lax.broadcasted_iota(jnp.int32, (GRID_W, LANES), 1) % GRID_W
    col_start = jnp.clip(cq - WIN_W // 2, 0, GRID_W - WIN_W)
    valid = (ck >= col_start) & (ck < col_start + WIN_W)
    for e in range(o_ref.shape[0]):
        t = first_tile + e
        hp = t // n_pairs
        dr = t % n_pairs
        for a in range(2):
            two = rpb_ref[2 * hp + a, pl.ds(dr, 2), :]
            lanes = jnp.broadcast_to(jnp.concatenate([two[0:1], two[1:2]], axis=1), (GRID_W, LANES))
            toeplitz = pltpu.roll(lanes, LANES - (WIN_W - 1), 1, stride=1, stride_axis=0)
            o_ref[e, a * GRID_W:(a + 1) * GRID_W, :] = jnp.where(valid, toeplitz * LOG2_E, MASK_VALUE)


def kernel(x, c, ctx, c_ctx, w_ada, b_ada, norm_g, w_in, q_norm_g, k_norm_g, rpb, conv_w, conv_b, w_out):
    depth = w_ada.shape[0]
    b, s, d = x.shape
    dc = conv_w.shape[2]
    da = (w_in.shape[2] - 4 * dc) // 4
    rows = s // GRID_W
    assert depth == 1 and s % GRID_W == 0 and rows >= WIN_H and da % LANES == 0

    cos, slo, shi = _rope_tables(s)
    assert da % MXU_DIM == 0 and MXU_DIM % HEAD_DIM == 0
    seg = np.arange(MXU_DIM) // HEAD_DIM
    bd = jnp.asarray((seg[:, None] == seg[None, :]).astype(np.float32), dtype=BF16)

    ada, gains = _prep(c, c_ctx, w_ada[0], b_ada, q_norm_g, k_norm_g, da, steps=PREP_STEPS)

    qrot, qpl, k, v, ga, conv, kc, vc, bias = _in_proj(x, ctx, ada, b, norm_g, w_in[0], gains, cos, slo, shi, bd,
                                                       conv_w, conv_b, rpb[0], tm=IN_PROJ_ROWS, sub=IN_PROJ_SUB)
    attn = _attention(qrot, qpl, k, v, kc, vc, ga, bias, group=ATTN_GROUP, n_batch=ATTN_BATCH, steps=ATTN_STEPS)
    return _out_proj(x, attn, conv, ada, w_out[0], tm=OUT_PROJ_ROWS)
```

```python
import functools

import numpy as np
import jax
import jax.numpy as jnp
from jax import lax
from jax.experimental import pallas as pl
from jax.experimental.pallas import tpu as pltpu

F32 = jnp.float32
BF16 = jnp.bfloat16

HEAD_DIM = 64
GRID_W = 64
WIN_H = 8
WIN_W = 16
CONV_K = 3
ROPE_THETA = 10000.0
RMS_EPS = 1e-6
MASK_VALUE = -1e30
LOG2_E = 1.4426950408889634

LANES = 128
MXU_DIM = 256
F32_ROWS = 8
BF16_ROWS = 16
HALO = BF16_ROWS
VMEM_CAPACITY = 64 * 1024 * 1024
VMEM_LIMIT = VMEM_CAPACITY * 7 // 8

PREP_STEPS = 4
IN_PROJ_ROWS, IN_PROJ_SUB = 1024, 512
OUT_PROJ_ROWS = 256
ATTN_GROUP, ATTN_BATCH, ATTN_STEPS = 2, 4, 30


def _silu(z):
    return z * jax.nn.sigmoid(z)


def _prep_kernel(c_ref, cctx_ref, w_ref, b_ref, qg_ref, kg_ref, ada_ref, gains_ref):
    nb, d = c_ref.shape
    cctx = cctx_ref[...].reshape(1, d)
    cond = jnp.concatenate([c_ref[...], jnp.broadcast_to(cctx, (ada_ref.shape[0] - nb, d))], axis=0)
    a = _silu(cond).astype(BF16)
    ada_ref[...] = jnp.dot(a, w_ref[...].astype(BF16), preferred_element_type=F32) + b_ref[...]
    g = jnp.concatenate([qg_ref[...] * (HEAD_DIM ** -0.5 * LOG2_E), kg_ref[...]], axis=0)
    gains_ref[...] = jnp.concatenate([g] * (gains_ref.shape[1] // HEAD_DIM), axis=1)


def _prep(c, c_ctx, w_ada, b_ada, q_gain, k_gain, d_attn, *, steps):
    nb, d = c.shape
    assert nb % F32_ROWS == 0
    rows = nb + F32_ROWS
    n = w_ada.shape[1]
    tn = n // steps
    assert tn % LANES == 0
    return pl.pallas_call(
        _prep_kernel,
        grid=(steps,),
        in_specs=[pl.BlockSpec((nb, d), lambda i: (0, 0)),
                  pl.BlockSpec((d,), lambda i: (0,)),
                  pl.BlockSpec((d, tn), lambda i: (0, i)),
                  pl.BlockSpec((1, tn), lambda i: (0, i)),
                  pl.BlockSpec((1, HEAD_DIM), lambda i: (0, 0)),
                  pl.BlockSpec((1, HEAD_DIM), lambda i: (0, 0))],
        out_specs=[pl.BlockSpec((rows, tn), lambda i: (0, i)),
                   pl.BlockSpec((2, d_attn), lambda i: (0, 0))],
        out_shape=[jax.ShapeDtypeStruct((rows, n), F32),
                   jax.ShapeDtypeStruct((2, d_attn), F32)],
        compiler_params=pltpu.CompilerParams(vmem_limit_bytes=VMEM_LIMIT),
        name="prep",
    )(c, c_ctx, w_ada, b_ada, q_gain, k_gain)


def _ada_row(ada_ref, row):
    r = ada_ref[pl.ds(row, 1), :]
    d = r.shape[1] // 3
    return r[:, 0:d], r[:, d:2 * d], r[:, 2 * d:]


def _modulated_norm(xt, mult, shift):
    ms = jnp.mean(xt * xt, axis=-1, keepdims=True)
    return (xt * lax.rsqrt(ms + RMS_EPS) * mult + shift).astype(BF16)


def _head_norm(t, gain, bd):
    t2 = (t * t).astype(BF16)
    w = bd.shape[0]
    ss = jnp.concatenate([jnp.dot(t2[:, c:c + w], bd, preferred_element_type=F32)
                          for c in range(0, t.shape[1], w)], axis=1)
    return t * lax.rsqrt(ss * (1.0 / HEAD_DIM) + RMS_EPS) * gain


def _rope(t, cos, sin_lo, sin_hi):
    outs = []
    for c in range(t.shape[1] // LANES):
        tc = t[:, c * LANES:(c + 1) * LANES]
        up = pltpu.roll(tc, LANES - HEAD_DIM // 4, 1)
        dn = pltpu.roll(tc, HEAD_DIM // 4, 1)
        outs.append(tc * cos + up * sin_lo + dn * sin_hi)
    return jnp.concatenate(outs, axis=1)


def _in_proj_kernel(x_ref, xp_ref, xn_ref, ctx_ref, ada_ref, ng_ref, w_ref, g_ref,
                    cos_ref, slo_ref, shi_ref, bd_ref, cw_ref, cb_ref, rpb_ref,
                    qrot_ref, qpl_ref, k_ref, v_ref, ga_ref, conv_ref, kc_ref, vc_ref, bias_ref,
                    hext_ref, cu_ref, wb_ref, *, tm, sub, da, dc, ctx_row, bias_blocks):
    j = pl.program_id(1)
    nj = pl.num_programs(1)

    @pl.when((pl.program_id(0) == 0) & (j == 0))
    def _():
        for c in range(0, w_ref.shape[1], MXU_DIM):
            wb_ref[:, c:c + MXU_DIM] = w_ref[:, c:c + MXU_DIM].astype(BF16)

    step = pl.program_id(0) * nj + j
    _bias_tiles(rpb_ref, bias_ref, jnp.minimum(step, bias_blocks - 1) * bias_ref.shape[0])
    shift, scale, _ = _ada_row(ada_ref, pl.program_id(0))
    mult = ng_ref[...] * (1.0 + scale)
    n_sub = tm // sub
    cr = ctx_ref.shape[1]

    def proj(src, lo, width):
        return jnp.dot(hext_ref[src, :], wb_ref[:, lo:lo + width], preferred_element_type=F32)

    bd = bd_ref[...]
    hext_ref[0:HALO, :] = _modulated_norm(xp_ref[0], mult, shift)
    hext_ref[HALO + tm:2 * HALO + tm, :] = _modulated_norm(xn_ref[0], mult, shift)
    shift_c, scale_c, _ = _ada_row(ada_ref, ctx_row)
    hext_ref[2 * HALO + tm:, :] = _modulated_norm(ctx_ref[0], ng_ref[...] * (1.0 + scale_c), shift_c)

    for n in range(n_sub):
        r0 = n * sub
        hext_ref[HALO + r0:HALO + r0 + sub, :] = _modulated_norm(x_ref[0, r0:r0 + sub, :], mult, shift)
    for n in range(n_sub):
        r0 = n * sub
        rows = slice(r0, r0 + sub)
        h = slice(HALO + r0, HALO + r0 + sub)
        with_ctx = n == n_sub - 1
        h_kv = slice(HALO + r0, 2 * HALO + tm + cr) if with_ctx else h
        cos, slo, shi = cos_ref[rows, :], slo_ref[rows, :], shi_ref[rows, :]


        hx = slice(r0, r0 + sub + 2 * HALO)
        cu = proj(hx, 4 * da, dc) * proj(hx, 4 * da + 2 * dc, dc)
        row = lax.broadcasted_iota(jnp.int32, (sub + 2 * HALO, 1), 0) + r0
        inside = ((row >= HALO) | (j > 0)) & ((row < HALO + tm) | (j < nj - 1))
        cu_ref[n] = jnp.where(inside, cu, 0.0)
        y = (cb_ref[...]
             + cw_ref[0, 0:1, :] * cu_ref[n, HALO - 1:HALO - 1 + sub, :]
             + cw_ref[0, 1:2, :] * cu_ref[n, HALO:HALO + sub, :]
             + cw_ref[0, 2:3, :] * cu_ref[n, HALO + 1:HALO + 1 + sub, :])
        bg = proj(h, 4 * da + dc, dc)
        zc = proj(h, 4 * da + 3 * dc, dc)
        conv_ref[0, rows, :] = (bg * y * _silu(zc)).astype(BF16)

        qn = _head_norm(proj(h, 0, da), g_ref[0:1, :], bd)
        qpl_ref[0, rows, :] = qn.astype(BF16)
        qrot_ref[0, rows, :] = _rope(qn, cos, slo, shi).astype(BF16)

        kf = proj(h_kv, da, da)
        kn = _head_norm(kf[0:sub], g_ref[1:2, :], bd)
        k_ref[0, rows, :] = _rope(kn, cos, slo, shi).astype(BF16)
        if with_ctx:
            kc_ref[0] = _head_norm(kf[sub + HALO:], g_ref[1:2, :], bd).astype(BF16)

        ga_ref[0, rows, :] = _silu(proj(h, 3 * da, da)).astype(BF16)
        vf = proj(h_kv, 2 * da, da)
        v_ref[0, rows, :] = vf[0:sub].astype(BF16)
        if with_ctx:
            vc_ref[0] = vf[sub + HALO:].astype(BF16)


def _in_proj(x, ctx, ada, ctx_row, norm_g, w_in, gains, cos, slo, shi, bd, conv_w, conv_b, rpb, *, tm, sub):
    b, s, d = x.shape
    n_heads, ndr, ndc = rpb.shape
    assert ndr == 2 * WIN_H - 1 and ndc == 2 * WIN_W - 1 and ndc <= GRID_W
    rpb_pad = jnp.pad(rpb, ((0, 0), (0, 0), (0, GRID_W - ndc)))
    n_tiles = (n_heads // 2) * (ndr - 1)
    bias_blocks = max(k for k in range(1, b * (s // tm) + 1) if n_tiles % k == 0)
    tiles_per_step = n_tiles // bias_blocks
    l = ctx.shape[1]
    assert s % tm == 0 and tm % sub == 0 and sub % HALO == 0
    cr = l // (s // tm)
    assert l % (s // tm) == 0 and cr % BF16_ROWS == 0
    dc = conv_w.shape[2]
    da = (w_in.shape[1] - 4 * dc) // 4
    nh = tm // HALO
    last_halo = s // HALO - 1
    tok = lambda width: pl.BlockSpec((1, tm, width), lambda i, j: (i, j, 0))
    const = lambda shape: pl.BlockSpec(shape, lambda i, j: (0,) * len(shape))
    tab = pl.BlockSpec((tm, LANES), lambda i, j: (j, 0))
    out_tok = jax.ShapeDtypeStruct((b, s, da), BF16)
    out_ctx = jax.ShapeDtypeStruct((b, l, da), BF16)
    ctx_spec = lambda width: pl.BlockSpec((1, cr, width), lambda i, j: (i, j, 0))
    nj = s // tm
    bias_spec = pl.BlockSpec((tiles_per_step, 2 * GRID_W, 2 * GRID_W),
                             lambda i, j: (jnp.minimum(i * nj + j, bias_blocks - 1), 0, 0))
    outs = pl.pallas_call(
        functools.partial(_in_proj_kernel, tm=tm, sub=sub, da=da, dc=dc, ctx_row=ctx_row,
                          bias_blocks=bias_blocks),
        grid=(b, s // tm),
        in_specs=[tok(d),
                  pl.BlockSpec((1, HALO, d), lambda i, j: (i, jnp.maximum(j * nh - 1, 0), 0)),
                  pl.BlockSpec((1, HALO, d), lambda i, j: (i, jnp.minimum((j + 1) * nh, last_halo), 0)),
                  ctx_spec(d),
                  const(ada.shape),
                  const((1, d)), const(w_in.shape), const(gains.shape),
                  tab, tab, tab, const(bd.shape), const((1, CONV_K, dc)), const((1, dc)),
                  const(rpb_pad.shape)],
        out_specs=[tok(da), tok(da), tok(da), tok(da), tok(da), tok(dc), ctx_spec(da), ctx_spec(da), bias_spec],
        out_shape=[out_tok, out_tok, out_tok, out_tok, out_tok,
                   jax.ShapeDtypeStruct((b, s, dc), BF16), out_ctx, out_ctx,
                   jax.ShapeDtypeStruct((n_tiles, 2 * GRID_W, 2 * GRID_W), F32)],
        scratch_shapes=[pltpu.VMEM((tm + 2 * HALO + cr, d), BF16),
                        pltpu.VMEM((tm // sub, sub + 2 * HALO, dc), F32),
                        pltpu.VMEM(w_in.shape, BF16)],
        compiler_params=pltpu.CompilerParams(vmem_limit_bytes=VMEM_LIMIT,
                                             dimension_semantics=("arbitrary", "arbitrary")),
        name="in_proj",
    )(x, x, x, ctx, ada, norm_g, w_in, gains, cos, slo, shi, bd, conv_w, conv_b, rpb_pad)
    bias = outs[-1].reshape(n_heads // 2, ndr - 1, 2 * GRID_W, 2 * GRID_W)
    return (*outs[:-1], bias)


def _dot_nt(a, b):
    return lax.dot_general(a, b, (((1,), (1,)), ((), ())), preferred_element_type=F32)


def _attn_kernel(qrot_ref, qpl_ref, k_ref, v_ref, kc_ref, vc_ref, ga_ref, bias_ref, o_ref,
                 s_lat_ref, s_ctx_ref, p_lat_ref, p_ctx_ref, *, rows, group, n_batch, steps):
    n_units = n_batch * rows // group
    nk = WIN_H * GRID_W

    lane = lax.broadcasted_iota(jnp.int32, (GRID_W, LANES), 1)
    first_head = lane < HEAD_DIM

    def stack_heads(q2):
        zero = jnp.zeros_like(q2)
        return jnp.concatenate([jnp.where(first_head, q2, zero), jnp.where(first_head, zero, q2)], axis=0)

    def geometry(u, g):
        r = u * group + g
        if isinstance(r, int):
            bb, i = divmod(r, rows)
            rs = min(max(i - WIN_H // 2, 0), rows - WIN_H)
            return bb, i - rs, i * GRID_W, rs * GRID_W
        bb = r // rows
        i = r % rows
        rs = jnp.clip(i - WIN_H // 2, 0, rows - WIN_H)
        return bb, i - rs, pl.multiple_of(i * GRID_W, GRID_W), pl.multiple_of(rs * GRID_W, GRID_W)


    def scores_matmul(u):
        out = []
        for g in range(group):
            bb, off, tok0, key0 = geometry(u, g)
            qs = stack_heads(qrot_ref[bb, pl.ds(tok0, GRID_W), :])
            kband = k_ref[bb, pl.ds(key0, nk), :]
            qp = stack_heads(qpl_ref[bb, pl.ds(tok0, GRID_W), :])
            out.append((_dot_nt(qs, kband), off, _dot_nt(qp, kc_ref[bb])))
        return out


    def scores_store(slot, vals):
        for g, (s_lat, off, s_ctx) in enumerate(vals):
            first = (WIN_H - 1) - off
            bias = jnp.concatenate([bias_ref[0, first + 2 * t] for t in range(WIN_H // 2)], axis=1)
            s_lat_ref[slot, g] = s_lat + bias
            s_ctx_ref[slot, g] = s_ctx

    def softmax(slot):
        for g in range(group):
            s_lat = s_lat_ref[slot, g]
            s_ctx = s_ctx_ref[slot, g]
            m = jnp.maximum(jnp.max(s_lat, axis=-1, keepdims=True), jnp.max(s_ctx, axis=-1, keepdims=True))
            p_lat_ref[slot, g] = jnp.exp2(s_lat - m).astype(BF16)
            p_ctx_ref[slot, g] = jnp.exp2(s_ctx - m).astype(BF16)

    ones_lat = jnp.ones((nk, LANES), BF16)
    ones_ctx = jnp.ones((kc_ref.shape[1], LANES), BF16)

    def pv_matmul(u, slot):
        out = []
        for g in range(group):
            bb, _, _, key0 = geometry(u, g)
            v_lat = jnp.concatenate([v_ref[bb, pl.ds(key0, nk), :], ones_lat], axis=1)
            v_ctx = jnp.concatenate([vc_ref[bb], ones_ctx], axis=1)
            out.append(jnp.dot(p_lat_ref[slot, g], v_lat, preferred_element_type=F32)
                       + jnp.dot(p_ctx_ref[slot, g], v_ctx, preferred_element_type=F32))
        return out

    def pv_store(u, vals):
        for g, ol in enumerate(vals):
            bb, _, tok0, _ = geometry(u, g)
            o = ol[:, 0:LANES] * (1.0 / ol[:, LANES:])
            o2 = jnp.where(first_head, o[0:GRID_W], o[GRID_W:2 * GRID_W])
            gate = ga_ref[bb, pl.ds(tok0, GRID_W), :].astype(F32)
            o_ref[bb, pl.ds(tok0, GRID_W), :] = (o2 * gate).astype(BF16)

    scores_store(0, scores_matmul(0))
    sc = scores_matmul(1)
    softmax(0)
    scores_store(1, sc)

    def step(t, slot):
        sc = scores_matmul(t)
        pv = pv_matmul(t - 2, slot)
        softmax(1 - slot)
        scores_store(slot, sc)
        pv_store(t - 2, pv)

    n_steady, n_left = divmod(n_units - 2, steps)

    def steady(n, carry):
        for j in range(steps):
            step(steps * n + 2 + j, j % 2)
        return carry

    lax.fori_loop(0, n_steady, steady, 0)
    for j in range(n_left):
        step(steps * n_steady + 2 + j, j % 2)

    pv = pv_matmul(n_units - 2, 0)
    softmax(1)
    pv_store(n_units - 2, pv)
    pv_store(n_units - 1, pv_matmul(n_units - 1, 1))


def _attention(qrot, qpl, k, v, kc, vc, ga, bias, *, group, n_batch, steps):
    b, s, da = qrot.shape
    l = vc.shape[1]
    rows = s // GRID_W
    n_hp = da // LANES
    nq = 2 * GRID_W
    nk = WIN_H * GRID_W
    assert rows % (2 * group) == 0 and b % n_batch == 0 and steps % 2 == 0
    tok = pl.BlockSpec((n_batch, s, LANES), lambda hp, i: (i, 0, hp))
    ctx_tok = pl.BlockSpec((n_batch, l, LANES), lambda hp, i: (i, 0, hp))
    return pl.pallas_call(
        functools.partial(_attn_kernel, rows=rows, group=group, n_batch=n_batch, steps=steps),
        grid=(n_hp, b // n_batch),
        in_specs=[tok, tok, tok, tok, ctx_tok, ctx_tok, tok,
                  pl.BlockSpec((1,) + bias.shape[1:], lambda hp, i: (hp, 0, 0, 0))],
        out_specs=tok,
        out_shape=jax.ShapeDtypeStruct((b, s, da), BF16),
        scratch_shapes=[pltpu.VMEM((2, group, nq, nk), F32),
                        pltpu.VMEM((2, group, nq, l), F32),
                        pltpu.VMEM((2, group, nq, nk), BF16),
                        pltpu.VMEM((2, group, nq, l), BF16)],
        compiler_params=pltpu.CompilerParams(vmem_limit_bytes=VMEM_LIMIT),
        name="attn",
    )(qrot, qpl, k, v, kc, vc, ga, bias)


X_RING = 6


def _out_proj_kernel(x_hbm, a_hbm, c_hbm, ada_ref, w_ref, o_ref, xbuf_ref, abuf_ref, cbuf_ref, sem_ref,
                     *, da, tm, nj):
    t = pl.program_id(0)
    n = pl.num_programs(0)
    streams = ((x_hbm, xbuf_ref), (a_hbm, abuf_ref), (c_hbm, cbuf_ref))

    def copies(tile, slot):
        rows = pl.ds(pl.multiple_of((tile % nj) * tm, tm), tm)
        return [pltpu.make_async_copy(hbm.at[tile // nj, rows, :], buf.at[slot], sem_ref.at[k, slot])
                for k, (hbm, buf) in enumerate(streams)]

    @pl.when(t == 0)
    def _():
        for k in range(X_RING - 1):
            for cp in copies(k, k):
                cp.start()

    @pl.when(t + (X_RING - 1) < n)
    def _():
        for cp in copies(t + (X_RING - 1), (t + (X_RING - 1)) % X_RING):
            cp.start()

    slot = t % X_RING
    for cp in copies(t, slot):
        cp.wait()
    _, _, gate = _ada_row(ada_ref, t // nj)
    upd = (jnp.dot(abuf_ref[slot], w_ref[0:da, :].astype(BF16), preferred_element_type=F32)
           + jnp.dot(cbuf_ref[slot], w_ref[da:, :].astype(BF16), preferred_element_type=F32))
    o_ref[0] = xbuf_ref[slot] + gate * upd


def _out_proj(x, attn, conv, ada, w_out, *, tm):
    b, s, d = x.shape
    da = attn.shape[2]
    dc = conv.shape[2]
    nj = s // tm
    assert s % tm == 0 and b * nj >= X_RING
    tok = lambda width: pl.BlockSpec((1, tm, width), lambda t: (t // nj, t % nj, 0))
    return pl.pallas_call(
        functools.partial(_out_proj_kernel, da=da, tm=tm, nj=nj),
        grid=(b * nj,),
        in_specs=[pl.BlockSpec(memory_space=pl.ANY),
                  pl.BlockSpec(memory_space=pl.ANY),
                  pl.BlockSpec(memory_space=pl.ANY),
                  pl.BlockSpec(ada.shape, lambda t: (0, 0)),
                  pl.BlockSpec(w_out.shape, lambda t: (0, 0))],
        out_specs=tok(d),
        out_shape=jax.ShapeDtypeStruct((b, s, d), F32),
        scratch_shapes=[pltpu.VMEM((X_RING, tm, d), F32),
                        pltpu.VMEM((X_RING, tm, da), BF16),
                        pltpu.VMEM((X_RING, tm, dc), BF16),
                        pltpu.SemaphoreType.DMA((3, X_RING))],
        compiler_params=pltpu.CompilerParams(vmem_limit_bytes=VMEM_LIMIT,
                                             dimension_semantics=("arbitrary",)),
        name="out_proj",
    )(x, attn, conv, ada, w_out)


def _rope_tables(s):
    nf = HEAD_DIM // 4
    inv = (ROPE_THETA ** (-np.arange(nf, dtype=np.float32) / nf)).astype(np.float32)
    pos = np.arange(s)
    lane = np.arange(LANES)
    d = lane % HEAD_DIM
    axis = d // (2 * nf)
    half = (d % (2 * nf)) // nf
    coord = np.where(axis[None, :] == 0, (pos // GRID_W)[:, None], (pos % GRID_W)[:, None]).astype(np.float32)
    ang = (coord * inv[d % nf][None, :]).astype(np.float32)
    cos = np.cos(ang).astype(np.float32)
    sin = np.sin(ang).astype(np.float32)
    sin_lo = np.where(half[None, :] == 0, -sin, 0.0).astype(np.float32)
    sin_hi = np.where(half[None, :] == 1, sin, 0.0).astype(np.float32)
    return jnp.asarray(cos), jnp.asarray(sin_lo), jnp.asarray(sin_hi)


def _bias_tiles(rpb_ref, o_ref, first_tile):
    n_pairs = rpb_ref.shape[1] - 1
    cq = lax.broadcasted_iota(jnp.int32, (GRID_W, LANES), 0)
    ck = lax.broadcasted_iota(jnp.int32, (GRID_W, LANES), 1) % GRID_W
    col_start = jnp.clip(cq - WIN_W // 2, 0, GRID_W - WIN_W)
    valid = (ck >= col_start) & (ck < col_start + WIN_W)
    for e in range(o_ref.shape[0]):
        t = first_tile + e
        hp = t // n_pairs
        dr = t % n_pairs
        for a in range(2):
            two = rpb_ref[2 * hp + a, pl.ds(dr, 2), :]
            lanes = jnp.broadcast_to(jnp.concatenate([two[0:1], two[1:2]], axis=1), (GRID_W, LANES))
            toeplitz = pltpu.roll(lanes, LANES - (WIN_W - 1), 1, stride=1, stride_axis=0)
            o_ref[e, a * GRID_W:(a + 1) * GRID_W, :] = jnp.where(valid, toeplitz * LOG2_E, MASK_VALUE)


def kernel(x, c, ctx, c_ctx, w_ada, b_ada, norm_g, w_in, q_norm_g, k_norm_g, rpb, conv_w, conv_b, w_out):
    depth = w_ada.shape[0]
    b, s, d = x.shape
    dc = conv_w.shape[2]
    da = (w_in.shape[2] - 4 * dc) // 4
    rows = s // GRID_W
    assert depth == 1 and s % GRID_W == 0 and rows >= WIN_H and da % LANES == 0

    cos, slo, shi = _rope_tables(s)
    assert da % MXU_DIM == 0 and MXU_DIM % HEAD_DIM == 0
    seg = np.arange(MXU_DIM) // HEAD_DIM
    bd = jnp.asarray((seg[:, None] == seg[None, :]).astype(np.float32), dtype=BF16)

    ada, gains = _prep(c, c_ctx, w_ada[0], b_ada, q_norm_g, k_norm_g, da, steps=PREP_STEPS)

    qrot, qpl, k, v, ga, conv, kc, vc, bias = _in_proj(x, ctx, ada, b, norm_g, w_in[0], gains, cos, slo, shi, bd,
                                                       conv_w, conv_b, rpb[0], tm=IN_PROJ_ROWS, sub=IN_PROJ_SUB)
    attn = _attention(qrot, qpl, k, v, kc, vc, ga, bias, group=ATTN_GROUP, n_batch=ATTN_BATCH, steps=ATTN_STEPS)
    return _out_proj(x, attn, conv, ada, w_out[0], tm=OUT_PROJ_ROWS)
```
